```python
import math
import jax
import jax.numpy as jnp
from jax import lax
import numpy as np

D_MODEL = 2048
BATCH = 1
SEQ = 8192
DEPTH = 2

GRID_W = 64
CTX_LEN = 256
NORM_EPS = 1e-6
POS_BASE = 10000.0
CONV_W = 4
LRU_WIDTH = 1024
LRU_BLOCKS = 16
LRU_BLOCK = LRU_WIDTH // LRU_BLOCKS
LRU_C = 8.0
GLA_HEADS = 8
GLA_DK = 64
GLA_DV = 128
GLA_QK = GLA_HEADS * GLA_DK
GLA_V = GLA_HEADS * GLA_DV
GLA_RANK = 16
GLA_TAU = 16.0
GLA_CHUNK = 64
EVEN_PROJ = 2 * LRU_WIDTH + 2 * GLA_QK + 2 * GLA_V + 2 * GLA_RANK
EVEN_SPLITS = (LRU_WIDTH, 2 * LRU_WIDTH, 2 * LRU_WIDTH + GLA_QK, 2 * LRU_WIDTH + 2 * GLA_QK, 2 * LRU_WIDTH + 2 * GLA_QK + GLA_V, 2 * LRU_WIDTH + 2 * GLA_QK + 2 * GLA_V)
EVEN_OUT = LRU_WIDTH + GLA_V
SSD_INNER = 2 * D_MODEL
SSD_HEADDIM = 64
SSD_HEADS = SSD_INNER // SSD_HEADDIM
SSD_GROUPS = 8
SSD_STATE = 128
SSD_CHUNK = 64
SSD_CONV_DIM = SSD_INNER + 2 * SSD_GROUPS * SSD_STATE
ODD_PROJ = SSD_INNER + SSD_CONV_DIM + 2 * SSD_HEADS
ODD_SPLITS = (SSD_INNER, SSD_INNER + SSD_CONV_DIM)
N_EXPERTS = 32
N_EXPERT_GROUPS = 4
EXPERTS_PER_GROUP = N_EXPERTS // N_EXPERT_GROUPS
TOP_K = 2
D_EXPERT = 1024
MOE_BLOCK = 128

kernel_name = 'hybrid_lru_gla_ssd_moe_flow_block'


def rmsnorm(x, g):
    xf = x.astype(jnp.float32)
    y = xf * lax.rsqrt(jnp.mean(xf * xf, axis=-1, keepdims=True) + NORM_EPS)
    return (y * g.astype(jnp.float32)).astype(x.dtype)


def ada_norm(x, g, shift, scale):
    return rmsnorm(x, g) * (1 + scale) + shift


def head_rmsnorm(o, g):
    y = o * lax.rsqrt(jnp.mean(o * o, axis=-1, keepdims=True) + NORM_EPS)
    return y.reshape(o.shape[0], o.shape[1], -1) * g.astype(jnp.float32)


def pos_embed_2d(rows, cols, dim):
    quarter = dim // 4
    omega = 1.0 / (POS_BASE ** (jnp.arange(quarter, dtype=jnp.float32) / quarter))
    ang_r = jnp.arange(rows, dtype=jnp.float32)[:, None] * omega
    ang_c = jnp.arange(cols, dtype=jnp.float32)[:, None] * omega
    emb_r = jnp.concatenate([jnp.sin(ang_r), jnp.cos(ang_r)], axis=-1)
    emb_c = jnp.concatenate([jnp.sin(ang_c), jnp.cos(ang_c)], axis=-1)
    pos = jnp.concatenate([jnp.broadcast_to(emb_r[:, None], (rows, cols, dim // 2)), jnp.broadcast_to(emb_c[None], (rows, cols, dim // 2))], axis=-1)
    return pos.reshape(rows * cols, dim)


def dwconv(x, w, b):
    left = CONV_W // 2
    y = lax.conv_general_dilated(x, w[:, None, :], window_strides=(1,), padding=[(left, CONV_W - 1 - left)], dimension_numbers=('NWC', 'WIO', 'NWC'), feature_group_count=x.shape[-1])
    return y + b


def ctx_then_latent(scan_fn, ctx_in, lat_in, state0, reverse):
    if reverse:
        ctx_in = tuple(jnp.flip(t, axis=1) for t in ctx_in)
        lat_in = tuple(jnp.flip(t, axis=1) for t in lat_in)
    y_c, state_c = scan_fn(*ctx_in, state0)
    y_l, _ = scan_fn(*lat_in, state_c)
    if reverse:
        y_c, y_l = jnp.flip(y_c, axis=1), jnp.flip(y_l, axis=1)
    return y_c, y_l


def _lin_comb(left, right):
    a1, b1 = left
    a2, b2 = right
    return a1 * a2, a2 * b1 + b2


def lru_scan(a, b, h0):
    a_cum, b_cum = lax.associative_scan(_lin_comb, (a, b), axis=1)
    h = a_cum * h0[:, None] + b_cum
    return h, h[:, -1]


def gla_scan(q, k, v, log_g, h0):
    bsz, t, nh, dk = q.shape
    dv = v.shape[-1]
    nc = t // GLA_CHUNK
    qc = q.reshape(bsz, nc, GLA_CHUNK, nh, dk)
    kc = k.reshape(bsz, nc, GLA_CHUNK, nh, dk)
    vc = v.reshape(bsz, nc, GLA_CHUNK, nh, dv)
    cum = jnp.cumsum(log_g.astype(jnp.float32).reshape(bsz, nc, GLA_CHUNK, nh, dk), axis=2)
    last = cum[:, :, -1:]
    q_dec = qc * jnp.exp(cum)
    k_inv = kc * jnp.exp(-cum)
    k_end = kc * jnp.exp(last - cum)
    mask = jnp.tril(jnp.ones((GLA_CHUNK, GLA_CHUNK), dtype=bool))
    scores = jnp.where(mask, jnp.einsum('bnchd,bnshd->bnhcs', q_dec, k_inv), 0.0)
    o_intra = jnp.einsum('bnhcs,bnshe->bnche', scores, vc)
    upd = jnp.einsum('bnchd,bnche->bnhde', k_end, vc).astype(jnp.float32)
    decay = jnp.exp(last[:, :, 0])

    def step(s, inp):
        dec, u = inp
        return dec[..., None] * s + u, s

    s_fin, s_start = lax.scan(step, h0, (jnp.moveaxis(decay, 1, 0), jnp.moveaxis(upd, 1, 0)))
    o_inter = jnp.einsum('bnchd,bnhde->bnche', q_dec, jnp.moveaxis(s_start, 0, 1))
    return (o_intra + o_inter).reshape(bsz, t, nh, dv), s_fin


def ssd_scan(xdt, log_a, b_in, c_in, h0):
    bsz, t, nh, hp = xdt.shape
    ng, ns = b_in.shape[2], b_in.shape[3]
    hg = nh // ng
    nc = t // SSD_CHUNK
    xc = xdt.reshape(bsz, nc, SSD_CHUNK, ng, hg, hp)
    bc = b_in.reshape(bsz, nc, SSD_CHUNK, ng, ns)
    cc = c_in.reshape(bsz, nc, SSD_CHUNK, ng, ns)
    cum = jnp.cumsum(log_a.astype(jnp.float32).reshape(bsz, nc, SSD_CHUNK, ng, hg), axis=2)
    cum = jnp.moveaxis(cum, 2, -1)
    mask = jnp.tril(jnp.ones((SSD_CHUNK, SSD_CHUNK), dtype=bool))
    seg = jnp.exp(jnp.where(mask, cum[..., :, None] - cum[..., None, :], -jnp.inf))
    cb = jnp.einsum('bncgz,bnsgz->bngcs', cc, bc)
    y_intra = jnp.einsum('bngcs,bnghcs,bnsghp->bncghp', cb, seg, xc)
    to_end = jnp.exp(cum[..., -1:] - cum)
    chunk_states = jnp.einsum('bncgz,bnghc,bncghp->bnghpz', bc, to_end, xc).astype(jnp.float32)
    chunk_decay = jnp.exp(cum[..., -1])

    def step(s, inp):
        dec, st = inp
        return dec[..., None, None] * s + st, s

    s_fin, s_start = lax.scan(step, h0.reshape(bsz, ng, hg, hp, ns), (jnp.moveaxis(chunk_decay, 1, 0), jnp.moveaxis(chunk_states, 1, 0)))
    y_inter = jnp.einsum('bncgz,bnghpz,bnghc->bncghp', cc, jnp.moveaxis(s_start, 0, 1), jnp.exp(cum))
    y = (y_intra + y_inter).reshape(bsz, t, nh, hp)
    return y, s_fin.reshape(bsz, nh, hp, ns)


def even_mixer(h_c, h_l, w_in, conv_w, conv_b, wa, ba, wx, bx, lam, wg_up, bg, norm_g, w_out):
    def inputs(h):
        bsz, t, _ = h.shape
        proj = h @ w_in
        xa, ga, q, k, v, og, ad = jnp.split(proj, EVEN_SPLITS, axis=-1)
        xa = dwconv(xa, conv_w, conv_b)
        xb = xa.reshape(bsz, t, LRU_BLOCKS, LRU_BLOCK)
        r = jax.nn.sigmoid((jnp.einsum('btnk,dnkj->btdnj', xb, wa).reshape(bsz, t, 2, LRU_WIDTH) + ba).astype(jnp.float32))
        ig = jax.nn.sigmoid((jnp.einsum('btnk,dnkj->btdnj', xb, wx).reshape(bsz, t, 2, LRU_WIDTH) + bx).astype(jnp.float32))
        log_a = -LRU_C * r * jax.nn.softplus(-lam.astype(jnp.float32))
        a = jnp.exp(log_a)
        b = jnp.sqrt(-jnp.expm1(2.0 * log_a)) * ig * xa[:, :, None, :].astype(jnp.float32)
        q = q.reshape(bsz, t, GLA_HEADS, GLA_DK) * (GLA_DK ** -0.5)
        k = k.reshape(bsz, t, GLA_HEADS, GLA_DK)
        v = v.reshape(bsz, t, GLA_HEADS, GLA_DV)
        log_g = jax.nn.log_sigmoid((jnp.einsum('btdr,dre->btde', ad.reshape(bsz, t, 2, GLA_RANK), wg_up) + bg).astype(jnp.float32)) / GLA_TAU
        log_g = log_g.reshape(bsz, t, 2, GLA_HEADS, GLA_DK)
        return ga, og, a, b, q, k, v, log_g

    ga_c, og_c, a_c, b_c, q_c, k_c, v_c, g_c = inputs(h_c)
    ga_l, og_l, a_l, b_l, q_l, k_l, v_l, g_l = inputs(h_l)
    bsz = h_l.shape[0]
    s0_lru = jnp.zeros((bsz, LRU_WIDTH), jnp.float32)
    s0_gla = jnp.zeros((bsz, GLA_HEADS, GLA_DK, GLA_DV), jnp.float32)
    lru_c_f, lru_l_f = ctx_then_latent(lru_scan, (a_c[:, :, 0], b_c[:, :, 0]), (a_l[:, :, 0], b_l[:, :, 0]), s0_lru, False)
    lru_c_b, lru_l_b = ctx_then_latent(lru_scan, (a_c[:, :, 1], b_c[:, :, 1]), (a_l[:, :, 1], b_l[:, :, 1]), s0_lru, True)
    gla_c_f, gla_l_f = ctx_then_latent(gla_scan, (q_c, k_c, v_c, g_c[:, :, 0]), (q_l, k_l, v_l, g_l[:, :, 0]), s0_gla, False)
    gla_c_b, gla_l_b = ctx_then_latent(gla_scan, (q_c, k_c, v_c, g_c[:, :, 1]), (q_l, k_l, v_l, g_l[:, :, 1]), s0_gla, True)

    def merge(lru_f, lru_b, gla_f, gla_b, ga, og):
        y_a = (lru_f + lru_b) * jax.nn.gelu(ga.astype(jnp.float32))
        y_b = head_rmsnorm(gla_f + gla_b, norm_g) * jax.nn.silu(og.astype(jnp.float32))
        return jnp.concatenate([y_a, y_b], axis=-1).astype(w_out.dtype) @ w_out

    return (merge(lru_c_f, lru_c_b, gla_c_f, gla_c_b, ga_c, og_c), merge(lru_l_f, lru_l_b, gla_l_f, gla_l_b, ga_l, og_l))


def odd_mixer(h_c, h_l, w_in, conv_w, conv_b, a_log, dt_bias, d_skip, norm_g, w_out):
    a_neg = -jnp.exp(a_log.astype(jnp.float32))

    def inputs(h):
        bsz, t, _ = h.shape
        z, xbc, dt = jnp.split(h @ w_in, ODD_SPLITS, axis=-1)
        xbc = jax.nn.silu(dwconv(xbc, conv_w, conv_b))
        xs, bm, cm = jnp.split(xbc, (SSD_INNER, SSD_INNER + SSD_GROUPS * SSD_STATE), axis=-1)
        xs = xs.reshape(bsz, t, SSD_HEADS, SSD_HEADDIM)
        bm = bm.reshape(bsz, t, SSD_GROUPS, SSD_STATE)
        cm = cm.reshape(bsz, t, SSD_GROUPS, SSD_STATE)
        dt = jax.nn.softplus(dt.astype(jnp.float32).reshape(bsz, t, 2, SSD_HEADS) + dt_bias.astype(jnp.float32))
        return z, xs, bm, cm, dt

    def direction(xs, bm, cm, dt, d):
        return (xs.astype(jnp.float32) * dt[:, :, d, :, None], dt[:, :, d] * a_neg[d], bm, cm)

    z_c, x_c, b_c, c_c, dt_c = inputs(h_c)
    z_l, x_l, b_l, c_l, dt_l = inputs(h_l)
    s0 = jnp.zeros((h_l.shape[0], SSD_HEADS, SSD_HEADDIM, SSD_STATE), jnp.float32)
    yc_f, yl_f = ctx_then_latent(ssd_scan, direction(x_c, b_c, c_c, dt_c, 0), direction(x_l, b_l, c_l, dt_l, 0), s0, False)
    yc_b, yl_b = ctx_then_latent(ssd_scan, direction(x_c, b_c, c_c, dt_c, 1), direction(x_l, b_l, c_l, dt_l, 1), s0, True)

    def finish(y_f, y_b, xs, z):
        y = y_f + y_b + d_skip.astype(jnp.float32)[:, None] * xs.astype(jnp.float32)
        y = y.reshape(xs.shape[0], xs.shape[1], SSD_INNER)
        return rmsnorm(y * jax.nn.silu(z.astype(jnp.float32)), norm_g).astype(w_out.dtype) @ w_out

    return (finish(yc_f, yc_b, x_c, z_c), finish(yl_f, yl_b, x_l, z_l))


def moe_ffn(h, router_w, router_b, w_gate, w_up, w_down):
    n_tok, d = h.shape
    n_exp = w_gate.shape[0]
    scores = jax.nn.sigmoid((h @ router_w).astype(jnp.float32))
    sel = (scores + router_b.astype(jnp.float32)).reshape(n_tok, N_EXPERT_GROUPS, EXPERTS_PER_GROUP)
    group_score = lax.top_k(sel, TOP_K)[0].sum(-1)
    group = jnp.argmax(group_score, axis=-1)
    in_group = jnp.take_along_axis(sel, group[:, None, None], axis=1)[:, 0]
    _, local = lax.top_k(in_group, TOP_K)
    expert = group[:, None] * EXPERTS_PER_GROUP + local
    weight = jnp.take_along_axis(scores, expert, axis=1)
    weight = weight / weight.sum(-1, keepdims=True)
    flat_e = expert.reshape(-1)
    n_assign = flat_e.shape[0]
    order = jnp.argsort(flat_e)
    e_sorted = flat_e[order]
    tok_sorted = (order // TOP_K).astype(jnp.int32)
    w_sorted = weight.reshape(-1)[order]
    sizes = jnp.bincount(flat_e, length=n_exp)
    padded = (sizes + MOE_BLOCK - 1) // MOE_BLOCK * MOE_BLOCK
    start = jnp.cumsum(sizes) - sizes
    ends = jnp.cumsum(padded)
    pstart = ends - padded
    dest = pstart[e_sorted] + jnp.arange(n_assign) - start[e_sorted]
    n_blocks = -(-n_assign // MOE_BLOCK) + n_exp
    slot_tok = jnp.full((n_blocks * MOE_BLOCK,), n_tok, jnp.int32).at[dest].set(tok_sorted)
    slot_w = jnp.zeros((n_blocks * MOE_BLOCK,), jnp.float32).at[dest].set(w_sorted)
    block_exp = jnp.minimum(jnp.searchsorted(ends, jnp.arange(n_blocks) * MOE_BLOCK, side='right'), n_exp - 1)
    h_pad = jnp.concatenate([h, jnp.zeros((1, d), h.dtype)], axis=0)
    xb = h_pad[slot_tok].reshape(n_blocks, MOE_BLOCK, d)

    def expert_block(args):
        xe, e = args
        return (jax.nn.silu(xe @ w_gate[e]) * (xe @ w_up[e])) @ w_down[e]

    yb = lax.map(expert_block, (xb, block_exp)).reshape(-1, d)
    out = jnp.zeros((n_tok + 1, d), jnp.float32).at[slot_tok].add(yb.astype(jnp.float32) * slot_w[:, None])
    return out[:n_tok].astype(h.dtype)


def setup_inputs(seed: int = 0) -> dict:
    key = jax.random.key(seed)
    ks = iter(jax.random.split(key, 64))
    f32 = jnp.float32
    dm = D_MODEL
    n_even = (DEPTH + 1) // 2
    n_odd = DEPTH // 2

    def nrm(shape, scale):
        return jax.random.normal(next(ks), shape, f32) * scale

    def unif(shape, lo, hi):
        return jax.random.uniform(next(ks), shape, f32, lo, hi)

    lru_s = unif((n_even, 2, LRU_WIDTH), 0.9, 0.999) ** (1.0 / LRU_C)
    dt0 = jnp.exp(unif((n_odd, 2, SSD_HEADS), math.log(1e-3), math.log(1e-1)))
    return {
        'x': nrm((BATCH, SEQ, dm), 1.0),
        'c': nrm((BATCH, dm), 1.0),
        'ctx': nrm((BATCH, CTX_LEN, dm), 1.0),
        'c_ctx': nrm((dm,), 1.0),
        'mod_w': nrm((DEPTH, dm, 6 * dm), 0.5 * dm ** -0.5),
        'mod_b': nrm((DEPTH, 6 * dm), 0.02),
        'norm1_g': 1.0 + nrm((DEPTH, dm), 0.02),
        'norm2_g': 1.0 + nrm((DEPTH, dm), 0.02),
        'ev_w_in': nrm((n_even, dm, EVEN_PROJ), dm ** -0.5),
        'ev_conv_w': nrm((n_even, CONV_W, LRU_WIDTH), CONV_W ** -0.5),
        'ev_conv_b': nrm((n_even, LRU_WIDTH), 0.02),
        'lru_wa': nrm((n_even, 2, LRU_BLOCKS, LRU_BLOCK, LRU_BLOCK), LRU_BLOCK ** -0.5),
        'lru_ba': nrm((n_even, 2, LRU_WIDTH), 0.02),
        'lru_wx': nrm((n_even, 2, LRU_BLOCKS, LRU_BLOCK, LRU_BLOCK), LRU_BLOCK ** -0.5),
        'lru_bx': nrm((n_even, 2, LRU_WIDTH), 0.02),
        'lru_lambda': jnp.log(lru_s) - jnp.log1p(-lru_s),
        'gla_wg_up': nrm((n_even, 2, GLA_RANK, GLA_QK), GLA_RANK ** -0.5),
        'gla_bg': unif((n_even, 2, GLA_QK), 1.0, 4.0),
        'gla_norm_g': 1.0 + nrm((n_even, GLA_V), 0.02),
        'ev_w_out': nrm((n_even, EVEN_OUT, dm), EVEN_OUT ** -0.5),
        'od_w_in': nrm((n_odd, dm, ODD_PROJ), dm ** -0.5),
        'od_conv_w': nrm((n_odd, CONV_W, SSD_CONV_DIM), CONV_W ** -0.5),
        'od_conv_b': nrm((n_odd, SSD_CONV_DIM), 0.02),
        'ssd_a_log': jnp.log(unif((n_odd, 2, SSD_HEADS), 1.0, 16.0)),
        'ssd_dt_bias': dt0 + jnp.log(-jnp.expm1(-dt0)),
        'ssd_d': 1.0 + nrm((n_odd, SSD_HEADS), 0.1),
        'ssd_norm_g': 1.0 + nrm((n_odd, SSD_INNER), 0.02),
        'od_w_out': nrm((n_odd, SSD_INNER, dm), SSD_INNER ** -0.5),
        'router_w': nrm((dm, N_EXPERTS), dm ** -0.5),
        'router_b': nrm((N_EXPERTS,), 0.01),
        'exp_w_gate': nrm((DEPTH, N_EXPERTS, dm, D_EXPERT), dm ** -0.5),
        'exp_w_up': nrm((DEPTH, N_EXPERTS, dm, D_EXPERT), dm ** -0.5),
        'exp_w_down': nrm((DEPTH, N_EXPERTS, D_EXPERT, dm), D_EXPERT ** -0.5),
        'final_norm_g': 1.0 + nrm((dm,), 0.02),
    }


def reference(x, c, ctx, c_ctx, mod_w, mod_b, norm1_g, norm2_g, ev_w_in, ev_conv_w, ev_conv_b, lru_wa, lru_ba, lru_wx, lru_bx, lru_lambda, gla_wg_up, gla_bg, gla_norm_g, ev_w_out, od_w_in, od_conv_w, od_conv_b, ssd_a_log, ssd_dt_bias, ssd_d, ssd_norm_g, od_w_out, router_w, router_b, exp_w_gate, exp_w_up, exp_w_down, final_norm_g):
    bsz, seq, dm = x.shape
    rows = seq // GRID_W
    x_l = x + pos_embed_2d(rows, GRID_W, dm).astype(x.dtype)[None]
    x_c = ctx
    for i in range(DEPTH):
        last = i == DEPTH - 1
        j = i // 2
        mod_l = jax.nn.silu(c) @ mod_w[i] + mod_b[i]
        mod_c = jax.nn.silu(c_ctx)[None] @ mod_w[i] + mod_b[i]
        sh1_l, sc1_l, g1_l, sh2_l, sc2_l, g2_l = jnp.split(mod_l[:, None, :], 6, axis=-1)
        sh1_c, sc1_c, g1_c, sh2_c, sc2_c, g2_c = jnp.split(mod_c[:, None, :], 6, axis=-1)
        h_l = ada_norm(x_l, norm1_g[i], sh1_l, sc1_l)
        h_c = ada_norm(x_c, norm1_g[i], sh1_c, sc1_c)
        if i % 2 == 0:
            y_c, y_l = even_mixer(h_c, h_l, ev_w_in[j], ev_conv_w[j], ev_conv_b[j], lru_wa[j], lru_ba[j], lru_wx[j], lru_bx[j], lru_lambda[j], gla_wg_up[j], gla_bg[j], gla_norm_g[j], ev_w_out[j])
        else:
            y_c, y_l = odd_mixer(h_c, h_l, od_w_in[j], od_conv_w[j], od_conv_b[j], ssd_a_log[j], ssd_dt_bias[j], ssd_d[j], ssd_norm_g[j], od_w_out[j])
        x_l = x_l + (g1_l * y_l).astype(x_l.dtype)
        h2_l = ada_norm(x_l, norm2_g[i], sh2_l, sc2_l)
        if last:
            f_l = moe_ffn(h2_l.reshape(-1, dm), router_w, router_b, exp_w_gate[i], exp_w_up[i], exp_w_down[i])
            x_l = x_l + (g2_l * f_l.reshape(x_l.shape)).astype(x_l.dtype)
        else:
            x_c = x_c + (g1_c * y_c).astype(x_c.dtype)
            h2_c = ada_norm(x_c, norm2_g[i], sh2_c, sc2_c)
            n_ctx = x_c.shape[0] * x_c.shape[1]
            f = moe_ffn(jnp.concatenate([h2_c.reshape(-1, dm), h2_l.reshape(-1, dm)], axis=0), router_w, router_b, exp_w_gate[i], exp_w_up[i], exp_w_down[i])
            x_c = x_c + (g2_c * f[:n_ctx].reshape(x_c.shape)).astype(x_c.dtype)
            x_l = x_l + (g2_l * f[n_ctx:].reshape(x_l.shape)).astype(x_l.dtype)
    return rmsnorm(x_l, final_norm_g)
```

```python
import functools
import math

import jax
import jax.numpy as jnp
from jax import lax
from jax.experimental import pallas as pl
from jax.experimental.pallas import tpu as pltpu

F32 = jnp.float32
BF16 = jnp.bfloat16
HIGHEST = lax.Precision.HIGHEST

D = 2048
T_LAT = 8192
T_CTX = 256
N_ALL = T_CTX + T_LAT
GRID_W = 64
EPS = 1e-6
CONV_W = 4
LRU_W = 1024
LRU_C = 8.0
GLA_H = 8
GLA_DK = 64
GLA_DV = 128
GLA_QK = GLA_H * GLA_DK
GLA_V = GLA_H * GLA_DV
GLA_RANK = 16
GLA_TAU = 16.0
CHUNK = 64
EVEN_MAIN = 2 * LRU_W + 2 * GLA_QK + 2 * GLA_V
SSD_INNER = 2 * D
SSD_P = 64
SSD_H = SSD_INNER // SSD_P
SSD_G = 8
SSD_Z = 128
SSD_XBC = SSD_INNER + 2 * SSD_G * SSD_Z
ODD_MAIN = SSD_INNER + SSD_XBC
N_EXP = 32
N_GRP = 4
GRP = N_EXP // N_GRP
D_EXP = 1024
MOE_TM = 256

ROWS = 256
LANES = 128
SUBLANES = 8
VMEM_LIMIT = 56 * 1024 * 1024


def _params(*sem):
    return pltpu.CompilerParams(dimension_semantics=sem, vmem_limit_bytes=VMEM_LIMIT)


def _silu(v):
    return v * jax.nn.sigmoid(v)


def _softplus(v):
    return jnp.maximum(v, 0.0) + jnp.log1p(jnp.exp(-jnp.abs(v)))


def _dot(a, b):
    return jnp.dot(a, b, preferred_element_type=F32)


def _dot_nt(a, b):
    return lax.dot_general(a, b, (((1,), (1,)), ((), ())), preferred_element_type=F32)


def _dot_tn(a, b):
    return lax.dot_general(a, b, (((0,), (0,)), ((), ())), preferred_element_type=F32)


def _mod_kernel(s_ref, w_ref, b_ref, o_ref):
    tn = w_ref.shape[1]
    nrep = tn // LANES

    def body(r, acc):
        a0, a1 = acc
        rows = pl.ds(pl.multiple_of(r * SUBLANES, SUBLANES), SUBLANES)
        w = w_ref[rows, :]
        s0 = _silu(s_ref[0, rows, :])
        s1 = _silu(s_ref[1, rows, :])
        a0 = a0 + w * jnp.concatenate([s0] * nrep, axis=1)
        a1 = a1 + w * jnp.concatenate([s1] * nrep, axis=1)
        return a0, a1

    zero = jnp.zeros((SUBLANES, tn), F32)
    a0, a1 = lax.fori_loop(0, D // SUBLANES, body, (zero, zero), unroll=4)
    o_ref[0:1, :] = jnp.sum(a0, axis=0, keepdims=True) + b_ref[...]
    o_ref[1:2, :] = jnp.sum(a1, axis=0, keepdims=True) + b_ref[...]


def _mod_vectors(c, c_ctx, mod_w, mod_b):
    depth = mod_w.shape[0]
    tn = 1024
    s = jnp.broadcast_to(jnp.stack([c[0], c_ctx])[:, :, None], (2, D, LANES))
    return pl.pallas_call(
        _mod_kernel,
        grid=(depth, 6 * D // tn),
        in_specs=[
            pl.BlockSpec((2, D, LANES), lambda l, j: (0, 0, 0)),
            pl.BlockSpec((None, D, tn), lambda l, j: (l, 0, j)),
            pl.BlockSpec((None, 1, tn), lambda l, j: (l, 0, j)),
        ],
        out_specs=pl.BlockSpec((None, 2, tn), lambda l, j: (l, 0, j)),
        out_shape=jax.ShapeDtypeStruct((depth, 2, 6 * D), F32),
        compiler_params=_params("arbitrary", "arbitrary"),
        name="mod_vectors",
    )(s, mod_w, mod_b.reshape(depth, 1, 6 * D))


def _mod_row(mod_ref, kind, col):
    return mod_ref[pl.ds(kind, 1), col * D:(col + 1) * D]


def _ada_norm(xv, g, shift, scale):
    ms = jnp.mean(xv * xv, axis=-1, keepdims=True)
    return (xv * lax.rsqrt(ms + EPS) * g) * (1.0 + scale) + shift


def _prep0_kernel(x_ref, ctx_ref, er_ref, ec_ref, mod_ref, g_ref, xo_ref, ho_ref):
    i = pl.program_id(0)

    @pl.when(i == 0)
    def _():
        xo_ref[...] = ctx_ref[...]

    @pl.when(i > 0)
    def _():
        r0 = (i - 1) * (ROWS // GRID_W)
        for j in range(ROWS // GRID_W):
            rs = slice(j * GRID_W, (j + 1) * GRID_W)
            xo_ref[rs, 0:D // 2] = x_ref[rs, 0:D // 2] + er_ref[pl.ds(r0 + j, 1), :]
            xo_ref[rs, D // 2:D] = x_ref[rs, D // 2:D] + ec_ref[...]

    kind = jnp.where(i == 0, 1, 0)
    h = _ada_norm(xo_ref[...], g_ref[...], _mod_row(mod_ref, kind, 0), _mod_row(mod_ref, kind, 1))
    ho_ref[...] = h.astype(BF16)


def _prep0(x, ctx, emb_r, emb_c, mod0, g):
    nblk = N_ALL // ROWS
    return pl.pallas_call(
        _prep0_kernel,
        grid=(nblk,),
        in_specs=[
            pl.BlockSpec((ROWS, D), lambda i: (jnp.maximum(i - 1, 0), 0)),
            pl.BlockSpec((ROWS, D), lambda i: (0, 0)),
            pl.BlockSpec(emb_r.shape, lambda i: (0, 0)),
            pl.BlockSpec(emb_c.shape, lambda i: (0, 0)),
            pl.BlockSpec((2, 6 * D), lambda i: (0, 0)),
            pl.BlockSpec((1, D), lambda i: (0, 0)),
        ],
        out_specs=[pl.BlockSpec((ROWS, D), lambda i: (i, 0)), pl.BlockSpec((ROWS, D), lambda i: (i, 0))],
        out_shape=[jax.ShapeDtypeStruct((N_ALL, D), F32), jax.ShapeDtypeStruct((N_ALL, D), BF16)],
        compiler_params=_params("arbitrary"),
        name="embed_norm",
    )(x, ctx, emb_r, emb_c, mod0, g)


def _route(hf32, rwt_ref, rb_ref, eo_ref, wo_ref):
    logits = lax.dot_general(rwt_ref[...], hf32, (((1,), (1,)), ((), ())),
                             precision=HIGHEST, preferred_element_type=F32)
    s = jax.nn.sigmoid(logits)
    sel = s + rb_ref[...]
    row = lax.broadcasted_iota(jnp.int32, (GRP, ROWS), 0)
    neg = jnp.float32(-jnp.inf)
    gs, i1s, i2s = [], [], []
    for g in range(N_GRP):
        blk = sel[g * GRP:(g + 1) * GRP, :]
        m1 = jnp.max(blk, axis=0, keepdims=True)
        i1 = jnp.min(jnp.where(blk == m1, row, GRP), axis=0, keepdims=True)
        blk2 = jnp.where(row == i1, neg, blk)
        m2 = jnp.max(blk2, axis=0, keepdims=True)
        i2 = jnp.min(jnp.where(blk2 == m2, row, GRP), axis=0, keepdims=True)
        gs.append(m1 + m2)
        i1s.append(i1)
        i2s.append(i2)
    best, gi, i1, i2 = gs[0], jnp.zeros((1, ROWS), jnp.int32), i1s[0], i2s[0]
    for g in range(1, N_GRP):
        upd = gs[g] > best
        best = jnp.where(upd, gs[g], best)
        gi = jnp.where(upd, g, gi)
        i1 = jnp.where(upd, i1s[g], i1)
        i2 = jnp.where(upd, i2s[g], i2)
    e1 = gi * GRP + i1
    e2 = gi * GRP + i2
    erow = lax.broadcasted_iota(jnp.int32, (N_EXP, ROWS), 0)
    s1 = jnp.sum(jnp.where(erow == e1, s, 0.0), axis=0, keepdims=True)
    s2 = jnp.sum(jnp.where(erow == e2, s, 0.0), axis=0, keepdims=True)
    tot = s1 + s2
    eo_ref[0:1, :] = e1
    eo_ref[1:2, :] = e2
    wo_ref[0:1, :] = s1 / tot
    wo_ref[1:2, :] = s2 / tot


def _resid_norm_kernel(x_ref, y_ref, modg_ref, modn_ref, g_ref, rwt_ref, rb_ref, *out_refs,
                       gate_col, shift_col, scale_col, ctx_blocks, route):
    i = pl.program_id(0)
    kind = jnp.where(i < ctx_blocks, 1, 0)
    xn = x_ref[...] + _mod_row(modg_ref, kind, gate_col) * y_ref[...]
    xo_ref, ho_ref = out_refs[0], out_refs[1]
    xo_ref[...] = xn
    h = _ada_norm(xn, g_ref[...], _mod_row(modn_ref, kind, shift_col), _mod_row(modn_ref, kind, scale_col))
    ho_ref[...] = h.astype(BF16)
    if route:
        _route(h, rwt_ref, rb_ref, out_refs[2], out_refs[3])


def _resid_norm(x, y, modg, modn, g, rwt, rb, *, gate_col, shift_col, scale_col, x_off, ctx_blocks, route):
    n = y.shape[0]
    nblk = n // ROWS
    out_specs = [pl.BlockSpec((ROWS, D), lambda i: (i, 0)), pl.BlockSpec((ROWS, D), lambda i: (i, 0))]
    out_shape = [jax.ShapeDtypeStruct((n, D), F32), jax.ShapeDtypeStruct((n, D), BF16)]
    if route:
        out_specs += [pl.BlockSpec((2, ROWS), lambda i: (0, i)), pl.BlockSpec((2, ROWS), lambda i: (0, i))]
        out_shape += [jax.ShapeDtypeStruct((2, n), jnp.int32), jax.ShapeDtypeStruct((2, n), F32)]
    kern = functools.partial(_resid_norm_kernel, gate_col=gate_col, shift_col=shift_col, scale_col=scale_col,
                             ctx_blocks=ctx_blocks, route=route)
    return pl.pallas_call(
        kern,
        grid=(nblk,),
        in_specs=[
            pl.BlockSpec((ROWS, D), lambda i: (i + x_off, 0)),
            pl.BlockSpec((ROWS, D), lambda i: (i, 0)),
            pl.BlockSpec((2, 6 * D), lambda i: (0, 0)),
            pl.BlockSpec((2, 6 * D), lambda i: (0, 0)),
            pl.BlockSpec((1, D), lambda i: (0, 0)),
            pl.BlockSpec((N_EXP, D), lambda i: (0, 0)),
            pl.BlockSpec((N_EXP, 1), lambda i: (0, 0)),
        ],
        out_specs=out_specs,
        out_shape=out_shape,
        compiler_params=_params("arbitrary"),
        name="resid_norm",
    )(x, y, modg, modn, g, rwt, rb)


def _final_kernel(x_ref, y_ref, mod_ref, g_ref, o_ref):
    xn = x_ref[...] + _mod_row(mod_ref, 0, 5) * y_ref[...]
    ms = jnp.mean(xn * xn, axis=-1, keepdims=True)
    o_ref[...] = xn * lax.rsqrt(ms + EPS) * g_ref[...]


def _final(x, y, mod, g):
    n = x.shape[0]
    return pl.pallas_call(
        _final_kernel,
        grid=(n // ROWS,),
        in_specs=[
            pl.BlockSpec((ROWS, D), lambda i: (i, 0)),
            pl.BlockSpec((ROWS, D), lambda i: (i, 0)),
            pl.BlockSpec((2, 6 * D), lambda i: (0, 0)),
            pl.BlockSpec((1, D), lambda i: (0, 0)),
        ],
        out_specs=pl.BlockSpec((ROWS, D), lambda i: (i, 0)),
        out_shape=jax.ShapeDtypeStruct((n, D), F32),
        compiler_params=_params("arbitrary"),
        name="final_norm",
    )(x, y, mod, g)


def _mm_kernel(*refs, nx):
    x_refs, w_refs, o_ref, wb_refs = refs[:nx], refs[nx:2 * nx], refs[2 * nx], refs[2 * nx + 1:]

    @pl.when(pl.program_id(1) == 0)
    def _():
        for w_ref, wb_ref in zip(w_refs, wb_refs):
            wb_ref[...] = w_ref[...].astype(BF16)

    acc = _dot(x_refs[0][...], wb_refs[0][...])
    for x_ref, wb_ref in zip(x_refs[1:], wb_refs[1:]):
        acc = acc + _dot(x_ref[...], wb_ref[...])
    o_ref[...] = acc


def _mm(xs, w, *, ncols, tm, tn, x_off=0):
    nx = len(xs)
    k = xs[0].shape[1]
    m = xs[0].shape[0] - x_off * tm
    in_specs = [pl.BlockSpec((tm, k), lambda j, i: (i + x_off, 0)) for _ in xs]
    in_specs += [pl.BlockSpec((k, tn), functools.partial(lambda j, i, kk: (kk, j), kk=kk)) for kk in range(nx)]
    return pl.pallas_call(
        functools.partial(_mm_kernel, nx=nx),
        grid=(ncols // tn, m // tm),
        in_specs=in_specs,
        out_specs=pl.BlockSpec((tm, tn), lambda j, i: (i, j)),
        out_shape=jax.ShapeDtypeStruct((m, ncols), F32),
        scratch_shapes=[pltpu.VMEM((k, tn), BF16) for _ in xs],
        compiler_params=_params("arbitrary", "arbitrary"),
        name="proj",
    )(*xs, *([w] * nx))


def _conv_block(cur, prev8, next8, w_ref, b_ref, i, nblk):
    ctx_edge = T_CTX // ROWS
    keep_prev = jnp.logical_and(i != 0, i != ctx_edge)
    keep_next = jnp.logical_and(i != nblk - 1, i != ctx_edge - 1)
    prev8 = jnp.where(keep_prev, prev8, 0.0)
    next8 = jnp.where(keep_next, next8, 0.0)
    ext = jnp.concatenate([prev8, cur, next8], axis=0)
    base = SUBLANES - CONV_W // 2
    y = b_ref[...] + w_ref[0:1, :] * ext[base:base + ROWS]
    for j in range(1, CONV_W):
        y = y + w_ref[j:j + 1, :] * ext[base + j:base + j + ROWS]
    return y


def _halo_specs(width, col_blk, nblk):
    per = ROWS // SUBLANES
    last8 = N_ALL // SUBLANES - 1
    return [
        pl.BlockSpec((ROWS, width), lambda i, *a: (i, col_blk(*a))),
        pl.BlockSpec((SUBLANES, width), lambda i, *a: (jnp.maximum(i * per - 1, 0), col_blk(*a))),
        pl.BlockSpec((SUBLANES, width), lambda i, *a: (jnp.minimum((i + 1) * per, last8), col_blk(*a))),
    ]


def _scan_tiles(a_ref, b_ref, h_ref, carry_ref, reverse):
    ntile = ROWS // SUBLANES
    row = lax.broadcasted_iota(jnp.int32, (SUBLANES, LRU_W), 0)

    def body(t, carry):
        tt = (ntile - 1 - t) if reverse else t
        rows = pl.ds(pl.multiple_of(tt * SUBLANES, SUBLANES), SUBLANES)
        a = a_ref[rows, :]
        b = b_ref[rows, :]
        for d in (1, 2, 4):
            shift = (SUBLANES - d) if reverse else d
            a_sh = pltpu.roll(a, shift, axis=0)
            b_sh = pltpu.roll(b, shift, axis=0)
            m = (row < SUBLANES - d) if reverse else (row >= d)
            b = jnp.where(m, a * b_sh, 0.0) + b
            a = jnp.where(m, a * a_sh, a)
        h = a * carry + b
        h_ref[rows, :] = h
        edge = h[0:1, :] if reverse else h[SUBLANES - 1:SUBLANES, :]
        return jnp.broadcast_to(edge, (SUBLANES, LRU_W))

    carry_ref[...] = lax.fori_loop(0, ntile, body, carry_ref[...])


def _lru_fwd_kernel(xa_ref, prev_ref, next_ref, cw_ref, cb_ref, wbd_ref, ba_ref, bx_ref, lam_ref,
                    hf_ref, a1_ref, b1_ref, carry_ref, a0_ref, b0_ref):
    i = pl.program_id(0)
    nblk = pl.num_programs(0)

    @pl.when(i == 0)
    def _():
        carry_ref[...] = jnp.zeros_like(carry_ref)

    xa = _conv_block(xa_ref[...], prev_ref[...], next_ref[...], cw_ref, cb_ref, i, nblk)
    xab = xa.astype(BF16)
    nsp = -LRU_C * _softplus(-lam_ref[...])
    gw = wbd_ref.shape[1]
    for g in range(LRU_W // gw):
        cs = slice(g * gw, (g + 1) * gw)
        z = _dot(xab[:, cs], wbd_ref[g])
        for d in range(2):
            r = jax.nn.sigmoid(z[:, d * gw:(d + 1) * gw] + ba_ref[d:d + 1, cs])
            ig = jax.nn.sigmoid(z[:, (2 + d) * gw:(3 + d) * gw] + bx_ref[d:d + 1, cs])
            log_a = r * nsp[d:d + 1, cs]
            a = jnp.exp(log_a)
            b = jnp.sqrt(-jnp.tanh(log_a) * (a * a + 1.0)) * ig * xa[:, cs]
            if d == 0:
                a0_ref[:, cs] = a
                b0_ref[:, cs] = b
            else:
                a1_ref[:, cs] = a
                b1_ref[:, cs] = b
    _scan_tiles(a0_ref, b0_ref, hf_ref, carry_ref, reverse=False)


def _gelu_tanh(v):
    return 0.5 * v * (1.0 + jnp.tanh(math.sqrt(2.0 / math.pi) * (v + 0.044715 * (v * v * v))))


def _lru_bwd_kernel(a1_ref, b1_ref, hf_ref, ga_ref, ya_ref, carry_ref, hb_ref):
    @pl.when(pl.program_id(0) == 0)
    def _():
        carry_ref[...] = jnp.zeros_like(carry_ref)

    _scan_tiles(a1_ref, b1_ref, hb_ref, carry_ref, reverse=True)
    ya_ref[...] = ((hf_ref[...] + hb_ref[...]) * _gelu_tanh(ga_ref[...])).astype(BF16)


def _rev_block(i, nblk):
    nctx = T_CTX // ROWS
    return jnp.where(i < nctx, nctx - 1 - i, nblk - 1 - (i - nctx))


def _lru(p0, conv_w, conv_b, wbd, ba, bx, lam):
    nblk = N_ALL // ROWS
    full = lambda shape: pl.BlockSpec(shape, lambda i: (0,) * len(shape))
    hf, a1, b1 = pl.pallas_call(
        _lru_fwd_kernel,
        grid=(nblk,),
        in_specs=_halo_specs(LRU_W, lambda: 0, nblk) + [
            full((CONV_W, LRU_W)), full((1, LRU_W)), full(wbd.shape),
            full((2, LRU_W)), full((2, LRU_W)), full((2, LRU_W)),
        ],
        out_specs=[pl.BlockSpec((ROWS, LRU_W), lambda i: (i, 0))] * 3,
        out_shape=[jax.ShapeDtypeStruct((N_ALL, LRU_W), F32)] * 3,
        scratch_shapes=[pltpu.VMEM((SUBLANES, LRU_W), F32), pltpu.VMEM((ROWS, LRU_W), F32),
                        pltpu.VMEM((ROWS, LRU_W), F32)],
        compiler_params=_params("arbitrary"),
        name="lru_fwd",
    )(p0, p0, p0, conv_w, conv_b, wbd, ba, bx, lam)
    rev = lambda i: (_rev_block(i, nblk), 0)
    ya = pl.pallas_call(
        _lru_bwd_kernel,
        grid=(nblk,),
        in_specs=[pl.BlockSpec((ROWS, LRU_W), rev)] * 3
        + [pl.BlockSpec((ROWS, LRU_W), lambda i: (_rev_block(i, nblk), 1))],
        out_specs=pl.BlockSpec((ROWS, LRU_W), rev),
        out_shape=jax.ShapeDtypeStruct((N_ALL, LRU_W), BF16),
        scratch_shapes=[pltpu.VMEM((SUBLANES, LRU_W), F32), pltpu.VMEM((ROWS, LRU_W), F32)],
        compiler_params=_params("arbitrary"),
        name="lru_bwd",
    )(a1, b1, hf, p0)
    return ya


def _tri(reverse):
    r = lax.broadcasted_iota(jnp.int32, (CHUNK, CHUNK), 0)
    c = lax.broadcasted_iota(jnp.int32, (CHUNK, CHUNK), 1)
    return (c >= r) if reverse else (c <= r)


def _gla_kernel(*refs, reverse):
    if reverse:
        q_ref, k_ref, v_ref, ad_ref, wg_ref, bg_ref, of_ref, og_ref, gn_ref, o_ref, s_ref = refs
    else:
        q_ref, k_ref, v_ref, ad_ref, wg_ref, bg_ref, o_ref, s_ref = refs
    d = 1 if reverse else 0

    @pl.when(pl.program_id(0) == 0)
    def _():
        s_ref[...] = jnp.zeros_like(s_ref)

    causal = _tri(reverse)
    tri_f = causal.astype(F32)
    lane = lax.broadcasted_iota(jnp.int32, (1, 2 * GLA_DK), 1)
    eye = (lax.broadcasted_iota(jnp.int32, (2 * GLA_DK, 2 * GLA_DK), 0)
           == lax.broadcasted_iota(jnp.int32, (2 * GLA_DK, 2 * GLA_DK), 1)).astype(F32)
    nch = ROWS // CHUNK
    for cc in range(nch):
        c = (nch - 1 - cc) if reverse else cc
        rs = slice(c * CHUNK, (c + 1) * CHUNK)
        ad = ad_ref[rs, d * GLA_RANK:(d + 1) * GLA_RANK]
        z = jnp.dot(ad, wg_ref[...], precision=HIGHEST, preferred_element_type=F32) + bg_ref[...]
        lg = -_softplus(-z) * (1.0 / GLA_TAU)
        cum = jnp.dot(tri_f, lg, precision=HIGHEST, preferred_element_type=F32)
        last = cum[0:1, :] if reverse else cum[CHUNK - 1:CHUNK, :]
        q_dec = q_ref[rs, :] * (GLA_DK ** -0.5) * jnp.exp(cum)
        k = k_ref[rs, :]
        k_inv = (k * jnp.exp(-cum)).astype(BF16)
        k_end = (k * jnp.exp(last - cum)).astype(BF16)
        decay = jnp.exp(last)
        for p in range(GLA_H // 2):
            ls = slice(p * 2 * GLA_DK, (p + 1) * 2 * GLA_DK)
            qd = q_dec[:, ls]
            vp = v_ref[rs, p * 2 * GLA_DV:(p + 1) * 2 * GLA_DV].astype(BF16)
            sp = s_ref[p]
            spb = sp.astype(BF16)
            for hh in range(2):
                head = 2 * p + hh
                hs = slice(head * GLA_DV, (head + 1) * GLA_DV)
                in_head = jnp.logical_and(lane >= hh * GLA_DK, lane < (hh + 1) * GLA_DK)
                qm = jnp.where(in_head, qd, 0.0).astype(BF16)
                sc = jnp.where(causal, _dot_nt(qm, k_inv[:, ls]), 0.0)
                o = _dot(sc.astype(BF16), vp[:, hh * GLA_DV:(hh + 1) * GLA_DV])
                o = o + _dot(qm, spb[:, hh * GLA_DV:(hh + 1) * GLA_DV])
                if reverse:
                    tot = o + of_ref[rs, hs]
                    ms = jnp.mean(tot * tot, axis=-1, keepdims=True)
                    y = tot * lax.rsqrt(ms + EPS) * gn_ref[:, hs] * _silu(og_ref[rs, hs])
                    o_ref[rs, hs] = y.astype(BF16)
                else:
                    o_ref[rs, hs] = o
            upd = _dot_tn(k_end[:, ls], vp)
            dcol = jnp.sum(eye * decay[:, ls], axis=1, keepdims=True)
            s_ref[p] = dcol * sp + upd


def _gla(p0, pad, wg, bg, gnorm, of=None):
    reverse = of is not None
    nblk = N_ALL // ROWS
    blk = (lambda i: _rev_block(i, nblk)) if reverse else (lambda i: i)
    d = 1 if reverse else 0
    qk_blk = 2 * LRU_W // GLA_QK
    in_specs = [
        pl.BlockSpec((ROWS, GLA_QK), lambda i: (blk(i), qk_blk)),
        pl.BlockSpec((ROWS, GLA_QK), lambda i: (blk(i), qk_blk + 1)),
        pl.BlockSpec((ROWS, GLA_V), lambda i: (blk(i), 3)),
        pl.BlockSpec((ROWS, LANES), lambda i: (blk(i), 0)),
        pl.BlockSpec((None, GLA_RANK, GLA_QK), lambda i: (d, 0, 0)),
        pl.BlockSpec((None, 1, GLA_QK), lambda i: (d, 0, 0)),
    ]
    args = [p0, p0, p0, pad, wg, bg.reshape(2, 1, GLA_QK)]
    if reverse:
        in_specs += [
            pl.BlockSpec((ROWS, GLA_V), lambda i: (blk(i), 0)),
            pl.BlockSpec((ROWS, GLA_V), lambda i: (blk(i), 4)),
            pl.BlockSpec((1, GLA_V), lambda i: (0, 0)),
        ]
        args += [of, p0, gnorm]
    return pl.pallas_call(
        functools.partial(_gla_kernel, reverse=reverse),
        grid=(nblk,),
        in_specs=in_specs,
        out_specs=pl.BlockSpec((ROWS, GLA_V), lambda i: (blk(i), 0)),
        out_shape=jax.ShapeDtypeStruct((N_ALL, GLA_V), BF16 if reverse else F32),
        scratch_shapes=[pltpu.VMEM((GLA_H // 2, 2 * GLA_DK, 2 * GLA_DV), F32)],
        compiler_params=_params("arbitrary"),
        name="gla_bwd" if reverse else "gla_fwd",
    )(*args)


def _conv_silu_kernel(x_ref, prev_ref, next_ref, w_ref, b_ref, o_ref):
    i = pl.program_id(0)
    y = _conv_block(x_ref[...], prev_ref[...], next_ref[...], w_ref, b_ref, i, pl.num_programs(0))
    o_ref[...] = _silu(y)


def _conv_silu(p1, conv_w, conv_b):
    nblk = N_ALL // ROWS
    tc = 1024
    off = SSD_INNER // tc
    return pl.pallas_call(
        _conv_silu_kernel,
        grid=(nblk, SSD_XBC // tc),
        in_specs=_halo_specs(tc, lambda j: j + off, nblk) + [
            pl.BlockSpec((CONV_W, tc), lambda i, j: (0, j)),
            pl.BlockSpec((1, tc), lambda i, j: (0, j)),
        ],
        out_specs=pl.BlockSpec((ROWS, tc), lambda i, j: (i, j)),
        out_shape=jax.ShapeDtypeStruct((N_ALL, SSD_XBC), F32),
        compiler_params=_params("arbitrary", "arbitrary"),
        name="ssd_conv",
    )(p1, p1, p1, conv_w, conv_b)


def _ssd_kernel(*refs, reverse):
    if reverse:
        xs_ref, b_ref, c_ref, dt_ref, dtb_ref, alog_ref, acc_ref, y_ref, st_ref = refs
    else:
        xs_ref, b_ref, c_ref, dt_ref, dtb_ref, alog_ref, y_ref, st_ref = refs
    d = 1 if reverse else 0

    @pl.when(pl.program_id(0) == 0)
    def _():
        st_ref[...] = jnp.zeros_like(st_ref)

    causal = _tri(reverse)
    tri_f = causal.astype(F32)
    lane = lax.broadcasted_iota(jnp.int32, (1, 2 * SSD_P), 1)
    low = lane < SSD_P
    a_neg = -jnp.exp(alog_ref[...])
    nch = ROWS // CHUNK
    hg = SSD_H // SSD_G

    def chunk(cc, carry):
        c = (nch - 1 - cc) if reverse else cc
        rs = pl.ds(pl.multiple_of(c * CHUNK, CHUNK), CHUNK)
        dtv = _softplus(dt_ref[rs, d * SSD_H:(d + 1) * SSD_H] + dtb_ref[...])
        la = dtv * a_neg
        cum = jnp.dot(tri_f, la, precision=HIGHEST, preferred_element_type=F32)
        cum_t = cum.T
        dt_t = dtv.T
        last = cum[0:1, :] if reverse else cum[CHUNK - 1:CHUNK, :]
        w2 = dtv * jnp.exp(last - cum)
        ecum = jnp.exp(cum)
        dec = jnp.exp(last)
        for g in range(SSD_G):
            gs = slice(g * SSD_Z, (g + 1) * SSD_Z)
            bg = b_ref[rs, gs]
            cg = c_ref[rs, gs].astype(BF16)
            bgb = bg.astype(BF16)
            cb = _dot_nt(cg, bgb)
            for pr in range(hg // 2):
                h0 = g * hg + 2 * pr
                ps = slice(h0 * SSD_P, (h0 + 2) * SSD_P)
                xp = xs_ref[rs, ps]
                xpb = xp.astype(BF16)
                ys = []
                for hh in range(2):
                    h = h0 + hh
                    seg = jnp.exp(jnp.where(causal, cum[:, h:h + 1] - cum_t[h:h + 1, :], -jnp.inf))
                    m = cb * seg * dt_t[h:h + 1, :]
                    ys.append(_dot(m.astype(BF16), xpb))
                y = jnp.where(low, ys[0], ys[1])
                stp = st_ref[g, :, 2 * pr * SSD_P:(2 * pr + 2) * SSD_P]
                ec = jnp.where(low, ecum[:, h0:h0 + 1], ecum[:, h0 + 1:h0 + 2])
                y = y + _dot(cg, stp.astype(BF16)) * ec
                if reverse:
                    y = y + acc_ref[rs, ps]
                y_ref[rs, ps] = y
                w2p = jnp.where(low, w2[:, h0:h0 + 1], w2[:, h0 + 1:h0 + 2])
                upd = _dot_tn(bgb, (xp * w2p).astype(BF16))
                dcp = jnp.where(low, dec[:, h0:h0 + 1], dec[:, h0 + 1:h0 + 2])
                st_ref[g, :, 2 * pr * SSD_P:(2 * pr + 2) * SSD_P] = stp * dcp + upd
        return carry

    lax.fori_loop(0, nch, chunk, 0)


def _ssd(xbc, pdt, dt_bias, a_log, acc=None):
    reverse = acc is not None
    nblk = N_ALL // ROWS
    blk = (lambda i: _rev_block(i, nblk)) if reverse else (lambda i: i)
    d = 1 if reverse else 0
    gz = SSD_G * SSD_Z
    in_specs = [
        pl.BlockSpec((ROWS, SSD_INNER), lambda i: (blk(i), 0)),
        pl.BlockSpec((ROWS, gz), lambda i: (blk(i), SSD_INNER // gz)),
        pl.BlockSpec((ROWS, gz), lambda i: (blk(i), SSD_INNER // gz + 1)),
        pl.BlockSpec((ROWS, LANES), lambda i: (blk(i), 0)),
        pl.BlockSpec((None, 1, SSD_H), lambda i: (d, 0, 0)),
        pl.BlockSpec((None, 1, SSD_H), lambda i: (d, 0, 0)),
    ]
    args = [xbc, xbc, xbc, pdt, dt_bias.reshape(2, 1, SSD_H), a_log.reshape(2, 1, SSD_H)]
    if reverse:
        in_specs.append(pl.BlockSpec((ROWS, SSD_INNER), lambda i: (blk(i), 0)))
        args.append(acc)
    return pl.pallas_call(
        functools.partial(_ssd_kernel, reverse=reverse),
        grid=(nblk,),
        in_specs=in_specs,
        out_specs=pl.BlockSpec((ROWS, SSD_INNER), lambda i: (blk(i), 0)),
        out_shape=jax.ShapeDtypeStruct((N_ALL, SSD_INNER), F32),
        scratch_shapes=[pltpu.VMEM((SSD_G, SSD_Z, SSD_INNER // SSD_G), F32)],
        compiler_params=_params("arbitrary"),
        name="ssd_bwd" if reverse else "ssd_fwd",
    )(*args)


def _ssd_finish_kernel(y_ref, xs_ref, z_ref, d_ref, g_ref, o_ref):
    y = y_ref[...] + d_ref[...] * xs_ref[...]
    u = y * _silu(z_ref[...])
    ms = jnp.mean(u * u, axis=-1, keepdims=True)
    o_ref[...] = (u * lax.rsqrt(ms + EPS) * g_ref[...]).astype(BF16)


def _ssd_finish(ysum, xbc, p1, d_rep, g):
    nctx = T_CTX // ROWS
    lat = lambda i: (i + nctx, 0)
    return pl.pallas_call(
        _ssd_finish_kernel,
        grid=(T_LAT // ROWS,),
        in_specs=[
            pl.BlockSpec((ROWS, SSD_INNER), lat),
            pl.BlockSpec((ROWS, SSD_INNER), lat),
            pl.BlockSpec((ROWS, SSD_INNER), lat),
            pl.BlockSpec((1, SSD_INNER), lambda i: (0, 0)),
            pl.BlockSpec((1, SSD_INNER), lambda i: (0, 0)),
        ],
        out_specs=pl.BlockSpec((ROWS, SSD_INNER), lambda i: (i, 0)),
        out_shape=jax.ShapeDtypeStruct((T_LAT, SSD_INNER), BF16),
        compiler_params=_params("arbitrary"),
        name="ssd_finish",
    )(ysum, xbc, p1, d_rep, g)


def _new_expert(be_ref, m):
    return jnp.logical_or(m == 0, be_ref[m] != be_ref[jnp.maximum(m - 1, 0)])


def _ffn_up_kernel(be_ref, nu_ref, x_ref, wg_ref, wu_ref, h_ref, wgb_ref, wub_ref):
    m = pl.program_id(1)

    @pl.when(_new_expert(be_ref, m))
    def _():
        wgb_ref[...] = wg_ref[...].astype(BF16)
        wub_ref[...] = wu_ref[...].astype(BF16)

    @pl.when(m < nu_ref[0])
    def _():
        x = x_ref[...]
        h_ref[...] = (_silu(_dot(x, wgb_ref[...])) * _dot(x, wub_ref[...])).astype(BF16)


def _ffn_down_kernel(be_ref, nu_ref, h_ref, wd_ref, y_ref, wdb_ref):
    m = pl.program_id(1)

    @pl.when(_new_expert(be_ref, m))
    def _():
        wdb_ref[...] = wd_ref[...].astype(BF16)

    @pl.when(m < nu_ref[0])
    def _():
        y_ref[...] = _dot(h_ref[...], wdb_ref[...])


def _moe_ffn(xs, block_exp, n_used, w_gate, w_up, w_down):
    rows = xs.shape[0]
    nblk = rows // MOE_TM
    tn_up = 512
    tn_dn = 1024
    row_blk = lambda j, m, be, nu: (jnp.minimum(m, nu[0] - 1), 0)
    hb = pl.pallas_call(
        _ffn_up_kernel,
        grid_spec=pltpu.PrefetchScalarGridSpec(
            num_scalar_prefetch=2,
            grid=(D_EXP // tn_up, nblk),
            in_specs=[
                pl.BlockSpec((MOE_TM, D), row_blk),
                pl.BlockSpec((None, D, tn_up), lambda j, m, be, nu: (be[m], 0, j)),
                pl.BlockSpec((None, D, tn_up), lambda j, m, be, nu: (be[m], 0, j)),
            ],
            out_specs=pl.BlockSpec((MOE_TM, tn_up), lambda j, m, be, nu: (jnp.minimum(m, nu[0] - 1), j)),
            scratch_shapes=[pltpu.VMEM((D, tn_up), BF16), pltpu.VMEM((D, tn_up), BF16)],
        ),
        out_shape=jax.ShapeDtypeStruct((rows, D_EXP), BF16),
        compiler_params=_params("arbitrary", "arbitrary"),
        name="moe_up",
    )(block_exp, n_used, xs, w_gate, w_up)
    return pl.pallas_call(
        _ffn_down_kernel,
        grid_spec=pltpu.PrefetchScalarGridSpec(
            num_scalar_prefetch=2,
            grid=(D // tn_dn, nblk),
            in_specs=[
                pl.BlockSpec((MOE_TM, D_EXP), row_blk),
                pl.BlockSpec((None, D_EXP, tn_dn), lambda j, m, be, nu: (be[m], 0, j)),
            ],
            out_specs=pl.BlockSpec((MOE_TM, tn_dn), lambda j, m, be, nu: (jnp.minimum(m, nu[0] - 1), j)),
            scratch_shapes=[pltpu.VMEM((D_EXP, tn_dn), BF16)],
        ),
        out_shape=jax.ShapeDtypeStruct((rows, D), F32),
        compiler_params=_params("arbitrary", "arbitrary"),
        name="moe_down",
    )(block_exp, n_used, hb, w_down)


def _moe(h, eidx, wts, w_gate, w_up, w_down):
    n = h.shape[0]
    flat_e = eidx.reshape(-1)
    onehot = (flat_e[:, None] == jnp.arange(N_EXP, dtype=jnp.int32)[None, :]).astype(jnp.int32)
    csum = jnp.cumsum(onehot, axis=0)
    rank = jnp.take_along_axis(csum, flat_e[:, None], axis=1)[:, 0] - 1
    sizes = csum[-1]
    padded = (sizes + MOE_TM - 1) // MOE_TM * MOE_TM
    ends = jnp.cumsum(padded)
    dest = (ends - padded)[flat_e] + rank
    nblk = -(-2 * n // MOE_TM) + N_EXP
    n_used = (ends[-1] // MOE_TM).astype(jnp.int32)
    blk_start = jnp.arange(nblk, dtype=jnp.int32) * MOE_TM
    blk_start = jnp.minimum(blk_start, (n_used - 1) * MOE_TM)
    block_exp = jnp.minimum(jnp.searchsorted(ends, blk_start, side="right"), N_EXP - 1).astype(jnp.int32)
    tok = jnp.tile(jnp.arange(n, dtype=jnp.int32), 2)
    slot_tok = jnp.zeros((nblk * MOE_TM,), jnp.int32).at[dest].set(tok)
    xs = jnp.take(h, slot_tok, axis=0)
    yb = _moe_ffn(xs, block_exp, n_used.reshape(1), w_gate, w_up, w_down)
    y1 = jnp.take(yb, dest[:n], axis=0)
    y2 = jnp.take(yb, dest[n:], axis=0)
    return wts[0][:, None] * y1 + wts[1][:, None] * y2


def _pos_tables():
    quarter = D // 4
    omega = 1.0 / (10000.0 ** (jnp.arange(quarter, dtype=F32) / quarter))
    ang_r = jnp.arange(T_LAT // GRID_W, dtype=F32)[:, None] * omega
    ang_c = jnp.arange(GRID_W, dtype=F32)[:, None] * omega
    emb_r = jnp.concatenate([jnp.sin(ang_r), jnp.cos(ang_r)], axis=-1)
    emb_c = jnp.concatenate([jnp.sin(ang_c), jnp.cos(ang_c)], axis=-1)
    return emb_r, emb_c


def _block_diag_gates(wa, wx, group):
    nb = wa.shape[1]
    per = group // wa.shape[2]
    eye = jnp.eye(per, dtype=F32)

    def bd(w):
        w = w.reshape(nb // per, per, w.shape[1], w.shape[2])
        return jnp.einsum("gnkj,nm->gnkmj", w, eye).reshape(nb // per, group, group)

    return jnp.concatenate([bd(wa[0]), bd(wa[1]), bd(wx[0]), bd(wx[1])], axis=-1).astype(BF16)


def kernel(x, c, ctx, c_ctx, mod_w, mod_b, norm1_g, norm2_g, ev_w_in, ev_conv_w, ev_conv_b, lru_wa, lru_ba, lru_wx, lru_bx, lru_lambda, gla_wg_up, gla_bg, gla_norm_g, ev_w_out, od_w_in, od_conv_w, od_conv_b, ssd_a_log, ssd_dt_bias, ssd_d, ssd_norm_g, od_w_out, router_w, router_b, exp_w_gate, exp_w_up, exp_w_down, final_norm_g):
    mods = _mod_vectors(c, c_ctx, mod_w, mod_b)
    emb_r, emb_c = _pos_tables()
    rwt = router_w.T
    rb = router_b.reshape(N_EXP, 1)
    tm = N_ALL // 8

    x0, h0 = _prep0(x[0], ctx[0], emb_r, emb_c, mods[0], norm1_g[0:1])
    p0 = _mm([h0], ev_w_in[0], ncols=EVEN_MAIN, tm=tm, tn=1024)
    w_ad = jnp.pad(ev_w_in[0][:, EVEN_MAIN:], ((0, 0), (0, LANES - 2 * GLA_RANK)))
    pad = _mm([h0], w_ad, ncols=LANES, tm=tm, tn=LANES)
    wbd = _block_diag_gates(lru_wa[0], lru_wx[0], 256)
    ya = _lru(p0, ev_conv_w[0], ev_conv_b[0:1], wbd, lru_ba[0], lru_bx[0], lru_lambda[0])
    of = _gla(p0, pad, gla_wg_up[0], gla_bg[0], None)
    yb = _gla(p0, pad, gla_wg_up[0], gla_bg[0], gla_norm_g[0:1], of=of)
    y0 = _mm([ya, yb], ev_w_out[0], ncols=D, tm=tm, tn=1024)
    x1, h1, e0, w0 = _resid_norm(x0, y0, mods[0], mods[0], norm2_g[0:1], rwt, rb, gate_col=2, shift_col=3,
                                 scale_col=4, x_off=0, ctx_blocks=T_CTX // ROWS, route=True)
    f0 = _moe(h1, e0, w0, exp_w_gate[0], exp_w_up[0], exp_w_down[0])
    x2, h2 = _resid_norm(x1, f0, mods[0], mods[1], norm1_g[1:2], rwt, rb, gate_col=5, shift_col=0,
                         scale_col=1, x_off=0, ctx_blocks=T_CTX // ROWS, route=False)

    p1 = _mm([h2], od_w_in[0], ncols=ODD_MAIN, tm=tm, tn=1024)
    pdt = _mm([h2], od_w_in[0][:, ODD_MAIN:], ncols=LANES, tm=tm, tn=LANES)
    xbc = _conv_silu(p1, od_conv_w[0], od_conv_b[0:1])
    yf = _ssd(xbc, pdt, ssd_dt_bias[0], ssd_a_log[0])
    ysum = _ssd(xbc, pdt, ssd_dt_bias[0], ssd_a_log[0], acc=yf)
    d_rep = jnp.repeat(ssd_d[0], SSD_P).reshape(1, SSD_INNER)
    gy = _ssd_finish(ysum, xbc, p1, d_rep, ssd_norm_g[0:1])
    y1 = _mm([gy], od_w_out[0], ncols=D, tm=1024, tn=512)
    x3, h3, e1, w1 = _resid_norm(x2, y1, mods[1], mods[1], norm2_g[1:2], rwt, rb, gate_col=2, shift_col=3,
                                 scale_col=4, x_off=T_CTX // ROWS, ctx_blocks=0, route=True)
    f1 = _moe(h3, e1, w1, exp_w_gate[1], exp_w_up[1], exp_w_down[1])
    out = _final(x3, f1, mods[1], final_norm_g.reshape(1, D))
    return out[None]
```

```python
import functools
import math

import jax
import jax.numpy as jnp
from jax import lax
from jax.experimental import pallas as pl
from jax.experimental.pallas import tpu as pltpu

F32 = jnp.float32
BF16 = jnp.bfloat16
HIGHEST = lax.Precision.HIGHEST

D = 2048
T_LAT = 8192
T_CTX = 256
N_ALL = T_CTX + T_LAT
GRID_W = 64
EPS = 1e-6
CONV_W = 4
LRU_W = 1024
LRU_C = 8.0
GLA_H = 8
GLA_DK = 64
GLA_DV = 128
GLA_QK = GLA_H * GLA_DK
GLA_V = GLA_H * GLA_DV
GLA_RANK = 16
GLA_TAU = 16.0
CHUNK = 64
EVEN_MAIN = 2 * LRU_W + 2 * GLA_QK + 2 * GLA_V
SSD_INNER = 2 * D
SSD_P = 64
SSD_H = SSD_INNER // SSD_P
SSD_G = 8
SSD_Z = 128
SSD_XBC = SSD_INNER + 2 * SSD_G * SSD_Z
ODD_MAIN = SSD_INNER + SSD_XBC
N_EXP = 32
N_GRP = 4
GRP = N_EXP // N_GRP
D_EXP = 1024
MOE_TM = 256

ROWS = 256
LANES = 128
SUBLANES = 8
VMEM_LIMIT = 56 * 1024 * 1024


def _params(*sem):
    return pltpu.CompilerParams(dimension_semantics=sem, vmem_limit_bytes=VMEM_LIMIT)


def _silu(v):
    return v * jax.nn.sigmoid(v)


def _softplus(v):
    return jnp.maximum(v, 0.0) + jnp.log1p(jnp.exp(-jnp.abs(v)))


def _dot(a, b):
    return jnp.dot(a, b, preferred_element_type=F32)


def _dot_nt(a, b):
    return lax.dot_general(a, b, (((1,), (1,)), ((), ())), preferred_element_type=F32)


def _dot_tn(a, b):
    return lax.dot_general(a, b, (((0,), (0,)), ((), ())), preferred_element_type=F32)


def _mod_kernel(s_ref, w_ref, b_ref, o_ref):
    tn = w_ref.shape[1]
    nrep = tn // LANES

    def body(r, acc):
        a0, a1 = acc
        rows = pl.ds(pl.multiple_of(r * SUBLANES, SUBLANES), SUBLANES)
        w = w_ref[rows, :]
        s0 = _silu(s_ref[0, rows, :])
        s1 = _silu(s_ref[1, rows, :])
        a0 = a0 + w * jnp.concatenate([s0] * nrep, axis=1)
        a1 = a1 + w * jnp.concatenate([s1] * nrep, axis=1)
        return a0, a1

    zero = jnp.zeros((SUBLANES, tn), F32)
    a0, a1 = lax.fori_loop(0, D // SUBLANES, body, (zero, zero), unroll=4)
    o_ref[0:1, :] = jnp.sum(a0, axis=0, keepdims=True) + b_ref[...]
    o_ref[1:2, :] = jnp.sum(a1, axis=0, keepdims=True) + b_ref[...]


def _mod_vectors(c, c_ctx, mod_w, mod_b):
    depth = mod_w.shape[0]
    tn = 1024
    s = jnp.broadcast_to(jnp.stack([c[0], c_ctx])[:, :, None], (2, D, LANES))
    return pl.pallas_call(
        _mod_kernel,
        grid=(depth, 6 * D // tn),
        in_specs=[
            pl.BlockSpec((2, D, LANES), lambda l, j: (0, 0, 0)),
            pl.BlockSpec((None, D, tn), lambda l, j: (l, 0, j)),
            pl.BlockSpec((None, 1, tn), lambda l, j: (l, 0, j)),
        ],
        out_specs=pl.BlockSpec((None, 2, tn), lambda l, j: (l, 0, j)),
        out_shape=jax.ShapeDtypeStruct((depth, 2, 6 * D), F32),
        compiler_params=_params("arbitrary", "arbitrary"),
        name="mod_vectors",
    )(s, mod_w, mod_b.reshape(depth, 1, 6 * D))


def _mod_row(mod_ref, kind, col):
    return mod_ref[pl.ds(kind, 1), col * D:(col + 1) * D]


def _ada_norm(xv, g, shift, scale):
    ms = jnp.mean(xv * xv, axis=-1, keepdims=True)
    return (xv * lax.rsqrt(ms + EPS) * g) * (1.0 + scale) + shift


def _prep0_kernel(x_ref, ctx_ref, er_ref, ec_ref, mod_ref, g_ref, xo_ref, ho_ref):
    i = pl.program_id(0)

    @pl.when(i == 0)
    def _():
        xo_ref[...] = ctx_ref[...]

    @pl.when(i > 0)
    def _():
        r0 = (i - 1) * (ROWS // GRID_W)
        for j in range(ROWS // GRID_W):
            rs = slice(j * GRID_W, (j + 1) * GRID_W)
            xo_ref[rs, 0:D // 2] = x_ref[rs, 0:D // 2] + er_ref[pl.ds(r0 + j, 1), :]
            xo_ref[rs, D // 2:D] = x_ref[rs, D // 2:D] + ec_ref[...]

    kind = jnp.where(i == 0, 1, 0)
    h = _ada_norm(xo_ref[...], g_ref[...], _mod_row(mod_ref, kind, 0), _mod_row(mod_ref, kind, 1))
    ho_ref[...] = h.astype(BF16)


def _prep0(x, ctx, emb_r, emb_c, mod0, g):
    nblk = N_ALL // ROWS
    return pl.pallas_call(
        _prep0_kernel,
        grid=(nblk,),
        in_specs=[
            pl.BlockSpec((ROWS, D), lambda i: (jnp.maximum(i - 1, 0), 0)),
            pl.BlockSpec((ROWS, D), lambda i: (0, 0)),
            pl.BlockSpec(emb_r.shape, lambda i: (0, 0)),
            pl.BlockSpec(emb_c.shape, lambda i: (0, 0)),
            pl.BlockSpec((2, 6 * D), lambda i: (0, 0)),
            pl.BlockSpec((1, D), lambda i: (0, 0)),
        ],
        out_specs=[pl.BlockSpec((ROWS, D), lambda i: (i, 0)), pl.BlockSpec((ROWS, D), lambda i: (i, 0))],
        out_shape=[jax.ShapeDtypeStruct((N_ALL, D), F32), jax.ShapeDtypeStruct((N_ALL, D), BF16)],
        compiler_params=_params("arbitrary"),
        name="embed_norm",
    )(x, ctx, emb_r, emb_c, mod0, g)


def _route(hf32, rwt_ref, rb_ref, eo_ref, wo_ref):
    logits = lax.dot_general(rwt_ref[...], hf32, (((1,), (1,)), ((), ())),
                             precision=HIGHEST, preferred_element_type=F32)
    s = jax.nn.sigmoid(logits)
    sel = s + rb_ref[...]
    row = lax.broadcasted_iota(jnp.int32, (GRP, ROWS), 0)
    neg = jnp.float32(-jnp.inf)
    gs, i1s, i2s = [], [], []
    for g in range(N_GRP):
        blk = sel[g * GRP:(g + 1) * GRP, :]
        m1 = jnp.max(blk, axis=0, keepdims=True)
        i1 = jnp.min(jnp.where(blk == m1, row, GRP), axis=0, keepdims=True)
        blk2 = jnp.where(row == i1, neg, blk)
        m2 = jnp.max(blk2, axis=0, keepdims=True)
        i2 = jnp.min(jnp.where(blk2 == m2, row, GRP), axis=0, keepdims=True)
        gs.append(m1 + m2)
        i1s.append(i1)
        i2s.append(i2)
    best, gi, i1, i2 = gs[0], jnp.zeros((1, ROWS), jnp.int32), i1s[0], i2s[0]
    for g in range(1, N_GRP):
        upd = gs[g] > best
        best = jnp.where(upd, gs[g], best)
        gi = jnp.where(upd, g, gi)
        i1 = jnp.where(upd, i1s[g], i1)
        i2 = jnp.where(upd, i2s[g], i2)
    e1 = gi * GRP + i1
    e2 = gi * GRP + i2
    erow = lax.broadcasted_iota(jnp.int32, (N_EXP, ROWS), 0)
    s1 = jnp.sum(jnp.where(erow == e1, s, 0.0), axis=0, keepdims=True)
    s2 = jnp.sum(jnp.where(erow == e2, s, 0.0), axis=0, keepdims=True)
    tot = s1 + s2
    eo_ref[0:1, :] = e1
    eo_ref[1:2, :] = e2
    wo_ref[0:1, :] = s1 / tot
    wo_ref[1:2, :] = s2 / tot


def _combine_wait(yb_hbm, ybuf, sem, slot):
    pltpu.make_async_copy(yb_hbm.at[pl.ds(0, 2 * ROWS), :], ybuf.at[slot], sem.at[slot]).wait()


def _combine_issue(dest_ref, yb_hbm, ybuf, sem, slot):
    def body(r, carry):
        for k in range(2):
            row = dest_ref[0, k, r]
            pltpu.make_async_copy(yb_hbm.at[pl.ds(row, 1), :], ybuf.at[slot, pl.ds(k * ROWS + r, 1), :],
                                  sem.at[slot]).start()
        return carry

    lax.fori_loop(0, ROWS, body, 0, unroll=8)


def _token_kernel(*refs, gate_col, shift_col, scale_col, ctx_blocks, route, combine, final):
    refs = list(refs)
    if combine:
        dcur_ref, dnxt_ref = refs.pop(0), refs.pop(0)
    x_ref, y_ref = refs.pop(0), refs.pop(0)
    if combine:
        wc_ref = refs.pop(0)
    modg_ref, modn_ref, g_ref, rwt_ref, rb_ref = (refs.pop(0) for _ in range(5))
    i = pl.program_id(0)
    if combine:
        ybuf, sem = refs[-2], refs[-1]
        slot = lax.rem(i, 2)

        @pl.when(i == 0)
        def _():
            _combine_issue(dcur_ref, y_ref, ybuf, sem, 0)

        @pl.when(i + 1 < pl.num_programs(0))
        def _():
            _combine_issue(dnxt_ref, y_ref, ybuf, sem, 1 - slot)

        _combine_wait(y_ref, ybuf, sem, slot)
        y = wc_ref[:, 0:1] * ybuf[slot, 0:ROWS, :] + wc_ref[:, 1:2] * ybuf[slot, ROWS:2 * ROWS, :]
    else:
        y = y_ref[...]
    kind = jnp.where(i < ctx_blocks, 1, 0)
    xn = x_ref[...] + _mod_row(modg_ref, kind, gate_col) * y
    if final:
        ms = jnp.mean(xn * xn, axis=-1, keepdims=True)
        refs[0][...] = xn * lax.rsqrt(ms + EPS) * g_ref[...]
        return
    xo_ref, ho_ref = refs[0], refs[1]
    xo_ref[...] = xn
    h = _ada_norm(xn, g_ref[...], _mod_row(modn_ref, kind, shift_col), _mod_row(modn_ref, kind, scale_col))
    ho_ref[...] = h.astype(ho_ref.dtype)
    if route:
        _route(h, rwt_ref, rb_ref, refs[2], refs[3])


def _token_stage(x, y, modg, modn, g, rwt, rb, *, gate_col, shift_col=0, scale_col=0, x_off=0, ctx_blocks=0,
                 route=False, combine=None, final=False):
    n = x.shape[0] - x_off * ROWS
    nblk = n // ROWS
    row = lambda i: (i, 0)
    const2 = lambda i: (0, 0)
    in_specs, args, scratch = [], [], []
    if combine is not None:
        dest, wc = combine
        dest3 = dest.reshape(2, nblk, ROWS).transpose(1, 0, 2)
        in_specs += [pl.BlockSpec((1, 2, ROWS), lambda i: (i, 0, 0), memory_space=pltpu.SMEM),
                     pl.BlockSpec((1, 2, ROWS), lambda i: (jnp.minimum(i + 1, nblk - 1), 0, 0),
                                  memory_space=pltpu.SMEM)]
        args += [dest3, dest3]
    in_specs.append(pl.BlockSpec((ROWS, D), lambda i: (i + x_off, 0)))
    args.append(x)
    if combine is not None:
        in_specs += [pl.BlockSpec(memory_space=pl.ANY), pl.BlockSpec((ROWS, 2), row)]
        args += [y, wc]
        scratch = [pltpu.VMEM((2, 2 * ROWS, D), F32), pltpu.SemaphoreType.DMA((2,))]
    else:
        in_specs.append(pl.BlockSpec((ROWS, D), row))
        args.append(y)
    in_specs += [pl.BlockSpec((2, 6 * D), const2), pl.BlockSpec((2, 6 * D), const2), pl.BlockSpec((1, D), const2),
                 pl.BlockSpec((N_EXP, D), const2), pl.BlockSpec((N_EXP, 1), const2)]
    args += [modg, modn, g, rwt, rb]
    if final:
        out_specs = [pl.BlockSpec((ROWS, D), row)]
        out_shape = [jax.ShapeDtypeStruct((n, D), F32)]
    else:
        out_specs = [pl.BlockSpec((ROWS, D), row), pl.BlockSpec((ROWS, D), row)]
        out_shape = [jax.ShapeDtypeStruct((n, D), F32), jax.ShapeDtypeStruct((n, D), F32 if route else BF16)]
        if route:
            out_specs += [pl.BlockSpec((2, ROWS), lambda i: (0, i)), pl.BlockSpec((2, ROWS), lambda i: (0, i))]
            out_shape += [jax.ShapeDtypeStruct((2, n), jnp.int32), jax.ShapeDtypeStruct((2, n), F32)]
    kern = functools.partial(_token_kernel, gate_col=gate_col, shift_col=shift_col, scale_col=scale_col,
                             ctx_blocks=ctx_blocks, route=route, combine=combine is not None, final=final)
    return pl.pallas_call(
        kern,
        grid=(nblk,),
        in_specs=in_specs,
        out_specs=out_specs,
        out_shape=out_shape,
        scratch_shapes=scratch,
        compiler_params=_params("arbitrary"),
        name="token_stage",
    )(*args)


def _mm_kernel(*refs, nx):
    x_refs, w_refs, o_ref, wb_refs = refs[:nx], refs[nx:2 * nx], refs[2 * nx], refs[2 * nx + 1:]

    @pl.when(pl.program_id(1) == 0)
    def _():
        for w_ref, wb_ref in zip(w_refs, wb_refs):
            wb_ref[...] = w_ref[...].astype(BF16)

    acc = _dot(x_refs[0][...], wb_refs[0][...])
    for x_ref, wb_ref in zip(x_refs[1:], wb_refs[1:]):
        acc = acc + _dot(x_ref[...], wb_ref[...])
    o_ref[...] = acc


def _mm(xs, w, *, ncols, tm, tn, x_off=0):
    nx = len(xs)
    k = xs[0].shape[1]
    m = xs[0].shape[0] - x_off * tm
    in_specs = [pl.BlockSpec((tm, k), lambda j, i: (i + x_off, 0)) for _ in xs]
    in_specs += [pl.BlockSpec((k, tn), functools.partial(lambda j, i, kk: (kk, j), kk=kk)) for kk in range(nx)]
    return pl.pallas_call(
        functools.partial(_mm_kernel, nx=nx),
        grid=(ncols // tn, m // tm),
        in_specs=in_specs,
        out_specs=pl.BlockSpec((tm, tn), lambda j, i: (i, j)),
        out_shape=jax.ShapeDtypeStruct((m, ncols), F32),
        scratch_shapes=[pltpu.VMEM((k, tn), BF16) for _ in xs],
        compiler_params=_params("arbitrary", "arbitrary"),
        name="proj",
    )(*xs, *([w] * nx))


def _conv_block(cur, prev8, next8, w_ref, b_ref, i, nblk):
    ctx_edge = T_CTX // ROWS
    keep_prev = jnp.logical_and(i != 0, i != ctx_edge)
    keep_next = jnp.logical_and(i != nblk - 1, i != ctx_edge - 1)
    prev8 = jnp.where(keep_prev, prev8, 0.0)
    next8 = jnp.where(keep_next, next8, 0.0)
    ext = jnp.concatenate([prev8, cur, next8], axis=0)
    base = SUBLANES - CONV_W // 2
    y = b_ref[...] + w_ref[0:1, :] * ext[base:base + ROWS]
    for j in range(1, CONV_W):
        y = y + w_ref[j:j + 1, :] * ext[base + j:base + j + ROWS]
    return y


def _halo_specs(width, col_blk, nblk):
    per = ROWS // SUBLANES
    last8 = N_ALL // SUBLANES - 1
    return [
        pl.BlockSpec((ROWS, width), lambda i, *a: (i, col_blk(*a))),
        pl.BlockSpec((SUBLANES, width), lambda i, *a: (jnp.maximum(i * per - 1, 0), col_blk(*a))),
        pl.BlockSpec((SUBLANES, width), lambda i, *a: (jnp.minimum((i + 1) * per, last8), col_blk(*a))),
    ]


def _scan_tiles(a_ref, b_ref, h_ref, carry_ref, reverse):
    ntile = ROWS // SUBLANES
    row = lax.broadcasted_iota(jnp.int32, (SUBLANES, LRU_W), 0)

    def body(t, carry):
        tt = (ntile - 1 - t) if reverse else t
        rows = pl.ds(pl.multiple_of(tt * SUBLANES, SUBLANES), SUBLANES)
        a = a_ref[rows, :]
        b = b_ref[rows, :]
        for d in (1, 2, 4):
            shift = (SUBLANES - d) if reverse else d
            a_sh = pltpu.roll(a, shift, axis=0)
            b_sh = pltpu.roll(b, shift, axis=0)
            m = (row < SUBLANES - d) if reverse else (row >= d)
            b = jnp.where(m, a * b_sh, 0.0) + b
            a = jnp.where(m, a * a_sh, a)
        h = a * carry + b
        h_ref[rows, :] = h
        edge = h[0:1, :] if reverse else h[SUBLANES - 1:SUBLANES, :]
        return jnp.broadcast_to(edge, (SUBLANES, LRU_W))

    carry_ref[...] = lax.fori_loop(0, ntile, body, carry_ref[...])


def _lru_fwd_kernel(xa_ref, prev_ref, next_ref, cw_ref, cb_ref, wbd_ref, ba_ref, bx_ref, lam_ref,
                    hf_ref, a1_ref, b1_ref, carry_ref, a0_ref, b0_ref):
    i = pl.program_id(0)
    nblk = pl.num_programs(0)

    @pl.when(i == 0)
    def _():
        carry_ref[...] = jnp.zeros_like(carry_ref)

    xa = _conv_block(xa_ref[...], prev_ref[...], next_ref[...], cw_ref, cb_ref, i, nblk)
    xab = xa.astype(BF16)
    nsp = -LRU_C * _softplus(-lam_ref[...])
    gw = wbd_ref.shape[1]
    for g in range(LRU_W // gw):
        cs = slice(g * gw, (g + 1) * gw)
        z = _dot(xab[:, cs], wbd_ref[g])
        for d in range(2):
            r = jax.nn.sigmoid(z[:, d * gw:(d + 1) * gw] + ba_ref[d:d + 1, cs])
            ig = jax.nn.sigmoid(z[:, (2 + d) * gw:(3 + d) * gw] + bx_ref[d:d + 1, cs])
            log_a = r * nsp[d:d + 1, cs]
            a = jnp.exp(log_a)
            b = jnp.sqrt(-jnp.tanh(log_a) * (a * a + 1.0)) * ig * xa[:, cs]
            if d == 0:
                a0_ref[:, cs] = a
                b0_ref[:, cs] = b
            else:
                a1_ref[:, cs] = a
                b1_ref[:, cs] = b
    _scan_tiles(a0_ref, b0_ref, hf_ref, carry_ref, reverse=False)


def _gelu_tanh(v):
    return 0.5 * v * (1.0 + jnp.tanh(math.sqrt(2.0 / math.pi) * (v + 0.044715 * (v * v * v))))


def _lru_bwd_kernel(a1_ref, b1_ref, hf_ref, ga_ref, ya_ref, carry_ref, hb_ref):
    @pl.when(pl.program_id(0) == 0)
    def _():
        carry_ref[...] = jnp.zeros_like(carry_ref)

    _scan_tiles(a1_ref, b1_ref, hb_ref, carry_ref, reverse=True)
    ya_ref[...] = ((hf_ref[...] + hb_ref[...]) * _gelu_tanh(ga_ref[...])).astype(BF16)


def _rev_block(i, nblk):
    nctx = T_CTX // ROWS
    return jnp.where(i < nctx, nctx - 1 - i, nblk - 1 - (i - nctx))


def _lru(p0, conv_w, conv_b, wbd, ba, bx, lam):
    nblk = N_ALL // ROWS
    full = lambda shape: pl.BlockSpec(shape, lambda i: (0,) * len(shape))
    hf, a1, b1 = pl.pallas_call(
        _lru_fwd_kernel,
        grid=(nblk,),
        in_specs=_halo_specs(LRU_W, lambda: 0, nblk) + [
            full((CONV_W, LRU_W)), full((1, LRU_W)), full(wbd.shape),
            full((2, LRU_W)), full((2, LRU_W)), full((2, LRU_W)),
        ],
        out_specs=[pl.BlockSpec((ROWS, LRU_W), lambda i: (i, 0))] * 3,
        out_shape=[jax.ShapeDtypeStruct((N_ALL, LRU_W), F32)] * 3,
        scratch_shapes=[pltpu.VMEM((SUBLANES, LRU_W), F32), pltpu.VMEM((ROWS, LRU_W), F32),
                        pltpu.VMEM((ROWS, LRU_W), F32)],
        compiler_params=_params("arbitrary"),
        name="lru_fwd",
    )(p0, p0, p0, conv_w, conv_b, wbd, ba, bx, lam)
    rev = lambda i: (_rev_block(i, nblk), 0)
    ya = pl.pallas_call(
        _lru_bwd_kernel,
        grid=(nblk,),
        in_specs=[pl.BlockSpec((ROWS, LRU_W), rev)] * 3
        + [pl.BlockSpec((ROWS, LRU_W), lambda i: (_rev_block(i, nblk), 1))],
        out_specs=pl.BlockSpec((ROWS, LRU_W), rev),
        out_shape=jax.ShapeDtypeStruct((N_ALL, LRU_W), BF16),
        scratch_shapes=[pltpu.VMEM((SUBLANES, LRU_W), F32), pltpu.VMEM((ROWS, LRU_W), F32)],
        compiler_params=_params("arbitrary"),
        name="lru_bwd",
    )(a1, b1, hf, p0)
    return ya


def _tri(reverse):
    r = lax.broadcasted_iota(jnp.int32, (CHUNK, CHUNK), 0)
    c = lax.broadcasted_iota(jnp.int32, (CHUNK, CHUNK), 1)
    return (c >= r) if reverse else (c <= r)


def _gla_kernel(*refs, reverse):
    if reverse:
        q_ref, k_ref, v_ref, ad_ref, wg_ref, bg_ref, of_ref, og_ref, gn_ref, o_ref, s_ref = refs
    else:
        q_ref, k_ref, v_ref, ad_ref, wg_ref, bg_ref, o_ref, s_ref = refs
    d = 1 if reverse else 0

    @pl.when(pl.program_id(0) == 0)
    def _():
        s_ref[...] = jnp.zeros_like(s_ref)

    causal = _tri(reverse)
    tri_f = causal.astype(F32)
    lane = lax.broadcasted_iota(jnp.int32, (1, 2 * GLA_DK), 1)
    eye = (lax.broadcasted_iota(jnp.int32, (2 * GLA_DK, 2 * GLA_DK), 0)
           == lax.broadcasted_iota(jnp.int32, (2 * GLA_DK, 2 * GLA_DK), 1)).astype(F32)
    nch = ROWS // CHUNK
    for cc in range(nch):
        c = (nch - 1 - cc) if reverse else cc
        rs = slice(c * CHUNK, (c + 1) * CHUNK)
        ad = ad_ref[rs, d * GLA_RANK:(d + 1) * GLA_RANK]
        z = jnp.dot(ad, wg_ref[...], precision=HIGHEST, preferred_element_type=F32) + bg_ref[...]
        lg = -_softplus(-z) * (1.0 / GLA_TAU)
        cum = jnp.dot(tri_f, lg, precision=HIGHEST, preferred_element_type=F32)
        last = cum[0:1, :] if reverse else cum[CHUNK - 1:CHUNK, :]
        q_dec = q_ref[rs, :] * (GLA_DK ** -0.5) * jnp.exp(cum)
        k = k_ref[rs, :]
        k_inv = (k * jnp.exp(-cum)).astype(BF16)
        k_end = (k * jnp.exp(last - cum)).astype(BF16)
        decay = jnp.exp(last)
        for p in range(GLA_H // 2):
            ls = slice(p * 2 * GLA_DK, (p + 1) * 2 * GLA_DK)
            qd = q_dec[:, ls]
            vp = v_ref[rs, p * 2 * GLA_DV:(p + 1) * 2 * GLA_DV].astype(BF16)
            sp = s_ref[p]
            spb = sp.astype(BF16)
            for hh in range(2):
                head = 2 * p + hh
                hs = slice(head * GLA_DV, (head + 1) * GLA_DV)
                in_head = jnp.logical_and(lane >= hh * GLA_DK, lane < (hh + 1) * GLA_DK)
                qm = jnp.where(in_head, qd, 0.0).astype(BF16)
                sc = jnp.where(causal, _dot_nt(qm, k_inv[:, ls]), 0.0)
                o = _dot(sc.astype(BF16), vp[:, hh * GLA_DV:(hh + 1) * GLA_DV])
                o = o + _dot(qm, spb[:, hh * GLA_DV:(hh + 1) * GLA_DV])
                if reverse:
                    tot = o + of_ref[rs, hs]
                    ms = jnp.mean(tot * tot, axis=-1, keepdims=True)
                    y = tot * lax.rsqrt(ms + EPS) * gn_ref[:, hs] * _silu(og_ref[rs, hs])
                    o_ref[rs, hs] = y.astype(BF16)
                else:
                    o_ref[rs, hs] = o
            upd = _dot_tn(k_end[:, ls], vp)
            dcol = jnp.sum(eye * decay[:, ls], axis=1, keepdims=True)
            s_ref[p] = dcol * sp + upd


def _gla(p0, pad, wg, bg, gnorm, of=None):
    reverse = of is not None
    nblk = N_ALL // ROWS
    blk = (lambda i: _rev_block(i, nblk)) if reverse else (lambda i: i)
    d = 1 if reverse else 0
    qk_blk = 2 * LRU_W // GLA_QK
    in_specs = [
        pl.BlockSpec((ROWS, GLA_QK), lambda i: (blk(i), qk_blk)),
        pl.BlockSpec((ROWS, GLA_QK), lambda i: (blk(i), qk_blk + 1)),
        pl.BlockSpec((ROWS, GLA_V), lambda i: (blk(i), 3)),
        pl.BlockSpec((ROWS, LANES), lambda i: (blk(i), 0)),
        pl.BlockSpec((None, GLA_RANK, GLA_QK), lambda i: (d, 0, 0)),
        pl.BlockSpec((None, 1, GLA_QK), lambda i: (d, 0, 0)),
    ]
    args = [p0, p0, p0, pad, wg, bg.reshape(2, 1, GLA_QK)]
    if reverse:
        in_specs += [
            pl.BlockSpec((ROWS, GLA_V), lambda i: (blk(i), 0)),
            pl.BlockSpec((ROWS, GLA_V), lambda i: (blk(i), 4)),
            pl.BlockSpec((1, GLA_V), lambda i: (0, 0)),
        ]
        args += [of, p0, gnorm]
    return pl.pallas_call(
        functools.partial(_gla_kernel, reverse=reverse),
        grid=(nblk,),
        in_specs=in_specs,
        out_specs=pl.BlockSpec((ROWS, GLA_V), lambda i: (blk(i), 0)),
        out_shape=jax.ShapeDtypeStruct((N_ALL, GLA_V), BF16 if reverse else F32),
        scratch_shapes=[pltpu.VMEM((GLA_H // 2, 2 * GLA_DK, 2 * GLA_DV), F32)],
        compiler_params=_params("arbitrary"),
        name="gla_bwd" if reverse else "gla_fwd",
    )(*args)


def _conv_silu_kernel(x_ref, prev_ref, next_ref, w_ref, b_ref, o_ref):
    i = pl.program_id(0)
    y = _conv_block(x_ref[...], prev_ref[...], next_ref[...], w_ref, b_ref, i, pl.num_programs(0))
    o_ref[...] = _silu(y)


def _conv_silu(p1, conv_w, conv_b):
    nblk = N_ALL // ROWS
    tc = 1024
    off = SSD_INNER // tc
    return pl.pallas_call(
        _conv_silu_kernel,
        grid=(nblk, SSD_XBC // tc),
        in_specs=_halo_specs(tc, lambda j: j + off, nblk) + [
            pl.BlockSpec((CONV_W, tc), lambda i, j: (0, j)),
            pl.BlockSpec((1, tc), lambda i, j: (0, j)),
        ],
        out_specs=pl.BlockSpec((ROWS, tc), lambda i, j: (i, j)),
        out_shape=jax.ShapeDtypeStruct((N_ALL, SSD_XBC), F32),
        compiler_params=_params("arbitrary", "arbitrary"),
        name="ssd_conv",
    )(p1, p1, p1, conv_w, conv_b)


def _ssd_kernel(*refs, reverse):
    if reverse:
        xs_ref, b_ref, c_ref, dt_ref, dtb_ref, alog_ref, acc_ref, y_ref, st_ref = refs
    else:
        xs_ref, b_ref, c_ref, dt_ref, dtb_ref, alog_ref, y_ref, st_ref = refs
    d = 1 if reverse else 0

    @pl.when(pl.program_id(0) == 0)
    def _():
        st_ref[...] = jnp.zeros_like(st_ref)

    causal = _tri(reverse)
    tri_f = causal.astype(F32)
    lane = lax.broadcasted_iota(jnp.int32, (1, 2 * SSD_P), 1)
    low = lane < SSD_P
    a_neg = -jnp.exp(alog_ref[...])
    nch = ROWS // CHUNK
    hg = SSD_H // SSD_G

    def chunk(cc, carry):
        c = (nch - 1 - cc) if reverse else cc
        rs = pl.ds(pl.multiple_of(c * CHUNK, CHUNK), CHUNK)
        dtv = _softplus(dt_ref[rs, d * SSD_H:(d + 1) * SSD_H] + dtb_ref[...])
        la = dtv * a_neg
        cum = jnp.dot(tri_f, la, precision=HIGHEST, preferred_element_type=F32)
        cum_t = cum.T
        dt_t = dtv.T
        last = cum[0:1, :] if reverse else cum[CHUNK - 1:CHUNK, :]
        w2 = dtv * jnp.exp(last - cum)
        ecum = jnp.exp(cum)
        dec = jnp.exp(last)
        for g in range(SSD_G):
            gs = slice(g * SSD_Z, (g + 1) * SSD_Z)
            bg = b_ref[rs, gs]
            cg = c_ref[rs, gs].astype(BF16)
            bgb = bg.astype(BF16)
            cb = _dot_nt(cg, bgb)
            for pr in range(hg // 2):
                h0 = g * hg + 2 * pr
                ps = slice(h0 * SSD_P, (h0 + 2) * SSD_P)
                xp = xs_ref[rs, ps]
                xpb = xp.astype(BF16)
                ys = []
                for hh in range(2):
                    h = h0 + hh
                    seg = jnp.exp(jnp.where(causal, cum[:, h:h + 1] - cum_t[h:h + 1, :], -jnp.inf))
                    m = cb * seg * dt_t[h:h + 1, :]
                    ys.append(_dot(m.astype(BF16), xpb))
                y = jnp.where(low, ys[0], ys[1])
                stp = st_ref[g, :, 2 * pr * SSD_P:(2 * pr + 2) * SSD_P]
                ec = jnp.where(low, ecum[:, h0:h0 + 1], ecum[:, h0 + 1:h0 + 2])
                y = y + _dot(cg, stp.astype(BF16)) * ec
                if reverse:
                    y = y + acc_ref[rs, ps]
                y_ref[rs, ps] = y
                w2p = jnp.where(low, w2[:, h0:h0 + 1], w2[:, h0 + 1:h0 + 2])
                upd = _dot_tn(bgb, (xp * w2p).astype(BF16))
                dcp = jnp.where(low, dec[:, h0:h0 + 1], dec[:, h0 + 1:h0 + 2])
                st_ref[g, :, 2 * pr * SSD_P:(2 * pr + 2) * SSD_P] = stp * dcp + upd
        return carry

    lax.fori_loop(0, nch, chunk, 0)


def _ssd(xbc, pdt, dt_bias, a_log, acc=None):
    reverse = acc is not None
    nblk = N_ALL // ROWS
    blk = (lambda i: _rev_block(i, nblk)) if reverse else (lambda i: i)
    d = 1 if reverse else 0
    gz = SSD_G * SSD_Z
    in_specs = [
        pl.BlockSpec((ROWS, SSD_INNER), lambda i: (blk(i), 0)),
        pl.BlockSpec((ROWS, gz), lambda i: (blk(i), SSD_INNER // gz)),
        pl.BlockSpec((ROWS, gz), lambda i: (blk(i), SSD_INNER // gz + 1)),
        pl.BlockSpec((ROWS, LANES), lambda i: (blk(i), 0)),
        pl.BlockSpec((None, 1, SSD_H), lambda i: (d, 0, 0)),
        pl.BlockSpec((None, 1, SSD_H), lambda i: (d, 0, 0)),
    ]
    args = [xbc, xbc, xbc, pdt, dt_bias.reshape(2, 1, SSD_H), a_log.reshape(2, 1, SSD_H)]
    if reverse:
        in_specs.append(pl.BlockSpec((ROWS, SSD_INNER), lambda i: (blk(i), 0)))
        args.append(acc)
    return pl.pallas_call(
        functools.partial(_ssd_kernel, reverse=reverse),
        grid=(nblk,),
        in_specs=in_specs,
        out_specs=pl.BlockSpec((ROWS, SSD_INNER), lambda i: (blk(i), 0)),
        out_shape=jax.ShapeDtypeStruct((N_ALL, SSD_INNER), F32),
        scratch_shapes=[pltpu.VMEM((SSD_G, SSD_Z, SSD_INNER // SSD_G), F32)],
        compiler_params=_params("arbitrary"),
        name="ssd_bwd" if reverse else "ssd_fwd",
    )(*args)


def _ssd_finish_kernel(y_ref, xs_ref, z_ref, d_ref, g_ref, o_ref):
    y = y_ref[...] + d_ref[...] * xs_ref[...]
    u = y * _silu(z_ref[...])
    ms = jnp.mean(u * u, axis=-1, keepdims=True)
    o_ref[...] = (u * lax.rsqrt(ms + EPS) * g_ref[...]).astype(BF16)


def _ssd_finish(ysum, xbc, p1, d_rep, g):
    nctx = T_CTX // ROWS
    lat = lambda i: (i + nctx, 0)
    return pl.pallas_call(
        _ssd_finish_kernel,
        grid=(T_LAT // ROWS,),
        in_specs=[
            pl.BlockSpec((ROWS, SSD_INNER), lat),
            pl.BlockSpec((ROWS, SSD_INNER), lat),
            pl.BlockSpec((ROWS, SSD_INNER), lat),
            pl.BlockSpec((1, SSD_INNER), lambda i: (0, 0)),
            pl.BlockSpec((1, SSD_INNER), lambda i: (0, 0)),
        ],
        out_specs=pl.BlockSpec((ROWS, SSD_INNER), lambda i: (i, 0)),
        out_shape=jax.ShapeDtypeStruct((T_LAT, SSD_INNER), BF16),
        compiler_params=_params("arbitrary"),
        name="ssd_finish",
    )(ysum, xbc, p1, d_rep, g)


def _dispatch_wait(h_hbm, buf, sem, slot):
    pltpu.make_async_copy(h_hbm.at[pl.ds(0, MOE_TM), :], buf.at[slot], sem.at[slot]).wait()


def _dispatch_issue(idx_ref, h_hbm, buf, sem, slot):
    def body(r, carry):
        tok = idx_ref[0, 0, r]
        pltpu.make_async_copy(h_hbm.at[pl.ds(tok, 1), :], buf.at[slot, pl.ds(r, 1), :], sem.at[slot]).start()
        return carry

    lax.fori_loop(0, MOE_TM, body, 0, unroll=8)


def _dispatch_kernel(nu_ref, icur_ref, inxt_ref, h_hbm, o_ref, buf, sem):
    m = pl.program_id(0)
    nu = nu_ref[0]
    slot = lax.rem(m, 2)

    @pl.when(m == 0)
    def _():
        _dispatch_issue(icur_ref, h_hbm, buf, sem, 0)

    @pl.when(m + 1 < nu)
    def _():
        _dispatch_issue(inxt_ref, h_hbm, buf, sem, 1 - slot)

    @pl.when(m < nu)
    def _():
        _dispatch_wait(h_hbm, buf, sem, slot)
        o_ref[...] = buf[slot].astype(BF16)


def _dispatch(h, slot_tok, n_used):
    nblk = slot_tok.shape[0]
    used = lambda m, nu: jnp.minimum(m, nu[0] - 1)
    return pl.pallas_call(
        _dispatch_kernel,
        grid_spec=pltpu.PrefetchScalarGridSpec(
            num_scalar_prefetch=1,
            grid=(nblk,),
            in_specs=[
                pl.BlockSpec((1, 1, MOE_TM), lambda m, nu: (used(m, nu), 0, 0), memory_space=pltpu.SMEM),
                pl.BlockSpec((1, 1, MOE_TM), lambda m, nu: (used(m + 1, nu), 0, 0), memory_space=pltpu.SMEM),
                pl.BlockSpec(memory_space=pl.ANY),
            ],
            out_specs=pl.BlockSpec((MOE_TM, D), lambda m, nu: (used(m, nu), 0)),
            scratch_shapes=[pltpu.VMEM((2, MOE_TM, D), F32), pltpu.SemaphoreType.DMA((2,))],
        ),
        out_shape=jax.ShapeDtypeStruct((nblk * MOE_TM, D), BF16),
        compiler_params=_params("arbitrary"),
        name="moe_dispatch",
    )(n_used, slot_tok, slot_tok, h)


def _expert_rows_loop(bs_ref, bc_ref, src_hbm, dst_hbm, ibuf, obuf, isem, osem, col, compute):
    e = pl.program_id(1)
    b0 = bs_ref[e]
    nb = bc_ref[e]
    tn = obuf.shape[2]

    def rows(rb):
        return pl.ds(pl.multiple_of((b0 + rb) * MOE_TM, MOE_TM), MOE_TM)

    def in_copy(rb, slot):
        return pltpu.make_async_copy(src_hbm.at[rows(rb), :], ibuf.at[slot], isem.at[slot])

    def out_copy(rb, slot):
        return pltpu.make_async_copy(obuf.at[slot], dst_hbm.at[rows(rb), pl.ds(col, tn)], osem.at[slot])

    @pl.when(nb > 0)
    def _():
        in_copy(0, 0).start()

        def body(rb, carry):
            slot = lax.rem(rb, 2)

            @pl.when(rb + 1 < nb)
            def _():
                in_copy(rb + 1, 1 - slot).start()

            in_copy(rb, slot).wait()

            @pl.when(rb >= 2)
            def _():
                out_copy(rb - 2, slot).wait()

            obuf[slot] = compute(ibuf[slot]).astype(obuf.dtype)
            out_copy(rb, slot).start()
            return carry

        lax.fori_loop(0, nb, body, 0)

        @pl.when(nb >= 2)
        def _():
            out_copy(nb - 2, lax.rem(nb, 2)).wait()

        out_copy(nb - 1, lax.rem(nb - 1, 2)).wait()


def _ffn_up_kernel(bs_ref, bc_ref, xs_hbm, wg_ref, wu_ref, hb_hbm, wgb_ref, wub_ref, ibuf, obuf, isem, osem):
    wgb_ref[...] = wg_ref[...].astype(BF16)
    wub_ref[...] = wu_ref[...].astype(BF16)
    col = pl.multiple_of(pl.program_id(0) * obuf.shape[2], LANES)
    _expert_rows_loop(bs_ref, bc_ref, xs_hbm, hb_hbm, ibuf, obuf, isem, osem, col,
                      lambda x: _silu(_dot(x, wgb_ref[...])) * _dot(x, wub_ref[...]))


def _ffn_down_kernel(bs_ref, bc_ref, hb_hbm, wd_ref, yb_hbm, wdb_ref, ibuf, obuf, isem, osem):
    wdb_ref[...] = wd_ref[...].astype(BF16)
    col = pl.multiple_of(pl.program_id(0) * obuf.shape[2], LANES)
    _expert_rows_loop(bs_ref, bc_ref, hb_hbm, yb_hbm, ibuf, obuf, isem, osem, col,
                      lambda h: _dot(h, wdb_ref[...]))


def _moe_ffn(xs, bstart, bcount, w_gate, w_up, w_down, layer):
    rows = xs.shape[0]
    tn_up = 512
    tn_dn = 2048
    wspec = lambda k, tn: pl.BlockSpec((None, None, k, tn), lambda j, e, bs, bc: (layer, e, 0, j))
    anyspec = pl.BlockSpec(memory_space=pl.ANY)
    hb = pl.pallas_call(
        _ffn_up_kernel,
        grid_spec=pltpu.PrefetchScalarGridSpec(
            num_scalar_prefetch=2,
            grid=(D_EXP // tn_up, N_EXP),
            in_specs=[anyspec, wspec(D, tn_up), wspec(D, tn_up)],
            out_specs=anyspec,
            scratch_shapes=[pltpu.VMEM((D, tn_up), BF16), pltpu.VMEM((D, tn_up), BF16),
                            pltpu.VMEM((2, MOE_TM, D), BF16), pltpu.VMEM((2, MOE_TM, tn_up), BF16),
                            pltpu.SemaphoreType.DMA((2,)), pltpu.SemaphoreType.DMA((2,))],
        ),
        out_shape=jax.ShapeDtypeStruct((rows, D_EXP), BF16),
        compiler_params=_params("arbitrary", "arbitrary"),
        name="moe_up",
    )(bstart, bcount, xs, w_gate, w_up)
    return pl.pallas_call(
        _ffn_down_kernel,
        grid_spec=pltpu.PrefetchScalarGridSpec(
            num_scalar_prefetch=2,
            grid=(D // tn_dn, N_EXP),
            in_specs=[anyspec, wspec(D_EXP, tn_dn)],
            out_specs=anyspec,
            scratch_shapes=[pltpu.VMEM((D_EXP, tn_dn), BF16),
                            pltpu.VMEM((2, MOE_TM, D_EXP), BF16), pltpu.VMEM((2, MOE_TM, tn_dn), F32),
                            pltpu.SemaphoreType.DMA((2,)), pltpu.SemaphoreType.DMA((2,))],
        ),
        out_shape=jax.ShapeDtypeStruct((rows, D), F32),
        compiler_params=_params("arbitrary", "arbitrary"),
        name="moe_down",
    )(bstart, bcount, hb, w_down)


def _moe(h, eidx, wts, w_gate, w_up, w_down, layer):
    n = h.shape[0]
    flat_e = eidx.reshape(-1)
    onehot = (flat_e[:, None] == jnp.arange(N_EXP, dtype=jnp.int32)[None, :]).astype(jnp.int32)
    csum = jnp.cumsum(onehot, axis=0)
    bcount = (csum[-1] + MOE_TM - 1) // MOE_TM
    bend = jnp.cumsum(bcount)
    bstart = bend - bcount
    dest = jnp.sum(onehot * (csum - 1 + (bstart * MOE_TM)[None, :]), axis=1)
    nblk = -(-2 * n // MOE_TM) + N_EXP
    tok = jnp.tile(jnp.arange(n, dtype=jnp.int32), 2)
    slot_tok = jnp.zeros((nblk * MOE_TM,), jnp.int32).at[dest].set(tok)
    xs = _dispatch(h, slot_tok.reshape(nblk, 1, MOE_TM), bend[-1:].astype(jnp.int32))
    yb = _moe_ffn(xs, bstart.astype(jnp.int32), bcount.astype(jnp.int32), w_gate, w_up, w_down, layer)
    return yb, dest.reshape(2, n), wts.T


def _pos_tables():
    quarter = D // 4
    omega = 1.0 / (10000.0 ** (jnp.arange(quarter, dtype=F32) / quarter))
    ang_r = jnp.arange(T_LAT // GRID_W, dtype=F32)[:, None] * omega
    ang_c = jnp.arange(GRID_W, dtype=F32)[:, None] * omega
    emb_r = jnp.concatenate([jnp.sin(ang_r), jnp.cos(ang_r)], axis=-1)
    emb_c = jnp.concatenate([jnp.sin(ang_c), jnp.cos(ang_c)], axis=-1)
    return emb_r, emb_c


def _block_diag_gates(wa, wx, group):
    nb = wa.shape[1]
    per = group // wa.shape[2]
    eye = jnp.eye(per, dtype=F32)

    def bd(w):
        w = w.reshape(nb // per, per, w.shape[1], w.shape[2])
        return jnp.einsum("gnkj,nm->gnkmj", w, eye).reshape(nb // per, group, group)

    return jnp.concatenate([bd(wa[0]), bd(wa[1]), bd(wx[0]), bd(wx[1])], axis=-1).astype(BF16)


def kernel(x, c, ctx, c_ctx, mod_w, mod_b, norm1_g, norm2_g, ev_w_in, ev_conv_w, ev_conv_b, lru_wa, lru_ba, lru_wx, lru_bx, lru_lambda, gla_wg_up, gla_bg, gla_norm_g, ev_w_out, od_w_in, od_conv_w, od_conv_b, ssd_a_log, ssd_dt_bias, ssd_d, ssd_norm_g, od_w_out, router_w, router_b, exp_w_gate, exp_w_up, exp_w_down, final_norm_g):
    mods = _mod_vectors(c, c_ctx, mod_w, mod_b)
    emb_r, emb_c = _pos_tables()
    rwt = router_w.T
    rb = router_b.reshape(N_EXP, 1)
    tm = N_ALL // 8

    x0, h0 = _prep0(x[0], ctx[0], emb_r, emb_c, mods[0], norm1_g[0:1])
    p0 = _mm([h0], ev_w_in[0], ncols=EVEN_MAIN, tm=tm, tn=1024)
    w_ad = jnp.pad(ev_w_in[0][:, EVEN_MAIN:], ((0, 0), (0, LANES - 2 * GLA_RANK)))
    pad = _mm([h0], w_ad, ncols=LANES, tm=tm, tn=LANES)
    wbd = _block_diag_gates(lru_wa[0], lru_wx[0], 256)
    ya = _lru(p0, ev_conv_w[0], ev_conv_b[0:1], wbd, lru_ba[0], lru_bx[0], lru_lambda[0])
    of = _gla(p0, pad, gla_wg_up[0], gla_bg[0], None)
    yb = _gla(p0, pad, gla_wg_up[0], gla_bg[0], gla_norm_g[0:1], of=of)
    y0 = _mm([ya, yb], ev_w_out[0], ncols=D, tm=tm, tn=1024)
    nctx = T_CTX // ROWS
    x1, h1, e0, w0 = _token_stage(x0, y0, mods[0], mods[0], norm2_g[0:1], rwt, rb, gate_col=2, shift_col=3,
                                  scale_col=4, ctx_blocks=nctx, route=True)
    f0, dest0, wc0 = _moe(h1, e0, w0, exp_w_gate, exp_w_up, exp_w_down, 0)
    x2, h2 = _token_stage(x1, f0, mods[0], mods[1], norm1_g[1:2], rwt, rb, gate_col=5, shift_col=0,
                          scale_col=1, ctx_blocks=nctx, combine=(dest0, wc0))

    p1 = _mm([h2], od_w_in[0], ncols=ODD_MAIN, tm=tm, tn=1024)
    pdt = _mm([h2], od_w_in[0][:, ODD_MAIN:], ncols=LANES, tm=tm, tn=LANES)
    xbc = _conv_silu(p1, od_conv_w[0], od_conv_b[0:1])
    yf = _ssd(xbc, pdt, ssd_dt_bias[0], ssd_a_log[0])
    ysum = _ssd(xbc, pdt, ssd_dt_bias[0], ssd_a_log[0], acc=yf)
    d_rep = jnp.repeat(ssd_d[0], SSD_P).reshape(1, SSD_INNER)
    gy = _ssd_finish(ysum, xbc, p1, d_rep, ssd_norm_g[0:1])
    y1 = _mm([gy], od_w_out[0], ncols=D, tm=1024, tn=512)
    x3, h3, e1, w1 = _token_stage(x2, y1, mods[1], mods[1], norm2_g[1:2], rwt, rb, gate_col=2, shift_col=3,
                                  scale_col=4, x_off=nctx, route=True)
    f1, dest1, wc1 = _moe(h3, e1, w1, exp_w_gate, exp_w_up, exp_w_down, 1)
    (out,) = _token_stage(x3, f1, mods[1], mods[1], final_norm_g.reshape(1, D), rwt, rb, gate_col=5,
                          combine=(dest1, wc1), final=True)
    return out[None]
```

```python
import functools
import math

import jax
import jax.numpy as jnp
from jax import lax
from jax.experimental import pallas as pl
from jax.experimental.pallas import tpu as pltpu

F32 = jnp.float32
BF16 = jnp.bfloat16
HIGHEST = lax.Precision.HIGHEST

D = 2048
T_LAT = 8192
T_CTX = 256
N_ALL = T_CTX + T_LAT
GRID_W = 64
EPS = 1e-6
CONV_W = 4
LRU_W = 1024
LRU_C = 8.0
GLA_H = 8
GLA_DK = 64
GLA_DV = 128
GLA_QK = GLA_H * GLA_DK
GLA_V = GLA_H * GLA_DV
GLA_RANK = 16
GLA_TAU = 16.0
CHUNK = 64
EVEN_MAIN = 2 * LRU_W + 2 * GLA_QK + 2 * GLA_V
SSD_INNER = 2 * D
SSD_P = 64
SSD_H = SSD_INNER // SSD_P
SSD_G = 8
SSD_Z = 128
SSD_XBC = SSD_INNER + 2 * SSD_G * SSD_Z
ODD_MAIN = SSD_INNER + SSD_XBC
N_EXP = 32
N_GRP = 4
GRP = N_EXP // N_GRP
D_EXP = 1024
MOE_TM = 256

ROWS = 256
LANES = 128
SLAB = D // LANES
SUBLANES = 8
VMEM_LIMIT = 56 * 1024 * 1024


def _params(*sem):
    return pltpu.CompilerParams(dimension_semantics=sem, vmem_limit_bytes=VMEM_LIMIT)


def _silu(v):
    return v * jax.nn.sigmoid(v)


def _softplus(v):
    return jnp.maximum(v, 0.0) + jnp.log1p(jnp.exp(-jnp.abs(v)))


def _dot(a, b):
    return jnp.dot(a, b, preferred_element_type=F32)


def _dot_nt(a, b):
    return lax.dot_general(a, b, (((1,), (1,)), ((), ())), preferred_element_type=F32)


def _dot_tn(a, b):
    return lax.dot_general(a, b, (((0,), (0,)), ((), ())), preferred_element_type=F32)


def _mod_kernel(s_ref, w_ref, b_ref, o_ref):
    tn = w_ref.shape[1]
    nrep = tn // LANES

    def body(r, acc):
        a0, a1 = acc
        rows = pl.ds(pl.multiple_of(r * SUBLANES, SUBLANES), SUBLANES)
        w = w_ref[rows, :]
        s0 = _silu(s_ref[0, rows, :])
        s1 = _silu(s_ref[1, rows, :])
        a0 = a0 + w * jnp.concatenate([s0] * nrep, axis=1)
        a1 = a1 + w * jnp.concatenate([s1] * nrep, axis=1)
        return a0, a1

    zero = jnp.zeros((SUBLANES, tn), F32)
    a0, a1 = lax.fori_loop(0, D // SUBLANES, body, (zero, zero), unroll=4)
    o_ref[0:1, :] = jnp.sum(a0, axis=0, keepdims=True) + b_ref[...]
    o_ref[1:2, :] = jnp.sum(a1, axis=0, keepdims=True) + b_ref[...]


def _mod_vectors(c, c_ctx, mod_w, mod_b):
    depth = mod_w.shape[0]
    tn = 1024
    s = jnp.broadcast_to(jnp.stack([c[0], c_ctx])[:, :, None], (2, D, LANES))
    return pl.pallas_call(
        _mod_kernel,
        grid=(depth, 6 * D // tn),
        in_specs=[
            pl.BlockSpec((2, D, LANES), lambda l, j: (0, 0, 0)),
            pl.BlockSpec((None, D, tn), lambda l, j: (l, 0, j)),
            pl.BlockSpec((None, 1, tn), lambda l, j: (l, 0, j)),
        ],
        out_specs=pl.BlockSpec((None, 2, tn), lambda l, j: (l, 0, j)),
        out_shape=jax.ShapeDtypeStruct((depth, 2, 6 * D), F32),
        compiler_params=_params("arbitrary", "arbitrary"),
        name="mod_vectors",
    )(s, mod_w, mod_b.reshape(depth, 1, 6 * D))


def _mod_row(mod_ref, kind, col):
    return mod_ref[pl.ds(kind, 1), col * D:(col + 1) * D]


def _ada_norm(xv, g, shift, scale):
    ms = jnp.mean(xv * xv, axis=-1, keepdims=True)
    return (xv * lax.rsqrt(ms + EPS) * g) * (1.0 + scale) + shift


def _prep0_kernel(x_ref, ctx_ref, er_ref, ec_ref, mod_ref, g_ref, xo_ref, ho_ref):
    i = pl.program_id(0)

    @pl.when(i == 0)
    def _():
        xo_ref[...] = ctx_ref[...]

    @pl.when(i > 0)
    def _():
        r0 = (i - 1) * (ROWS // GRID_W)
        for j in range(ROWS // GRID_W):
            rs = slice(j * GRID_W, (j + 1) * GRID_W)
            xo_ref[rs, 0:D // 2] = x_ref[rs, 0:D // 2] + er_ref[pl.ds(r0 + j, 1), :]
            xo_ref[rs, D // 2:D] = x_ref[rs, D // 2:D] + ec_ref[...]

    kind = jnp.where(i == 0, 1, 0)
    h = _ada_norm(xo_ref[...], g_ref[...], _mod_row(mod_ref, kind, 0), _mod_row(mod_ref, kind, 1))
    ho_ref[...] = h.astype(BF16)


def _prep0(x, ctx, emb_r, emb_c, mod0, g):
    nblk = N_ALL // ROWS
    return pl.pallas_call(
        _prep0_kernel,
        grid=(nblk,),
        in_specs=[
            pl.BlockSpec((ROWS, D), lambda i: (jnp.maximum(i - 1, 0), 0)),
            pl.BlockSpec((ROWS, D), lambda i: (0, 0)),
            pl.BlockSpec(emb_r.shape, lambda i: (0, 0)),
            pl.BlockSpec(emb_c.shape, lambda i: (0, 0)),
            pl.BlockSpec((2, 6 * D), lambda i: (0, 0)),
            pl.BlockSpec((1, D), lambda i: (0, 0)),
        ],
        out_specs=[pl.BlockSpec((ROWS, D), lambda i: (i, 0)), pl.BlockSpec((ROWS, D), lambda i: (i, 0))],
        out_shape=[jax.ShapeDtypeStruct((N_ALL, D), F32), jax.ShapeDtypeStruct((N_ALL, D), BF16)],
        compiler_params=_params("arbitrary"),
        name="embed_norm",
    )(x, ctx, emb_r, emb_c, mod0, g)


def _route(hf32, rwt_ref, rb_ref, eo_ref, wo_ref):
    logits = lax.dot_general(rwt_ref[...], hf32, (((1,), (1,)), ((), ())),
                             precision=HIGHEST, preferred_element_type=F32)
    s = jax.nn.sigmoid(logits)
    sel = s + rb_ref[...]
    row = lax.broadcasted_iota(jnp.int32, (GRP, ROWS), 0)
    neg = jnp.float32(-jnp.inf)
    gs, i1s, i2s = [], [], []
    for g in range(N_GRP):
        blk = sel[g * GRP:(g + 1) * GRP, :]
        m1 = jnp.max(blk, axis=0, keepdims=True)
        i1 = jnp.min(jnp.where(blk == m1, row, GRP), axis=0, keepdims=True)
        blk2 = jnp.where(row == i1, neg, blk)
        m2 = jnp.max(blk2, axis=0, keepdims=True)
        i2 = jnp.min(jnp.where(blk2 == m2, row, GRP), axis=0, keepdims=True)
        gs.append(m1 + m2)
        i1s.append(i1)
        i2s.append(i2)
    best, gi, i1, i2 = gs[0], jnp.zeros((1, ROWS), jnp.int32), i1s[0], i2s[0]
    for g in range(1, N_GRP):
        upd = gs[g] > best
        best = jnp.where(upd, gs[g], best)
        gi = jnp.where(upd, g, gi)
        i1 = jnp.where(upd, i1s[g], i1)
        i2 = jnp.where(upd, i2s[g], i2)
    e1 = gi * GRP + i1
    e2 = gi * GRP + i2
    erow = lax.broadcasted_iota(jnp.int32, (N_EXP, ROWS), 0)
    s1 = jnp.sum(jnp.where(erow == e1, s, 0.0), axis=0, keepdims=True)
    s2 = jnp.sum(jnp.where(erow == e2, s, 0.0), axis=0, keepdims=True)
    tot = s1 + s2
    eo_ref[0:1, :] = e1
    eo_ref[1:2, :] = e2
    wo_ref[0:1, :] = s1 / tot
    wo_ref[1:2, :] = s2 / tot


def _combine_wait(yb_hbm, ybuf, sem, slot):
    pltpu.make_async_copy(yb_hbm.at[pl.ds(0, 2 * ROWS), :], ybuf.at[slot], sem.at[slot]).wait()


def _combine_issue(dest_ref, yb_hbm, ybuf, sem, slot):
    def body(r, carry):
        for k in range(2):
            row = dest_ref[0, k, r]
            pltpu.make_async_copy(yb_hbm.at[pl.ds(row, 1), :], ybuf.at[slot, pl.ds(k * ROWS + r, 1), :],
                                  sem.at[slot]).start()
        return carry

    lax.fori_loop(0, ROWS, body, 0, unroll=8)


def _token_kernel(*refs, gate_col, shift_col, scale_col, ctx_blocks, route, combine, final):
    refs = list(refs)
    if combine:
        dcur_ref, dnxt_ref = refs.pop(0), refs.pop(0)
    x_ref, y_ref = refs.pop(0), refs.pop(0)
    if combine:
        wc_ref = refs.pop(0)
    modg_ref, modn_ref, g_ref, rwt_ref, rb_ref = (refs.pop(0) for _ in range(5))
    i = pl.program_id(0)
    if combine:
        ybuf, sem = refs[-2], refs[-1]
        slot = lax.rem(i, 2)

        @pl.when(i == 0)
        def _():
            _combine_issue(dcur_ref, y_ref, ybuf, sem, 0)

        @pl.when(i + 1 < pl.num_programs(0))
        def _():
            _combine_issue(dnxt_ref, y_ref, ybuf, sem, 1 - slot)

        _combine_wait(y_ref, ybuf, sem, slot)
        y = wc_ref[:, 0:1] * ybuf[slot, 0:ROWS, :] + wc_ref[:, 1:2] * ybuf[slot, ROWS:2 * ROWS, :]
    else:
        y = y_ref[...]
    kind = jnp.where(i < ctx_blocks, 1, 0)
    xn = x_ref[...] + _mod_row(modg_ref, kind, gate_col) * y
    if final:
        ms = jnp.mean(xn * xn, axis=-1, keepdims=True)
        refs[0][...] = xn * lax.rsqrt(ms + EPS) * g_ref[...]
        return
    xo_ref, ho_ref = refs[0], refs[1]
    xo_ref[...] = xn
    h = _ada_norm(xn, g_ref[...], _mod_row(modn_ref, kind, shift_col), _mod_row(modn_ref, kind, scale_col))
    if route:
        for j in range(SLAB):
            ho_ref[pl.ds(j, ROWS, stride=SLAB), :] = h[:, j * LANES:(j + 1) * LANES]
        _route(h, rwt_ref, rb_ref, refs[2], refs[3])
    else:
        ho_ref[...] = h.astype(BF16)


def _token_stage(x, y, modg, modn, g, rwt, rb, *, gate_col, shift_col=0, scale_col=0, x_off=0, ctx_blocks=0,
                 route=False, combine=None, final=False):
    n = x.shape[0] - x_off * ROWS
    nblk = n // ROWS
    row = lambda i: (i, 0)
    const2 = lambda i: (0, 0)
    in_specs, args, scratch = [], [], []
    if combine is not None:
        dest, wc = combine
        dest3 = dest.reshape(2, nblk, ROWS).transpose(1, 0, 2)
        in_specs += [pl.BlockSpec((1, 2, ROWS), lambda i: (i, 0, 0), memory_space=pltpu.SMEM),
                     pl.BlockSpec((1, 2, ROWS), lambda i: (jnp.minimum(i + 1, nblk - 1), 0, 0),
                                  memory_space=pltpu.SMEM)]
        args += [dest3, dest3]
    in_specs.append(pl.BlockSpec((ROWS, D), lambda i: (i + x_off, 0)))
    args.append(x)
    if combine is not None:
        in_specs += [pl.BlockSpec(memory_space=pl.ANY), pl.BlockSpec((ROWS, 2), row)]
        args += [y, wc]
        scratch = [pltpu.VMEM((2, 2 * ROWS, D), F32), pltpu.SemaphoreType.DMA((2,))]
    else:
        in_specs.append(pl.BlockSpec((ROWS, D), row))
        args.append(y)
    in_specs += [pl.BlockSpec((2, 6 * D), const2), pl.BlockSpec((2, 6 * D), const2), pl.BlockSpec((1, D), const2),
                 pl.BlockSpec((N_EXP, D), const2), pl.BlockSpec((N_EXP, 1), const2)]
    args += [modg, modn, g, rwt, rb]
    if final:
        out_specs = [pl.BlockSpec((ROWS, D), row)]
        out_shape = [jax.ShapeDtypeStruct((n, D), F32)]
    else:
        if route:
            out_specs = [pl.BlockSpec((ROWS, D), row), pl.BlockSpec((ROWS * SLAB, LANES), row)]
            out_shape = [jax.ShapeDtypeStruct((n, D), F32), jax.ShapeDtypeStruct((n * SLAB, LANES), F32)]
        else:
            out_specs = [pl.BlockSpec((ROWS, D), row), pl.BlockSpec((ROWS, D), row)]
            out_shape = [jax.ShapeDtypeStruct((n, D), F32), jax.ShapeDtypeStruct((n, D), BF16)]
        if route:
            out_specs += [pl.BlockSpec((2, ROWS), lambda i: (0, i)), pl.BlockSpec((2, ROWS), lambda i: (0, i))]
            out_shape += [jax.ShapeDtypeStruct((2, n), jnp.int32), jax.ShapeDtypeStruct((2, n), F32)]
    kern = functools.partial(_token_kernel, gate_col=gate_col, shift_col=shift_col, scale_col=scale_col,
                             ctx_blocks=ctx_blocks, route=route, combine=combine is not None, final=final)
    return pl.pallas_call(
        kern,
        grid=(nblk,),
        in_specs=in_specs,
        out_specs=out_specs,
        out_shape=out_shape,
        scratch_shapes=scratch,
        compiler_params=_params("arbitrary"),
        name="token_stage",
    )(*args)


def _mm_kernel(*refs, nx):
    x_refs, w_refs, o_ref, wb_refs = refs[:nx], refs[nx:2 * nx], refs[2 * nx], refs[2 * nx + 1:]

    @pl.when(pl.program_id(1) == 0)
    def _():
        for w_ref, wb_ref in zip(w_refs, wb_refs):
            wb_ref[...] = w_ref[...].astype(BF16)

    acc = _dot(x_refs[0][...], wb_refs[0][...])
    for x_ref, wb_ref in zip(x_refs[1:], wb_refs[1:]):
        acc = acc + _dot(x_ref[...], wb_ref[...])
    o_ref[...] = acc


def _mm(xs, w, *, ncols, tm, tn, x_off=0):
    nx = len(xs)
    k = xs[0].shape[1]
    m = xs[0].shape[0] - x_off * tm
    in_specs = [pl.BlockSpec((tm, k), lambda j, i: (i + x_off, 0)) for _ in xs]
    in_specs += [pl.BlockSpec((k, tn), functools.partial(lambda j, i, kk: (kk, j), kk=kk)) for kk in range(nx)]
    return pl.pallas_call(
        functools.partial(_mm_kernel, nx=nx),
        grid=(ncols // tn, m // tm),
        in_specs=in_specs,
        out_specs=pl.BlockSpec((tm, tn), lambda j, i: (i, j)),
        out_shape=jax.ShapeDtypeStruct((m, ncols), F32),
        scratch_shapes=[pltpu.VMEM((k, tn), BF16) for _ in xs],
        compiler_params=_params("arbitrary", "arbitrary"),
        name="proj",
    )(*xs, *([w] * nx))


def _conv_block(cur, prev8, next8, w_ref, b_ref, i, nblk):
    ctx_edge = T_CTX // ROWS
    keep_prev = jnp.logical_and(i != 0, i != ctx_edge)
    keep_next = jnp.logical_and(i != nblk - 1, i != ctx_edge - 1)
    prev8 = jnp.where(keep_prev, prev8, 0.0)
    next8 = jnp.where(keep_next, next8, 0.0)
    ext = jnp.concatenate([prev8, cur, next8], axis=0)
    base = SUBLANES - CONV_W // 2
    y = b_ref[...] + w_ref[0:1, :] * ext[base:base + ROWS]
    for j in range(1, CONV_W):
        y = y + w_ref[j:j + 1, :] * ext[base + j:base + j + ROWS]
    return y


def _halo_specs(width, col_blk, nblk):
    per = ROWS // SUBLANES
    last8 = N_ALL // SUBLANES - 1
    return [
        pl.BlockSpec((ROWS, width), lambda i, *a: (i, col_blk(*a))),
        pl.BlockSpec((SUBLANES, width), lambda i, *a: (jnp.maximum(i * per - 1, 0), col_blk(*a))),
        pl.BlockSpec((SUBLANES, width), lambda i, *a: (jnp.minimum((i + 1) * per, last8), col_blk(*a))),
    ]


def _scan_tiles(a_ref, b_ref, h_ref, carry_ref, reverse):
    ntile = ROWS // SUBLANES
    row = lax.broadcasted_iota(jnp.int32, (SUBLANES, LRU_W), 0)

    def body(t, carry):
        tt = (ntile - 1 - t) if reverse else t
        rows = pl.ds(pl.multiple_of(tt * SUBLANES, SUBLANES), SUBLANES)
        a = a_ref[rows, :]
        b = b_ref[rows, :]
        for d in (1, 2, 4):
            shift = (SUBLANES - d) if reverse else d
            a_sh = pltpu.roll(a, shift, axis=0)
            b_sh = pltpu.roll(b, shift, axis=0)
            m = (row < SUBLANES - d) if reverse else (row >= d)
            b = jnp.where(m, a * b_sh, 0.0) + b
            a = jnp.where(m, a * a_sh, a)
        h = a * carry + b
        h_ref[rows, :] = h
        edge = h[0:1, :] if reverse else h[SUBLANES - 1:SUBLANES, :]
        return jnp.broadcast_to(edge, (SUBLANES, LRU_W))

    carry_ref[...] = lax.fori_loop(0, ntile, body, carry_ref[...])


def _lru_fwd_kernel(xa_ref, prev_ref, next_ref, cw_ref, cb_ref, wbd_ref, ba_ref, bx_ref, lam_ref,
                    hf_ref, a1_ref, b1_ref, carry_ref, a0_ref, b0_ref):
    i = pl.program_id(0)
    nblk = pl.num_programs(0)

    @pl.when(i == 0)
    def _():
        carry_ref[...] = jnp.zeros_like(carry_ref)

    xa = _conv_block(xa_ref[...], prev_ref[...], next_ref[...], cw_ref, cb_ref, i, nblk)
    xab = xa.astype(BF16)
    nsp = -LRU_C * _softplus(-lam_ref[...])
    gw = wbd_ref.shape[1]
    for g in range(LRU_W // gw):
        cs = slice(g * gw, (g + 1) * gw)
        z = _dot(xab[:, cs], wbd_ref[g])
        for d in range(2):
            r = jax.nn.sigmoid(z[:, d * gw:(d + 1) * gw] + ba_ref[d:d + 1, cs])
            ig = jax.nn.sigmoid(z[:, (2 + d) * gw:(3 + d) * gw] + bx_ref[d:d + 1, cs])
            log_a = r * nsp[d:d + 1, cs]
            a = jnp.exp(log_a)
            b = jnp.sqrt(-jnp.tanh(log_a) * (a * a + 1.0)) * ig * xa[:, cs]
            if d == 0:
                a0_ref[:, cs] = a
                b0_ref[:, cs] = b
            else:
                a1_ref[:, cs] = a
                b1_ref[:, cs] = b
    _scan_tiles(a0_ref, b0_ref, hf_ref, carry_ref, reverse=False)


def _gelu_tanh(v):
    return 0.5 * v * (1.0 + jnp.tanh(math.sqrt(2.0 / math.pi) * (v + 0.044715 * (v * v * v))))


def _lru_bwd_kernel(a1_ref, b1_ref, hf_ref, ga_ref, ya_ref, carry_ref, hb_ref):
    @pl.when(pl.program_id(0) == 0)
    def _():
        carry_ref[...] = jnp.zeros_like(carry_ref)

    _scan_tiles(a1_ref, b1_ref, hb_ref, carry_ref, reverse=True)
    ya_ref[...] = ((hf_ref[...] + hb_ref[...]) * _gelu_tanh(ga_ref[...])).astype(BF16)


def _rev_block(i, nblk):
    nctx = T_CTX // ROWS
    return jnp.where(i < nctx, nctx - 1 - i, nblk - 1 - (i - nctx))


def _lru(p0, conv_w, conv_b, wbd, ba, bx, lam):
    nblk = N_ALL // ROWS
    full = lambda shape: pl.BlockSpec(shape, lambda i: (0,) * len(shape))
    hf, a1, b1 = pl.pallas_call(
        _lru_fwd_kernel,
        grid=(nblk,),
        in_specs=_halo_specs(LRU_W, lambda: 0, nblk) + [
            full((CONV_W, LRU_W)), full((1, LRU_W)), full(wbd.shape),
            full((2, LRU_W)), full((2, LRU_W)), full((2, LRU_W)),
        ],
        out_specs=[pl.BlockSpec((ROWS, LRU_W), lambda i: (i, 0))] * 3,
        out_shape=[jax.ShapeDtypeStruct((N_ALL, LRU_W), F32)] * 3,
        scratch_shapes=[pltpu.VMEM((SUBLANES, LRU_W), F32), pltpu.VMEM((ROWS, LRU_W), F32),
                        pltpu.VMEM((ROWS, LRU_W), F32)],
        compiler_params=_params("arbitrary"),
        name="lru_fwd",
    )(p0, p0, p0, conv_w, conv_b, wbd, ba, bx, lam)
    rev = lambda i: (_rev_block(i, nblk), 0)
    ya = pl.pallas_call(
        _lru_bwd_kernel,
        grid=(nblk,),
        in_specs=[pl.BlockSpec((ROWS, LRU_W), rev)] * 3
        + [pl.BlockSpec((ROWS, LRU_W), lambda i: (_rev_block(i, nblk), 1))],
        out_specs=pl.BlockSpec((ROWS, LRU_W), rev),
        out_shape=jax.ShapeDtypeStruct((N_ALL, LRU_W), BF16),
        scratch_shapes=[pltpu.VMEM((SUBLANES, LRU_W), F32), pltpu.VMEM((ROWS, LRU_W), F32)],
        compiler_params=_params("arbitrary"),
        name="lru_bwd",
    )(a1, b1, hf, p0)
    return ya


def _tri(reverse):
    r = lax.broadcasted_iota(jnp.int32, (CHUNK, CHUNK), 0)
    c = lax.broadcasted_iota(jnp.int32, (CHUNK, CHUNK), 1)
    return (c >= r) if reverse else (c <= r)


def _gla_kernel(*refs, reverse):
    if reverse:
        q_ref, k_ref, v_ref, ad_ref, wg_ref, bg_ref, of_ref, og_ref, gn_ref, o_ref, s_ref = refs
    else:
        q_ref, k_ref, v_ref, ad_ref, wg_ref, bg_ref, o_ref, s_ref = refs
    d = 1 if reverse else 0

    @pl.when(pl.program_id(0) == 0)
    def _():
        s_ref[...] = jnp.zeros_like(s_ref)

    causal = _tri(reverse)
    tri_f = causal.astype(F32)
    lane = lax.broadcasted_iota(jnp.int32, (1, 2 * GLA_DK), 1)
    eye = (lax.broadcasted_iota(jnp.int32, (2 * GLA_DK, 2 * GLA_DK), 0)
           == lax.broadcasted_iota(jnp.int32, (2 * GLA_DK, 2 * GLA_DK), 1)).astype(F32)
    nch = ROWS // CHUNK
    for cc in range(nch):
        c = (nch - 1 - cc) if reverse else cc
        rs = slice(c * CHUNK, (c + 1) * CHUNK)
        ad = ad_ref[rs, d * GLA_RANK:(d + 1) * GLA_RANK]
        z = jnp.dot(ad, wg_ref[...], precision=HIGHEST, preferred_element_type=F32) + bg_ref[...]
        lg = -_softplus(-z) * (1.0 / GLA_TAU)
        cum = jnp.dot(tri_f, lg, precision=HIGHEST, preferred_element_type=F32)
        last = cum[0:1, :] if reverse else cum[CHUNK - 1:CHUNK, :]
        q_dec = q_ref[rs, :] * (GLA_DK ** -0.5) * jnp.exp(cum)
        k = k_ref[rs, :]
        k_inv = (k * jnp.exp(-cum)).astype(BF16)
        k_end = (k * jnp.exp(last - cum)).astype(BF16)
        decay = jnp.exp(last)
        for p in range(GLA_H // 2):
            ls = slice(p * 2 * GLA_DK, (p + 1) * 2 * GLA_DK)
            qd = q_dec[:, ls]
            vp = v_ref[rs, p * 2 * GLA_DV:(p + 1) * 2 * GLA_DV].astype(BF16)
            sp = s_ref[p]
            spb = sp.astype(BF16)
            for hh in range(2):
                head = 2 * p + hh
                hs = slice(head * GLA_DV, (head + 1) * GLA_DV)
                in_head = jnp.logical_and(lane >= hh * GLA_DK, lane < (hh + 1) * GLA_DK)
                qm = jnp.where(in_head, qd, 0.0).astype(BF16)
                sc = jnp.where(causal, _dot_nt(qm, k_inv[:, ls]), 0.0)
                o = _dot(sc.astype(BF16), vp[:, hh * GLA_DV:(hh + 1) * GLA_DV])
                o = o + _dot(qm, spb[:, hh * GLA_DV:(hh + 1) * GLA_DV])
                if reverse:
                    tot = o + of_ref[rs, hs]
                    ms = jnp.mean(tot * tot, axis=-1, keepdims=True)
                    y = tot * lax.rsqrt(ms + EPS) * gn_ref[:, hs] * _silu(og_ref[rs, hs])
                    o_ref[rs, hs] = y.astype(BF16)
                else:
                    o_ref[rs, hs] = o
            upd = _dot_tn(k_end[:, ls], vp)
            dcol = jnp.sum(eye * decay[:, ls], axis=1, keepdims=True)
            s_ref[p] = dcol * sp + upd


def _gla(p0, pad, wg, bg, gnorm, of=None):
    reverse = of is not None
    nblk = N_ALL // ROWS
    blk = (lambda i: _rev_block(i, nblk)) if reverse else (lambda i: i)
    d = 1 if reverse else 0
    qk_blk = 2 * LRU_W // GLA_QK
    in_specs = [
        pl.BlockSpec((ROWS, GLA_QK), lambda i: (blk(i), qk_blk)),
        pl.BlockSpec((ROWS, GLA_QK), lambda i: (blk(i), qk_blk + 1)),
        pl.BlockSpec((ROWS, GLA_V), lambda i: (blk(i), 3)),
        pl.BlockSpec((ROWS, LANES), lambda i: (blk(i), 0)),
        pl.BlockSpec((None, GLA_RANK, GLA_QK), lambda i: (d, 0, 0)),
        pl.BlockSpec((None, 1, GLA_QK), lambda i: (d, 0, 0)),
    ]
    args = [p0, p0, p0, pad, wg, bg.reshape(2, 1, GLA_QK)]
    if reverse:
        in_specs += [
            pl.BlockSpec((ROWS, GLA_V), lambda i: (blk(i), 0)),
            pl.BlockSpec((ROWS, GLA_V), lambda i: (blk(i), 4)),
            pl.BlockSpec((1, GLA_V), lambda i: (0, 0)),
        ]
        args += [of, p0, gnorm]
    return pl.pallas_call(
        functools.partial(_gla_kernel, reverse=reverse),
        grid=(nblk,),
        in_specs=in_specs,
        out_specs=pl.BlockSpec((ROWS, GLA_V), lambda i: (blk(i), 0)),
        out_shape=jax.ShapeDtypeStruct((N_ALL, GLA_V), BF16 if reverse else F32),
        scratch_shapes=[pltpu.VMEM((GLA_H // 2, 2 * GLA_DK, 2 * GLA_DV), F32)],
        compiler_params=_params("arbitrary"),
        name="gla_bwd" if reverse else "gla_fwd",
    )(*args)


def _conv_silu_kernel(x_ref, prev_ref, next_ref, w_ref, b_ref, o_ref):
    i = pl.program_id(0)
    y = _conv_block(x_ref[...], prev_ref[...], next_ref[...], w_ref, b_ref, i, pl.num_programs(0))
    o_ref[...] = _silu(y)


def _conv_silu(p1, conv_w, conv_b):
    nblk = N_ALL // ROWS
    tc = 1024
    off = SSD_INNER // tc
    return pl.pallas_call(
        _conv_silu_kernel,
        grid=(nblk, SSD_XBC // tc),
        in_specs=_halo_specs(tc, lambda j: j + off, nblk) + [
            pl.BlockSpec((CONV_W, tc), lambda i, j: (0, j)),
            pl.BlockSpec((1, tc), lambda i, j: (0, j)),
        ],
        out_specs=pl.BlockSpec((ROWS, tc), lambda i, j: (i, j)),
        out_shape=jax.ShapeDtypeStruct((N_ALL, SSD_XBC), F32),
        compiler_params=_params("arbitrary", "arbitrary"),
        name="ssd_conv",
    )(p1, p1, p1, conv_w, conv_b)


def _ssd_kernel(*refs, reverse):
    if reverse:
        xs_ref, b_ref, c_ref, dt_ref, dtb_ref, alog_ref, acc_ref, y_ref, st_ref = refs
    else:
        xs_ref, b_ref, c_ref, dt_ref, dtb_ref, alog_ref, y_ref, st_ref = refs
    d = 1 if reverse else 0

    @pl.when(pl.program_id(0) == 0)
    def _():
        st_ref[...] = jnp.zeros_like(st_ref)

    causal = _tri(reverse)
    tri_f = causal.astype(F32)
    lane = lax.broadcasted_iota(jnp.int32, (1, 2 * SSD_P), 1)
    low = lane < SSD_P
    a_neg = -jnp.exp(alog_ref[...])
    nch = ROWS // CHUNK
    hg = SSD_H // SSD_G

    def chunk(cc, carry):
        c = (nch - 1 - cc) if reverse else cc
        rs = pl.ds(pl.multiple_of(c * CHUNK, CHUNK), CHUNK)
        dtv = _softplus(dt_ref[rs, d * SSD_H:(d + 1) * SSD_H] + dtb_ref[...])
        la = dtv * a_neg
        cum = jnp.dot(tri_f, la, precision=HIGHEST, preferred_element_type=F32)
        cum_t = cum.T
        dt_t = dtv.T
        last = cum[0:1, :] if reverse else cum[CHUNK - 1:CHUNK, :]
        w2 = dtv * jnp.exp(last - cum)
        ecum = jnp.exp(cum)
        dec = jnp.exp(last)
        for g in range(SSD_G):
            gs = slice(g * SSD_Z, (g + 1) * SSD_Z)
            bg = b_ref[rs, gs]
            cg = c_ref[rs, gs].astype(BF16)
            bgb = bg.astype(BF16)
            cb = _dot_nt(cg, bgb)
            for pr in range(hg // 2):
                h0 = g * hg + 2 * pr
                ps = slice(h0 * SSD_P, (h0 + 2) * SSD_P)
                xp = xs_ref[rs, ps]
                xpb = xp.astype(BF16)
                ys = []
                for hh in range(2):
                    h = h0 + hh
                    seg = jnp.exp(jnp.where(causal, cum[:, h:h + 1] - cum_t[h:h + 1, :], -jnp.inf))
                    m = cb * seg * dt_t[h:h + 1, :]
                    ys.append(_dot(m.astype(BF16), xpb))
                y = jnp.where(low, ys[0], ys[1])
                stp = st_ref[g, :, 2 * pr * SSD_P:(2 * pr + 2) * SSD_P]
                ec = jnp.where(low, ecum[:, h0:h0 + 1], ecum[:, h0 + 1:h0 + 2])
                y = y + _dot(cg, stp.astype(BF16)) * ec
                if reverse:
                    y = y + acc_ref[rs, ps]
                y_ref[rs, ps] = y
                w2p = jnp.where(low, w2[:, h0:h0 + 1], w2[:, h0 + 1:h0 + 2])
                upd = _dot_tn(bgb, (xp * w2p).astype(BF16))
                dcp = jnp.where(low, dec[:, h0:h0 + 1], dec[:, h0 + 1:h0 + 2])
                st_ref[g, :, 2 * pr * SSD_P:(2 * pr + 2) * SSD_P] = stp * dcp + upd
        return carry

    lax.fori_loop(0, nch, chunk, 0)


def _ssd(xbc, pdt, dt_bias, a_log, acc=None):
    reverse = acc is not None
    nblk = N_ALL // ROWS
    blk = (lambda i: _rev_block(i, nblk)) if reverse else (lambda i: i)
    d = 1 if reverse else 0
    gz = SSD_G * SSD_Z
    in_specs = [
        pl.BlockSpec((ROWS, SSD_INNER), lambda i: (blk(i), 0)),
        pl.BlockSpec((ROWS, gz), lambda i: (blk(i), SSD_INNER // gz)),
        pl.BlockSpec((ROWS, gz), lambda i: (blk(i), SSD_INNER // gz + 1)),
        pl.BlockSpec((ROWS, LANES), lambda i: (blk(i), 0)),
        pl.BlockSpec((None, 1, SSD_H), lambda i: (d, 0, 0)),
        pl.BlockSpec((None, 1, SSD_H), lambda i: (d, 0, 0)),
    ]
    args = [xbc, xbc, xbc, pdt, dt_bias.reshape(2, 1, SSD_H), a_log.reshape(2, 1, SSD_H)]
    if reverse:
        in_specs.append(pl.BlockSpec((ROWS, SSD_INNER), lambda i: (blk(i), 0)))
        args.append(acc)
    return pl.pallas_call(
        functools.partial(_ssd_kernel, reverse=reverse),
        grid=(nblk,),
        in_specs=in_specs,
        out_specs=pl.BlockSpec((ROWS, SSD_INNER), lambda i: (blk(i), 0)),
        out_shape=jax.ShapeDtypeStruct((N_ALL, SSD_INNER), F32),
        scratch_shapes=[pltpu.VMEM((SSD_G, SSD_Z, SSD_INNER // SSD_G), F32)],
        compiler_params=_params("arbitrary"),
        name="ssd_bwd" if reverse else "ssd_fwd",
    )(*args)


def _ssd_finish_kernel(y_ref, xs_ref, z_ref, d_ref, g_ref, o_ref):
    y = y_ref[...] + d_ref[...] * xs_ref[...]
    u = y * _silu(z_ref[...])
    ms = jnp.mean(u * u, axis=-1, keepdims=True)
    o_ref[...] = (u * lax.rsqrt(ms + EPS) * g_ref[...]).astype(BF16)


def _ssd_finish(ysum, xbc, p1, d_rep, g):
    nctx = T_CTX // ROWS
    lat = lambda i: (i + nctx, 0)
    return pl.pallas_call(
        _ssd_finish_kernel,
        grid=(T_LAT // ROWS,),
        in_specs=[
            pl.BlockSpec((ROWS, SSD_INNER), lat),
            pl.BlockSpec((ROWS, SSD_INNER), lat),
            pl.BlockSpec((ROWS, SSD_INNER), lat),
            pl.BlockSpec((1, SSD_INNER), lambda i: (0, 0)),
            pl.BlockSpec((1, SSD_INNER), lambda i: (0, 0)),
        ],
        out_specs=pl.BlockSpec((ROWS, SSD_INNER), lambda i: (i, 0)),
        out_shape=jax.ShapeDtypeStruct((T_LAT, SSD_INNER), BF16),
        compiler_params=_params("arbitrary"),
        name="ssd_finish",
    )(ysum, xbc, p1, d_rep, g)


def _dispatch_wait(h_hbm, buf, sem, slot):
    pltpu.make_async_copy(h_hbm.at[pl.ds(0, MOE_TM * SLAB), :], buf.at[slot], sem.at[slot]).wait()


def _dispatch_issue(idx_ref, h_hbm, buf, sem, slot):
    def body(r, carry):
        src = pl.multiple_of(idx_ref[0, 0, r] * SLAB, SLAB)
        dst = pl.multiple_of(r * SLAB, SLAB)
        pltpu.make_async_copy(h_hbm.at[pl.ds(src, SLAB), :], buf.at[slot, pl.ds(dst, SLAB), :],
                              sem.at[slot]).start()
        return carry

    lax.fori_loop(0, MOE_TM, body, 0, unroll=8)


def _dispatch_kernel(nu_ref, icur_ref, inxt_ref, h_hbm, o_ref, buf, sem):
    m = pl.program_id(0)
    nu = nu_ref[0]
    slot = lax.rem(m, 2)

    @pl.when(m == 0)
    def _():
        _dispatch_issue(icur_ref, h_hbm, buf, sem, 0)

    @pl.when(m + 1 < nu)
    def _():
        _dispatch_issue(inxt_ref, h_hbm, buf, sem, 1 - slot)

    @pl.when(m < nu)
    def _():
        _dispatch_wait(h_hbm, buf, sem, slot)
        for j in range(SLAB):
            o_ref[:, j * LANES:(j + 1) * LANES] = buf[slot, pl.ds(j, MOE_TM, stride=SLAB), :].astype(BF16)


def _dispatch(h, slot_tok, n_used):
    nblk = slot_tok.shape[0]
    used = lambda m, nu: jnp.minimum(m, nu[0] - 1)
    return pl.pallas_call(
        _dispatch_kernel,
        grid_spec=pltpu.PrefetchScalarGridSpec(
            num_scalar_prefetch=1,
            grid=(nblk,),
            in_specs=[
                pl.BlockSpec((1, 1, MOE_TM), lambda m, nu: (used(m, nu), 0, 0), memory_space=pltpu.SMEM),
                pl.BlockSpec((1, 1, MOE_TM), lambda m, nu: (used(m + 1, nu), 0, 0), memory_space=pltpu.SMEM),
                pl.BlockSpec(memory_space=pl.ANY),
            ],
            out_specs=pl.BlockSpec((MOE_TM, D), lambda m, nu: (used(m, nu), 0)),
            scratch_shapes=[pltpu.VMEM((2, MOE_TM * SLAB, LANES), F32), pltpu.SemaphoreType.DMA((2,))],
        ),
        out_shape=jax.ShapeDtypeStruct((nblk * MOE_TM, D), BF16),
        compiler_params=_params("arbitrary"),
        name="moe_dispatch",
    )(n_used, slot_tok, slot_tok, h)


def _expert_ffn_kernel(bs_ref, bc_ref, src_hbm, *refs, n_w, layer, compute):
    w_hbm, dst_hbm = refs[:n_w], refs[n_w]
    scr = refs[n_w + 1:]
    w32, wb = scr[:n_w], scr[n_w:2 * n_w]
    ibuf, obuf, wsem, isem, osem = scr[2 * n_w:]
    e = pl.program_id(0)
    b0 = bs_ref[e]
    nb = bc_ref[e]

    def rows(rb):
        return pl.ds(pl.multiple_of((b0 + rb) * MOE_TM, MOE_TM), MOE_TM)

    def in_copy(rb, slot):
        return pltpu.make_async_copy(src_hbm.at[rows(rb), :], ibuf.at[slot], isem.at[slot])

    def out_copy(rb, slot):
        return pltpu.make_async_copy(obuf.at[slot], dst_hbm.at[rows(rb), :], osem.at[slot])

    def w_copy(k, ee, slot):
        return pltpu.make_async_copy(w_hbm[k].at[layer, ee], w32[k].at[slot], wsem.at[k, slot])

    wslot = lax.rem(e, 2)

    @pl.when(e == 0)
    def _():
        for k in range(n_w):
            w_copy(k, 0, 0).start()

    for first in range(2):
        @pl.when(nb > first)
        def _():
            in_copy(first, first).start()

    @pl.when(e + 1 < pl.num_programs(0))
    def _():
        for k in range(n_w):
            w_copy(k, e + 1, 1 - wslot).start()

    for k in range(n_w):
        w_copy(k, e, wslot).wait()
        wb[k][...] = w32[k][wslot].astype(BF16)

    def body(rb, carry):
        slot = lax.rem(rb, 2)
        in_copy(rb, slot).wait()

        @pl.when(rb >= 2)
        def _():
            out_copy(rb - 2, slot).wait()

        obuf[slot] = compute(ibuf[slot], *wb).astype(obuf.dtype)
        out_copy(rb, slot).start()

        @pl.when(rb + 2 < nb)
        def _():
            in_copy(rb + 2, slot).start()

        return carry

    lax.fori_loop(0, nb, body, 0)

    @pl.when(nb >= 2)
    def _():
        out_copy(nb - 2, lax.rem(nb, 2)).wait()

    @pl.when(nb >= 1)
    def _():
        out_copy(nb - 1, lax.rem(nb + 1, 2)).wait()


def _expert_ffn(src, ws, bstart, bcount, layer, n_out, out_dtype, compute, name):
    rows, k_in = src.shape
    anyspec = pl.BlockSpec(memory_space=pl.ANY)
    n_w = len(ws)
    wshape = ws[0].shape[2:]
    return pl.pallas_call(
        functools.partial(_expert_ffn_kernel, n_w=n_w, layer=layer, compute=compute),
        grid_spec=pltpu.PrefetchScalarGridSpec(
            num_scalar_prefetch=2,
            grid=(N_EXP,),
            in_specs=[anyspec] * (1 + n_w),
            out_specs=anyspec,
            scratch_shapes=[pltpu.VMEM((2,) + wshape, F32) for _ in ws] + [pltpu.VMEM(wshape, BF16) for _ in ws]
            + [pltpu.VMEM((2, MOE_TM, k_in), src.dtype), pltpu.VMEM((2, MOE_TM, n_out), out_dtype),
               pltpu.SemaphoreType.DMA((n_w, 2)), pltpu.SemaphoreType.DMA((2,)), pltpu.SemaphoreType.DMA((2,))],
        ),
        out_shape=jax.ShapeDtypeStruct((rows, n_out), out_dtype),
        compiler_params=_params("arbitrary"),
        name=name,
    )(bstart, bcount, src, *ws)


def _moe_ffn(xs, bstart, bcount, w_gate, w_up, w_down, layer):
    hb = _expert_ffn(xs, [w_gate, w_up], bstart, bcount, layer, D_EXP, BF16,
                     lambda x, wg, wu: _silu(_dot(x, wg[...])) * _dot(x, wu[...]), "moe_up")
    return _expert_ffn(hb, [w_down], bstart, bcount, layer, D, F32, lambda h, wd: _dot(h, wd[...]), "moe_down")


def _moe(h, eidx, wts, w_gate, w_up, w_down, layer):
    n = eidx.shape[1]
    flat_e = eidx.reshape(-1)
    onehot = (flat_e[:, None] == jnp.arange(N_EXP, dtype=jnp.int32)[None, :]).astype(jnp.int32)
    csum = jnp.cumsum(onehot, axis=0)
    bcount = (csum[-1] + MOE_TM - 1) // MOE_TM
    bend = jnp.cumsum(bcount)
    bstart = bend - bcount
    dest = jnp.sum(onehot * (csum - 1 + (bstart * MOE_TM)[None, :]), axis=1)
    nblk = -(-2 * n // MOE_TM) + N_EXP
    tok = jnp.tile(jnp.arange(n, dtype=jnp.int32), 2)
    slot_tok = jnp.zeros((nblk * MOE_TM,), jnp.int32).at[dest].set(tok)
    xs = _dispatch(h, slot_tok.reshape(nblk, 1, MOE_TM), bend[-1:].astype(jnp.int32))
    yb = _moe_ffn(xs, bstart.astype(jnp.int32), bcount.astype(jnp.int32), w_gate, w_up, w_down, layer)
    return yb, dest.reshape(2, n), wts.T


def _pos_tables():
    quarter = D // 4
    omega = 1.0 / (10000.0 ** (jnp.arange(quarter, dtype=F32) / quarter))
    ang_r = jnp.arange(T_LAT // GRID_W, dtype=F32)[:, None] * omega
    ang_c = jnp.arange(GRID_W, dtype=F32)[:, None] * omega
    emb_r = jnp.concatenate([jnp.sin(ang_r), jnp.cos(ang_r)], axis=-1)
    emb_c = jnp.concatenate([jnp.sin(ang_c), jnp.cos(ang_c)], axis=-1)
    return emb_r, emb_c


def _block_diag_gates(wa, wx, group):
    nb = wa.shape[1]
    per = group // wa.shape[2]
    eye = jnp.eye(per, dtype=F32)

    def bd(w):
        w = w.reshape(nb // per, per, w.shape[1], w.shape[2])
        return jnp.einsum("gnkj,nm->gnkmj", w, eye).reshape(nb // per, group, group)

    return jnp.concatenate([bd(wa[0]), bd(wa[1]), bd(wx[0]), bd(wx[1])], axis=-1).astype(BF16)


def kernel(x, c, ctx, c_ctx, mod_w, mod_b, norm1_g, norm2_g, ev_w_in, ev_conv_w, ev_conv_b, lru_wa, lru_ba, lru_wx, lru_bx, lru_lambda, gla_wg_up, gla_bg, gla_norm_g, ev_w_out, od_w_in, od_conv_w, od_conv_b, ssd_a_log, ssd_dt_bias, ssd_d, ssd_norm_g, od_w_out, router_w, router_b, exp_w_gate, exp_w_up, exp_w_down, final_norm_g):
    mods = _mod_vectors(c, c_ctx, mod_w, mod_b)
    emb_r, emb_c = _pos_tables()
    rwt = router_w.T
    rb = router_b.reshape(N_EXP, 1)
    tm = N_ALL // 8

    x0, h0 = _prep0(x[0], ctx[0], emb_r, emb_c, mods[0], norm1_g[0:1])
    p0 = _mm([h0], ev_w_in[0], ncols=EVEN_MAIN, tm=tm, tn=1024)
    w_ad = jnp.pad(ev_w_in[0][:, EVEN_MAIN:], ((0, 0), (0, LANES - 2 * GLA_RANK)))
    pad = _mm([h0], w_ad, ncols=LANES, tm=tm, tn=LANES)
    wbd = _block_diag_gates(lru_wa[0], lru_wx[0], 256)
    ya = _lru(p0, ev_conv_w[0], ev_conv_b[0:1], wbd, lru_ba[0], lru_bx[0], lru_lambda[0])
    of = _gla(p0, pad, gla_wg_up[0], gla_bg[0], None)
    yb = _gla(p0, pad, gla_wg_up[0], gla_bg[0], gla_norm_g[0:1], of=of)
    y0 = _mm([ya, yb], ev_w_out[0], ncols=D, tm=tm, tn=1024)
    nctx = T_CTX // ROWS
    x1, h1, e0, w0 = _token_stage(x0, y0, mods[0], mods[0], norm2_g[0:1], rwt, rb, gate_col=2, shift_col=3,
                                  scale_col=4, ctx_blocks=nctx, route=True)
    f0, dest0, wc0 = _moe(h1, e0, w0, exp_w_gate, exp_w_up, exp_w_down, 0)
    x2, h2 = _token_stage(x1, f0, mods[0], mods[1], norm1_g[1:2], rwt, rb, gate_col=5, shift_col=0,
                          scale_col=1, ctx_blocks=nctx, combine=(dest0, wc0))

    p1 = _mm([h2], od_w_in[0], ncols=ODD_MAIN, tm=tm, tn=1024)
    pdt = _mm([h2], od_w_in[0][:, ODD_MAIN:], ncols=LANES, tm=tm, tn=LANES)
    xbc = _conv_silu(p1, od_conv_w[0], od_conv_b[0:1])
    yf = _ssd(xbc, pdt, ssd_dt_bias[0], ssd_a_log[0])
    ysum = _ssd(xbc, pdt, ssd_dt_bias[0], ssd_a_log[0], acc=yf)
    d_rep = jnp.repeat(ssd_d[0], SSD_P).reshape(1, SSD_INNER)
    gy = _ssd_finish(ysum, xbc, p1, d_rep, ssd_norm_g[0:1])
    y1 = _mm([gy], od_w_out[0], ncols=D, tm=1024, tn=512)
    x3, h3, e1, w1 = _token_stage(x2, y1, mods[1], mods[1], norm2_g[1:2], rwt, rb, gate_col=2, shift_col=3,
                                  scale_col=4, x_off=nctx, route=True)
    f1, dest1, wc1 = _moe(h3, e1, w1, exp_w_gate, exp_w_up, exp_w_down, 1)
    (out,) = _token_stage(x3, f1, mods[1], mods[1], final_norm_g.reshape(1, D), rwt, rb, gate_col=5,
                          combine=(dest1, wc1), final=True)
    return out[None]
```

```python
import functools
import math

import jax
import jax.numpy as jnp
from jax import lax
from jax.experimental import pallas as pl
from jax.experimental.pallas import tpu as pltpu

F32 = jnp.float32
BF16 = jnp.bfloat16
HIGHEST = lax.Precision.HIGHEST

D = 2048
T_LAT = 8192
T_CTX = 256
N_ALL = T_CTX + T_LAT
GRID_W = 64
EPS = 1e-6
CONV_W = 4
LRU_W = 1024
LRU_C = 8.0
GLA_H = 8
GLA_DK = 64
GLA_DV = 128
GLA_QK = GLA_H * GLA_DK
GLA_V = GLA_H * GLA_DV
GLA_RANK = 16
GLA_TAU = 16.0
CHUNK = 64
EVEN_MAIN = 2 * LRU_W + 2 * GLA_QK + 2 * GLA_V
SSD_INNER = 2 * D
SSD_P = 64
SSD_H = SSD_INNER // SSD_P
SSD_G = 8
SSD_Z = 128
SSD_XBC = SSD_INNER + 2 * SSD_G * SSD_Z
ODD_MAIN = SSD_INNER + SSD_XBC
N_EXP = 32
N_GRP = 4
GRP = N_EXP // N_GRP
D_EXP = 1024
MOE_TM = 256

ROWS = 256
LANES = 128
SUBLANES = 8
VMEM_LIMIT = 56 * 1024 * 1024


def _params(*sem):
    return pltpu.CompilerParams(dimension_semantics=sem, vmem_limit_bytes=VMEM_LIMIT)


def _silu(v):
    return v * jax.nn.sigmoid(v)


def _softplus(v):
    return jnp.maximum(v, 0.0) + jnp.log1p(jnp.exp(-jnp.abs(v)))


def _dot(a, b):
    return jnp.dot(a, b, preferred_element_type=F32)


def _dot_nt(a, b):
    return lax.dot_general(a, b, (((1,), (1,)), ((), ())), preferred_element_type=F32)


def _dot_tn(a, b):
    return lax.dot_general(a, b, (((0,), (0,)), ((), ())), preferred_element_type=F32)


def _mod_kernel(s_ref, w_ref, b_ref, o_ref):
    tn = w_ref.shape[1]
    nrep = tn // LANES

    def body(r, acc):
        a0, a1 = acc
        rows = pl.ds(pl.multiple_of(r * SUBLANES, SUBLANES), SUBLANES)
        w = w_ref[rows, :]
        s0 = _silu(s_ref[0, rows, :])
        s1 = _silu(s_ref[1, rows, :])
        a0 = a0 + w * jnp.concatenate([s0] * nrep, axis=1)
        a1 = a1 + w * jnp.concatenate([s1] * nrep, axis=1)
        return a0, a1

    zero = jnp.zeros((SUBLANES, tn), F32)
    a0, a1 = lax.fori_loop(0, D // SUBLANES, body, (zero, zero), unroll=4)
    o_ref[0:1, :] = jnp.sum(a0, axis=0, keepdims=True) + b_ref[...]
    o_ref[1:2, :] = jnp.sum(a1, axis=0, keepdims=True) + b_ref[...]


def _mod_vectors(c, c_ctx, mod_w, mod_b):
    depth = mod_w.shape[0]
    tn = 1024
    s = jnp.broadcast_to(jnp.stack([c[0], c_ctx])[:, :, None], (2, D, LANES))
    return pl.pallas_call(
        _mod_kernel,
        grid=(depth, 6 * D // tn),
        in_specs=[
            pl.BlockSpec((2, D, LANES), lambda l, j: (0, 0, 0)),
            pl.BlockSpec((None, D, tn), lambda l, j: (l, 0, j)),
            pl.BlockSpec((None, 1, tn), lambda l, j: (l, 0, j)),
        ],
        out_specs=pl.BlockSpec((None, 2, tn), lambda l, j: (l, 0, j)),
        out_shape=jax.ShapeDtypeStruct((depth, 2, 6 * D), F32),
        compiler_params=_params("arbitrary", "arbitrary"),
        name="mod_vectors",
    )(s, mod_w, mod_b.reshape(depth, 1, 6 * D))


def _mod_row(mod_ref, kind, col):
    return mod_ref[pl.ds(kind, 1), col * D:(col + 1) * D]


def _ada_norm(xv, g, shift, scale):
    ms = jnp.mean(xv * xv, axis=-1, keepdims=True)
    return (xv * lax.rsqrt(ms + EPS) * g) * (1.0 + scale) + shift


def _prep0_kernel(x_ref, ctx_ref, er_ref, ec_ref, mod_ref, g_ref, xo_ref, ho_ref):
    i = pl.program_id(0)

    @pl.when(i == 0)
    def _():
        xo_ref[...] = ctx_ref[...]

    @pl.when(i > 0)
    def _():
        r0 = (i - 1) * (ROWS // GRID_W)
        for j in range(ROWS // GRID_W):
            rs = slice(j * GRID_W, (j + 1) * GRID_W)
            xo_ref[rs, 0:D // 2] = x_ref[rs, 0:D // 2] + er_ref[pl.ds(r0 + j, 1), :]
            xo_ref[rs, D // 2:D] = x_ref[rs, D // 2:D] + ec_ref[...]

    kind = jnp.where(i == 0, 1, 0)
    h = _ada_norm(xo_ref[...], g_ref[...], _mod_row(mod_ref, kind, 0), _mod_row(mod_ref, kind, 1))
    ho_ref[...] = h.astype(BF16)


def _prep0(x, ctx, emb_r, emb_c, mod0, g):
    nblk = N_ALL // ROWS
    return pl.pallas_call(
        _prep0_kernel,
        grid=(nblk,),
        in_specs=[
            pl.BlockSpec((ROWS, D), lambda i: (jnp.maximum(i - 1, 0), 0)),
            pl.BlockSpec((ROWS, D), lambda i: (0, 0)),
            pl.BlockSpec(emb_r.shape, lambda i: (0, 0)),
            pl.BlockSpec(emb_c.shape, lambda i: (0, 0)),
            pl.BlockSpec((2, 6 * D), lambda i: (0, 0)),
            pl.BlockSpec((1, D), lambda i: (0, 0)),
        ],
        out_specs=[pl.BlockSpec((ROWS, D), lambda i: (i, 0)), pl.BlockSpec((ROWS, D), lambda i: (i, 0))],
        out_shape=[jax.ShapeDtypeStruct((N_ALL, D), F32), jax.ShapeDtypeStruct((N_ALL, D), BF16)],
        compiler_params=_params("arbitrary"),
        name="embed_norm",
    )(x, ctx, emb_r, emb_c, mod0, g)


def _route(hf32, rwt_ref, rb_ref, eo_ref, wo_ref):
    logits = lax.dot_general(rwt_ref[...], hf32, (((1,), (1,)), ((), ())),
                             precision=HIGHEST, preferred_element_type=F32)
    s = jax.nn.sigmoid(logits)
    sel = s + rb_ref[...]
    row = lax.broadcasted_iota(jnp.int32, (GRP, ROWS), 0)
    neg = jnp.float32(-jnp.inf)
    gs, i1s, i2s = [], [], []
    for g in range(N_GRP):
        blk = sel[g * GRP:(g + 1) * GRP, :]
        m1 = jnp.max(blk, axis=0, keepdims=True)
        i1 = jnp.min(jnp.where(blk == m1, row, GRP), axis=0, keepdims=True)
        blk2 = jnp.where(row == i1, neg, blk)
        m2 = jnp.max(blk2, axis=0, keepdims=True)
        i2 = jnp.min(jnp.where(blk2 == m2, row, GRP), axis=0, keepdims=True)
        gs.append(m1 + m2)
        i1s.append(i1)
        i2s.append(i2)
    best, gi, i1, i2 = gs[0], jnp.zeros((1, ROWS), jnp.int32), i1s[0], i2s[0]
    for g in range(1, N_GRP):
        upd = gs[g] > best
        best = jnp.where(upd, gs[g], best)
        gi = jnp.where(upd, g, gi)
        i1 = jnp.where(upd, i1s[g], i1)
        i2 = jnp.where(upd, i2s[g], i2)
    e1 = gi * GRP + i1
    e2 = gi * GRP + i2
    erow = lax.broadcasted_iota(jnp.int32, (N_EXP, ROWS), 0)
    s1 = jnp.sum(jnp.where(erow == e1, s, 0.0), axis=0, keepdims=True)
    s2 = jnp.sum(jnp.where(erow == e2, s, 0.0), axis=0, keepdims=True)
    tot = s1 + s2
    eo_ref[0:1, :] = e1
    eo_ref[1:2, :] = e2
    wo_ref[0:1, :] = s1 / tot
    wo_ref[1:2, :] = s2 / tot


def _combine_wait(yb_hbm, ybuf, sem, slot):
    pltpu.make_async_copy(yb_hbm.at[pl.ds(0, 2 * ROWS), :], ybuf.at[slot], sem.at[slot]).wait()


def _combine_issue(dest_ref, yb_hbm, ybuf, sem, slot):
    def body(r, carry):
        for k in range(2):
            row = dest_ref[0, k, r]
            pltpu.make_async_copy(yb_hbm.at[pl.ds(row, 1), :], ybuf.at[slot, pl.ds(k * ROWS + r, 1), :],
                                  sem.at[slot]).start()
        return carry

    lax.fori_loop(0, ROWS, body, 0, unroll=8)


def _token_kernel(*refs, gate_col, shift_col, scale_col, ctx_blocks, route, combine, final):
    refs = list(refs)
    if combine:
        dcur_ref, dnxt_ref = refs.pop(0), refs.pop(0)
    x_ref, y_ref = refs.pop(0), refs.pop(0)
    if combine:
        wc_ref = refs.pop(0)
    modg_ref, modn_ref, g_ref, rwt_ref, rb_ref = (refs.pop(0) for _ in range(5))
    i = pl.program_id(0)
    if combine:
        ybuf, sem = refs[-2], refs[-1]
        slot = lax.rem(i, 2)

        @pl.when(i == 0)
        def _():
            _combine_issue(dcur_ref, y_ref, ybuf, sem, 0)

        @pl.when(i + 1 < pl.num_programs(0))
        def _():
            _combine_issue(dnxt_ref, y_ref, ybuf, sem, 1 - slot)

        _combine_wait(y_ref, ybuf, sem, slot)
        y = wc_ref[:, 0:1] * ybuf[slot, 0:ROWS, :] + wc_ref[:, 1:2] * ybuf[slot, ROWS:2 * ROWS, :]
    else:
        y = y_ref[...]
    kind = jnp.where(i < ctx_blocks, 1, 0)
    xn = x_ref[...] + _mod_row(modg_ref, kind, gate_col) * y
    if final:
        ms = jnp.mean(xn * xn, axis=-1, keepdims=True)
        refs[0][...] = xn * lax.rsqrt(ms + EPS) * g_ref[...]
        return
    xo_ref, ho_ref = refs[0], refs[1]
    xo_ref[...] = xn
    h = _ada_norm(xn, g_ref[...], _mod_row(modn_ref, kind, shift_col), _mod_row(modn_ref, kind, scale_col))
    ho_ref[...] = h.astype(ho_ref.dtype)
    if route:
        _route(h, rwt_ref, rb_ref, refs[2], refs[3])


def _token_stage(x, y, modg, modn, g, rwt, rb, *, gate_col, shift_col=0, scale_col=0, x_off=0, ctx_blocks=0,
                 route=False, combine=None, final=False):
    n = x.shape[0] - x_off * ROWS
    nblk = n // ROWS
    row = lambda i: (i, 0)
    const2 = lambda i: (0, 0)
    in_specs, args, scratch = [], [], []
    if combine is not None:
        dest, wc = combine
        dest3 = dest.reshape(2, nblk, ROWS).transpose(1, 0, 2)
        in_specs += [pl.BlockSpec((1, 2, ROWS), lambda i: (i, 0, 0), memory_space=pltpu.SMEM),
                     pl.BlockSpec((1, 2, ROWS), lambda i: (jnp.minimum(i + 1, nblk - 1), 0, 0),
                                  memory_space=pltpu.SMEM)]
        args += [dest3, dest3]
    in_specs.append(pl.BlockSpec((ROWS, D), lambda i: (i + x_off, 0)))
    args.append(x)
    if combine is not None:
        in_specs += [pl.BlockSpec(memory_space=pl.ANY), pl.BlockSpec((ROWS, 2), row)]
        args += [y, wc]
        scratch = [pltpu.VMEM((2, 2 * ROWS, D), F32), pltpu.SemaphoreType.DMA((2,))]
    else:
        in_specs.append(pl.BlockSpec((ROWS, D), row))
        args.append(y)
    in_specs += [pl.BlockSpec((2, 6 * D), const2), pl.BlockSpec((2, 6 * D), const2), pl.BlockSpec((1, D), const2),
                 pl.BlockSpec((N_EXP, D), const2), pl.BlockSpec((N_EXP, 1), const2)]
    args += [modg, modn, g, rwt, rb]
    if final:
        out_specs = [pl.BlockSpec((ROWS, D), row)]
        out_shape = [jax.ShapeDtypeStruct((n, D), F32)]
    else:
        out_specs = [pl.BlockSpec((ROWS, D), row), pl.BlockSpec((ROWS, D), row)]
        out_shape = [jax.ShapeDtypeStruct((n, D), F32), jax.ShapeDtypeStruct((n, D), F32 if route else BF16)]
        if route:
            out_specs += [pl.BlockSpec((2, ROWS), lambda i: (0, i)), pl.BlockSpec((2, ROWS), lambda i: (0, i))]
            out_shape += [jax.ShapeDtypeStruct((2, n), jnp.int32), jax.ShapeDtypeStruct((2, n), F32)]
    kern = functools.partial(_token_kernel, gate_col=gate_col, shift_col=shift_col, scale_col=scale_col,
                             ctx_blocks=ctx_blocks, route=route, combine=combine is not None, final=final)
    return pl.pallas_call(
        kern,
        grid=(nblk,),
        in_specs=in_specs,
        out_specs=out_specs,
        out_shape=out_shape,
        scratch_shapes=scratch,
        compiler_params=_params("arbitrary"),
        name="token_stage",
    )(*args)


def _mm_kernel(*refs, nx):
    x_refs, w_refs, o_ref, wb_refs = refs[:nx], refs[nx:2 * nx], refs[2 * nx], refs[2 * nx + 1:]

    @pl.when(pl.program_id(1) == 0)
    def _():
        for w_ref, wb_ref in zip(w_refs, wb_refs):
            wb_ref[...] = w_ref[...].astype(BF16)

    acc = _dot(x_refs[0][...], wb_refs[0][...])
    for x_ref, wb_ref in zip(x_refs[1:], wb_refs[1:]):
        acc = acc + _dot(x_ref[...], wb_ref[...])
    o_ref[...] = acc


def _mm(xs, w, *, ncols, tm, tn, x_off=0):
    nx = len(xs)
    k = xs[0].shape[1]
    m = xs[0].shape[0] - x_off * tm
    in_specs = [pl.BlockSpec((tm, k), lambda j, i: (i + x_off, 0)) for _ in xs]
    in_specs += [pl.BlockSpec((k, tn), functools.partial(lambda j, i, kk: (kk, j), kk=kk)) for kk in range(nx)]
    return pl.pallas_call(
        functools.partial(_mm_kernel, nx=nx),
        grid=(ncols // tn, m // tm),
        in_specs=in_specs,
        out_specs=pl.BlockSpec((tm, tn), lambda j, i: (i, j)),
        out_shape=jax.ShapeDtypeStruct((m, ncols), F32),
        scratch_shapes=[pltpu.VMEM((k, tn), BF16) for _ in xs],
        compiler_params=_params("arbitrary", "arbitrary"),
        name="proj",
    )(*xs, *([w] * nx))


def _conv_block(cur, prev8, next8, w_ref, b_ref, i, nblk):
    ctx_edge = T_CTX // ROWS
    keep_prev = jnp.logical_and(i != 0, i != ctx_edge)
    keep_next = jnp.logical_and(i != nblk - 1, i != ctx_edge - 1)
    prev8 = jnp.where(keep_prev, prev8, 0.0)
    next8 = jnp.where(keep_next, next8, 0.0)
    ext = jnp.concatenate([prev8, cur, next8], axis=0)
    base = SUBLANES - CONV_W // 2
    y = b_ref[...] + w_ref[0:1, :] * ext[base:base + ROWS]
    for j in range(1, CONV_W):
        y = y + w_ref[j:j + 1, :] * ext[base + j:base + j + ROWS]
    return y


def _halo_specs(width, col_blk, nblk):
    per = ROWS // SUBLANES
    last8 = N_ALL // SUBLANES - 1
    return [
        pl.BlockSpec((ROWS, width), lambda i, *a: (i, col_blk(*a))),
        pl.BlockSpec((SUBLANES, width), lambda i, *a: (jnp.maximum(i * per - 1, 0), col_blk(*a))),
        pl.BlockSpec((SUBLANES, width), lambda i, *a: (jnp.minimum((i + 1) * per, last8), col_blk(*a))),
    ]


def _scan_tiles(a_ref, b_ref, h_ref, carry_ref, reverse):
    ntile = ROWS // SUBLANES
    row = lax.broadcasted_iota(jnp.int32, (SUBLANES, LRU_W), 0)

    def body(t, carry):
        tt = (ntile - 1 - t) if reverse else t
        rows = pl.ds(pl.multiple_of(tt * SUBLANES, SUBLANES), SUBLANES)
        a = a_ref[rows, :]
        b = b_ref[rows, :]
        for d in (1, 2, 4):
            shift = (SUBLANES - d) if reverse else d
            a_sh = pltpu.roll(a, shift, axis=0)
            b_sh = pltpu.roll(b, shift, axis=0)
            m = (row < SUBLANES - d) if reverse else (row >= d)
            b = jnp.where(m, a * b_sh, 0.0) + b
            a = jnp.where(m, a * a_sh, a)
        h = a * carry + b
        h_ref[rows, :] = h
        edge = h[0:1, :] if reverse else h[SUBLANES - 1:SUBLANES, :]
        return jnp.broadcast_to(edge, (SUBLANES, LRU_W))

    carry_ref[...] = lax.fori_loop(0, ntile, body, carry_ref[...])


def _lru_fwd_kernel(xa_ref, prev_ref, next_ref, cw_ref, cb_ref, wbd_ref, ba_ref, bx_ref, lam_ref,
                    hf_ref, a1_ref, b1_ref, carry_ref, a0_ref, b0_ref):
    i = pl.program_id(0)
    nblk = pl.num_programs(0)

    @pl.when(i == 0)
    def _():
        carry_ref[...] = jnp.zeros_like(carry_ref)

    xa = _conv_block(xa_ref[...], prev_ref[...], next_ref[...], cw_ref, cb_ref, i, nblk)
    xab = xa.astype(BF16)
    nsp = -LRU_C * _softplus(-lam_ref[...])
    gw = wbd_ref.shape[1]
    for g in range(LRU_W // gw):
        cs = slice(g * gw, (g + 1) * gw)
        z = _dot(xab[:, cs], wbd_ref[g])
        for d in range(2):
            r = jax.nn.sigmoid(z[:, d * gw:(d + 1) * gw] + ba_ref[d:d + 1, cs])
            ig = jax.nn.sigmoid(z[:, (2 + d) * gw:(3 + d) * gw] + bx_ref[d:d + 1, cs])
            log_a = r * nsp[d:d + 1, cs]
            a = jnp.exp(log_a)
            b = jnp.sqrt(-jnp.tanh(log_a) * (a * a + 1.0)) * ig * xa[:, cs]
            if d == 0:
                a0_ref[:, cs] = a
                b0_ref[:, cs] = b
            else:
                a1_ref[:, cs] = a
                b1_ref[:, cs] = b
    _scan_tiles(a0_ref, b0_ref, hf_ref, carry_ref, reverse=False)


def _gelu_tanh(v):
    return 0.5 * v * (1.0 + jnp.tanh(math.sqrt(2.0 / math.pi) * (v + 0.044715 * (v * v * v))))


def _lru_bwd_kernel(a1_ref, b1_ref, hf_ref, ga_ref, ya_ref, carry_ref, hb_ref):
    @pl.when(pl.program_id(0) == 0)
    def _():
        carry_ref[...] = jnp.zeros_like(carry_ref)

    _scan_tiles(a1_ref, b1_ref, hb_ref, carry_ref, reverse=True)
    ya_ref[...] = ((hf_ref[...] + hb_ref[...]) * _gelu_tanh(ga_ref[...])).astype(BF16)


def _rev_block(i, nblk):
    nctx = T_CTX // ROWS
    return jnp.where(i < nctx, nctx - 1 - i, nblk - 1 - (i - nctx))


def _lru(p0, conv_w, conv_b, wbd, ba, bx, lam):
    nblk = N_ALL // ROWS
    full = lambda shape: pl.BlockSpec(shape, lambda i: (0,) * len(shape))
    hf, a1, b1 = pl.pallas_call(
        _lru_fwd_kernel,
        grid=(nblk,),
        in_specs=_halo_specs(LRU_W, lambda: 0, nblk) + [
            full((CONV_W, LRU_W)), full((1, LRU_W)), full(wbd.shape),
            full((2, LRU_W)), full((2, LRU_W)), full((2, LRU_W)),
        ],
        out_specs=[pl.BlockSpec((ROWS, LRU_W), lambda i: (i, 0))] * 3,
        out_shape=[jax.ShapeDtypeStruct((N_ALL, LRU_W), F32)] * 3,
        scratch_shapes=[pltpu.VMEM((SUBLANES, LRU_W), F32), pltpu.VMEM((ROWS, LRU_W), F32),
                        pltpu.VMEM((ROWS, LRU_W), F32)],
        compiler_params=_params("arbitrary"),
        name="lru_fwd",
    )(p0, p0, p0, conv_w, conv_b, wbd, ba, bx, lam)
    rev = lambda i: (_rev_block(i, nblk), 0)
    ya = pl.pallas_call(
        _lru_bwd_kernel,
        grid=(nblk,),
        in_specs=[pl.BlockSpec((ROWS, LRU_W), rev)] * 3
        + [pl.BlockSpec((ROWS, LRU_W), lambda i: (_rev_block(i, nblk), 1))],
        out_specs=pl.BlockSpec((ROWS, LRU_W), rev),
        out_shape=jax.ShapeDtypeStruct((N_ALL, LRU_W), BF16),
        scratch_shapes=[pltpu.VMEM((SUBLANES, LRU_W), F32), pltpu.VMEM((ROWS, LRU_W), F32)],
        compiler_params=_params("arbitrary"),
        name="lru_bwd",
    )(a1, b1, hf, p0)
    return ya


def _tri(reverse):
    r = lax.broadcasted_iota(jnp.int32, (CHUNK, CHUNK), 0)
    c = lax.broadcasted_iota(jnp.int32, (CHUNK, CHUNK), 1)
    return (c >= r) if reverse else (c <= r)


def _gla_kernel(*refs, reverse):
    if reverse:
        q_ref, k_ref, v_ref, ad_ref, wg_ref, bg_ref, of_ref, og_ref, gn_ref, o_ref, s_ref = refs
    else:
        q_ref, k_ref, v_ref, ad_ref, wg_ref, bg_ref, o_ref, s_ref = refs
    d = 1 if reverse else 0

    @pl.when(pl.program_id(0) == 0)
    def _():
        s_ref[...] = jnp.zeros_like(s_ref)

    causal = _tri(reverse)
    tri_f = causal.astype(F32)
    lane = lax.broadcasted_iota(jnp.int32, (1, 2 * GLA_DK), 1)
    eye = (lax.broadcasted_iota(jnp.int32, (2 * GLA_DK, 2 * GLA_DK), 0)
           == lax.broadcasted_iota(jnp.int32, (2 * GLA_DK, 2 * GLA_DK), 1)).astype(F32)
    nch = ROWS // CHUNK
    for cc in range(nch):
        c = (nch - 1 - cc) if reverse else cc
        rs = slice(c * CHUNK, (c + 1) * CHUNK)
        ad = ad_ref[rs, d * GLA_RANK:(d + 1) * GLA_RANK]
        z = jnp.dot(ad, wg_ref[...], precision=HIGHEST, preferred_element_type=F32) + bg_ref[...]
        lg = -_softplus(-z) * (1.0 / GLA_TAU)
        cum = jnp.dot(tri_f, lg, precision=HIGHEST, preferred_element_type=F32)
        last = cum[0:1, :] if reverse else cum[CHUNK - 1:CHUNK, :]
        q_dec = q_ref[rs, :] * (GLA_DK ** -0.5) * jnp.exp(cum)
        k = k_ref[rs, :]
        k_inv = (k * jnp.exp(-cum)).astype(BF16)
        k_end = (k * jnp.exp(last - cum)).astype(BF16)
        decay = jnp.exp(last)
        for p in range(GLA_H // 2):
            ls = slice(p * 2 * GLA_DK, (p + 1) * 2 * GLA_DK)
            qd = q_dec[:, ls]
            vp = v_ref[rs, p * 2 * GLA_DV:(p + 1) * 2 * GLA_DV].astype(BF16)
            sp = s_ref[p]
            spb = sp.astype(BF16)
            for hh in range(2):
                head = 2 * p + hh
                hs = slice(head * GLA_DV, (head + 1) * GLA_DV)
                in_head = jnp.logical_and(lane >= hh * GLA_DK, lane < (hh + 1) * GLA_DK)
                qm = jnp.where(in_head, qd, 0.0).astype(BF16)
                sc = jnp.where(causal, _dot_nt(qm, k_inv[:, ls]), 0.0)
                o = _dot(sc.astype(BF16), vp[:, hh * GLA_DV:(hh + 1) * GLA_DV])
                o = o + _dot(qm, spb[:, hh * GLA_DV:(hh + 1) * GLA_DV])
                if reverse:
                    tot = o + of_ref[rs, hs]
                    ms = jnp.mean(tot * tot, axis=-1, keepdims=True)
                    y = tot * lax.rsqrt(ms + EPS) * gn_ref[:, hs] * _silu(og_ref[rs, hs])
                    o_ref[rs, hs] = y.astype(BF16)
                else:
                    o_ref[rs, hs] = o
            upd = _dot_tn(k_end[:, ls], vp)
            dcol = jnp.sum(eye * decay[:, ls], axis=1, keepdims=True)
            s_ref[p] = dcol * sp + upd


def _gla(p0, pad, wg, bg, gnorm, of=None):
    reverse = of is not None
    nblk = N_ALL // ROWS
    blk = (lambda i: _rev_block(i, nblk)) if reverse else (lambda i: i)
    d = 1 if reverse else 0
    qk_blk = 2 * LRU_W // GLA_QK
    in_specs = [
        pl.BlockSpec((ROWS, GLA_QK), lambda i: (blk(i), qk_blk)),
        pl.BlockSpec((ROWS, GLA_QK), lambda i: (blk(i), qk_blk + 1)),
        pl.BlockSpec((ROWS, GLA_V), lambda i: (blk(i), 3)),
        pl.BlockSpec((ROWS, LANES), lambda i: (blk(i), 0)),
        pl.BlockSpec((None, GLA_RANK, GLA_QK), lambda i: (d, 0, 0)),
        pl.BlockSpec((None, 1, GLA_QK), lambda i: (d, 0, 0)),
    ]
    args = [p0, p0, p0, pad, wg, bg.reshape(2, 1, GLA_QK)]
    if reverse:
        in_specs += [
            pl.BlockSpec((ROWS, GLA_V), lambda i: (blk(i), 0)),
            pl.BlockSpec((ROWS, GLA_V), lambda i: (blk(i), 4)),
            pl.BlockSpec((1, GLA_V), lambda i: (0, 0)),
        ]
        args += [of, p0, gnorm]
    return pl.pallas_call(
        functools.partial(_gla_kernel, reverse=reverse),
        grid=(nblk,),
        in_specs=in_specs,
        out_specs=pl.BlockSpec((ROWS, GLA_V), lambda i: (blk(i), 0)),
        out_shape=jax.ShapeDtypeStruct((N_ALL, GLA_V), BF16 if reverse else F32),
        scratch_shapes=[pltpu.VMEM((GLA_H // 2, 2 * GLA_DK, 2 * GLA_DV), F32)],
        compiler_params=_params("arbitrary"),
        name="gla_bwd" if reverse else "gla_fwd",
    )(*args)


def _conv_silu_kernel(x_ref, prev_ref, next_ref, w_ref, b_ref, o_ref):
    i = pl.program_id(0)
    y = _conv_block(x_ref[...], prev_ref[...], next_ref[...], w_ref, b_ref, i, pl.num_programs(0))
    o_ref[...] = _silu(y)


def _conv_silu(p1, conv_w, conv_b):
    nblk = N_ALL // ROWS
    tc = 2048
    off = SSD_INNER // tc
    return pl.pallas_call(
        _conv_silu_kernel,
        grid=(nblk, SSD_XBC // tc),
        in_specs=_halo_specs(tc, lambda j: j + off, nblk) + [
            pl.BlockSpec((CONV_W, tc), lambda i, j: (0, j)),
            pl.BlockSpec((1, tc), lambda i, j: (0, j)),
        ],
        out_specs=pl.BlockSpec((ROWS, tc), lambda i, j: (i, j)),
        out_shape=jax.ShapeDtypeStruct((N_ALL, SSD_XBC), F32),
        compiler_params=_params("arbitrary", "arbitrary"),
        name="ssd_conv",
    )(p1, p1, p1, conv_w, conv_b)


def _ssd_kernel(*refs, reverse):
    if reverse:
        (xs_ref, b_ref, c_ref, dt_ref, dtb_ref, alog_ref, acc_ref, z_ref, dsk_ref, gn_ref, y_ref,
         st_ref, cum_s, w2_s, ct2_s, dt2_s, ybuf) = refs
    else:
        xs_ref, b_ref, c_ref, dt_ref, dtb_ref, alog_ref, y_ref, st_ref, cum_s, w2_s, ct2_s, dt2_s = refs
    d = 1 if reverse else 0
    i = pl.program_id(0)

    @pl.when(i == 0)
    def _():
        st_ref[...] = jnp.zeros_like(st_ref)

    tri_f = _tri(reverse).astype(F32)
    row2 = lax.broadcasted_iota(jnp.int32, (CHUNK, 2 * SSD_P), 0)
    col2 = jnp.bitwise_and(lax.broadcasted_iota(jnp.int32, (CHUNK, 2 * SSD_P), 1), SSD_P - 1)
    causal2 = (col2 >= row2) if reverse else (col2 <= row2)
    low = lax.broadcasted_iota(jnp.int32, (1, 2 * SSD_P), 1) < SSD_P
    a_neg = -jnp.exp(alog_ref[...])
    nch = ROWS // CHUNK
    hg = SSD_H // SSD_G
    pw = 2 * SSD_P

    def pair(v, h0):
        return jnp.where(low, v[:, h0:h0 + 1], v[:, h0 + 1:h0 + 2])

    dtv = _softplus(dt_ref[:, d * SSD_H:(d + 1) * SSD_H] + dtb_ref[...])
    la = dtv * a_neg
    for c in range(nch):
        rs = slice(c * CHUNK, (c + 1) * CHUNK)
        cum = jnp.dot(tri_f, la[rs], precision=HIGHEST, preferred_element_type=F32)
        last = cum[0:1, :] if reverse else cum[CHUNK - 1:CHUNK, :]
        cum_s[c] = cum
        w2_s[c] = dtv[rs] * jnp.exp(last - cum)
        cum_t = cum.T
        dt_t = dtv[rs].T
        ct2_s[c] = jnp.concatenate([cum_t, cum_t], axis=1)
        dt2_s[c] = jnp.concatenate([dt_t, dt_t], axis=1)

    def chunk(cc, carry):
        c = (nch - 1 - cc) if reverse else cc
        rs = pl.ds(pl.multiple_of(c * CHUNK, CHUNK), CHUNK)
        cum = cum_s[c]
        w2 = w2_s[c]
        ct2 = ct2_s[c]
        dt2 = dt2_s[c]
        dec = jnp.exp(cum[0:1, :] if reverse else cum[CHUNK - 1:CHUNK, :])
        for g in range(SSD_G):
            gs = slice(g * SSD_Z, (g + 1) * SSD_Z)
            cg = c_ref[rs, gs].astype(BF16)
            bgb = b_ref[rs, gs].astype(BF16)
            cb = _dot_nt(cg, bgb)
            cb2 = jnp.concatenate([cb, cb], axis=1)
            st = st_ref[g]
            y_inter = _dot(cg, st.astype(BF16))
            x2s, decs = [], []
            for pr in range(hg // 2):
                h0 = g * hg + 2 * pr
                ps = slice(h0 * SSD_P, (h0 + 2) * SSD_P)
                xp = xs_ref[rs, ps]
                ccol = pair(cum, h0)
                rrow = jnp.where(low, ct2[h0:h0 + 1, :], ct2[h0 + 1:h0 + 2, :])
                drow = jnp.where(low, dt2[h0:h0 + 1, :], dt2[h0 + 1:h0 + 2, :])
                seg = jnp.exp(jnp.where(causal2, ccol - rrow, -jnp.inf))
                m = (cb2 * seg * drow).astype(BF16)
                xbd = jnp.concatenate([jnp.where(low, xp, 0.0), jnp.where(low, 0.0, xp)], axis=0).astype(BF16)
                y = _dot(m, xbd) + y_inter[:, pr * pw:(pr + 1) * pw] * jnp.exp(ccol)
                if reverse:
                    ybuf[:, ps] = y + acc_ref[rs, ps]
                else:
                    y_ref[rs, ps] = y
                x2s.append((xp * pair(w2, h0)).astype(BF16))
                decs.append(pair(dec, h0))
            upd = _dot_tn(bgb, jnp.concatenate(x2s, axis=1))
            st_ref[g] = st * jnp.concatenate(decs, axis=1) + upd
        if reverse:
            u = (ybuf[...] + dsk_ref[...] * xs_ref[rs, :]) * _silu(z_ref[rs, :])
            ms = jnp.mean(u * u, axis=-1, keepdims=True)
            y_ref[rs, :] = (u * lax.rsqrt(ms + EPS) * gn_ref[...]).astype(BF16)
        return carry

    lax.fori_loop(0, nch, chunk, 0)


def _ssd(xbc, pdt, dt_bias, a_log, fin=None):
    reverse = fin is not None
    nblk = N_ALL // ROWS
    nctx = T_CTX // ROWS
    blk = (lambda i: _rev_block(i, nblk)) if reverse else (lambda i: i)
    d = 1 if reverse else 0
    gz = SSD_G * SSD_Z
    full = lambda i: (blk(i), 0)
    in_specs = [
        pl.BlockSpec((ROWS, SSD_INNER), full),
        pl.BlockSpec((ROWS, gz), lambda i: (blk(i), SSD_INNER // gz)),
        pl.BlockSpec((ROWS, gz), lambda i: (blk(i), SSD_INNER // gz + 1)),
        pl.BlockSpec((ROWS, LANES), full),
        pl.BlockSpec((None, 1, SSD_H), lambda i: (d, 0, 0)),
        pl.BlockSpec((None, 1, SSD_H), lambda i: (d, 0, 0)),
    ]
    args = [xbc, xbc, xbc, pdt, dt_bias.reshape(2, 1, SSD_H), a_log.reshape(2, 1, SSD_H)]
    nch = ROWS // CHUNK
    scratch = [pltpu.VMEM((SSD_G, SSD_Z, SSD_INNER // SSD_G), F32),
               pltpu.VMEM((nch, CHUNK, SSD_H), F32), pltpu.VMEM((nch, CHUNK, SSD_H), F32),
               pltpu.VMEM((nch, SSD_H, 2 * CHUNK), F32), pltpu.VMEM((nch, SSD_H, 2 * CHUNK), F32)]
    if reverse:
        yf, p1, d_rep, norm_g = fin
        in_specs += [pl.BlockSpec((ROWS, SSD_INNER), full), pl.BlockSpec((ROWS, SSD_INNER), full),
                     pl.BlockSpec((1, SSD_INNER), lambda i: (0, 0)), pl.BlockSpec((1, SSD_INNER), lambda i: (0, 0))]
        args += [yf, p1, d_rep, norm_g]
        scratch.append(pltpu.VMEM((CHUNK, SSD_INNER), F32))
        out_spec = pl.BlockSpec((ROWS, SSD_INNER), lambda i: (blk(jnp.maximum(i, nctx)) - nctx, 0))
        out_shape = jax.ShapeDtypeStruct((T_LAT, SSD_INNER), BF16)
    else:
        out_spec = pl.BlockSpec((ROWS, SSD_INNER), full)
        out_shape = jax.ShapeDtypeStruct((N_ALL, SSD_INNER), F32)
    return pl.pallas_call(
        functools.partial(_ssd_kernel, reverse=reverse),
        grid=(nblk,),
        in_specs=in_specs,
        out_specs=out_spec,
        out_shape=out_shape,
        scratch_shapes=scratch,
        compiler_params=_params("arbitrary"),
        name="ssd_bwd" if reverse else "ssd_fwd",
    )(*args)


def _gather_wait(h_hbm, buf, sem, slot):
    pltpu.make_async_copy(h_hbm.at[pl.ds(0, MOE_TM), :], buf.at[slot], sem.at[slot]).wait()


def _gather_issue(idx_ref, h_hbm, buf, sem, slot):
    def body(r, carry):
        tok = idx_ref[0, 0, r]
        pltpu.make_async_copy(h_hbm.at[pl.ds(tok, 1), :], buf.at[slot, pl.ds(r, 1), :], sem.at[slot]).start()
        return carry

    lax.fori_loop(0, MOE_TM, body, 0, unroll=8)


def _expert_block_kernel(be_ref, ws_ref, nx_ref, nu_ref, *refs, n_w, layer, gather, compute):
    if gather:
        icur_ref, inxt_ref, h_hbm = refs[:3]
        refs = refs[3:]
    else:
        x_ref = refs[0]
        refs = refs[1:]
    w_hbm, o_ref = refs[:n_w], refs[n_w]
    scr = refs[n_w + 1:]
    w32, wb, wsem = scr[:n_w], scr[n_w:2 * n_w], scr[2 * n_w]
    m = pl.program_id(0)
    nu = nu_ref[0]
    e = be_ref[m]
    wslot = ws_ref[m]
    first = jnp.logical_and(m < nu, jnp.logical_or(m == 0, e != be_ref[jnp.maximum(m - 1, 0)]))

    def w_copy(k, ee, slot):
        return pltpu.make_async_copy(w_hbm[k].at[layer, ee], w32[k].at[slot], wsem.at[k, slot])

    @pl.when(m == 0)
    def _():
        for k in range(n_w):
            w_copy(k, e, wslot).start()

    if gather:
        gbuf, gsem = scr[2 * n_w + 1:]
        gslot = lax.rem(m, 2)

        @pl.when(m == 0)
        def _():
            _gather_issue(icur_ref, h_hbm, gbuf, gsem, 0)

        @pl.when(m + 1 < nu)
        def _():
            _gather_issue(inxt_ref, h_hbm, gbuf, gsem, 1 - gslot)

    @pl.when(jnp.logical_and(first, nx_ref[m] >= 0))
    def _():
        for k in range(n_w):
            w_copy(k, nx_ref[m], 1 - wslot).start()

    @pl.when(first)
    def _():
        for k in range(n_w):
            w_copy(k, e, wslot).wait()
            wb[k][...] = w32[k][wslot].astype(BF16)

    @pl.when(m < nu)
    def _():
        if gather:
            _gather_wait(h_hbm, gbuf, gsem, gslot)
            x = gbuf[gslot].astype(BF16)
        else:
            x = x_ref[...]
        o_ref[...] = compute(x, *wb).astype(o_ref.dtype)


def _expert_stage(src, ws, plan, layer, n_out, out_dtype, compute, name, slot_tok=None):
    be, wsl, nx, nu = plan
    nblk = be.shape[0]
    gather = slot_tok is not None
    anyspec = pl.BlockSpec(memory_space=pl.ANY)
    used = lambda m, nu_ref: jnp.minimum(m, nu_ref[0] - 1)
    n_w = len(ws)
    wshape = ws[0].shape[2:]
    scratch = [pltpu.VMEM((2,) + wshape, F32) for _ in ws] + [pltpu.VMEM(wshape, BF16) for _ in ws]
    scratch.append(pltpu.SemaphoreType.DMA((n_w, 2)))
    if gather:
        in_specs = [
            pl.BlockSpec((1, 1, MOE_TM), lambda m, be, ws, nx, nu: (used(m, nu), 0, 0), memory_space=pltpu.SMEM),
            pl.BlockSpec((1, 1, MOE_TM), lambda m, be, ws, nx, nu: (used(m + 1, nu), 0, 0),
                         memory_space=pltpu.SMEM),
            anyspec,
        ]
        args = [slot_tok, slot_tok, src]
        scratch += [pltpu.VMEM((2, MOE_TM, src.shape[1]), src.dtype), pltpu.SemaphoreType.DMA((2,))]
    else:
        in_specs = [pl.BlockSpec((MOE_TM, src.shape[1]), lambda m, be, ws, nx, nu: (used(m, nu), 0))]
        args = [src]
    return pl.pallas_call(
        functools.partial(_expert_block_kernel, n_w=n_w, layer=layer, gather=gather, compute=compute),
        grid_spec=pltpu.PrefetchScalarGridSpec(
            num_scalar_prefetch=4,
            grid=(nblk,),
            in_specs=in_specs + [anyspec] * n_w,
            out_specs=pl.BlockSpec((MOE_TM, n_out), lambda m, be, ws, nx, nu: (used(m, nu), 0)),
            scratch_shapes=scratch,
        ),
        out_shape=jax.ShapeDtypeStruct((nblk * MOE_TM, n_out), out_dtype),
        compiler_params=_params("arbitrary"),
        name=name,
    )(be, wsl, nx, nu, *args, *ws)


def _moe(h, eidx, wts, w_gate, w_up, w_down, layer):
    n = h.shape[0]
    i32 = jnp.int32
    flat_e = eidx.reshape(-1)
    experts = jnp.arange(N_EXP, dtype=i32)
    onehot = (flat_e[:, None] == experts[None, :]).astype(i32)
    csum = jnp.cumsum(onehot, axis=0)
    bcount = (csum[-1] + MOE_TM - 1) // MOE_TM
    bend = jnp.cumsum(bcount)
    bstart = bend - bcount
    dest = jnp.sum(onehot * (csum - 1 + (bstart * MOE_TM)[None, :]), axis=1)
    nblk = -(-2 * n // MOE_TM) + N_EXP
    n_used = bend[-1].astype(i32)
    blk = jnp.minimum(jnp.arange(nblk, dtype=i32), n_used - 1)
    block_exp = jnp.sum((blk[:, None] >= bend[None, :]).astype(i32), axis=1)
    owns = bcount > 0
    wslot = (jnp.cumsum(owns.astype(i32)) - 1) % 2
    later = jnp.logical_and(owns[None, :], experts[None, :] > experts[:, None])
    nxt = jnp.min(jnp.where(later, experts[None, :], N_EXP), axis=1)
    nxt = jnp.where(nxt == N_EXP, -1, nxt)
    of_blk = (block_exp[:, None] == experts[None, :]).astype(i32)
    plan = (block_exp.astype(i32), jnp.sum(of_blk * wslot[None, :], axis=1).astype(i32),
            jnp.sum(of_blk * nxt[None, :], axis=1).astype(i32), n_used.reshape(1))
    tok = jnp.tile(jnp.arange(n, dtype=i32), 2)
    pad_tok = jnp.arange(nblk * MOE_TM, dtype=i32) % n
    slot_tok = pad_tok.at[dest].set(tok).reshape(nblk, 1, MOE_TM)
    hb = _expert_stage(h, [w_gate, w_up], plan, layer, D_EXP, BF16,
                       lambda x, wg, wu: _silu(_dot(x, wg[...])) * _dot(x, wu[...]), "moe_up", slot_tok=slot_tok)
    yb = _expert_stage(hb, [w_down], plan, layer, D, F32, lambda x, wd: _dot(x, wd[...]), "moe_down")
    return yb, dest.reshape(2, n), wts.T


def _pos_tables():
    quarter = D // 4
    omega = 1.0 / (10000.0 ** (jnp.arange(quarter, dtype=F32) / quarter))
    ang_r = jnp.arange(T_LAT // GRID_W, dtype=F32)[:, None] * omega
    ang_c = jnp.arange(GRID_W, dtype=F32)[:, None] * omega
    emb_r = jnp.concatenate([jnp.sin(ang_r), jnp.cos(ang_r)], axis=-1)
    emb_c = jnp.concatenate([jnp.sin(ang_c), jnp.cos(ang_c)], axis=-1)
    return emb_r, emb_c


def _block_diag_gates(wa, wx, group):
    nb = wa.shape[1]
    per = group // wa.shape[2]
    eye = jnp.eye(per, dtype=F32)

    def bd(w):
        w = w.reshape(nb // per, per, w.shape[1], w.shape[2])
        return jnp.einsum("gnkj,nm->gnkmj", w, eye).reshape(nb // per, group, group)

    return jnp.concatenate([bd(wa[0]), bd(wa[1]), bd(wx[0]), bd(wx[1])], axis=-1).astype(BF16)


def kernel(x, c, ctx, c_ctx, mod_w, mod_b, norm1_g, norm2_g, ev_w_in, ev_conv_w, ev_conv_b, lru_wa, lru_ba, lru_wx, lru_bx, lru_lambda, gla_wg_up, gla_bg, gla_norm_g, ev_w_out, od_w_in, od_conv_w, od_conv_b, ssd_a_log, ssd_dt_bias, ssd_d, ssd_norm_g, od_w_out, router_w, router_b, exp_w_gate, exp_w_up, exp_w_down, final_norm_g):
    mods = _mod_vectors(c, c_ctx, mod_w, mod_b)
    emb_r, emb_c = _pos_tables()
    rwt = router_w.T
    rb = router_b.reshape(N_EXP, 1)
    tm = N_ALL // 8

    x0, h0 = _prep0(x[0], ctx[0], emb_r, emb_c, mods[0], norm1_g[0:1])
    p0 = _mm([h0], ev_w_in[0], ncols=EVEN_MAIN, tm=tm, tn=1024)
    w_ad = jnp.pad(ev_w_in[0][:, EVEN_MAIN:], ((0, 0), (0, LANES - 2 * GLA_RANK)))
    pad = _mm([h0], w_ad, ncols=LANES, tm=tm, tn=LANES)
    wbd = _block_diag_gates(lru_wa[0], lru_wx[0], 256)
    ya = _lru(p0, ev_conv_w[0], ev_conv_b[0:1], wbd, lru_ba[0], lru_bx[0], lru_lambda[0])
    of = _gla(p0, pad, gla_wg_up[0], gla_bg[0], None)
    yb = _gla(p0, pad, gla_wg_up[0], gla_bg[0], gla_norm_g[0:1], of=of)
    y0 = _mm([ya, yb], ev_w_out[0], ncols=D, tm=tm, tn=1024)
    nctx = T_CTX // ROWS
    x1, h1, e0, w0 = _token_stage(x0, y0, mods[0], mods[0], norm2_g[0:1], rwt, rb, gate_col=2, shift_col=3,
                                  scale_col=4, ctx_blocks=nctx, route=True)
    f0, dest0, wc0 = _moe(h1, e0, w0, exp_w_gate, exp_w_up, exp_w_down, 0)
    x2, h2 = _token_stage(x1, f0, mods[0], mods[1], norm1_g[1:2], rwt, rb, gate_col=5, shift_col=0,
                          scale_col=1, ctx_blocks=nctx, combine=(dest0, wc0))

    p1 = _mm([h2], od_w_in[0], ncols=ODD_MAIN, tm=tm, tn=1024)
    pdt = _mm([h2], od_w_in[0][:, ODD_MAIN:], ncols=LANES, tm=tm, tn=LANES)
    xbc = _conv_silu(p1, od_conv_w[0], od_conv_b[0:1])
    yf = _ssd(xbc, pdt, ssd_dt_bias[0], ssd_a_log[0])
    d_rep = jnp.repeat(ssd_d[0], SSD_P).reshape(1, SSD_INNER)
    gy = _ssd(xbc, pdt, ssd_dt_bias[0], ssd_a_log[0], fin=(yf, p1, d_rep, ssd_norm_g[0:1]))
    y1 = _mm([gy], od_w_out[0], ncols=D, tm=1024, tn=512)
    x3, h3, e1, w1 = _token_stage(x2, y1, mods[1], mods[1], norm2_g[1:2], rwt, rb, gate_col=2, shift_col=3,
                                  scale_col=4, x_off=nctx, route=True)
    f1, dest1, wc1 = _moe(h3, e1, w1, exp_w_gate, exp_w_up, exp_w_down, 1)
    (out,) = _token_stage(x3, f1, mods[1], mods[1], final_norm_g.reshape(1, D), rwt, rb, gate_col=5,
                          combine=(dest1, wc1), final=True)
    return out[None]
```

```python
import functools
import math

import jax
import jax.numpy as jnp
from jax import lax
from jax.experimental import pallas as pl
from jax.experimental.pallas import tpu as pltpu

F32 = jnp.float32
BF16 = jnp.bfloat16
HIGHEST = lax.Precision.HIGHEST

D = 2048
T_LAT = 8192
T_CTX = 256
N_ALL = T_CTX + T_LAT
GRID_W = 64
EPS = 1e-6
CONV_W = 4
LRU_W = 1024
LRU_C = 8.0
GLA_H = 8
GLA_DK = 64
GLA_DV = 128
GLA_QK = GLA_H * GLA_DK
GLA_V = GLA_H * GLA_DV
GLA_RANK = 16
GLA_TAU = 16.0
CHUNK = 64
EVEN_MAIN = 2 * LRU_W + 2 * GLA_QK + 2 * GLA_V
SSD_INNER = 2 * D
SSD_P = 64
SSD_H = SSD_INNER // SSD_P
SSD_G = 8
SSD_Z = 128
SSD_XBC = SSD_INNER + 2 * SSD_G * SSD_Z
ODD_MAIN = SSD_INNER + SSD_XBC
N_EXP = 32
N_GRP = 4
GRP = N_EXP // N_GRP
D_EXP = 1024
MOE_TM = 256

ROWS = 256
LANES = 128
SUBLANES = 8
VMEM_LIMIT = 56 * 1024 * 1024


def _params(*sem):
    return pltpu.CompilerParams(dimension_semantics=sem, vmem_limit_bytes=VMEM_LIMIT)


def _silu(v):
    return v * jax.nn.sigmoid(v)


def _softplus(v):
    return jnp.maximum(v, 0.0) + jnp.log1p(jnp.exp(-jnp.abs(v)))


def _dot(a, b):
    return jnp.dot(a, b, preferred_element_type=F32)


def _dot_nt(a, b):
    return lax.dot_general(a, b, (((1,), (1,)), ((), ())), preferred_element_type=F32)


def _dot_tn(a, b):
    return lax.dot_general(a, b, (((0,), (0,)), ((), ())), preferred_element_type=F32)


def _mod_kernel(s_ref, w_ref, b_ref, o_ref):
    tn = w_ref.shape[1]
    nrep = tn // LANES

    def body(r, acc):
        a0, a1 = acc
        rows = pl.ds(pl.multiple_of(r * SUBLANES, SUBLANES), SUBLANES)
        w = w_ref[rows, :]
        s0 = _silu(s_ref[0, rows, :])
        s1 = _silu(s_ref[1, rows, :])
        a0 = a0 + w * jnp.concatenate([s0] * nrep, axis=1)
        a1 = a1 + w * jnp.concatenate([s1] * nrep, axis=1)
        return a0, a1

    zero = jnp.zeros((SUBLANES, tn), F32)
    a0, a1 = lax.fori_loop(0, D // SUBLANES, body, (zero, zero), unroll=4)
    o_ref[0:1, :] = jnp.sum(a0, axis=0, keepdims=True) + b_ref[...]
    o_ref[1:2, :] = jnp.sum(a1, axis=0, keepdims=True) + b_ref[...]


def _mod_vectors(c, c_ctx, mod_w, mod_b):
    depth = mod_w.shape[0]
    tn = 1024
    s = jnp.broadcast_to(jnp.stack([c[0], c_ctx])[:, :, None], (2, D, LANES))
    return pl.pallas_call(
        _mod_kernel,
        grid=(depth, 6 * D // tn),
        in_specs=[
            pl.BlockSpec((2, D, LANES), lambda l, j: (0, 0, 0)),
            pl.BlockSpec((None, D, tn), lambda l, j: (l, 0, j)),
            pl.BlockSpec((None, 1, tn), lambda l, j: (l, 0, j)),
        ],
        out_specs=pl.BlockSpec((None, 2, tn), lambda l, j: (l, 0, j)),
        out_shape=jax.ShapeDtypeStruct((depth, 2, 6 * D), F32),
        compiler_params=_params("arbitrary", "arbitrary"),
        name="mod_vectors",
    )(s, mod_w, mod_b.reshape(depth, 1, 6 * D))


def _mod_row(mod_ref, kind, col):
    return mod_ref[pl.ds(kind, 1), col * D:(col + 1) * D]


def _ada_norm(xv, g, shift, scale):
    ms = jnp.mean(xv * xv, axis=-1, keepdims=True)
    return (xv * lax.rsqrt(ms + EPS) * g) * (1.0 + scale) + shift


def _prep0_kernel(x_ref, ctx_ref, er_ref, ec_ref, mod_ref, g_ref, xo_ref, ho_ref):
    i = pl.program_id(0)

    @pl.when(i == 0)
    def _():
        xo_ref[...] = ctx_ref[...]

    @pl.when(i > 0)
    def _():
        r0 = (i - 1) * (ROWS // GRID_W)
        for j in range(ROWS // GRID_W):
            rs = slice(j * GRID_W, (j + 1) * GRID_W)
            xo_ref[rs, 0:D // 2] = x_ref[rs, 0:D // 2] + er_ref[pl.ds(r0 + j, 1), :]
            xo_ref[rs, D // 2:D] = x_ref[rs, D // 2:D] + ec_ref[...]

    kind = jnp.where(i == 0, 1, 0)
    h = _ada_norm(xo_ref[...], g_ref[...], _mod_row(mod_ref, kind, 0), _mod_row(mod_ref, kind, 1))
    ho_ref[...] = h.astype(BF16)


def _prep0(x, ctx, emb_r, emb_c, mod0, g):
    nblk = N_ALL // ROWS
    return pl.pallas_call(
        _prep0_kernel,
        grid=(nblk,),
        in_specs=[
            pl.BlockSpec((ROWS, D), lambda i: (jnp.maximum(i - 1, 0), 0)),
            pl.BlockSpec((ROWS, D), lambda i: (0, 0)),
            pl.BlockSpec(emb_r.shape, lambda i: (0, 0)),
            pl.BlockSpec(emb_c.shape, lambda i: (0, 0)),
            pl.BlockSpec((2, 6 * D), lambda i: (0, 0)),
            pl.BlockSpec((1, D), lambda i: (0, 0)),
        ],
        out_specs=[pl.BlockSpec((ROWS, D), lambda i: (i, 0)), pl.BlockSpec((ROWS, D), lambda i: (i, 0))],
        out_shape=[jax.ShapeDtypeStruct((N_ALL, D), F32), jax.ShapeDtypeStruct((N_ALL, D), BF16)],
        compiler_params=_params("arbitrary"),
        name="embed_norm",
    )(x, ctx, emb_r, emb_c, mod0, g)


def _route(hf32, rwt_ref, rb_ref, eo_ref, wo_ref):
    logits = lax.dot_general(rwt_ref[...], hf32, (((1,), (1,)), ((), ())),
                             precision=HIGHEST, preferred_element_type=F32)
    s = jax.nn.sigmoid(logits)
    sel = s + rb_ref[...]
    row = lax.broadcasted_iota(jnp.int32, (GRP, ROWS), 0)
    neg = jnp.float32(-jnp.inf)
    gs, i1s, i2s = [], [], []
    for g in range(N_GRP):
        blk = sel[g * GRP:(g + 1) * GRP, :]
        m1 = jnp.max(blk, axis=0, keepdims=True)
        i1 = jnp.min(jnp.where(blk == m1, row, GRP), axis=0, keepdims=True)
        blk2 = jnp.where(row == i1, neg, blk)
        m2 = jnp.max(blk2, axis=0, keepdims=True)
        i2 = jnp.min(jnp.where(blk2 == m2, row, GRP), axis=0, keepdims=True)
        gs.append(m1 + m2)
        i1s.append(i1)
        i2s.append(i2)
    best, gi, i1, i2 = gs[0], jnp.zeros((1, ROWS), jnp.int32), i1s[0], i2s[0]
    for g in range(1, N_GRP):
        upd = gs[g] > best
        best = jnp.where(upd, gs[g], best)
        gi = jnp.where(upd, g, gi)
        i1 = jnp.where(upd, i1s[g], i1)
        i2 = jnp.where(upd, i2s[g], i2)
    e1 = gi * GRP + i1
    e2 = gi * GRP + i2
    erow = lax.broadcasted_iota(jnp.int32, (N_EXP, ROWS), 0)
    s1 = jnp.sum(jnp.where(erow == e1, s, 0.0), axis=0, keepdims=True)
    s2 = jnp.sum(jnp.where(erow == e2, s, 0.0), axis=0, keepdims=True)
    tot = s1 + s2
    eo_ref[0:1, :] = e1
    eo_ref[1:2, :] = e2
    wo_ref[0:1, :] = s1 / tot
    wo_ref[1:2, :] = s2 / tot


def _combine_wait(yb_hbm, ybuf, sem, slot):
    pltpu.make_async_copy(yb_hbm.at[pl.ds(0, 2 * ROWS), :], ybuf.at[slot], sem.at[slot]).wait()


def _combine_issue(dest_ref, yb_hbm, ybuf, sem, slot):
    def body(r, carry):
        for k in range(2):
            row = dest_ref[0, k, r]
            pltpu.make_async_copy(yb_hbm.at[pl.ds(row, 1), :], ybuf.at[slot, pl.ds(k * ROWS + r, 1), :],
                                  sem.at[slot]).start()
        return carry

    lax.fori_loop(0, ROWS, body, 0, unroll=8)


def _token_kernel(*refs, gate_col, shift_col, scale_col, ctx_blocks, route, combine, final):
    refs = list(refs)
    if combine:
        dcur_ref, dnxt_ref = refs.pop(0), refs.pop(0)
    x_ref, y_ref = refs.pop(0), refs.pop(0)
    if combine:
        wc_ref = refs.pop(0)
    modg_ref, modn_ref, g_ref, rwt_ref, rb_ref = (refs.pop(0) for _ in range(5))
    i = pl.program_id(0)
    if combine:
        ybuf, sem = refs[-2], refs[-1]
        slot = lax.rem(i, 2)

        @pl.when(i == 0)
        def _():
            _combine_issue(dcur_ref, y_ref, ybuf, sem, 0)

        @pl.when(i + 1 < pl.num_programs(0))
        def _():
            _combine_issue(dnxt_ref, y_ref, ybuf, sem, 1 - slot)

        _combine_wait(y_ref, ybuf, sem, slot)
        y = wc_ref[:, 0:1] * ybuf[slot, 0:ROWS, :] + wc_ref[:, 1:2] * ybuf[slot, ROWS:2 * ROWS, :]
    else:
        y = y_ref[...]
    kind = jnp.where(i < ctx_blocks, 1, 0)
    xn = x_ref[...] + _mod_row(modg_ref, kind, gate_col) * y
    if final:
        ms = jnp.mean(xn * xn, axis=-1, keepdims=True)
        refs[0][...] = xn * lax.rsqrt(ms + EPS) * g_ref[...]
        return
    xo_ref, ho_ref = refs[0], refs[1]
    xo_ref[...] = xn
    h = _ada_norm(xn, g_ref[...], _mod_row(modn_ref, kind, shift_col), _mod_row(modn_ref, kind, scale_col))
    ho_ref[...] = h.astype(ho_ref.dtype)
    if route:
        _route(h, rwt_ref, rb_ref, refs[2], refs[3])


def _token_stage(x, y, modg, modn, g, rwt, rb, *, gate_col, shift_col=0, scale_col=0, x_off=0, ctx_blocks=0,
                 route=False, combine=None, final=False):
    n = x.shape[0] - x_off * ROWS
    nblk = n // ROWS
    row = lambda i: (i, 0)
    const2 = lambda i: (0, 0)
    in_specs, args, scratch = [], [], []
    if combine is not None:
        dest, wc = combine
        dest3 = dest.reshape(2, nblk, ROWS).transpose(1, 0, 2)
        in_specs += [pl.BlockSpec((1, 2, ROWS), lambda i: (i, 0, 0), memory_space=pltpu.SMEM),
                     pl.BlockSpec((1, 2, ROWS), lambda i: (jnp.minimum(i + 1, nblk - 1), 0, 0),
                                  memory_space=pltpu.SMEM)]
        args += [dest3, dest3]
    in_specs.append(pl.BlockSpec((ROWS, D), lambda i: (i + x_off, 0)))
    args.append(x)
    if combine is not None:
        in_specs += [pl.BlockSpec(memory_space=pl.ANY), pl.BlockSpec((ROWS, 2), row)]
        args += [y, wc]
        scratch = [pltpu.VMEM((2, 2 * ROWS, D), F32), pltpu.SemaphoreType.DMA((2,))]
    else:
        in_specs.append(pl.BlockSpec((ROWS, D), row))
        args.append(y)
    in_specs += [pl.BlockSpec((2, 6 * D), const2), pl.BlockSpec((2, 6 * D), const2), pl.BlockSpec((1, D), const2),
                 pl.BlockSpec((N_EXP, D), const2), pl.BlockSpec((N_EXP, 1), const2)]
    args += [modg, modn, g, rwt, rb]
    if final:
        out_specs = [pl.BlockSpec((ROWS, D), row)]
        out_shape = [jax.ShapeDtypeStruct((n, D), F32)]
    else:
        out_specs = [pl.BlockSpec((ROWS, D), row), pl.BlockSpec((ROWS, D), row)]
        out_shape = [jax.ShapeDtypeStruct((n, D), F32), jax.ShapeDtypeStruct((n, D), F32 if route else BF16)]
        if route:
            out_specs += [pl.BlockSpec((2, ROWS), lambda i: (0, i)), pl.BlockSpec((2, ROWS), lambda i: (0, i))]
            out_shape += [jax.ShapeDtypeStruct((2, n), jnp.int32), jax.ShapeDtypeStruct((2, n), F32)]
    kern = functools.partial(_token_kernel, gate_col=gate_col, shift_col=shift_col, scale_col=scale_col,
                             ctx_blocks=ctx_blocks, route=route, combine=combine is not None, final=final)
    return pl.pallas_call(
        kern,
        grid=(nblk,),
        in_specs=in_specs,
        out_specs=out_specs,
        out_shape=out_shape,
        scratch_shapes=scratch,
        compiler_params=_params("arbitrary"),
        name="token_stage",
    )(*args)


def _mm_kernel(*refs, nx):
    x_refs, w_refs, o_ref, wb_refs = refs[:nx], refs[nx:2 * nx], refs[2 * nx], refs[2 * nx + 1:]

    @pl.when(pl.program_id(1) == 0)
    def _():
        for w_ref, wb_ref in zip(w_refs, wb_refs):
            wb_ref[...] = w_ref[...].astype(BF16)

    acc = _dot(x_refs[0][...], wb_refs[0][...])
    for x_ref, wb_ref in zip(x_refs[1:], wb_refs[1:]):
        acc = acc + _dot(x_ref[...], wb_ref[...])
    o_ref[...] = acc


def _mm(xs, w, *, ncols, tm, tn, x_off=0):
    nx = len(xs)
    k = xs[0].shape[1]
    m = xs[0].shape[0] - x_off * tm
    in_specs = [pl.BlockSpec((tm, k), lambda j, i: (i + x_off, 0)) for _ in xs]
    in_specs += [pl.BlockSpec((k, tn), functools.partial(lambda j, i, kk: (kk, j), kk=kk)) for kk in range(nx)]
    return pl.pallas_call(
        functools.partial(_mm_kernel, nx=nx),
        grid=(ncols // tn, m // tm),
        in_specs=in_specs,
        out_specs=pl.BlockSpec((tm, tn), lambda j, i: (i, j)),
        out_shape=jax.ShapeDtypeStruct((m, ncols), F32),
        scratch_shapes=[pltpu.VMEM((k, tn), BF16) for _ in xs],
        compiler_params=_params("arbitrary", "arbitrary"),
        name="proj",
    )(*xs, *([w] * nx))


def _conv_block(cur, prev8, next8, w_ref, b_ref, i, nblk):
    ctx_edge = T_CTX // ROWS
    keep_prev = jnp.logical_and(i != 0, i != ctx_edge)
    keep_next = jnp.logical_and(i != nblk - 1, i != ctx_edge - 1)
    prev8 = jnp.where(keep_prev, prev8, 0.0)
    next8 = jnp.where(keep_next, next8, 0.0)
    ext = jnp.concatenate([prev8, cur, next8], axis=0)
    base = SUBLANES - CONV_W // 2
    y = b_ref[...] + w_ref[0:1, :] * ext[base:base + ROWS]
    for j in range(1, CONV_W):
        y = y + w_ref[j:j + 1, :] * ext[base + j:base + j + ROWS]
    return y


def _halo_specs(width, col_blk, nblk):
    per = ROWS // SUBLANES
    last8 = N_ALL // SUBLANES - 1
    return [
        pl.BlockSpec((ROWS, width), lambda i, *a: (i, col_blk(*a))),
        pl.BlockSpec((SUBLANES, width), lambda i, *a: (jnp.maximum(i * per - 1, 0), col_blk(*a))),
        pl.BlockSpec((SUBLANES, width), lambda i, *a: (jnp.minimum((i + 1) * per, last8), col_blk(*a))),
    ]


def _scan_tiles(a_ref, b_ref, h_ref, carry_ref, reverse):
    ntile = ROWS // SUBLANES
    row = lax.broadcasted_iota(jnp.int32, (SUBLANES, LRU_W), 0)

    def body(t, carry):
        tt = (ntile - 1 - t) if reverse else t
        rows = pl.ds(pl.multiple_of(tt * SUBLANES, SUBLANES), SUBLANES)
        a = a_ref[rows, :]
        b = b_ref[rows, :]
        for d in (1, 2, 4):
            shift = (SUBLANES - d) if reverse else d
            a_sh = pltpu.roll(a, shift, axis=0)
            b_sh = pltpu.roll(b, shift, axis=0)
            m = (row < SUBLANES - d) if reverse else (row >= d)
            b = jnp.where(m, a * b_sh, 0.0) + b
            a = jnp.where(m, a * a_sh, a)
        h = a * carry + b
        h_ref[rows, :] = h
        edge = h[0:1, :] if reverse else h[SUBLANES - 1:SUBLANES, :]
        return jnp.broadcast_to(edge, (SUBLANES, LRU_W))

    carry_ref[...] = lax.fori_loop(0, ntile, body, carry_ref[...])


def _lru_fwd_kernel(xa_ref, prev_ref, next_ref, cw_ref, cb_ref, wbd_ref, ba_ref, bx_ref, lam_ref,
                    hf_ref, a1_ref, b1_ref, carry_ref, a0_ref, b0_ref):
    i = pl.program_id(0)
    nblk = pl.num_programs(0)

    @pl.when(i == 0)
    def _():
        carry_ref[...] = jnp.zeros_like(carry_ref)

    xa = _conv_block(xa_ref[...], prev_ref[...], next_ref[...], cw_ref, cb_ref, i, nblk)
    xab = xa.astype(BF16)
    nsp = -LRU_C * _softplus(-lam_ref[...])
    gw = wbd_ref.shape[1]
    for g in range(LRU_W // gw):
        cs = slice(g * gw, (g + 1) * gw)
        z = _dot(xab[:, cs], wbd_ref[g])
        for d in range(2):
            r = jax.nn.sigmoid(z[:, d * gw:(d + 1) * gw] + ba_ref[d:d + 1, cs])
            ig = jax.nn.sigmoid(z[:, (2 + d) * gw:(3 + d) * gw] + bx_ref[d:d + 1, cs])
            log_a = r * nsp[d:d + 1, cs]
            a = jnp.exp(log_a)
            b = jnp.sqrt(-jnp.tanh(log_a) * (a * a + 1.0)) * ig * xa[:, cs]
            if d == 0:
                a0_ref[:, cs] = a
                b0_ref[:, cs] = b
            else:
                a1_ref[:, cs] = a
                b1_ref[:, cs] = b
    _scan_tiles(a0_ref, b0_ref, hf_ref, carry_ref, reverse=False)


def _gelu_tanh(v):
    return 0.5 * v * (1.0 + jnp.tanh(math.sqrt(2.0 / math.pi) * (v + 0.044715 * (v * v * v))))


def _lru_bwd_kernel(a1_ref, b1_ref, hf_ref, ga_ref, ya_ref, carry_ref, hb_ref):
    @pl.when(pl.program_id(0) == 0)
    def _():
        carry_ref[...] = jnp.zeros_like(carry_ref)

    _scan_tiles(a1_ref, b1_ref, hb_ref, carry_ref, reverse=True)
    ya_ref[...] = ((hf_ref[...] + hb_ref[...]) * _gelu_tanh(ga_ref[...])).astype(BF16)


def _rev_block(i, nblk):
    nctx = T_CTX // ROWS
    return jnp.where(i < nctx, nctx - 1 - i, nblk - 1 - (i - nctx))


def _lru(p0, conv_w, conv_b, wbd, ba, bx, lam):
    nblk = N_ALL // ROWS
    full = lambda shape: pl.BlockSpec(shape, lambda i: (0,) * len(shape))
    hf, a1, b1 = pl.pallas_call(
        _lru_fwd_kernel,
        grid=(nblk,),
        in_specs=_halo_specs(LRU_W, lambda: 0, nblk) + [
            full((CONV_W, LRU_W)), full((1, LRU_W)), full(wbd.shape),
            full((2, LRU_W)), full((2, LRU_W)), full((2, LRU_W)),
        ],
        out_specs=[pl.BlockSpec((ROWS, LRU_W), lambda i: (i, 0))] * 3,
        out_shape=[jax.ShapeDtypeStruct((N_ALL, LRU_W), F32)] * 3,
        scratch_shapes=[pltpu.VMEM((SUBLANES, LRU_W), F32), pltpu.VMEM((ROWS, LRU_W), F32),
                        pltpu.VMEM((ROWS, LRU_W), F32)],
        compiler_params=_params("arbitrary"),
        name="lru_fwd",
    )(p0, p0, p0, conv_w, conv_b, wbd, ba, bx, lam)
    rev = lambda i: (_rev_block(i, nblk), 0)
    ya = pl.pallas_call(
        _lru_bwd_kernel,
        grid=(nblk,),
        in_specs=[pl.BlockSpec((ROWS, LRU_W), rev)] * 3
        + [pl.BlockSpec((ROWS, LRU_W), lambda i: (_rev_block(i, nblk), 1))],
        out_specs=pl.BlockSpec((ROWS, LRU_W), rev),
        out_shape=jax.ShapeDtypeStruct((N_ALL, LRU_W), BF16),
        scratch_shapes=[pltpu.VMEM((SUBLANES, LRU_W), F32), pltpu.VMEM((ROWS, LRU_W), F32)],
        compiler_params=_params("arbitrary"),
        name="lru_bwd",
    )(a1, b1, hf, p0)
    return ya


def _tri(reverse):
    r = lax.broadcasted_iota(jnp.int32, (CHUNK, CHUNK), 0)
    c = lax.broadcasted_iota(jnp.int32, (CHUNK, CHUNK), 1)
    return (c >= r) if reverse else (c <= r)


def _cumsum_rows(tri_b, v):
    hi = v.astype(BF16)
    r1 = v - hi.astype(F32)
    mid = r1.astype(BF16)
    lo = (r1 - mid.astype(F32)).astype(BF16)
    return _dot(tri_b, hi) + _dot(tri_b, mid) + _dot(tri_b, lo)


def _gla_kernel(*refs, reverse):
    if reverse:
        (q_ref, k_ref, v_ref, ad_ref, wg_ref, bg_ref, of_ref, og_ref, gn_ref, o_ref,
         s_ref, qd_s, klo_s, khi_s, ke_s, dec_s, s2_s, upd_s) = refs
    else:
        (q_ref, k_ref, v_ref, ad_ref, wg_ref, bg_ref, o_ref,
         s_ref, qd_s, klo_s, khi_s, ke_s, dec_s, s2_s, upd_s) = refs
    d = 1 if reverse else 0

    @pl.when(pl.program_id(0) == 0)
    def _():
        s_ref[...] = jnp.zeros_like(s_ref)

    nch = ROWS // CHUNK
    pw = 2 * GLA_DK
    row2 = lax.broadcasted_iota(jnp.int32, (CHUNK, pw), 0)
    col2 = jnp.bitwise_and(lax.broadcasted_iota(jnp.int32, (CHUNK, pw), 1), GLA_DK - 1)
    causal2 = (col2 >= row2) if reverse else (col2 <= row2)
    eye = (lax.broadcasted_iota(jnp.int32, (pw, pw), 0) == lax.broadcasted_iota(jnp.int32, (pw, pw), 1)).astype(F32)
    zeros_v = jnp.zeros((CHUNK, GLA_DV), BF16)

    rr = lax.broadcasted_iota(jnp.int32, (ROWS, ROWS), 0)
    rc = lax.broadcasted_iota(jnp.int32, (ROWS, ROWS), 1)
    same = (rr // CHUNK) == (rc // CHUNK)
    tri_b = jnp.logical_and(same, (rc >= rr) if reverse else (rc <= rr)).astype(BF16)
    ad = ad_ref[:, d * GLA_RANK:(d + 1) * GLA_RANK]
    z = _dot(ad.astype(BF16), wg_ref[...].astype(BF16)) + bg_ref[...]
    lg = -_softplus(-z) * (1.0 / GLA_TAU)
    cum = _cumsum_rows(tri_b, lg)
    cum3 = cum.reshape(nch, CHUNK, GLA_QK)
    last3 = cum3[:, 0:1, :] if reverse else cum3[:, CHUNK - 1:CHUNK, :]
    to_end = jnp.exp(last3 - cum3).reshape(ROWS, GLA_QK)
    k = k_ref[...]
    k_inv = k * jnp.exp(-cum)
    low_all = jnp.bitwise_and(lax.broadcasted_iota(jnp.int32, (1, GLA_QK), 1), pw - 1) < GLA_DK
    qd_s[...] = (q_ref[...] * (GLA_DK ** -0.5) * jnp.exp(cum)).astype(BF16)
    klo_s[...] = jnp.where(low_all, k_inv, 0.0).astype(BF16)
    khi_s[...] = jnp.where(low_all, 0.0, k_inv).astype(BF16)
    ke_s[...] = (k * to_end).astype(BF16)
    dec_s[0:nch, :] = jnp.exp(last3.reshape(nch, GLA_QK))

    npair = GLA_H // 2
    for c in range(nch):
        rs = slice(c * CHUNK, (c + 1) * CHUNK)
        for p in range(npair):
            ls = slice(p * pw, (p + 1) * pw)
            kbd = jnp.concatenate([klo_s[rs, ls], khi_s[rs, ls]], axis=0)
            s2_s[c * npair + p] = jnp.where(causal2, _dot_nt(qd_s[rs, ls], kbd), 0.0).astype(BF16)
            vp = v_ref[rs, p * 2 * GLA_DV:(p + 1) * 2 * GLA_DV].astype(BF16)
            upd_s[c * npair + p] = _dot_tn(ke_s[rs, ls], vp)

    for cc in range(nch):
        c = (nch - 1 - cc) if reverse else cc
        rs = slice(c * CHUNK, (c + 1) * CHUNK)
        decay = dec_s[c:c + 1, :]
        for p in range(npair):
            ls = slice(p * pw, (p + 1) * pw)
            vs = slice(p * 2 * GLA_DV, (p + 1) * 2 * GLA_DV)
            qb = qd_s[rs, ls]
            s2 = s2_s[c * npair + p]
            upd = upd_s[c * npair + p]
            vp = v_ref[rs, vs].astype(BF16)
            vbd = jnp.concatenate([jnp.concatenate([vp[:, :GLA_DV], zeros_v], axis=1),
                                   jnp.concatenate([zeros_v, vp[:, GLA_DV:]], axis=1)], axis=0)
            sp = s_ref[p]
            o = _dot(jnp.concatenate([s2, qb], axis=1), jnp.concatenate([vbd, sp.astype(BF16)], axis=0))
            if reverse:
                for hh in range(2):
                    hs = slice((2 * p + hh) * GLA_DV, (2 * p + hh + 1) * GLA_DV)
                    tot = o[:, hh * GLA_DV:(hh + 1) * GLA_DV] + of_ref[rs, hs]
                    ms = jnp.mean(tot * tot, axis=-1, keepdims=True)
                    y = tot * lax.rsqrt(ms + EPS) * gn_ref[:, hs] * _silu(og_ref[rs, hs])
                    o_ref[rs, hs] = y.astype(BF16)
            else:
                o_ref[rs, vs] = o
            dcol =jnp.sum(eye * decay[:, ls], axis=1, keepdims=True)
            for hh in range(2):
                qr = slice(hh * GLA_DK, (hh + 1) * GLA_DK)
                qc = slice(hh * GLA_DV, (hh + 1) * GLA_DV)
                s_ref[p, qr, qc] = dcol[qr] * sp[qr, qc] + upd[qr, qc]


def _gla(p0, pad, wg, bg, gnorm, of=None):
    reverse = of is not None
    nblk = N_ALL // ROWS
    blk = (lambda i: _rev_block(i, nblk)) if reverse else (lambda i: i)
    d = 1 if reverse else 0
    qk_blk = 2 * LRU_W // GLA_QK
    in_specs = [
        pl.BlockSpec((ROWS, GLA_QK), lambda i: (blk(i), qk_blk)),
        pl.BlockSpec((ROWS, GLA_QK), lambda i: (blk(i), qk_blk + 1)),
        pl.BlockSpec((ROWS, GLA_V), lambda i: (blk(i), 3)),
        pl.BlockSpec((ROWS, LANES), lambda i: (blk(i), 0)),
        pl.BlockSpec((None, GLA_RANK, GLA_QK), lambda i: (d, 0, 0)),
        pl.BlockSpec((None, 1, GLA_QK), lambda i: (d, 0, 0)),
    ]
    args = [p0, p0, p0, pad, wg, bg.reshape(2, 1, GLA_QK)]
    if reverse:
        in_specs += [
            pl.BlockSpec((ROWS, GLA_V), lambda i: (blk(i), 0)),
            pl.BlockSpec((ROWS, GLA_V), lambda i: (blk(i), 4)),
            pl.BlockSpec((1, GLA_V), lambda i: (0, 0)),
        ]
        args += [of, p0, gnorm]
    return pl.pallas_call(
        functools.partial(_gla_kernel, reverse=reverse),
        grid=(nblk,),
        in_specs=in_specs,
        out_specs=pl.BlockSpec((ROWS, GLA_V), lambda i: (blk(i), 0)),
        out_shape=jax.ShapeDtypeStruct((N_ALL, GLA_V), BF16 if reverse else F32),
        scratch_shapes=[pltpu.VMEM((GLA_H // 2, 2 * GLA_DK, 2 * GLA_DV), F32)]
        + [pltpu.VMEM((ROWS, GLA_QK), BF16) for _ in range(4)] + [pltpu.VMEM((SUBLANES, GLA_QK), F32)]
        + [pltpu.VMEM((ROWS // CHUNK * GLA_H // 2, CHUNK, 2 * GLA_DK), BF16),
           pltpu.VMEM((ROWS // CHUNK * GLA_H // 2, 2 * GLA_DK, 2 * GLA_DV), F32)],
        compiler_params=_params("arbitrary"),
        name="gla_bwd" if reverse else "gla_fwd",
    )(*args)


def _conv_silu_kernel(x_ref, prev_ref, next_ref, w_ref, b_ref, o_ref):
    i = pl.program_id(0)
    y = _conv_block(x_ref[...], prev_ref[...], next_ref[...], w_ref, b_ref, i, pl.num_programs(0))
    o_ref[...] = _silu(y)


def _conv_silu(p1, conv_w, conv_b):
    nblk = N_ALL // ROWS
    tc = 2048
    off = SSD_INNER // tc
    return pl.pallas_call(
        _conv_silu_kernel,
        grid=(nblk, SSD_XBC // tc),
        in_specs=_halo_specs(tc, lambda j: j + off, nblk) + [
            pl.BlockSpec((CONV_W, tc), lambda i, j: (0, j)),
            pl.BlockSpec((1, tc), lambda i, j: (0, j)),
        ],
        out_specs=pl.BlockSpec((ROWS, tc), lambda i, j: (i, j)),
        out_shape=jax.ShapeDtypeStruct((N_ALL, SSD_XBC), F32),
        compiler_params=_params("arbitrary", "arbitrary"),
        name="ssd_conv",
    )(p1, p1, p1, conv_w, conv_b)


def _ssd_kernel(*refs, reverse):
    if reverse:
        (xs_ref, b_ref, c_ref, dt_ref, dtb_ref, alog_ref, acc_ref, z_ref, dsk_ref, gn_ref, y_ref,
         st_ref, cum_s, w2_s, ct2_s, dt2_s, ybuf) = refs
    else:
        xs_ref, b_ref, c_ref, dt_ref, dtb_ref, alog_ref, y_ref, st_ref, cum_s, w2_s, ct2_s, dt2_s = refs
    d = 1 if reverse else 0
    i = pl.program_id(0)

    @pl.when(i == 0)
    def _():
        st_ref[...] = jnp.zeros_like(st_ref)

    tri_b = _tri(reverse).astype(BF16)
    row2 = lax.broadcasted_iota(jnp.int32, (CHUNK, 2 * SSD_P), 0)
    col2 = jnp.bitwise_and(lax.broadcasted_iota(jnp.int32, (CHUNK, 2 * SSD_P), 1), SSD_P - 1)
    causal2 = (col2 >= row2) if reverse else (col2 <= row2)
    low = lax.broadcasted_iota(jnp.int32, (1, 2 * SSD_P), 1) < SSD_P
    a_neg = -jnp.exp(alog_ref[...])
    nch = ROWS // CHUNK
    hg = SSD_H // SSD_G
    pw = 2 * SSD_P

    low_c = lax.broadcasted_iota(jnp.int32, (CHUNK, 2 * SSD_P), 1) < SSD_P

    def pair(v, h0):
        if v.shape == (CHUNK, 2 * SSD_P):
            return jnp.take_along_axis(v, jnp.where(low_c, h0, h0 + 1), axis=1)
        return jnp.where(low, v[:, h0:h0 + 1], v[:, h0 + 1:h0 + 2])

    dtv = _softplus(dt_ref[:, d * SSD_H:(d + 1) * SSD_H] + dtb_ref[...])
    la = dtv * a_neg
    for c in range(nch):
        rs = slice(c * CHUNK, (c + 1) * CHUNK)
        cum = _cumsum_rows(tri_b, la[rs])
        last = cum[0:1, :] if reverse else cum[CHUNK - 1:CHUNK, :]
        w2 = dtv[rs] * jnp.exp(last - cum)
        cum_s[c] = jnp.concatenate([cum, cum], axis=1)
        w2_s[c] = jnp.concatenate([w2, w2], axis=1)
        cum_t = cum.T
        dt_t = dtv[rs].T
        ct2_s[c] = jnp.concatenate([cum_t, cum_t], axis=1)
        dt2_s[c] = jnp.concatenate([dt_t, dt_t], axis=1)

    def chunk(cc, carry):
        c = (nch - 1 - cc) if reverse else cc
        rs = pl.ds(pl.multiple_of(c * CHUNK, CHUNK), CHUNK)
        cum = cum_s[c]
        w2 = w2_s[c]
        ct2 = ct2_s[c]
        dt2 = dt2_s[c]
        dec = jnp.exp(cum[0:1, :] if reverse else cum[CHUNK - 1:CHUNK, :])
        cbs, yis = [], []
        for g in range(SSD_G):
            gs = slice(g * SSD_Z, (g + 1) * SSD_Z)
            cg = c_ref[rs, gs].astype(BF16)
            cbs.append(_dot_nt(cg, b_ref[rs, gs].astype(BF16)))
            yis.append(_dot(cg, st_ref[g].astype(BF16)))
        for g in range(SSD_G):
            gs = slice(g * SSD_Z, (g + 1) * SSD_Z)
            bgb = b_ref[rs, gs].astype(BF16)
            cb2 = jnp.concatenate([cbs[g], cbs[g]], axis=1)
            st = st_ref[g]
            y_inter = yis[g]
            x2s, decs = [], []
            for pr in range(hg // 2):
                h0 = g * hg + 2 * pr
                ps = slice(h0 * SSD_P, (h0 + 2) * SSD_P)
                xp = xs_ref[rs, ps]
                ccol = pair(cum, h0)
                rrow = jnp.where(low, ct2[h0:h0 + 1, :], ct2[h0 + 1:h0 + 2, :])
                drow = jnp.where(low, dt2[h0:h0 + 1, :], dt2[h0 + 1:h0 + 2, :])
                seg = jnp.exp(jnp.where(causal2, ccol - rrow, -jnp.inf))
                m = (cb2 * seg * drow).astype(BF16)
                xbd = jnp.concatenate([jnp.where(low, xp, 0.0), jnp.where(low, 0.0, xp)], axis=0).astype(BF16)
                y = _dot(m, xbd) + y_inter[:, pr * pw:(pr + 1) * pw] * jnp.exp(ccol)
                if reverse:
                    ybuf[:, ps] = y + acc_ref[rs, ps]
                else:
                    y_ref[rs, ps] = y
                x2s.append((xp * pair(w2, h0)).astype(BF16))
                decs.append(pair(dec, h0))
            upd = _dot_tn(bgb, jnp.concatenate(x2s, axis=1))
            st_ref[g] = st * jnp.concatenate(decs, axis=1) + upd
        if reverse:
            u = (ybuf[...] + dsk_ref[...] * xs_ref[rs, :]) * _silu(z_ref[rs, :])
            ms = jnp.mean(u * u, axis=-1, keepdims=True)
            y_ref[rs, :] = (u * lax.rsqrt(ms + EPS) * gn_ref[...]).astype(BF16)
        return carry

    lax.fori_loop(0, nch, chunk, 0)


def _ssd(xbc, pdt, dt_bias, a_log, fin=None):
    reverse = fin is not None
    nblk = N_ALL // ROWS
    nctx = T_CTX // ROWS
    blk = (lambda i: _rev_block(i, nblk)) if reverse else (lambda i: i)
    d = 1 if reverse else 0
    gz = SSD_G * SSD_Z
    full = lambda i: (blk(i), 0)
    in_specs = [
        pl.BlockSpec((ROWS, SSD_INNER), full),
        pl.BlockSpec((ROWS, gz), lambda i: (blk(i), SSD_INNER // gz)),
        pl.BlockSpec((ROWS, gz), lambda i: (blk(i), SSD_INNER // gz + 1)),
        pl.BlockSpec((ROWS, LANES), full),
        pl.BlockSpec((None, 1, SSD_H), lambda i: (d, 0, 0)),
        pl.BlockSpec((None, 1, SSD_H), lambda i: (d, 0, 0)),
    ]
    args = [xbc, xbc, xbc, pdt, dt_bias.reshape(2, 1, SSD_H), a_log.reshape(2, 1, SSD_H)]
    nch = ROWS // CHUNK
    scratch = [pltpu.VMEM((SSD_G, SSD_Z, SSD_INNER // SSD_G), F32),
               pltpu.VMEM((nch, CHUNK, 2 * SSD_H), F32), pltpu.VMEM((nch, CHUNK, 2 * SSD_H), F32),
               pltpu.VMEM((nch, SSD_H, 2 * CHUNK), F32), pltpu.VMEM((nch, SSD_H, 2 * CHUNK), F32)]
    if reverse:
        yf, p1, d_rep, norm_g = fin
        in_specs += [pl.BlockSpec((ROWS, SSD_INNER), full), pl.BlockSpec((ROWS, SSD_INNER), full),
                     pl.BlockSpec((1, SSD_INNER), lambda i: (0, 0)), pl.BlockSpec((1, SSD_INNER), lambda i: (0, 0))]
        args += [yf, p1, d_rep, norm_g]
        scratch.append(pltpu.VMEM((CHUNK, SSD_INNER), F32))
        out_spec = pl.BlockSpec((ROWS, SSD_INNER), lambda i: (blk(jnp.maximum(i, nctx)) - nctx, 0))
        out_shape = jax.ShapeDtypeStruct((T_LAT, SSD_INNER), BF16)
    else:
        out_spec = pl.BlockSpec((ROWS, SSD_INNER), full)
        out_shape = jax.ShapeDtypeStruct((N_ALL, SSD_INNER), F32)
    return pl.pallas_call(
        functools.partial(_ssd_kernel, reverse=reverse),
        grid=(nblk,),
        in_specs=in_specs,
        out_specs=out_spec,
        out_shape=out_shape,
        scratch_shapes=scratch,
        compiler_params=_params("arbitrary"),
        name="ssd_bwd" if reverse else "ssd_fwd",
    )(*args)


def _gather_wait(h_hbm, buf, sem, slot):
    pltpu.make_async_copy(h_hbm.at[pl.ds(0, MOE_TM), :], buf.at[slot], sem.at[slot]).wait()


def _gather_issue(idx_ref, h_hbm, buf, sem, slot):
    def body(r, carry):
        tok = idx_ref[0, 0, r]
        pltpu.make_async_copy(h_hbm.at[pl.ds(tok, 1), :], buf.at[slot, pl.ds(r, 1), :], sem.at[slot]).start()
        return carry

    lax.fori_loop(0, MOE_TM, body, 0, unroll=8)


def _expert_block_kernel(be_ref, ws_ref, nx_ref, nu_ref, *refs, n_w, layer, gather, compute):
    if gather:
        icur_ref, inxt_ref, h_hbm = refs[:3]
        refs = refs[3:]
    else:
        x_ref = refs[0]
        refs = refs[1:]
    w_hbm, o_ref = refs[:n_w], refs[n_w]
    scr = refs[n_w + 1:]
    w32, wb, wsem = scr[:n_w], scr[n_w:2 * n_w], scr[2 * n_w]
    m = pl.program_id(0)
    nu = nu_ref[0]
    e = be_ref[m]
    wslot = ws_ref[m]
    first = jnp.logical_and(m < nu, jnp.logical_or(m == 0, e != be_ref[jnp.maximum(m - 1, 0)]))

    def w_copy(k, ee, slot):
        return pltpu.make_async_copy(w_hbm[k].at[layer, ee], w32[k].at[slot], wsem.at[k, slot])

    @pl.when(m == 0)
    def _():
        for k in range(n_w):
            w_copy(k, e, wslot).start()

    if gather:
        gbuf, gsem = scr[2 * n_w + 1:]
        gslot = lax.rem(m, 2)

        @pl.when(m == 0)
        def _():
            _gather_issue(icur_ref, h_hbm, gbuf, gsem, 0)

        @pl.when(m + 1 < nu)
        def _():
            _gather_issue(inxt_ref, h_hbm, gbuf, gsem, 1 - gslot)

    @pl.when(jnp.logical_and(first, nx_ref[m] >= 0))
    def _():
        for k in range(n_w):
            w_copy(k, nx_ref[m], 1 - wslot).start()

    @pl.when(first)
    def _():
        for k in range(n_w):
            w_copy(k, e, wslot).wait()
            wb[k][...] = w32[k][wslot].astype(BF16)

    @pl.when(m < nu)
    def _():
        if gather:
            _gather_wait(h_hbm, gbuf, gsem, gslot)
            x = gbuf[gslot].astype(BF16)
        else:
            x = x_ref[...]
        o_ref[...] = compute(x, *wb).astype(o_ref.dtype)


def _expert_stage(src, ws, plan, layer, n_out, out_dtype, compute, name, slot_tok=None):
    be, wsl, nx, nu = plan
    nblk = be.shape[0]
    gather = slot_tok is not None
    anyspec = pl.BlockSpec(memory_space=pl.ANY)
    used = lambda m, nu_ref: jnp.minimum(m, nu_ref[0] - 1)
    n_w = len(ws)
    wshape = ws[0].shape[2:]
    scratch = [pltpu.VMEM((2,) + wshape, F32) for _ in ws] + [pltpu.VMEM(wshape, BF16) for _ in ws]
    scratch.append(pltpu.SemaphoreType.DMA((n_w, 2)))
    if gather:
        in_specs = [
            pl.BlockSpec((1, 1, MOE_TM), lambda m, be, ws, nx, nu: (used(m, nu), 0, 0), memory_space=pltpu.SMEM),
            pl.BlockSpec((1, 1, MOE_TM), lambda m, be, ws, nx, nu: (used(m + 1, nu), 0, 0),
                         memory_space=pltpu.SMEM),
            anyspec,
        ]
        args = [slot_tok, slot_tok, src]
        scratch += [pltpu.VMEM((2, MOE_TM, src.shape[1]), src.dtype), pltpu.SemaphoreType.DMA((2,))]
    else:
        in_specs = [pl.BlockSpec((MOE_TM, src.shape[1]), lambda m, be, ws, nx, nu: (used(m, nu), 0))]
        args = [src]
    return pl.pallas_call(
        functools.partial(_expert_block_kernel, n_w=n_w, layer=layer, gather=gather, compute=compute),
        grid_spec=pltpu.PrefetchScalarGridSpec(
            num_scalar_prefetch=4,
            grid=(nblk,),
            in_specs=in_specs + [anyspec] * n_w,
            out_specs=pl.BlockSpec((MOE_TM, n_out), lambda m, be, ws, nx, nu: (used(m, nu), 0)),
            scratch_shapes=scratch,
        ),
        out_shape=jax.ShapeDtypeStruct((nblk * MOE_TM, n_out), out_dtype),
        compiler_params=_params("arbitrary"),
        name=name,
    )(be, wsl, nx, nu, *args, *ws)


def _moe(h, eidx, wts, w_gate, w_up, w_down, layer):
    n = h.shape[0]
    i32 = jnp.int32
    flat_e = eidx.reshape(-1)
    experts = jnp.arange(N_EXP, dtype=i32)
    onehot = (flat_e[:, None] == experts[None, :]).astype(i32)
    csum = jnp.cumsum(onehot, axis=0)
    bcount = (csum[-1] + MOE_TM - 1) // MOE_TM
    bend = jnp.cumsum(bcount)
    bstart = bend - bcount
    dest = jnp.sum(onehot * (csum - 1 + (bstart * MOE_TM)[None, :]), axis=1)
    nblk = -(-2 * n // MOE_TM) + N_EXP
    n_used = bend[-1].astype(i32)
    blk = jnp.minimum(jnp.arange(nblk, dtype=i32), n_used - 1)
    block_exp = jnp.sum((blk[:, None] >= bend[None, :]).astype(i32), axis=1)
    owns = bcount > 0
    wslot = (jnp.cumsum(owns.astype(i32)) - 1) % 2
    later = jnp.logical_and(owns[None, :], experts[None, :] > experts[:, None])
    nxt = jnp.min(jnp.where(later, experts[None, :], N_EXP), axis=1)
    nxt = jnp.where(nxt == N_EXP, -1, nxt)
    of_blk = (block_exp[:, None] == experts[None, :]).astype(i32)
    plan = (block_exp.astype(i32), jnp.sum(of_blk * wslot[None, :], axis=1).astype(i32),
            jnp.sum(of_blk * nxt[None, :], axis=1).astype(i32), n_used.reshape(1))
    tok = jnp.tile(jnp.arange(n, dtype=i32), 2)
    pad_tok = jnp.arange(nblk * MOE_TM, dtype=i32) % n
    slot_tok = pad_tok.at[dest].set(tok).reshape(nblk, 1, MOE_TM)
    hb = _expert_stage(h, [w_gate, w_up], plan, layer, D_EXP, BF16,
                       lambda x, wg, wu: _silu(_dot(x, wg[...])) * _dot(x, wu[...]), "moe_up", slot_tok=slot_tok)
    yb = _expert_stage(hb, [w_down], plan, layer, D, F32, lambda x, wd: _dot(x, wd[...]), "moe_down")
    return yb, dest.reshape(2, n), wts.T


def _pos_tables():
    quarter = D // 4
    omega = 1.0 / (10000.0 ** (jnp.arange(quarter, dtype=F32) / quarter))
    ang_r = jnp.arange(T_LAT // GRID_W, dtype=F32)[:, None] * omega
    ang_c = jnp.arange(GRID_W, dtype=F32)[:, None] * omega
    emb_r = jnp.concatenate([jnp.sin(ang_r), jnp.cos(ang_r)], axis=-1)
    emb_c = jnp.concatenate([jnp.sin(ang_c), jnp.cos(ang_c)], axis=-1)
    return emb_r, emb_c


def _block_diag_gates(wa, wx, group):
    nb = wa.shape[1]
    per = group // wa.shape[2]
    eye = jnp.eye(per, dtype=F32)

    def bd(w):
        w = w.reshape(nb // per, per, w.shape[1], w.shape[2])
        return jnp.einsum("gnkj,nm->gnkmj", w, eye).reshape(nb // per, group, group)

    return jnp.concatenate([bd(wa[0]), bd(wa[1]), bd(wx[0]), bd(wx[1])], axis=-1).astype(BF16)


def kernel(x, c, ctx, c_ctx, mod_w, mod_b, norm1_g, norm2_g, ev_w_in, ev_conv_w, ev_conv_b, lru_wa, lru_ba, lru_wx, lru_bx, lru_lambda, gla_wg_up, gla_bg, gla_norm_g, ev_w_out, od_w_in, od_conv_w, od_conv_b, ssd_a_log, ssd_dt_bias, ssd_d, ssd_norm_g, od_w_out, router_w, router_b, exp_w_gate, exp_w_up, exp_w_down, final_norm_g):
    mods = _mod_vectors(c, c_ctx, mod_w, mod_b)
    emb_r, emb_c = _pos_tables()
    rwt = router_w.T
    rb = router_b.reshape(N_EXP, 1)
    tm = N_ALL // 8

    x0, h0 = _prep0(x[0], ctx[0], emb_r, emb_c, mods[0], norm1_g[0:1])
    p0 = _mm([h0], ev_w_in[0], ncols=EVEN_MAIN, tm=tm, tn=1024)
    w_ad = jnp.pad(ev_w_in[0][:, EVEN_MAIN:], ((0, 0), (0, LANES - 2 * GLA_RANK)))
    pad = _mm([h0], w_ad, ncols=LANES, tm=tm, tn=LANES)
    wbd = _block_diag_gates(lru_wa[0], lru_wx[0], 256)
    ya = _lru(p0, ev_conv_w[0], ev_conv_b[0:1], wbd, lru_ba[0], lru_bx[0], lru_lambda[0])
    of = _gla(p0, pad, gla_wg_up[0], gla_bg[0], None)
    yb = _gla(p0, pad, gla_wg_up[0], gla_bg[0], gla_norm_g[0:1], of=of)
    y0 = _mm([ya, yb], ev_w_out[0], ncols=D, tm=tm, tn=1024)
    nctx = T_CTX // ROWS
    x1, h1, e0, w0 = _token_stage(x0, y0, mods[0], mods[0], norm2_g[0:1], rwt, rb, gate_col=2, shift_col=3,
                                  scale_col=4, ctx_blocks=nctx, route=True)
    f0, dest0, wc0 = _moe(h1, e0, w0, exp_w_gate, exp_w_up, exp_w_down, 0)
    x2, h2 = _token_stage(x1, f0, mods[0], mods[1], norm1_g[1:2], rwt, rb, gate_col=5, shift_col=0,
                          scale_col=1, ctx_blocks=nctx, combine=(dest0, wc0))

    p1 = _mm([h2], od_w_in[0], ncols=ODD_MAIN, tm=tm, tn=1024)
    pdt = _mm([h2], od_w_in[0][:, ODD_MAIN:], ncols=LANES, tm=tm, tn=LANES)
    xbc = _conv_silu(p1, od_conv_w[0], od_conv_b[0:1])
    yf = _ssd(xbc, pdt, ssd_dt_bias[0], ssd_a_log[0])
    d_rep = jnp.repeat(ssd_d[0], SSD_P).reshape(1, SSD_INNER)
    gy = _ssd(xbc, pdt, ssd_dt_bias[0], ssd_a_log[0], fin=(yf, p1, d_rep, ssd_norm_g[0:1]))
    y1 = _mm([gy], od_w_out[0], ncols=D, tm=1024, tn=512)
    x3, h3, e1, w1 = _token_stage(x2, y1, mods[1], mods[1], norm2_g[1:2], rwt, rb, gate_col=2, shift_col=3,
                                  scale_col=4, x_off=nctx, route=True)
    f1, dest1, wc1 = _moe(h3, e1, w1, exp_w_gate, exp_w_up, exp_w_down, 1)
    (out,) = _token_stage(x3, f1, mods[1], mods[1], final_norm_g.reshape(1, D), rwt, rb, gate_col=5,
                          combine=(dest1, wc1), final=True)
    return out[None]
```

```python
import functools
import math

import jax
import jax.numpy as jnp
from jax import lax
from jax.experimental import pallas as pl
from jax.experimental.pallas import tpu as pltpu

F32 = jnp.float32
BF16 = jnp.bfloat16
HIGHEST = lax.Precision.HIGHEST

D = 2048
T_LAT = 8192
T_CTX = 256
N_ALL = T_CTX + T_LAT
GRID_W = 64
EPS = 1e-6
CONV_W = 4
LRU_W = 1024
LRU_C = 8.0
GLA_H = 8
GLA_DK = 64
GLA_DV = 128
GLA_QK = GLA_H * GLA_DK
GLA_V = GLA_H * GLA_DV
GLA_RANK = 16
GLA_TAU = 16.0
CHUNK = 64
EVEN_MAIN = 2 * LRU_W + 2 * GLA_QK + 2 * GLA_V
SSD_INNER = 2 * D
SSD_P = 64
SSD_H = SSD_INNER // SSD_P
SSD_G = 8
SSD_Z = 128
SSD_XBC = SSD_INNER + 2 * SSD_G * SSD_Z
ODD_MAIN = SSD_INNER + SSD_XBC
N_EXP = 32
N_GRP = 4
GRP = N_EXP // N_GRP
D_EXP = 1024
MOE_TM = 256

ROWS = 256
LANES = 128
SUBLANES = 8
VMEM_LIMIT = 56 * 1024 * 1024


def _params(*sem):
    return pltpu.CompilerParams(dimension_semantics=sem, vmem_limit_bytes=VMEM_LIMIT)


def _silu(v):
    return v * jax.nn.sigmoid(v)


def _softplus(v):
    return jnp.maximum(v, 0.0) + jnp.log1p(jnp.exp(-jnp.abs(v)))


def _dot(a, b):
    return jnp.dot(a, b, preferred_element_type=F32)


def _dot_nt(a, b):
    return lax.dot_general(a, b, (((1,), (1,)), ((), ())), preferred_element_type=F32)


def _dot_tn(a, b):
    return lax.dot_general(a, b, (((0,), (0,)), ((), ())), preferred_element_type=F32)


def _mod_kernel(s_ref, w_ref, b_ref, o_ref):
    tn = w_ref.shape[1]
    nrep = tn // LANES

    def body(r, acc):
        a0, a1 = acc
        rows = pl.ds(pl.multiple_of(r * SUBLANES, SUBLANES), SUBLANES)
        w = w_ref[rows, :]
        s0 = _silu(s_ref[0, rows, :])
        s1 = _silu(s_ref[1, rows, :])
        a0 = a0 + w * jnp.concatenate([s0] * nrep, axis=1)
        a1 = a1 + w * jnp.concatenate([s1] * nrep, axis=1)
        return a0, a1

    zero = jnp.zeros((SUBLANES, tn), F32)
    a0, a1 = lax.fori_loop(0, D // SUBLANES, body, (zero, zero), unroll=4)
    o_ref[0:1, :] = jnp.sum(a0, axis=0, keepdims=True) + b_ref[...]
    o_ref[1:2, :] = jnp.sum(a1, axis=0, keepdims=True) + b_ref[...]


def _mod_vectors(c, c_ctx, mod_w, mod_b):
    depth = mod_w.shape[0]
    tn = 1024
    s = jnp.broadcast_to(jnp.stack([c[0], c_ctx])[:, :, None], (2, D, LANES))
    return pl.pallas_call(
        _mod_kernel,
        grid=(depth, 6 * D // tn),
        in_specs=[
            pl.BlockSpec((2, D, LANES), lambda l, j: (0, 0, 0)),
            pl.BlockSpec((None, D, tn), lambda l, j: (l, 0, j)),
            pl.BlockSpec((None, 1, tn), lambda l, j: (l, 0, j)),
        ],
        out_specs=pl.BlockSpec((None, 2, tn), lambda l, j: (l, 0, j)),
        out_shape=jax.ShapeDtypeStruct((depth, 2, 6 * D), F32),
        compiler_params=_params("arbitrary", "arbitrary"),
        name="mod_vectors",
    )(s, mod_w, mod_b.reshape(depth, 1, 6 * D))


def _mod_row(mod_ref, kind, col):
    return mod_ref[pl.ds(kind, 1), col * D:(col + 1) * D]


def _ada_norm(xv, g, shift, scale):
    ms = jnp.mean(xv * xv, axis=-1, keepdims=True)
    return (xv * lax.rsqrt(ms + EPS) * g) * (1.0 + scale) + shift


def _prep0_kernel(x_ref, ctx_ref, er_ref, ec_ref, mod_ref, g_ref, xo_ref, ho_ref):
    i = pl.program_id(0)

    @pl.when(i == 0)
    def _():
        xo_ref[...] = ctx_ref[...]

    @pl.when(i > 0)
    def _():
        r0 = (i - 1) * (ROWS // GRID_W)
        for j in range(ROWS // GRID_W):
            rs = slice(j * GRID_W, (j + 1) * GRID_W)
            xo_ref[rs, 0:D // 2] = x_ref[rs, 0:D // 2] + er_ref[pl.ds(r0 + j, 1), :]
            xo_ref[rs, D // 2:D] = x_ref[rs, D // 2:D] + ec_ref[...]

    kind = jnp.where(i == 0, 1, 0)
    h = _ada_norm(xo_ref[...], g_ref[...], _mod_row(mod_ref, kind, 0), _mod_row(mod_ref, kind, 1))
    ho_ref[...] = h.astype(BF16)


def _prep0(x, ctx, emb_r, emb_c, mod0, g):
    nblk = N_ALL // ROWS
    return pl.pallas_call(
        _prep0_kernel,
        grid=(nblk,),
        in_specs=[
            pl.BlockSpec((ROWS, D), lambda i: (jnp.maximum(i - 1, 0), 0)),
            pl.BlockSpec((ROWS, D), lambda i: (0, 0)),
            pl.BlockSpec(emb_r.shape, lambda i: (0, 0)),
            pl.BlockSpec(emb_c.shape, lambda i: (0, 0)),
            pl.BlockSpec((2, 6 * D), lambda i: (0, 0)),
            pl.BlockSpec((1, D), lambda i: (0, 0)),
        ],
        out_specs=[pl.BlockSpec((ROWS, D), lambda i: (i, 0)), pl.BlockSpec((ROWS, D), lambda i: (i, 0))],
        out_shape=[jax.ShapeDtypeStruct((N_ALL, D), F32), jax.ShapeDtypeStruct((N_ALL, D), BF16)],
        compiler_params=_params("arbitrary"),
        name="embed_norm",
    )(x, ctx, emb_r, emb_c, mod0, g)


def _route(hf32, rwt_ref, rb_ref, eo_ref, wo_ref):
    logits = lax.dot_general(rwt_ref[...], hf32, (((1,), (1,)), ((), ())),
                             precision=HIGHEST, preferred_element_type=F32)
    s = jax.nn.sigmoid(logits)
    sel = s + rb_ref[...]
    row = lax.broadcasted_iota(jnp.int32, (GRP, ROWS), 0)
    neg = jnp.float32(-jnp.inf)
    gs, i1s, i2s = [], [], []
    for g in range(N_GRP):
        blk = sel[g * GRP:(g + 1) * GRP, :]
        m1 = jnp.max(blk, axis=0, keepdims=True)
        i1 = jnp.min(jnp.where(blk == m1, row, GRP), axis=0, keepdims=True)
        blk2 = jnp.where(row == i1, neg, blk)
        m2 = jnp.max(blk2, axis=0, keepdims=True)
        i2 = jnp.min(jnp.where(blk2 == m2, row, GRP), axis=0, keepdims=True)
        gs.append(m1 + m2)
        i1s.append(i1)
        i2s.append(i2)
    best, gi, i1, i2 = gs[0], jnp.zeros((1, ROWS), jnp.int32), i1s[0], i2s[0]
    for g in range(1, N_GRP):
        upd = gs[g] > best
        best = jnp.where(upd, gs[g], best)
        gi = jnp.where(upd, g, gi)
        i1 = jnp.where(upd, i1s[g], i1)
        i2 = jnp.where(upd, i2s[g], i2)
    e1 = gi * GRP + i1
    e2 = gi * GRP + i2
    erow = lax.broadcasted_iota(jnp.int32, (N_EXP, ROWS), 0)
    s1 = jnp.sum(jnp.where(erow == e1, s, 0.0), axis=0, keepdims=True)
    s2 = jnp.sum(jnp.where(erow == e2, s, 0.0), axis=0, keepdims=True)
    tot = s1 + s2
    eo_ref[0:1, :] = e1
    eo_ref[1:2, :] = e2
    wo_ref[0:1, :] = s1 / tot
    wo_ref[1:2, :] = s2 / tot


def _combine_wait(yb_hbm, ybuf, sem, slot):
    pltpu.make_async_copy(yb_hbm.at[pl.ds(0, 2 * ROWS), :], ybuf.at[slot], sem.at[slot]).wait()


def _combine_issue(dest_ref, yb_hbm, ybuf, sem, slot):
    def body(r, carry):
        for k in range(2):
            row = dest_ref[0, k, r]
            pltpu.make_async_copy(yb_hbm.at[pl.ds(row, 1), :], ybuf.at[slot, pl.ds(k * ROWS + r, 1), :],
                                  sem.at[slot]).start()
        return carry

    lax.fori_loop(0, ROWS, body, 0, unroll=8)


def _token_kernel(*refs, gate_col, shift_col, scale_col, ctx_blocks, route, combine, final):
    refs = list(refs)
    if combine:
        dcur_ref, dnxt_ref = refs.pop(0), refs.pop(0)
    x_ref, y_ref = refs.pop(0), refs.pop(0)
    if combine:
        wc_ref = refs.pop(0)
    modg_ref, modn_ref, g_ref, rwt_ref, rb_ref = (refs.pop(0) for _ in range(5))
    i = pl.program_id(0)
    if combine:
        ybuf, sem = refs[-2], refs[-1]
        slot = lax.rem(i, 2)

        @pl.when(i == 0)
        def _():
            _combine_issue(dcur_ref, y_ref, ybuf, sem, 0)

        @pl.when(i + 1 < pl.num_programs(0))
        def _():
            _combine_issue(dnxt_ref, y_ref, ybuf, sem, 1 - slot)

        _combine_wait(y_ref, ybuf, sem, slot)
        y = wc_ref[:, 0:1] * ybuf[slot, 0:ROWS, :] + wc_ref[:, 1:2] * ybuf[slot, ROWS:2 * ROWS, :]
    else:
        y = y_ref[...]
    kind = jnp.where(i < ctx_blocks, 1, 0)
    xn = x_ref[...] + _mod_row(modg_ref, kind, gate_col) * y
    if final:
        ms = jnp.mean(xn * xn, axis=-1, keepdims=True)
        refs[0][...] = xn * lax.rsqrt(ms + EPS) * g_ref[...]
        return
    xo_ref, ho_ref = refs[0], refs[1]
    xo_ref[...] = xn
    h = _ada_norm(xn, g_ref[...], _mod_row(modn_ref, kind, shift_col), _mod_row(modn_ref, kind, scale_col))
    ho_ref[...] = h.astype(ho_ref.dtype)
    if route:
        _route(h, rwt_ref, rb_ref, refs[2], refs[3])


def _token_stage(x, y, modg, modn, g, rwt, rb, *, gate_col, shift_col=0, scale_col=0, x_off=0, ctx_blocks=0,
                 route=False, combine=None, final=False):
    n = x.shape[0] - x_off * ROWS
    nblk = n // ROWS
    row = lambda i: (i, 0)
    const2 = lambda i: (0, 0)
    in_specs, args, scratch = [], [], []
    if combine is not None:
        dest, wc = combine
        dest3 = dest.reshape(2, nblk, ROWS).transpose(1, 0, 2)
        in_specs += [pl.BlockSpec((1, 2, ROWS), lambda i: (i, 0, 0), memory_space=pltpu.SMEM),
                     pl.BlockSpec((1, 2, ROWS), lambda i: (jnp.minimum(i + 1, nblk - 1), 0, 0),
                                  memory_space=pltpu.SMEM)]
        args += [dest3, dest3]
    in_specs.append(pl.BlockSpec((ROWS, D), lambda i: (i + x_off, 0)))
    args.append(x)
    if combine is not None:
        in_specs += [pl.BlockSpec(memory_space=pl.ANY), pl.BlockSpec((ROWS, 2), row)]
        args += [y, wc]
        scratch = [pltpu.VMEM((2, 2 * ROWS, D), F32), pltpu.SemaphoreType.DMA((2,))]
    else:
        in_specs.append(pl.BlockSpec((ROWS, D), row))
        args.append(y)
    in_specs += [pl.BlockSpec((2, 6 * D), const2), pl.BlockSpec((2, 6 * D), const2), pl.BlockSpec((1, D), const2),
                 pl.BlockSpec((N_EXP, D), const2), pl.BlockSpec((N_EXP, 1), const2)]
    args += [modg, modn, g, rwt, rb]
    if final:
        out_specs = [pl.BlockSpec((ROWS, D), row)]
        out_shape = [jax.ShapeDtypeStruct((n, D), F32)]
    else:
        out_specs = [pl.BlockSpec((ROWS, D), row), pl.BlockSpec((ROWS, D), row)]
        out_shape = [jax.ShapeDtypeStruct((n, D), F32), jax.ShapeDtypeStruct((n, D), F32 if route else BF16)]
        if route:
            out_specs += [pl.BlockSpec((2, ROWS), lambda i: (0, i)), pl.BlockSpec((2, ROWS), lambda i: (0, i))]
            out_shape += [jax.ShapeDtypeStruct((2, n), jnp.int32), jax.ShapeDtypeStruct((2, n), F32)]
    kern = functools.partial(_token_kernel, gate_col=gate_col, shift_col=shift_col, scale_col=scale_col,
                             ctx_blocks=ctx_blocks, route=route, combine=combine is not None, final=final)
    return pl.pallas_call(
        kern,
        grid=(nblk,),
        in_specs=in_specs,
        out_specs=out_specs,
        out_shape=out_shape,
        scratch_shapes=scratch,
        compiler_params=_params("arbitrary"),
        name="token_stage",
    )(*args)


def _mm_kernel(*refs, nx, valid):
    x_refs, w_refs, o_ref, wb_refs = refs[:nx], refs[nx:2 * nx], refs[2 * nx], refs[2 * nx + 1:]

    @pl.when(pl.program_id(1) == 0)
    def _():
        for w_ref, wb_ref in zip(w_refs, wb_refs):
            w = w_ref[...]
            if valid < w.shape[1]:
                w = jnp.where(lax.broadcasted_iota(jnp.int32, (1, w.shape[1]), 1) < valid, w, 0.0)
            wb_ref[...] = w.astype(BF16)

    acc = _dot(x_refs[0][...], wb_refs[0][...])
    for x_ref, wb_ref in zip(x_refs[1:], wb_refs[1:]):
        acc = acc + _dot(x_ref[...], wb_ref[...])
    o_ref[...] = acc


def _mm(xs, w, *, ncols, tm, tn, x_off=0, col_off=0):
    nx = len(xs)
    k = xs[0].shape[1]
    m = xs[0].shape[0] - x_off * tm
    valid = min(tn, w.shape[1] - col_off * tn) if ncols == tn else tn
    in_specs = [pl.BlockSpec((tm, k), lambda j, i: (i + x_off, 0)) for _ in xs]
    in_specs += [pl.BlockSpec((k, tn), functools.partial(lambda j, i, kk: (kk, j + col_off), kk=kk))
                 for kk in range(nx)]
    return pl.pallas_call(
        functools.partial(_mm_kernel, nx=nx, valid=valid),
        grid=(ncols // tn, m // tm),
        in_specs=in_specs,
        out_specs=pl.BlockSpec((tm, tn), lambda j, i: (i, j)),
        out_shape=jax.ShapeDtypeStruct((m, ncols), F32),
        scratch_shapes=[pltpu.VMEM((k, tn), BF16) for _ in xs],
        compiler_params=_params("arbitrary", "arbitrary"),
        name="proj",
    )(*xs, *([w] * nx))


def _conv_block(cur, prev8, next8, w_ref, b_ref, i, nblk):
    ctx_edge = T_CTX // ROWS
    keep_prev = jnp.logical_and(i != 0, i != ctx_edge)
    keep_next = jnp.logical_and(i != nblk - 1, i != ctx_edge - 1)
    prev8 = jnp.where(keep_prev, prev8, 0.0)
    next8 = jnp.where(keep_next, next8, 0.0)
    ext = jnp.concatenate([prev8, cur, next8], axis=0)
    base = SUBLANES - CONV_W // 2
    y = b_ref[...] + w_ref[0:1, :] * ext[base:base + ROWS]
    for j in range(1, CONV_W):
        y = y + w_ref[j:j + 1, :] * ext[base + j:base + j + ROWS]
    return y


def _halo_specs(width, col_blk, nblk):
    per = ROWS // SUBLANES
    last8 = N_ALL // SUBLANES - 1
    return [
        pl.BlockSpec((ROWS, width), lambda i, *a: (i, col_blk(*a))),
        pl.BlockSpec((SUBLANES, width), lambda i, *a: (jnp.maximum(i * per - 1, 0), col_blk(*a))),
        pl.BlockSpec((SUBLANES, width), lambda i, *a: (jnp.minimum((i + 1) * per, last8), col_blk(*a))),
    ]


def _scan_tiles(a_ref, b_ref, h_ref, carry_ref, reverse):
    ntile = ROWS // SUBLANES
    row = lax.broadcasted_iota(jnp.int32, (SUBLANES, LRU_W), 0)

    def body(t, carry):
        tt = (ntile - 1 - t) if reverse else t
        rows = pl.ds(pl.multiple_of(tt * SUBLANES, SUBLANES), SUBLANES)
        a = a_ref[rows, :]
        b = b_ref[rows, :]
        for d in (1, 2, 4):
            shift = (SUBLANES - d) if reverse else d
            a_sh = pltpu.roll(a, shift, axis=0)
            b_sh = pltpu.roll(b, shift, axis=0)
            m = (row < SUBLANES - d) if reverse else (row >= d)
            b = jnp.where(m, a * b_sh, 0.0) + b
            a = jnp.where(m, a * a_sh, a)
        h = a * carry + b
        h_ref[rows, :] = h
        edge = h[0:1, :] if reverse else h[SUBLANES - 1:SUBLANES, :]
        return jnp.broadcast_to(edge, (SUBLANES, LRU_W))

    carry_ref[...] = lax.fori_loop(0, ntile, body, carry_ref[...])


def _lru_fwd_kernel(xa_ref, prev_ref, next_ref, cw_ref, cb_ref, wbd_ref, ba_ref, bx_ref, lam_ref,
                    hf_ref, a1_ref, b1_ref, carry_ref, a0_ref, b0_ref):
    i = pl.program_id(0)
    nblk = pl.num_programs(0)

    @pl.when(i == 0)
    def _():
        carry_ref[...] = jnp.zeros_like(carry_ref)

    xa = _conv_block(xa_ref[...], prev_ref[...], next_ref[...], cw_ref, cb_ref, i, nblk)
    xab = xa.astype(BF16)
    nsp = -LRU_C * _softplus(-lam_ref[...])
    gw = wbd_ref.shape[1]
    for g in range(LRU_W // gw):
        cs = slice(g * gw, (g + 1) * gw)
        z = _dot(xab[:, cs], wbd_ref[g])
        for d in range(2):
            r = jax.nn.sigmoid(z[:, d * gw:(d + 1) * gw] + ba_ref[d:d + 1, cs])
            ig = jax.nn.sigmoid(z[:, (2 + d) * gw:(3 + d) * gw] + bx_ref[d:d + 1, cs])
            log_a = r * nsp[d:d + 1, cs]
            a = jnp.exp(log_a)
            b = jnp.sqrt(-jnp.tanh(log_a) * (a * a + 1.0)) * ig * xa[:, cs]
            if d == 0:
                a0_ref[:, cs] = a
                b0_ref[:, cs] = b
            else:
                a1_ref[:, cs] = a
                b1_ref[:, cs] = b
    _scan_tiles(a0_ref, b0_ref, hf_ref, carry_ref, reverse=False)


def _gelu_tanh(v):
    return 0.5 * v * (1.0 + jnp.tanh(math.sqrt(2.0 / math.pi) * (v + 0.044715 * (v * v * v))))


def _lru_bwd_kernel(a1_ref, b1_ref, hf_ref, ga_ref, ya_ref, carry_ref, hb_ref):
    @pl.when(pl.program_id(0) == 0)
    def _():
        carry_ref[...] = jnp.zeros_like(carry_ref)

    _scan_tiles(a1_ref, b1_ref, hb_ref, carry_ref, reverse=True)
    ya_ref[...] = ((hf_ref[...] + hb_ref[...]) * _gelu_tanh(ga_ref[...])).astype(BF16)


def _rev_block(i, nblk):
    nctx = T_CTX // ROWS
    return jnp.where(i < nctx, nctx - 1 - i, nblk - 1 - (i - nctx))


def _lru(p0, conv_w, conv_b, wbd, ba, bx, lam):
    nblk = N_ALL // ROWS
    full = lambda shape: pl.BlockSpec(shape, lambda i: (0,) * len(shape))
    hf, a1, b1 = pl.pallas_call(
        _lru_fwd_kernel,
        grid=(nblk,),
        in_specs=_halo_specs(LRU_W, lambda: 0, nblk) + [
            full((CONV_W, LRU_W)), full((1, LRU_W)), full(wbd.shape),
            full((2, LRU_W)), full((2, LRU_W)), full((2, LRU_W)),
        ],
        out_specs=[pl.BlockSpec((ROWS, LRU_W), lambda i: (i, 0))] * 3,
        out_shape=[jax.ShapeDtypeStruct((N_ALL, LRU_W), F32)] * 3,
        scratch_shapes=[pltpu.VMEM((SUBLANES, LRU_W), F32), pltpu.VMEM((ROWS, LRU_W), F32),
                        pltpu.VMEM((ROWS, LRU_W), F32)],
        compiler_params=_params("arbitrary"),
        name="lru_fwd",
    )(p0, p0, p0, conv_w, conv_b, wbd, ba, bx, lam)
    rev = lambda i: (_rev_block(i, nblk), 0)
    ya = pl.pallas_call(
        _lru_bwd_kernel,
        grid=(nblk,),
        in_specs=[pl.BlockSpec((ROWS, LRU_W), rev)] * 3
        + [pl.BlockSpec((ROWS, LRU_W), lambda i: (_rev_block(i, nblk), 1))],
        out_specs=pl.BlockSpec((ROWS, LRU_W), rev),
        out_shape=jax.ShapeDtypeStruct((N_ALL, LRU_W), BF16),
        scratch_shapes=[pltpu.VMEM((SUBLANES, LRU_W), F32), pltpu.VMEM((ROWS, LRU_W), F32)],
        compiler_params=_params("arbitrary"),
        name="lru_bwd",
    )(a1, b1, hf, p0)
    return ya


def _tri(reverse):
    r = lax.broadcasted_iota(jnp.int32, (CHUNK, CHUNK), 0)
    c = lax.broadcasted_iota(jnp.int32, (CHUNK, CHUNK), 1)
    return (c >= r) if reverse else (c <= r)


def _cumsum_rows(tri_b, v):
    hi = v.astype(BF16)
    r1 = v - hi.astype(F32)
    mid = r1.astype(BF16)
    lo = (r1 - mid.astype(F32)).astype(BF16)
    return _dot(tri_b, hi) + _dot(tri_b, mid) + _dot(tri_b, lo)


def _gla_kernel(*refs, reverse):
    if reverse:
        (q_ref, k_ref, v_ref, ad_ref, wg_ref, bg_ref, of_ref, og_ref, gn_ref, o_ref,
         s_ref, qd_s, klo_s, khi_s, ke_s, dec_s, s2_s, upd_s) = refs
    else:
        (q_ref, k_ref, v_ref, ad_ref, wg_ref, bg_ref, o_ref,
         s_ref, qd_s, klo_s, khi_s, ke_s, dec_s, s2_s, upd_s) = refs
    d = 1 if reverse else 0

    @pl.when(pl.program_id(0) == 0)
    def _():
        s_ref[...] = jnp.zeros_like(s_ref)

    nch = ROWS // CHUNK
    pw = 2 * GLA_DK
    row2 = lax.broadcasted_iota(jnp.int32, (CHUNK, pw), 0)
    col2 = jnp.bitwise_and(lax.broadcasted_iota(jnp.int32, (CHUNK, pw), 1), GLA_DK - 1)
    causal2 = (col2 >= row2) if reverse else (col2 <= row2)
    eye = (lax.broadcasted_iota(jnp.int32, (pw, pw), 0) == lax.broadcasted_iota(jnp.int32, (pw, pw), 1)).astype(F32)
    zeros_v = jnp.zeros((CHUNK, GLA_DV), BF16)

    rr = lax.broadcasted_iota(jnp.int32, (ROWS, ROWS), 0)
    rc = lax.broadcasted_iota(jnp.int32, (ROWS, ROWS), 1)
    same = (rr // CHUNK) == (rc // CHUNK)
    tri_b = jnp.logical_and(same, (rc >= rr) if reverse else (rc <= rr)).astype(BF16)
    ad = ad_ref[:, d * GLA_RANK:(d + 1) * GLA_RANK]
    z = _dot(ad.astype(BF16), wg_ref[...].astype(BF16)) + bg_ref[...]
    lg = -_softplus(-z) * (1.0 / GLA_TAU)
    cum = _cumsum_rows(tri_b, lg)
    cum3 = cum.reshape(nch, CHUNK, GLA_QK)
    last3 = cum3[:, 0:1, :] if reverse else cum3[:, CHUNK - 1:CHUNK, :]
    to_end = jnp.exp(last3 - cum3).reshape(ROWS, GLA_QK)
    k = k_ref[...]
    k_inv = k * jnp.exp(-cum)
    low_all = jnp.bitwise_and(lax.broadcasted_iota(jnp.int32, (1, GLA_QK), 1), pw - 1) < GLA_DK
    qd_s[...] = (q_ref[...] * (GLA_DK ** -0.5) * jnp.exp(cum)).astype(BF16)
    klo_s[...] = jnp.where(low_all, k_inv, 0.0).astype(BF16)
    khi_s[...] = jnp.where(low_all, 0.0, k_inv).astype(BF16)
    ke_s[...] = (k * to_end).astype(BF16)
    dec_s[0:nch, :] = jnp.exp(last3.reshape(nch, GLA_QK))

    npair = GLA_H // 2
    for c in range(nch):
        rs = slice(c * CHUNK, (c + 1) * CHUNK)
        for p in range(npair):
            ls = slice(p * pw, (p + 1) * pw)
            kbd = jnp.concatenate([klo_s[rs, ls], khi_s[rs, ls]], axis=0)
            s2_s[c * npair + p] = jnp.where(causal2, _dot_nt(qd_s[rs, ls], kbd), 0.0).astype(BF16)
            vp = v_ref[rs, p * 2 * GLA_DV:(p + 1) * 2 * GLA_DV].astype(BF16)
            upd_s[c * npair + p] = _dot_tn(ke_s[rs, ls], vp)

    for cc in range(nch):
        c = (nch - 1 - cc) if reverse else cc
        rs = slice(c * CHUNK, (c + 1) * CHUNK)
        decay = dec_s[c:c + 1, :]
        for p in range(npair):
            ls = slice(p * pw, (p + 1) * pw)
            vs = slice(p * 2 * GLA_DV, (p + 1) * 2 * GLA_DV)
            qb = qd_s[rs, ls]
            s2 = s2_s[c * npair + p]
            upd = upd_s[c * npair + p]
            vp = v_ref[rs, vs].astype(BF16)
            vbd = jnp.concatenate([jnp.concatenate([vp[:, :GLA_DV], zeros_v], axis=1),
                                   jnp.concatenate([zeros_v, vp[:, GLA_DV:]], axis=1)], axis=0)
            sp = s_ref[p]
            o = _dot(jnp.concatenate([s2, qb], axis=1), jnp.concatenate([vbd, sp.astype(BF16)], axis=0))
            if reverse:
                for hh in range(2):
                    hs = slice((2 * p + hh) * GLA_DV, (2 * p + hh + 1) * GLA_DV)
                    tot = o[:, hh * GLA_DV:(hh + 1) * GLA_DV] + of_ref[rs, hs]
                    ms = jnp.mean(tot * tot, axis=-1, keepdims=True)
                    y = tot * lax.rsqrt(ms + EPS) * gn_ref[:, hs] * _silu(og_ref[rs, hs])
                    o_ref[rs, hs] = y.astype(BF16)
            else:
                o_ref[rs, vs] = o
            dcol =jnp.sum(eye * decay[:, ls], axis=1, keepdims=True)
            for hh in range(2):
                qr = slice(hh * GLA_DK, (hh + 1) * GLA_DK)
                qc = slice(hh * GLA_DV, (hh + 1) * GLA_DV)
                s_ref[p, qr, qc] = dcol[qr] * sp[qr, qc] + upd[qr, qc]


def _gla(p0, pad, wg, bg, gnorm, of=None):
    reverse = of is not None
    nblk = N_ALL // ROWS
    blk = (lambda i: _rev_block(i, nblk)) if reverse else (lambda i: i)
    d = 1 if reverse else 0
    qk_blk = 2 * LRU_W // GLA_QK
    in_specs = [
        pl.BlockSpec((ROWS, GLA_QK), lambda i: (blk(i), qk_blk)),
        pl.BlockSpec((ROWS, GLA_QK), lambda i: (blk(i), qk_blk + 1)),
        pl.BlockSpec((ROWS, GLA_V), lambda i: (blk(i), 3)),
        pl.BlockSpec((ROWS, LANES), lambda i: (blk(i), 0)),
        pl.BlockSpec((None, GLA_RANK, GLA_QK), lambda i: (d, 0, 0)),
        pl.BlockSpec((None, 1, GLA_QK), lambda i: (d, 0, 0)),
    ]
    args = [p0, p0, p0, pad, wg, bg.reshape(2, 1, GLA_QK)]
    if reverse:
        in_specs += [
            pl.BlockSpec((ROWS, GLA_V), lambda i: (blk(i), 0)),
            pl.BlockSpec((ROWS, GLA_V), lambda i: (blk(i), 4)),
            pl.BlockSpec((1, GLA_V), lambda i: (0, 0)),
        ]
        args += [of, p0, gnorm]
    return pl.pallas_call(
        functools.partial(_gla_kernel, reverse=reverse),
        grid=(nblk,),
        in_specs=in_specs,
        out_specs=pl.BlockSpec((ROWS, GLA_V), lambda i: (blk(i), 0)),
        out_shape=jax.ShapeDtypeStruct((N_ALL, GLA_V), BF16 if reverse else F32),
        scratch_shapes=[pltpu.VMEM((GLA_H // 2, 2 * GLA_DK, 2 * GLA_DV), F32)]
        + [pltpu.VMEM((ROWS, GLA_QK), BF16) for _ in range(4)] + [pltpu.VMEM((SUBLANES, GLA_QK), F32)]
        + [pltpu.VMEM((ROWS // CHUNK * GLA_H // 2, CHUNK, 2 * GLA_DK), BF16),
           pltpu.VMEM((ROWS // CHUNK * GLA_H // 2, 2 * GLA_DK, 2 * GLA_DV), F32)],
        compiler_params=_params("arbitrary"),
        name="gla_bwd" if reverse else "gla_fwd",
    )(*args)


def _conv_silu_kernel(x_ref, prev_ref, next_ref, w_ref, b_ref, o_ref):
    i = pl.program_id(0)
    y = _conv_block(x_ref[...], prev_ref[...], next_ref[...], w_ref, b_ref, i, pl.num_programs(0))
    o_ref[...] = _silu(y)


def _conv_silu(p1, conv_w, conv_b):
    nblk = N_ALL // ROWS
    tc = 2048
    off = SSD_INNER // tc
    return pl.pallas_call(
        _conv_silu_kernel,
        grid=(nblk, SSD_XBC // tc),
        in_specs=_halo_specs(tc, lambda j: j + off, nblk) + [
            pl.BlockSpec((CONV_W, tc), lambda i, j: (0, j)),
            pl.BlockSpec((1, tc), lambda i, j: (0, j)),
        ],
        out_specs=pl.BlockSpec((ROWS, tc), lambda i, j: (i, j)),
        out_shape=jax.ShapeDtypeStruct((N_ALL, SSD_XBC), F32),
        compiler_params=_params("arbitrary", "arbitrary"),
        name="ssd_conv",
    )(p1, p1, p1, conv_w, conv_b)


def _ssd_kernel(*refs, reverse):
    if reverse:
        (xs_ref, b_ref, c_ref, dt_ref, dtb_ref, alog_ref, acc_ref, z_ref, dsk_ref, gn_ref, y_ref,
         st_ref, cum_s, w2_s, ct2_s, dt2_s, ybuf) = refs
    else:
        xs_ref, b_ref, c_ref, dt_ref, dtb_ref, alog_ref, y_ref, st_ref, cum_s, w2_s, ct2_s, dt2_s = refs
    d = 1 if reverse else 0
    i = pl.program_id(0)

    @pl.when(i == 0)
    def _():
        st_ref[...] = jnp.zeros_like(st_ref)

    tri_b = _tri(reverse).astype(BF16)
    row2 = lax.broadcasted_iota(jnp.int32, (CHUNK, 2 * SSD_P), 0)
    col2 = jnp.bitwise_and(lax.broadcasted_iota(jnp.int32, (CHUNK, 2 * SSD_P), 1), SSD_P - 1)
    causal2 = (col2 >= row2) if reverse else (col2 <= row2)
    low = lax.broadcasted_iota(jnp.int32, (1, 2 * SSD_P), 1) < SSD_P
    a_neg = -jnp.exp(alog_ref[...])
    nch = ROWS // CHUNK
    hg = SSD_H // SSD_G
    pw = 2 * SSD_P

    low_c = lax.broadcasted_iota(jnp.int32, (CHUNK, 2 * SSD_P), 1) < SSD_P

    def pair(v, h0):
        if v.shape == (CHUNK, 2 * SSD_P):
            return jnp.take_along_axis(v, jnp.where(low_c, h0, h0 + 1), axis=1)
        return jnp.where(low, v[:, h0:h0 + 1], v[:, h0 + 1:h0 + 2])

    dtv = _softplus(dt_ref[:, d * SSD_H:(d + 1) * SSD_H] + dtb_ref[...])
    la = dtv * a_neg
    for c in range(nch):
        rs = slice(c * CHUNK, (c + 1) * CHUNK)
        cum = _cumsum_rows(tri_b, la[rs])
        last = cum[0:1, :] if reverse else cum[CHUNK - 1:CHUNK, :]
        w2 = dtv[rs] * jnp.exp(last - cum)
        cum_s[c] = jnp.concatenate([cum, cum], axis=1)
        w2_s[c] = jnp.concatenate([w2, w2], axis=1)
        cum_t = cum.T
        dt_t = dtv[rs].T
        ct2_s[c] = jnp.concatenate([cum_t, cum_t], axis=1)
        dt2_s[c] = jnp.concatenate([dt_t, dt_t], axis=1)

    def chunk(cc, carry):
        c = (nch - 1 - cc) if reverse else cc
        rs = pl.ds(pl.multiple_of(c * CHUNK, CHUNK), CHUNK)
        cum = cum_s[c]
        w2 = w2_s[c]
        ct2 = ct2_s[c]
        dt2 = dt2_s[c]
        dec = jnp.exp(cum[0:1, :] if reverse else cum[CHUNK - 1:CHUNK, :])
        cbs, yis = [], []
        for g in range(SSD_G):
            gs = slice(g * SSD_Z, (g + 1) * SSD_Z)
            cg = c_ref[rs, gs].astype(BF16)
            cbs.append(_dot_nt(cg, b_ref[rs, gs].astype(BF16)))
            yis.append(_dot(cg, st_ref[g].astype(BF16)))
        for g in range(SSD_G):
            gs = slice(g * SSD_Z, (g + 1) * SSD_Z)
            bgb = b_ref[rs, gs].astype(BF16)
            cb2 = jnp.concatenate([cbs[g], cbs[g]], axis=1)
            st = st_ref[g]
            y_inter = yis[g]
            x2s, decs = [], []
            for pr in range(hg // 2):
                h0 = g * hg + 2 * pr
                ps = slice(h0 * SSD_P, (h0 + 2) * SSD_P)
                xp = xs_ref[rs, ps]
                ccol = pair(cum, h0)
                rrow = jnp.where(low, ct2[h0:h0 + 1, :], ct2[h0 + 1:h0 + 2, :])
                drow = jnp.where(low, dt2[h0:h0 + 1, :], dt2[h0 + 1:h0 + 2, :])
                seg = jnp.exp(jnp.where(causal2, ccol - rrow, -jnp.inf))
                m = (cb2 * seg * drow).astype(BF16)
                xbd = jnp.concatenate([jnp.where(low, xp, 0.0), jnp.where(low, 0.0, xp)], axis=0).astype(BF16)
                y = _dot(m, xbd) + y_inter[:, pr * pw:(pr + 1) * pw] * jnp.exp(ccol)
                if reverse:
                    ybuf[:, ps] = y + acc_ref[rs, ps]
                else:
                    y_ref[rs, ps] = y
                x2s.append((xp * pair(w2, h0)).astype(BF16))
                decs.append(pair(dec, h0))
            upd = _dot_tn(bgb, jnp.concatenate(x2s, axis=1))
            st_ref[g] = st * jnp.concatenate(decs, axis=1) + upd
        if reverse:
            u = (ybuf[...] + dsk_ref[...] * xs_ref[rs, :]) * _silu(z_ref[rs, :])
            ms = jnp.mean(u * u, axis=-1, keepdims=True)
            y_ref[rs, :] = (u * lax.rsqrt(ms + EPS) * gn_ref[...]).astype(BF16)
        return carry

    lax.fori_loop(0, nch, chunk, 0)


def _ssd(xbc, pdt, dt_bias, a_log, fin=None):
    reverse = fin is not None
    nblk = N_ALL // ROWS
    nctx = T_CTX // ROWS
    blk = (lambda i: _rev_block(i, nblk)) if reverse else (lambda i: i)
    d = 1 if reverse else 0
    gz = SSD_G * SSD_Z
    full = lambda i: (blk(i), 0)
    in_specs = [
        pl.BlockSpec((ROWS, SSD_INNER), full),
        pl.BlockSpec((ROWS, gz), lambda i: (blk(i), SSD_INNER // gz)),
        pl.BlockSpec((ROWS, gz), lambda i: (blk(i), SSD_INNER // gz + 1)),
        pl.BlockSpec((ROWS, LANES), full),
        pl.BlockSpec((None, 1, SSD_H), lambda i: (d, 0, 0)),
        pl.BlockSpec((None, 1, SSD_H), lambda i: (d, 0, 0)),
    ]
    args = [xbc, xbc, xbc, pdt, dt_bias.reshape(2, 1, SSD_H), a_log.reshape(2, 1, SSD_H)]
    nch = ROWS // CHUNK
    scratch = [pltpu.VMEM((SSD_G, SSD_Z, SSD_INNER // SSD_G), F32),
               pltpu.VMEM((nch, CHUNK, 2 * SSD_H), F32), pltpu.VMEM((nch, CHUNK, 2 * SSD_H), F32),
               pltpu.VMEM((nch, SSD_H, 2 * CHUNK), F32), pltpu.VMEM((nch, SSD_H, 2 * CHUNK), F32)]
    if reverse:
        yf, p1, d_rep, norm_g = fin
        in_specs += [pl.BlockSpec((ROWS, SSD_INNER), full), pl.BlockSpec((ROWS, SSD_INNER), full),
                     pl.BlockSpec((1, SSD_INNER), lambda i: (0, 0)), pl.BlockSpec((1, SSD_INNER), lambda i: (0, 0))]
        args += [yf, p1, d_rep, norm_g]
        scratch.append(pltpu.VMEM((CHUNK, SSD_INNER), F32))
        out_spec = pl.BlockSpec((ROWS, SSD_INNER), lambda i: (blk(jnp.maximum(i, nctx)) - nctx, 0))
        out_shape = jax.ShapeDtypeStruct((T_LAT, SSD_INNER), BF16)
    else:
        out_spec = pl.BlockSpec((ROWS, SSD_INNER), full)
        out_shape = jax.ShapeDtypeStruct((N_ALL, SSD_INNER), F32)
    return pl.pallas_call(
        functools.partial(_ssd_kernel, reverse=reverse),
        grid=(nblk,),
        in_specs=in_specs,
        out_specs=out_spec,
        out_shape=out_shape,
        scratch_shapes=scratch,
        compiler_params=_params("arbitrary"),
        name="ssd_bwd" if reverse else "ssd_fwd",
    )(*args)


def _gather_wait(h_hbm, buf, sem, slot):
    pltpu.make_async_copy(h_hbm.at[pl.ds(0, MOE_TM), :], buf.at[slot], sem.at[slot]).wait()


def _gather_issue(idx_ref, h_hbm, buf, sem, slot):
    def body(r8, carry):
        base = pl.multiple_of(r8 * SUBLANES, SUBLANES)
        for u in range(SUBLANES):
            tok = idx_ref[0, 0, base + u]
            pltpu.make_async_copy(h_hbm.at[pl.ds(tok, 1), :], buf.at[slot, pl.ds(base + u, 1), :],
                                  sem.at[slot]).start()
        return carry

    lax.fori_loop(0, MOE_TM // SUBLANES, body, 0)


def _expert_block_kernel(be_ref, ws_ref, nx_ref, nu_ref, *refs, n_w, layer, gather, compute):
    if gather:
        icur_ref, inxt_ref, h_hbm = refs[:3]
        refs = refs[3:]
    else:
        x_ref = refs[0]
        refs = refs[1:]
    w_hbm, o_ref = refs[:n_w], refs[n_w]
    scr = refs[n_w + 1:]
    w32, wb, wsem = scr[:n_w], scr[n_w:2 * n_w], scr[2 * n_w]
    m = pl.program_id(0)
    nu = nu_ref[0]
    e = be_ref[m]
    wslot = ws_ref[m]
    first = jnp.logical_and(m < nu, jnp.logical_or(m == 0, e != be_ref[jnp.maximum(m - 1, 0)]))

    def w_copy(k, ee, slot):
        return pltpu.make_async_copy(w_hbm[k].at[layer, ee], w32[k].at[slot], wsem.at[k, slot])

    @pl.when(m == 0)
    def _():
        for k in range(n_w):
            w_copy(k, e, wslot).start()

    if gather:
        gbuf, gsem = scr[2 * n_w + 1:]
        gslot = lax.rem(m, 2)

        @pl.when(m == 0)
        def _():
            _gather_issue(icur_ref, h_hbm, gbuf, gsem, 0)

        @pl.when(m + 1 < nu)
        def _():
            _gather_issue(inxt_ref, h_hbm, gbuf, gsem, 1 - gslot)

    @pl.when(jnp.logical_and(first, nx_ref[m] >= 0))
    def _():
        for k in range(n_w):
            w_copy(k, nx_ref[m], 1 - wslot).start()

    @pl.when(first)
    def _():
        for k in range(n_w):
            w_copy(k, e, wslot).wait()
            wb[k][...] = w32[k][wslot].astype(BF16)

    @pl.when(m < nu)
    def _():
        if gather:
            _gather_wait(h_hbm, gbuf, gsem, gslot)
            x = gbuf[gslot].astype(BF16)
        else:
            x = x_ref[...]
        o_ref[...] = compute(x, *wb).astype(o_ref.dtype)


def _expert_stage(src, ws, plan, layer, n_out, out_dtype, compute, name, slot_tok=None):
    be, wsl, nx, nu = plan
    nblk = be.shape[0]
    gather = slot_tok is not None
    anyspec = pl.BlockSpec(memory_space=pl.ANY)
    used = lambda m, nu_ref: jnp.minimum(m, nu_ref[0] - 1)
    n_w = len(ws)
    wshape = ws[0].shape[2:]
    scratch = [pltpu.VMEM((2,) + wshape, F32) for _ in ws] + [pltpu.VMEM(wshape, BF16) for _ in ws]
    scratch.append(pltpu.SemaphoreType.DMA((n_w, 2)))
    if gather:
        in_specs = [
            pl.BlockSpec((1, 1, MOE_TM), lambda m, be, ws, nx, nu: (used(m, nu), 0, 0), memory_space=pltpu.SMEM),
            pl.BlockSpec((1, 1, MOE_TM), lambda m, be, ws, nx, nu: (used(m + 1, nu), 0, 0),
                         memory_space=pltpu.SMEM),
            anyspec,
        ]
        args = [slot_tok, slot_tok, src]
        scratch += [pltpu.VMEM((2, MOE_TM, src.shape[1]), src.dtype), pltpu.SemaphoreType.DMA((2,))]
    else:
        in_specs = [pl.BlockSpec((MOE_TM, src.shape[1]), lambda m, be, ws, nx, nu: (used(m, nu), 0))]
        args = [src]
    return pl.pallas_call(
        functools.partial(_expert_block_kernel, n_w=n_w, layer=layer, gather=gather, compute=compute),
        grid_spec=pltpu.PrefetchScalarGridSpec(
            num_scalar_prefetch=4,
            grid=(nblk,),
            in_specs=in_specs + [anyspec] * n_w,
            out_specs=pl.BlockSpec((MOE_TM, n_out), lambda m, be, ws, nx, nu: (used(m, nu), 0)),
            scratch_shapes=scratch,
        ),
        out_shape=jax.ShapeDtypeStruct((nblk * MOE_TM, n_out), out_dtype),
        compiler_params=_params("arbitrary"),
        name=name,
    )(be, wsl, nx, nu, *args, *ws)


def _moe(h, eidx, wts, w_gate, w_up, w_down, layer):
    n = h.shape[0]
    i32 = jnp.int32
    flat_e = eidx.reshape(-1)
    experts = jnp.arange(N_EXP, dtype=i32)
    onehot = (flat_e[:, None] == experts[None, :]).astype(i32)
    csum = jnp.cumsum(onehot, axis=0)
    bcount = (csum[-1] + MOE_TM - 1) // MOE_TM
    bend = jnp.cumsum(bcount)
    bstart = bend - bcount
    dest = jnp.sum(onehot * (csum - 1 + (bstart * MOE_TM)[None, :]), axis=1)
    nblk = -(-2 * n // MOE_TM) + N_EXP
    n_used = bend[-1].astype(i32)
    blk = jnp.minimum(jnp.arange(nblk, dtype=i32), n_used - 1)
    block_exp = jnp.sum((blk[:, None] >= bend[None, :]).astype(i32), axis=1)
    owns = bcount > 0
    wslot = (jnp.cumsum(owns.astype(i32)) - 1) % 2
    later = jnp.logical_and(owns[None, :], experts[None, :] > experts[:, None])
    nxt = jnp.min(jnp.where(later, experts[None, :], N_EXP), axis=1)
    nxt = jnp.where(nxt == N_EXP, -1, nxt)
    of_blk = (block_exp[:, None] == experts[None, :]).astype(i32)
    plan = (block_exp.astype(i32), jnp.sum(of_blk * wslot[None, :], axis=1).astype(i32),
            jnp.sum(of_blk * nxt[None, :], axis=1).astype(i32), n_used.reshape(1))
    tok = jnp.tile(jnp.arange(n, dtype=i32), 2)
    pad_tok = jnp.arange(nblk * MOE_TM, dtype=i32) % n
    slot_tok = pad_tok.at[dest].set(tok).reshape(nblk, 1, MOE_TM)
    hb = _expert_stage(h, [w_gate, w_up], plan, layer, D_EXP, BF16,
                       lambda x, wg, wu: _silu(_dot(x, wg[...])) * _dot(x, wu[...]), "moe_up", slot_tok=slot_tok)
    yb = _expert_stage(hb, [w_down], plan, layer, D, F32, lambda x, wd: _dot(x, wd[...]), "moe_down")
    return yb, dest.reshape(2, n), wts.T


def _pos_tables():
    quarter = D // 4
    omega = 1.0 / (10000.0 ** (jnp.arange(quarter, dtype=F32) / quarter))
    ang_r = jnp.arange(T_LAT // GRID_W, dtype=F32)[:, None] * omega
    ang_c = jnp.arange(GRID_W, dtype=F32)[:, None] * omega
    emb_r = jnp.concatenate([jnp.sin(ang_r), jnp.cos(ang_r)], axis=-1)
    emb_c = jnp.concatenate([jnp.sin(ang_c), jnp.cos(ang_c)], axis=-1)
    return emb_r, emb_c


def _block_diag_gates(wa, wx, group):
    nb = wa.shape[1]
    per = group // wa.shape[2]
    eye = jnp.eye(per, dtype=F32)

    def bd(w):
        w = w.reshape(nb // per, per, w.shape[1], w.shape[2])
        return jnp.einsum("gnkj,nm->gnkmj", w, eye).reshape(nb // per, group, group)

    return jnp.concatenate([bd(wa[0]), bd(wa[1]), bd(wx[0]), bd(wx[1])], axis=-1).astype(BF16)


def kernel(x, c, ctx, c_ctx, mod_w, mod_b, norm1_g, norm2_g, ev_w_in, ev_conv_w, ev_conv_b, lru_wa, lru_ba, lru_wx, lru_bx, lru_lambda, gla_wg_up, gla_bg, gla_norm_g, ev_w_out, od_w_in, od_conv_w, od_conv_b, ssd_a_log, ssd_dt_bias, ssd_d, ssd_norm_g, od_w_out, router_w, router_b, exp_w_gate, exp_w_up, exp_w_down, final_norm_g):
    mods = _mod_vectors(c, c_ctx, mod_w, mod_b)
    emb_r, emb_c = _pos_tables()
    rwt = router_w.T
    rb = router_b.reshape(N_EXP, 1)
    tm = N_ALL // 8

    x0, h0 = _prep0(x[0], ctx[0], emb_r, emb_c, mods[0], norm1_g[0:1])
    p0 = _mm([h0], ev_w_in[0], ncols=EVEN_MAIN, tm=tm, tn=1024)
    pad = _mm([h0], ev_w_in[0], ncols=LANES, tm=tm, tn=LANES, col_off=EVEN_MAIN // LANES)
    wbd = _block_diag_gates(lru_wa[0], lru_wx[0], 256)
    ya = _lru(p0, ev_conv_w[0], ev_conv_b[0:1], wbd, lru_ba[0], lru_bx[0], lru_lambda[0])
    of = _gla(p0, pad, gla_wg_up[0], gla_bg[0], None)
    yb = _gla(p0, pad, gla_wg_up[0], gla_bg[0], gla_norm_g[0:1], of=of)
    y0 = _mm([ya, yb], ev_w_out[0], ncols=D, tm=tm, tn=1024)
    nctx = T_CTX // ROWS
    x1, h1, e0, w0 = _token_stage(x0, y0, mods[0], mods[0], norm2_g[0:1], rwt, rb, gate_col=2, shift_col=3,
                                  scale_col=4, ctx_blocks=nctx, route=True)
    f0, dest0, wc0 = _moe(h1, e0, w0, exp_w_gate, exp_w_up, exp_w_down, 0)
    x2, h2 = _token_stage(x1, f0, mods[0], mods[1], norm1_g[1:2], rwt, rb, gate_col=5, shift_col=0,
                          scale_col=1, ctx_blocks=nctx, combine=(dest0, wc0))

    p1 = _mm([h2], od_w_in[0], ncols=ODD_MAIN, tm=tm, tn=1024)
    pdt = _mm([h2], od_w_in[0], ncols=LANES, tm=tm, tn=LANES, col_off=ODD_MAIN // LANES)
    xbc = _conv_silu(p1, od_conv_w[0], od_conv_b[0:1])
    yf = _ssd(xbc, pdt, ssd_dt_bias[0], ssd_a_log[0])
    d_rep = jnp.repeat(ssd_d[0], SSD_P).reshape(1, SSD_INNER)
    gy = _ssd(xbc, pdt, ssd_dt_bias[0], ssd_a_log[0], fin=(yf, p1, d_rep, ssd_norm_g[0:1]))
    y1 = _mm([gy], od_w_out[0], ncols=D, tm=1024, tn=512)
    x3, h3, e1, w1 = _token_stage(x2, y1, mods[1], mods[1], norm2_g[1:2], rwt, rb, gate_col=2, shift_col=3,
                                  scale_col=4, x_off=nctx, route=True)
    f1, dest1, wc1 = _moe(h3, e1, w1, exp_w_gate, exp_w_up, exp_w_down, 1)
    (out,) = _token_stage(x3, f1, mods[1], mods[1], final_norm_g.reshape(1, D), rwt, rb, gate_col=5,
                          combine=(dest1, wc1), final=True)
    return out[None]
```

```python
import functools
import math

import jax
import jax.numpy as jnp
from jax import lax
from jax.experimental import pallas as pl
from jax.experimental.pallas import tpu as pltpu

F32 = jnp.float32
BF16 = jnp.bfloat16
HIGHEST = lax.Precision.HIGHEST

D = 2048
T_LAT = 8192
T_CTX = 256
N_ALL = T_CTX + T_LAT
GRID_W = 64
EPS = 1e-6
CONV_W = 4
LRU_W = 1024
LRU_C = 8.0
GLA_H = 8
GLA_DK = 64
GLA_DV = 128
GLA_QK = GLA_H * GLA_DK
GLA_V = GLA_H * GLA_DV
GLA_RANK = 16
GLA_TAU = 16.0
CHUNK = 64
EVEN_MAIN = 2 * LRU_W + 2 * GLA_QK + 2 * GLA_V
SSD_INNER = 2 * D
SSD_P = 64
SSD_H = SSD_INNER // SSD_P
SSD_G = 8
SSD_Z = 128
SSD_XBC = SSD_INNER + 2 * SSD_G * SSD_Z
ODD_MAIN = SSD_INNER + SSD_XBC
N_EXP = 32
N_GRP = 4
GRP = N_EXP // N_GRP
D_EXP = 1024
MOE_TM = 256

ROWS = 256
LANES = 128
SUBLANES = 8
VMEM_LIMIT = 56 * 1024 * 1024


def _params(*sem):
    return pltpu.CompilerParams(dimension_semantics=sem, vmem_limit_bytes=VMEM_LIMIT)


def _silu(v):
    return v * jax.nn.sigmoid(v)


def _softplus(v):
    return jnp.maximum(v, 0.0) + jnp.log1p(jnp.exp(-jnp.abs(v)))


def _dot(a, b):
    return jnp.dot(a, b, preferred_element_type=F32)


def _dot_nt(a, b):
    return lax.dot_general(a, b, (((1,), (1,)), ((), ())), preferred_element_type=F32)


def _dot_tn(a, b):
    return lax.dot_general(a, b, (((0,), (0,)), ((), ())), preferred_element_type=F32)


def _mod_kernel(s_ref, w_ref, b_ref, o_ref):
    tn = w_ref.shape[1]
    nrep = tn // LANES

    def body(r, acc):
        a0, a1 = acc
        rows = pl.ds(pl.multiple_of(r * SUBLANES, SUBLANES), SUBLANES)
        w = w_ref[rows, :]
        s0 = _silu(s_ref[0, rows, :])
        s1 = _silu(s_ref[1, rows, :])
        a0 = a0 + w * jnp.concatenate([s0] * nrep, axis=1)
        a1 = a1 + w * jnp.concatenate([s1] * nrep, axis=1)
        return a0, a1

    zero = jnp.zeros((SUBLANES, tn), F32)
    a0, a1 = lax.fori_loop(0, D // SUBLANES, body, (zero, zero), unroll=4)
    o_ref[0:1, :] = jnp.sum(a0, axis=0, keepdims=True) + b_ref[...]
    o_ref[1:2, :] = jnp.sum(a1, axis=0, keepdims=True) + b_ref[...]


def _mod_vectors(c, c_ctx, mod_w, mod_b):
    depth = mod_w.shape[0]
    tn = 1024
    s = jnp.broadcast_to(jnp.stack([c[0], c_ctx])[:, :, None], (2, D, LANES))
    return pl.pallas_call(
        _mod_kernel,
        grid=(depth, 6 * D // tn),
        in_specs=[
            pl.BlockSpec((2, D, LANES), lambda l, j: (0, 0, 0)),
            pl.BlockSpec((None, D, tn), lambda l, j: (l, 0, j)),
            pl.BlockSpec((None, 1, tn), lambda l, j: (l, 0, j)),
        ],
        out_specs=pl.BlockSpec((None, 2, tn), lambda l, j: (l, 0, j)),
        out_shape=jax.ShapeDtypeStruct((depth, 2, 6 * D), F32),
        compiler_params=_params("arbitrary", "arbitrary"),
        name="mod_vectors",
    )(s, mod_w, mod_b.reshape(depth, 1, 6 * D))


def _mod_row(mod_ref, kind, col):
    return mod_ref[pl.ds(kind, 1), col * D:(col + 1) * D]


def _ada_norm(xv, g, shift, scale):
    ms = jnp.mean(xv * xv, axis=-1, keepdims=True)
    return (xv * lax.rsqrt(ms + EPS) * g) * (1.0 + scale) + shift


def _prep0_kernel(x_ref, ctx_ref, er_ref, ec_ref, mod_ref, g_ref, xo_ref, ho_ref):
    i = pl.program_id(0)

    @pl.when(i == 0)
    def _():
        xo_ref[...] = ctx_ref[...]

    @pl.when(i > 0)
    def _():
        r0 = (i - 1) * (ROWS // GRID_W)
        for j in range(ROWS // GRID_W):
            rs = slice(j * GRID_W, (j + 1) * GRID_W)
            xo_ref[rs, 0:D // 2] = x_ref[rs, 0:D // 2] + er_ref[pl.ds(r0 + j, 1), :]
            xo_ref[rs, D // 2:D] = x_ref[rs, D // 2:D] + ec_ref[...]

    kind = jnp.where(i == 0, 1, 0)
    h = _ada_norm(xo_ref[...], g_ref[...], _mod_row(mod_ref, kind, 0), _mod_row(mod_ref, kind, 1))
    ho_ref[...] = h.astype(BF16)


def _prep0(x, ctx, emb_r, emb_c, mod0, g):
    nblk = N_ALL // ROWS
    return pl.pallas_call(
        _prep0_kernel,
        grid=(nblk,),
        in_specs=[
            pl.BlockSpec((ROWS, D), lambda i: (jnp.maximum(i - 1, 0), 0)),
            pl.BlockSpec((ROWS, D), lambda i: (0, 0)),
            pl.BlockSpec(emb_r.shape, lambda i: (0, 0)),
            pl.BlockSpec(emb_c.shape, lambda i: (0, 0)),
            pl.BlockSpec((2, 6 * D), lambda i: (0, 0)),
            pl.BlockSpec((1, D), lambda i: (0, 0)),
        ],
        out_specs=[pl.BlockSpec((ROWS, D), lambda i: (i, 0)), pl.BlockSpec((ROWS, D), lambda i: (i, 0))],
        out_shape=[jax.ShapeDtypeStruct((N_ALL, D), F32), jax.ShapeDtypeStruct((N_ALL, D), BF16)],
        compiler_params=_params("arbitrary"),
        name="embed_norm",
    )(x, ctx, emb_r, emb_c, mod0, g)


def _route(hf32, rwt_ref, rb_ref, eo_ref, wo_ref):
    logits = lax.dot_general(rwt_ref[...], hf32, (((1,), (1,)), ((), ())),
                             precision=HIGHEST, preferred_element_type=F32)
    s = jax.nn.sigmoid(logits)
    sel = s + rb_ref[...]
    row = lax.broadcasted_iota(jnp.int32, (GRP, ROWS), 0)
    neg = jnp.float32(-jnp.inf)
    gs, i1s, i2s = [], [], []
    for g in range(N_GRP):
        blk = sel[g * GRP:(g + 1) * GRP, :]
        m1 = jnp.max(blk, axis=0, keepdims=True)
        i1 = jnp.min(jnp.where(blk == m1, row, GRP), axis=0, keepdims=True)
        blk2 = jnp.where(row == i1, neg, blk)
        m2 = jnp.max(blk2, axis=0, keepdims=True)
        i2 = jnp.min(jnp.where(blk2 == m2, row, GRP), axis=0, keepdims=True)
        gs.append(m1 + m2)
        i1s.append(i1)
        i2s.append(i2)
    best, gi, i1, i2 = gs[0], jnp.zeros((1, ROWS), jnp.int32), i1s[0], i2s[0]
    for g in range(1, N_GRP):
        upd = gs[g] > best
        best = jnp.where(upd, gs[g], best)
        gi = jnp.where(upd, g, gi)
        i1 = jnp.where(upd, i1s[g], i1)
        i2 = jnp.where(upd, i2s[g], i2)
    e1 = gi * GRP + i1
    e2 = gi * GRP + i2
    erow = lax.broadcasted_iota(jnp.int32, (N_EXP, ROWS), 0)
    s1 = jnp.sum(jnp.where(erow == e1, s, 0.0), axis=0, keepdims=True)
    s2 = jnp.sum(jnp.where(erow == e2, s, 0.0), axis=0, keepdims=True)
    tot = s1 + s2
    eo_ref[0:1, :] = e1
    eo_ref[1:2, :] = e2
    wo_ref[0:1, :] = s1 / tot
    wo_ref[1:2, :] = s2 / tot


def _combine_wait(yb_hbm, ybuf, sem, slot):
    pltpu.make_async_copy(yb_hbm.at[pl.ds(0, 2 * ROWS), :], ybuf.at[slot], sem.at[slot]).wait()


def _combine_issue(dest_ref, yb_hbm, ybuf, sem, slot):
    def body(r, carry):
        for k in range(2):
            row = dest_ref[0, k, r]
            pltpu.make_async_copy(yb_hbm.at[pl.ds(row, 1), :], ybuf.at[slot, pl.ds(k * ROWS + r, 1), :],
                                  sem.at[slot]).start()
        return carry

    lax.fori_loop(0, ROWS, body, 0, unroll=8)


def _token_kernel(*refs, gate_col, shift_col, scale_col, ctx_blocks, route, combine, final):
    refs = list(refs)
    if combine:
        dcur_ref, dnxt_ref = refs.pop(0), refs.pop(0)
    x_ref, y_ref = refs.pop(0), refs.pop(0)
    if combine:
        wc_ref = refs.pop(0)
    modg_ref, modn_ref, g_ref, rwt_ref, rb_ref = (refs.pop(0) for _ in range(5))
    i = pl.program_id(0)
    if combine:
        ybuf, sem = refs[-2], refs[-1]
        slot = lax.rem(i, 2)

        @pl.when(i == 0)
        def _():
            _combine_issue(dcur_ref, y_ref, ybuf, sem, 0)

        @pl.when(i + 1 < pl.num_programs(0))
        def _():
            _combine_issue(dnxt_ref, y_ref, ybuf, sem, 1 - slot)

        _combine_wait(y_ref, ybuf, sem, slot)
        y = wc_ref[:, 0:1] * ybuf[slot, 0:ROWS, :] + wc_ref[:, 1:2] * ybuf[slot, ROWS:2 * ROWS, :]
    else:
        y = y_ref[...]
    kind = jnp.where(i < ctx_blocks, 1, 0)
    xn = x_ref[...] + _mod_row(modg_ref, kind, gate_col) * y
    if final:
        ms = jnp.mean(xn * xn, axis=-1, keepdims=True)
        refs[0][...] = xn * lax.rsqrt(ms + EPS) * g_ref[...]
        return
    xo_ref, ho_ref = refs[0], refs[1]
    xo_ref[...] = xn
    h = _ada_norm(xn, g_ref[...], _mod_row(modn_ref, kind, shift_col), _mod_row(modn_ref, kind, scale_col))
    ho_ref[...] = h.astype(ho_ref.dtype)
    if route:
        _route(h, rwt_ref, rb_ref, refs[2], refs[3])


def _token_stage(x, y, modg, modn, g, rwt, rb, *, gate_col, shift_col=0, scale_col=0, x_off=0, ctx_blocks=0,
                 route=False, combine=None, final=False):
    n = x.shape[0] - x_off * ROWS
    nblk = n // ROWS
    row = lambda i: (i, 0)
    const2 = lambda i: (0, 0)
    in_specs, args, scratch = [], [], []
    if combine is not None:
        dest, wc = combine
        dest3 = dest.reshape(2, nblk, ROWS).transpose(1, 0, 2)
        in_specs += [pl.BlockSpec((1, 2, ROWS), lambda i: (i, 0, 0), memory_space=pltpu.SMEM),
                     pl.BlockSpec((1, 2, ROWS), lambda i: (jnp.minimum(i + 1, nblk - 1), 0, 0),
                                  memory_space=pltpu.SMEM)]
        args += [dest3, dest3]
    in_specs.append(pl.BlockSpec((ROWS, D), lambda i: (i + x_off, 0)))
    args.append(x)
    if combine is not None:
        in_specs += [pl.BlockSpec(memory_space=pl.ANY), pl.BlockSpec((ROWS, 2), row)]
        args += [y, wc]
        scratch = [pltpu.VMEM((2, 2 * ROWS, D), F32), pltpu.SemaphoreType.DMA((2,))]
    else:
        in_specs.append(pl.BlockSpec((ROWS, D), row))
        args.append(y)
    in_specs += [pl.BlockSpec((2, 6 * D), const2), pl.BlockSpec((2, 6 * D), const2), pl.BlockSpec((1, D), const2),
                 pl.BlockSpec((N_EXP, D), const2), pl.BlockSpec((N_EXP, 1), const2)]
    args += [modg, modn, g, rwt, rb]
    if final:
        out_specs = [pl.BlockSpec((ROWS, D), row)]
        out_shape = [jax.ShapeDtypeStruct((n, D), F32)]
    else:
        out_specs = [pl.BlockSpec((ROWS, D), row), pl.BlockSpec((ROWS, D), row)]
        out_shape = [jax.ShapeDtypeStruct((n, D), F32), jax.ShapeDtypeStruct((n, D), F32 if route else BF16)]
        if route:
            out_specs += [pl.BlockSpec((2, ROWS), lambda i: (0, i)), pl.BlockSpec((2, ROWS), lambda i: (0, i))]
            out_shape += [jax.ShapeDtypeStruct((2, n), jnp.int32), jax.ShapeDtypeStruct((2, n), F32)]
    kern = functools.partial(_token_kernel, gate_col=gate_col, shift_col=shift_col, scale_col=scale_col,
                             ctx_blocks=ctx_blocks, route=route, combine=combine is not None, final=final)
    return pl.pallas_call(
        kern,
        grid=(nblk,),
        in_specs=in_specs,
        out_specs=out_specs,
        out_shape=out_shape,
        scratch_shapes=scratch,
        compiler_params=_params("arbitrary"),
        name="token_stage",
    )(*args)


def _mm_kernel(*refs, nx, valid, w_t):
    x_refs, w_refs, o_ref, wb_refs = refs[:nx], refs[nx:2 * nx], refs[2 * nx], refs[2 * nx + 1:]
    out_axis = 0 if w_t else 1

    @pl.when(pl.program_id(1) == 0)
    def _():
        for w_ref, wb_ref in zip(w_refs, wb_refs):
            w = w_ref[...]
            if valid < w.shape[out_axis]:
                w = jnp.where(lax.broadcasted_iota(jnp.int32, w.shape, out_axis) < valid, w, 0.0)
            wb_ref[...] = w.astype(BF16)

    mul = _dot_nt if w_t else _dot
    acc = mul(x_refs[0][...], wb_refs[0][...])
    for x_ref, wb_ref in zip(x_refs[1:], wb_refs[1:]):
        acc = acc + mul(x_ref[...], wb_ref[...])
    o_ref[...] = acc


def _mm(xs, w, *, ncols, tm, tn, x_off=0, col_off=0, w_t=False):
    nx = len(xs)
    k = xs[0].shape[1]
    m = xs[0].shape[0] - x_off * tm
    n_total = w.shape[0] if w_t else w.shape[1]
    valid = min(tn, n_total - col_off * tn) if ncols == tn else tn
    in_specs = [pl.BlockSpec((tm, k), lambda j, i: (i + x_off, 0)) for _ in xs]
    if w_t:
        in_specs += [pl.BlockSpec((tn, k), functools.partial(lambda j, i, kk: (j + col_off, kk), kk=kk))
                     for kk in range(nx)]
    else:
        in_specs += [pl.BlockSpec((k, tn), functools.partial(lambda j, i, kk: (kk, j + col_off), kk=kk))
                     for kk in range(nx)]
    return pl.pallas_call(
        functools.partial(_mm_kernel, nx=nx, valid=valid, w_t=w_t),
        grid=(ncols // tn, m // tm),
        in_specs=in_specs,
        out_specs=pl.BlockSpec((tm, tn), lambda j, i: (i, j)),
        out_shape=jax.ShapeDtypeStruct((m, ncols), F32),
        scratch_shapes=[pltpu.VMEM((tn, k) if w_t else (k, tn), BF16) for _ in xs],
        compiler_params=_params("arbitrary", "arbitrary"),
        name="proj",
    )(*xs, *([w] * nx))


def _conv_block(cur, prev8, next8, w_ref, b_ref, i, nblk):
    ctx_edge = T_CTX // ROWS
    keep_prev = jnp.logical_and(i != 0, i != ctx_edge)
    keep_next = jnp.logical_and(i != nblk - 1, i != ctx_edge - 1)
    prev8 = jnp.where(keep_prev, prev8, 0.0)
    next8 = jnp.where(keep_next, next8, 0.0)
    ext = jnp.concatenate([prev8, cur, next8], axis=0)
    n_ext = ROWS + 2 * SUBLANES
    y = b_ref[...] + w_ref[CONV_W // 2:CONV_W // 2 + 1, :] * cur
    for j in range(CONV_W):
        off = j - CONV_W // 2
        if off != 0:
            shifted = pltpu.roll(ext, (-off) % n_ext, axis=0)[SUBLANES:SUBLANES + ROWS]
            y = y + w_ref[j:j + 1, :] * shifted
    return y


def _halo_specs(width, col_blk, nblk):
    per = ROWS // SUBLANES
    last8 = N_ALL // SUBLANES - 1
    return [
        pl.BlockSpec((ROWS, width), lambda i, *a: (i, col_blk(*a))),
        pl.BlockSpec((SUBLANES, width), lambda i, *a: (jnp.maximum(i * per - 1, 0), col_blk(*a))),
        pl.BlockSpec((SUBLANES, width), lambda i, *a: (jnp.minimum((i + 1) * per, last8), col_blk(*a))),
    ]


def _scan_tiles(a_ref, b_ref, h_ref, carry_ref, reverse):
    ntile = ROWS // SUBLANES
    row = lax.broadcasted_iota(jnp.int32, (SUBLANES, LRU_W), 0)

    def body(t, carry):
        tt = (ntile - 1 - t) if reverse else t
        rows = pl.ds(pl.multiple_of(tt * SUBLANES, SUBLANES), SUBLANES)
        a = a_ref[rows, :]
        b = b_ref[rows, :]
        for d in (1, 2, 4):
            shift = (SUBLANES - d) if reverse else d
            a_sh = pltpu.roll(a, shift, axis=0)
            b_sh = pltpu.roll(b, shift, axis=0)
            m = (row < SUBLANES - d) if reverse else (row >= d)
            b = jnp.where(m, a * b_sh, 0.0) + b
            a = jnp.where(m, a * a_sh, a)
        h = a * carry + b
        h_ref[rows, :] = h
        edge = h[0:1, :] if reverse else h[SUBLANES - 1:SUBLANES, :]
        return jnp.broadcast_to(edge, (SUBLANES, LRU_W))

    carry_ref[...] = lax.fori_loop(0, ntile, body, carry_ref[...])


def _lru_fwd_kernel(xa_ref, prev_ref, next_ref, cw_ref, cb_ref, wbd_ref, ba_ref, bx_ref, lam_ref,
                    hf_ref, a1_ref, b1_ref, carry_ref, a0_ref, b0_ref):
    i = pl.program_id(0)
    nblk = pl.num_programs(0)

    @pl.when(i == 0)
    def _():
        carry_ref[...] = jnp.zeros_like(carry_ref)

    xa = _conv_block(xa_ref[...], prev_ref[...], next_ref[...], cw_ref, cb_ref, i, nblk)
    xab = xa.astype(BF16)
    nsp = -LRU_C * _softplus(-lam_ref[...])
    gw = wbd_ref.shape[1]
    for g in range(LRU_W // gw):
        cs = slice(g * gw, (g + 1) * gw)
        z = _dot(xab[:, cs], wbd_ref[g])
        for d in range(2):
            r = jax.nn.sigmoid(z[:, d * gw:(d + 1) * gw] + ba_ref[d:d + 1, cs])
            ig = jax.nn.sigmoid(z[:, (2 + d) * gw:(3 + d) * gw] + bx_ref[d:d + 1, cs])
            log_a = r * nsp[d:d + 1, cs]
            a = jnp.exp(log_a)
            b = jnp.sqrt(-jnp.tanh(log_a) * (a * a + 1.0)) * ig * xa[:, cs]
            if d == 0:
                a0_ref[:, cs] = a
                b0_ref[:, cs] = b
            else:
                a1_ref[:, cs] = a
                b1_ref[:, cs] = b
    _scan_tiles(a0_ref, b0_ref, hf_ref, carry_ref, reverse=False)


def _gelu_tanh(v):
    return 0.5 * v * (1.0 + jnp.tanh(math.sqrt(2.0 / math.pi) * (v + 0.044715 * (v * v * v))))


def _lru_bwd_kernel(a1_ref, b1_ref, hf_ref, ga_ref, ya_ref, carry_ref, hb_ref):
    @pl.when(pl.program_id(0) == 0)
    def _():
        carry_ref[...] = jnp.zeros_like(carry_ref)

    _scan_tiles(a1_ref, b1_ref, hb_ref, carry_ref, reverse=True)
    ya_ref[...] = ((hf_ref[...] + hb_ref[...]) * _gelu_tanh(ga_ref[...])).astype(BF16)


def _rev_block(i, nblk):
    nctx = T_CTX // ROWS
    return jnp.where(i < nctx, nctx - 1 - i, nblk - 1 - (i - nctx))


def _lru(p0, conv_w, conv_b, wbd, ba, bx, lam):
    nblk = N_ALL // ROWS
    full = lambda shape: pl.BlockSpec(shape, lambda i: (0,) * len(shape))
    hf, a1, b1 = pl.pallas_call(
        _lru_fwd_kernel,
        grid=(nblk,),
        in_specs=_halo_specs(LRU_W, lambda: 0, nblk) + [
            full((CONV_W, LRU_W)), full((1, LRU_W)), full(wbd.shape),
            full((2, LRU_W)), full((2, LRU_W)), full((2, LRU_W)),
        ],
        out_specs=[pl.BlockSpec((ROWS, LRU_W), lambda i: (i, 0))] * 3,
        out_shape=[jax.ShapeDtypeStruct((N_ALL, LRU_W), F32)] * 3,
        scratch_shapes=[pltpu.VMEM((SUBLANES, LRU_W), F32), pltpu.VMEM((ROWS, LRU_W), F32),
                        pltpu.VMEM((ROWS, LRU_W), F32)],
        compiler_params=_params("arbitrary"),
        name="lru_fwd",
    )(p0, p0, p0, conv_w, conv_b, wbd, ba, bx, lam)
    rev = lambda i: (_rev_block(i, nblk), 0)
    ya = pl.pallas_call(
        _lru_bwd_kernel,
        grid=(nblk,),
        in_specs=[pl.BlockSpec((ROWS, LRU_W), rev)] * 3
        + [pl.BlockSpec((ROWS, LRU_W), lambda i: (_rev_block(i, nblk), 1))],
        out_specs=pl.BlockSpec((ROWS, LRU_W), rev),
        out_shape=jax.ShapeDtypeStruct((N_ALL, LRU_W), BF16),
        scratch_shapes=[pltpu.VMEM((SUBLANES, LRU_W), F32), pltpu.VMEM((ROWS, LRU_W), F32)],
        compiler_params=_params("arbitrary"),
        name="lru_bwd",
    )(a1, b1, hf, p0)
    return ya


def _tri(reverse):
    r = lax.broadcasted_iota(jnp.int32, (CHUNK, CHUNK), 0)
    c = lax.broadcasted_iota(jnp.int32, (CHUNK, CHUNK), 1)
    return (c >= r) if reverse else (c <= r)


def _cumsum_rows(tri_b, v):
    hi = v.astype(BF16)
    r1 = v - hi.astype(F32)
    mid = r1.astype(BF16)
    lo = (r1 - mid.astype(F32)).astype(BF16)
    return _dot(tri_b, hi) + _dot(tri_b, mid) + _dot(tri_b, lo)


def _gla_kernel(*refs, reverse):
    if reverse:
        (q_ref, k_ref, v_ref, ad_ref, wg_ref, bg_ref, of_ref, og_ref, gn_ref, o_ref,
         s_ref, qd_s, klo_s, khi_s, ke_s, dec_s, s2_s, upd_s) = refs
    else:
        (q_ref, k_ref, v_ref, ad_ref, wg_ref, bg_ref, o_ref,
         s_ref, qd_s, klo_s, khi_s, ke_s, dec_s, s2_s, upd_s) = refs
    d = 1 if reverse else 0

    @pl.when(pl.program_id(0) == 0)
    def _():
        s_ref[...] = jnp.zeros_like(s_ref)

    nch = ROWS // CHUNK
    pw = 2 * GLA_DK
    row2 = lax.broadcasted_iota(jnp.int32, (CHUNK, pw), 0)
    col2 = jnp.bitwise_and(lax.broadcasted_iota(jnp.int32, (CHUNK, pw), 1), GLA_DK - 1)
    causal2 = (col2 >= row2) if reverse else (col2 <= row2)
    eye = (lax.broadcasted_iota(jnp.int32, (pw, pw), 0) == lax.broadcasted_iota(jnp.int32, (pw, pw), 1)).astype(F32)
    zeros_v = jnp.zeros((CHUNK, GLA_DV), BF16)

    rr = lax.broadcasted_iota(jnp.int32, (ROWS, ROWS), 0)
    rc = lax.broadcasted_iota(jnp.int32, (ROWS, ROWS), 1)
    same = (rr // CHUNK) == (rc // CHUNK)
    tri_b = jnp.logical_and(same, (rc >= rr) if reverse else (rc <= rr)).astype(BF16)
    ad = ad_ref[:, d * GLA_RANK:(d + 1) * GLA_RANK]
    z = _dot(ad.astype(BF16), wg_ref[...].astype(BF16)) + bg_ref[...]
    lg = -_softplus(-z) * (1.0 / GLA_TAU)
    cum = _cumsum_rows(tri_b, lg)
    cum3 = cum.reshape(nch, CHUNK, GLA_QK)
    last3 = cum3[:, 0:1, :] if reverse else cum3[:, CHUNK - 1:CHUNK, :]
    to_end = jnp.exp(last3 - cum3).reshape(ROWS, GLA_QK)
    k = k_ref[...]
    k_inv = k * jnp.exp(-cum)
    low_all = jnp.bitwise_and(lax.broadcasted_iota(jnp.int32, (1, GLA_QK), 1), pw - 1) < GLA_DK
    qd_s[...] = (q_ref[...] * (GLA_DK ** -0.5) * jnp.exp(cum)).astype(BF16)
    klo_s[...] = jnp.where(low_all, k_inv, 0.0).astype(BF16)
    khi_s[...] = jnp.where(low_all, 0.0, k_inv).astype(BF16)
    ke_s[...] = (k * to_end).astype(BF16)
    dec_s[0:nch, :] = jnp.exp(last3.reshape(nch, GLA_QK))

    npair = GLA_H // 2
    for c in range(nch):
        rs = slice(c * CHUNK, (c + 1) * CHUNK)
        for p in range(npair):
            ls = slice(p * pw, (p + 1) * pw)
            kbd = jnp.concatenate([klo_s[rs, ls], khi_s[rs, ls]], axis=0)
            s2_s[c * npair + p] = jnp.where(causal2, _dot_nt(qd_s[rs, ls], kbd), 0.0).astype(BF16)
            vp = v_ref[rs, p * 2 * GLA_DV:(p + 1) * 2 * GLA_DV].astype(BF16)
            upd_s[c * npair + p] = _dot_tn(ke_s[rs, ls], vp)

    for cc in range(nch):
        c = (nch - 1 - cc) if reverse else cc
        rs = slice(c * CHUNK, (c + 1) * CHUNK)
        decay = dec_s[c:c + 1, :]
        for p in range(npair):
            ls = slice(p * pw, (p + 1) * pw)
            vs = slice(p * 2 * GLA_DV, (p + 1) * 2 * GLA_DV)
            qb = qd_s[rs, ls]
            s2 = s2_s[c * npair + p]
            upd = upd_s[c * npair + p]
            vp = v_ref[rs, vs].astype(BF16)
            vbd = jnp.concatenate([jnp.concatenate([vp[:, :GLA_DV], zeros_v], axis=1),
                                   jnp.concatenate([zeros_v, vp[:, GLA_DV:]], axis=1)], axis=0)
            sp = s_ref[p]
            o = _dot(jnp.concatenate([s2, qb], axis=1), jnp.concatenate([vbd, sp.astype(BF16)], axis=0))
            if reverse:
                for hh in range(2):
                    hs = slice((2 * p + hh) * GLA_DV, (2 * p + hh + 1) * GLA_DV)
                    tot = o[:, hh * GLA_DV:(hh + 1) * GLA_DV] + of_ref[rs, hs]
                    ms = jnp.mean(tot * tot, axis=-1, keepdims=True)
                    y = tot * lax.rsqrt(ms + EPS) * gn_ref[:, hs] * _silu(og_ref[rs, hs])
                    o_ref[rs, hs] = y.astype(BF16)
            else:
                o_ref[rs, vs] = o
            dcol =jnp.sum(eye * decay[:, ls], axis=1, keepdims=True)
            for hh in range(2):
                qr = slice(hh * GLA_DK, (hh + 1) * GLA_DK)
                qc = slice(hh * GLA_DV, (hh + 1) * GLA_DV)
                s_ref[p, qr, qc] = dcol[qr] * sp[qr, qc] + upd[qr, qc]


def _gla(p0, pad, wg, bg, gnorm, of=None):
    reverse = of is not None
    nblk = N_ALL // ROWS
    blk = (lambda i: _rev_block(i, nblk)) if reverse else (lambda i: i)
    d = 1 if reverse else 0
    qk_blk = 2 * LRU_W // GLA_QK
    in_specs = [
        pl.BlockSpec((ROWS, GLA_QK), lambda i: (blk(i), qk_blk)),
        pl.BlockSpec((ROWS, GLA_QK), lambda i: (blk(i), qk_blk + 1)),
        pl.BlockSpec((ROWS, GLA_V), lambda i: (blk(i), 3)),
        pl.BlockSpec((ROWS, LANES), lambda i: (blk(i), 0)),
        pl.BlockSpec((None, GLA_RANK, GLA_QK), lambda i: (d, 0, 0)),
        pl.BlockSpec((None, 1, GLA_QK), lambda i: (d, 0, 0)),
    ]
    args = [p0, p0, p0, pad, wg, bg.reshape(2, 1, GLA_QK)]
    if reverse:
        in_specs += [
            pl.BlockSpec((ROWS, GLA_V), lambda i: (blk(i), 0)),
            pl.BlockSpec((ROWS, GLA_V), lambda i: (blk(i), 4)),
            pl.BlockSpec((1, GLA_V), lambda i: (0, 0)),
        ]
        args += [of, p0, gnorm]
    return pl.pallas_call(
        functools.partial(_gla_kernel, reverse=reverse),
        grid=(nblk,),
        in_specs=in_specs,
        out_specs=pl.BlockSpec((ROWS, GLA_V), lambda i: (blk(i), 0)),
        out_shape=jax.ShapeDtypeStruct((N_ALL, GLA_V), BF16 if reverse else F32),
        scratch_shapes=[pltpu.VMEM((GLA_H // 2, 2 * GLA_DK, 2 * GLA_DV), F32)]
        + [pltpu.VMEM((ROWS, GLA_QK), BF16) for _ in range(4)] + [pltpu.VMEM((SUBLANES, GLA_QK), F32)]
        + [pltpu.VMEM((ROWS // CHUNK * GLA_H // 2, CHUNK, 2 * GLA_DK), BF16),
           pltpu.VMEM((ROWS // CHUNK * GLA_H // 2, 2 * GLA_DK, 2 * GLA_DV), F32)],
        compiler_params=_params("arbitrary"),
        name="gla_bwd" if reverse else "gla_fwd",
    )(*args)


def _conv_silu_kernel(x_ref, prev_ref, next_ref, w_ref, b_ref, o_ref):
    i = pl.program_id(0)
    y = _conv_block(x_ref[...], prev_ref[...], next_ref[...], w_ref, b_ref, i, pl.num_programs(0))
    o_ref[...] = _silu(y)


def _conv_silu(p1, conv_w, conv_b):
    nblk = N_ALL // ROWS
    tc = 2048
    off = SSD_INNER // tc
    return pl.pallas_call(
        _conv_silu_kernel,
        grid=(nblk, SSD_XBC // tc),
        in_specs=_halo_specs(tc, lambda j: j + off, nblk) + [
            pl.BlockSpec((CONV_W, tc), lambda i, j: (0, j)),
            pl.BlockSpec((1, tc), lambda i, j: (0, j)),
        ],
        out_specs=pl.BlockSpec((ROWS, tc), lambda i, j: (i, j)),
        out_shape=jax.ShapeDtypeStruct((N_ALL, SSD_XBC), F32),
        compiler_params=_params("arbitrary", "arbitrary"),
        name="ssd_conv",
    )(p1, p1, p1, conv_w, conv_b)


def _ssd_kernel(*refs, reverse):
    if reverse:
        (xs_ref, b_ref, c_ref, dt_ref, dtb_ref, alog_ref, acc_ref, z_ref, dsk_ref, gn_ref, y_ref,
         st_ref, cum_s, w2_s, ct2_s, dt2_s, ybuf) = refs
    else:
        xs_ref, b_ref, c_ref, dt_ref, dtb_ref, alog_ref, y_ref, st_ref, cum_s, w2_s, ct2_s, dt2_s = refs
    d = 1 if reverse else 0
    i = pl.program_id(0)

    @pl.when(i == 0)
    def _():
        st_ref[...] = jnp.zeros_like(st_ref)

    tri_b = _tri(reverse).astype(BF16)
    row2 = lax.broadcasted_iota(jnp.int32, (CHUNK, 2 * SSD_P), 0)
    col2 = jnp.bitwise_and(lax.broadcasted_iota(jnp.int32, (CHUNK, 2 * SSD_P), 1), SSD_P - 1)
    causal2 = (col2 >= row2) if reverse else (col2 <= row2)
    low = lax.broadcasted_iota(jnp.int32, (1, 2 * SSD_P), 1) < SSD_P
    a_neg = -jnp.exp(alog_ref[...])
    nch = ROWS // CHUNK
    hg = SSD_H // SSD_G
    pw = 2 * SSD_P

    low_c = lax.broadcasted_iota(jnp.int32, (CHUNK, 2 * SSD_P), 1) < SSD_P

    def pair(v, h0):
        if v.shape == (CHUNK, 2 * SSD_P):
            return jnp.take_along_axis(v, jnp.where(low_c, h0, h0 + 1), axis=1)
        return jnp.where(low, v[:, h0:h0 + 1], v[:, h0 + 1:h0 + 2])

    dtv = _softplus(dt_ref[:, d * SSD_H:(d + 1) * SSD_H] + dtb_ref[...])
    la = dtv * a_neg
    for c in range(nch):
        rs = slice(c * CHUNK, (c + 1) * CHUNK)
        cum = _cumsum_rows(tri_b, la[rs])
        last = cum[0:1, :] if reverse else cum[CHUNK - 1:CHUNK, :]
        w2 = dtv[rs] * jnp.exp(last - cum)
        cum_s[c] = jnp.concatenate([cum, cum], axis=1)
        w2_s[c] = jnp.concatenate([w2, w2], axis=1)
        cum_t = cum.T
        dt_t = dtv[rs].T
        ct2_s[c] = jnp.concatenate([cum_t, cum_t], axis=1)
        dt2_s[c] = jnp.concatenate([dt_t, dt_t], axis=1)

    def chunk(cc, carry):
        c = (nch - 1 - cc) if reverse else cc
        rs = pl.ds(pl.multiple_of(c * CHUNK, CHUNK), CHUNK)
        cum = cum_s[c]
        w2 = w2_s[c]
        ct2 = ct2_s[c]
        dt2 = dt2_s[c]
        dec = jnp.exp(cum[0:1, :] if reverse else cum[CHUNK - 1:CHUNK, :])
        cbs, yis = [], []
        for g in range(SSD_G):
            gs = slice(g * SSD_Z, (g + 1) * SSD_Z)
            cg = c_ref[rs, gs].astype(BF16)
            cbs.append(_dot_nt(cg, b_ref[rs, gs].astype(BF16)))
            yis.append(_dot(cg, st_ref[g].astype(BF16)))
        for g in range(SSD_G):
            gs = slice(g * SSD_Z, (g + 1) * SSD_Z)
            bgb = b_ref[rs, gs].astype(BF16)
            cb2 = jnp.concatenate([cbs[g], cbs[g]], axis=1)
            st = st_ref[g]
            y_inter = yis[g]
            x2s, decs = [], []
            for pr in range(hg // 2):
                h0 = g * hg + 2 * pr
                ps = slice(h0 * SSD_P, (h0 + 2) * SSD_P)
                xp = xs_ref[rs, ps]
                ccol = pair(cum, h0)
                rrow = jnp.where(low, ct2[h0:h0 + 1, :], ct2[h0 + 1:h0 + 2, :])
                drow = jnp.where(low, dt2[h0:h0 + 1, :], dt2[h0 + 1:h0 + 2, :])
                seg = jnp.exp(jnp.where(causal2, ccol - rrow, -jnp.inf))
                m = (cb2 * seg * drow).astype(BF16)
                xbd = jnp.concatenate([jnp.where(low, xp, 0.0), jnp.where(low, 0.0, xp)], axis=0).astype(BF16)
                y = _dot(m, xbd) + y_inter[:, pr * pw:(pr + 1) * pw] * jnp.exp(ccol)
                if reverse:
                    ybuf[:, ps] = y + acc_ref[rs, ps]
                else:
                    y_ref[rs, ps] = y
                x2s.append((xp * pair(w2, h0)).astype(BF16))
                decs.append(pair(dec, h0))
            upd = _dot_tn(bgb, jnp.concatenate(x2s, axis=1))
            st_ref[g] = st * jnp.concatenate(decs, axis=1) + upd
        if reverse:
            u = (ybuf[...] + dsk_ref[...] * xs_ref[rs, :]) * _silu(z_ref[rs, :])
            ms = jnp.mean(u * u, axis=-1, keepdims=True)
            y_ref[rs, :] = (u * lax.rsqrt(ms + EPS) * gn_ref[...]).astype(BF16)
        return carry

    lax.fori_loop(0, nch, chunk, 0)


def _ssd(xbc, pdt, dt_bias, a_log, fin=None):
    reverse = fin is not None
    nblk = N_ALL // ROWS
    nctx = T_CTX // ROWS
    blk = (lambda i: _rev_block(i, nblk)) if reverse else (lambda i: i)
    d = 1 if reverse else 0
    gz = SSD_G * SSD_Z
    full = lambda i: (blk(i), 0)
    in_specs = [
        pl.BlockSpec((ROWS, SSD_INNER), full),
        pl.BlockSpec((ROWS, gz), lambda i: (blk(i), SSD_INNER // gz)),
        pl.BlockSpec((ROWS, gz), lambda i: (blk(i), SSD_INNER // gz + 1)),
        pl.BlockSpec((ROWS, LANES), full),
        pl.BlockSpec((None, 1, SSD_H), lambda i: (d, 0, 0)),
        pl.BlockSpec((None, 1, SSD_H), lambda i: (d, 0, 0)),
    ]
    args = [xbc, xbc, xbc, pdt, dt_bias.reshape(2, 1, SSD_H), a_log.reshape(2, 1, SSD_H)]
    nch = ROWS // CHUNK
    scratch = [pltpu.VMEM((SSD_G, SSD_Z, SSD_INNER // SSD_G), F32),
               pltpu.VMEM((nch, CHUNK, 2 * SSD_H), F32), pltpu.VMEM((nch, CHUNK, 2 * SSD_H), F32),
               pltpu.VMEM((nch, SSD_H, 2 * CHUNK), F32), pltpu.VMEM((nch, SSD_H, 2 * CHUNK), F32)]
    if reverse:
        yf, p1, d_rep, norm_g = fin
        in_specs += [pl.BlockSpec((ROWS, SSD_INNER), full), pl.BlockSpec((ROWS, SSD_INNER), full),
                     pl.BlockSpec((1, SSD_INNER), lambda i: (0, 0)), pl.BlockSpec((1, SSD_INNER), lambda i: (0, 0))]
        args += [yf, p1, d_rep, norm_g]
        scratch.append(pltpu.VMEM((CHUNK, SSD_INNER), F32))
        out_spec = pl.BlockSpec((ROWS, SSD_INNER), lambda i: (blk(jnp.maximum(i, nctx)) - nctx, 0))
        out_shape = jax.ShapeDtypeStruct((T_LAT, SSD_INNER), BF16)
    else:
        out_spec = pl.BlockSpec((ROWS, SSD_INNER), full)
        out_shape = jax.ShapeDtypeStruct((N_ALL, SSD_INNER), F32)
    return pl.pallas_call(
        functools.partial(_ssd_kernel, reverse=reverse),
        grid=(nblk,),
        in_specs=in_specs,
        out_specs=out_spec,
        out_shape=out_shape,
        scratch_shapes=scratch,
        compiler_params=_params("arbitrary"),
        name="ssd_bwd" if reverse else "ssd_fwd",
    )(*args)


def _gather_wait(h_hbm, buf, sem, slot):
    pltpu.make_async_copy(h_hbm.at[pl.ds(0, MOE_TM), :], buf.at[slot], sem.at[slot]).wait()


def _gather_issue(idx_ref, h_hbm, buf, sem, slot):
    def body(r8, carry):
        base = pl.multiple_of(r8 * SUBLANES, SUBLANES)
        for u in range(SUBLANES):
            tok = idx_ref[0, 0, base + u]
            pltpu.make_async_copy(h_hbm.at[pl.ds(tok, 1), :], buf.at[slot, pl.ds(base + u, 1), :],
                                  sem.at[slot]).start()
        return carry

    lax.fori_loop(0, MOE_TM // SUBLANES, body, 0)


def _expert_block_kernel(be_ref, ws_ref, nx_ref, nu_ref, *refs, n_w, layer, gather, compute):
    if gather:
        icur_ref, inxt_ref, h_hbm = refs[:3]
        refs = refs[3:]
    else:
        x_ref = refs[0]
        refs = refs[1:]
    w_hbm, o_ref = refs[:n_w], refs[n_w]
    scr = refs[n_w + 1:]
    w32, wb, wsem = scr[:n_w], scr[n_w:2 * n_w], scr[2 * n_w]
    m = pl.program_id(0)
    nu = nu_ref[0]
    e = be_ref[m]
    wslot = ws_ref[m]
    first = jnp.logical_and(m < nu, jnp.logical_or(m == 0, e != be_ref[jnp.maximum(m - 1, 0)]))

    def w_copy(k, ee, slot):
        return pltpu.make_async_copy(w_hbm[k].at[layer, ee], w32[k].at[slot], wsem.at[k, slot])

    @pl.when(m == 0)
    def _():
        for k in range(n_w):
            w_copy(k, e, wslot).start()

    if gather:
        gbuf, gsem = scr[2 * n_w + 1:]
        gslot = lax.rem(m, 2)

        @pl.when(m == 0)
        def _():
            _gather_issue(icur_ref, h_hbm, gbuf, gsem, 0)

        @pl.when(m + 1 < nu)
        def _():
            _gather_issue(inxt_ref, h_hbm, gbuf, gsem, 1 - gslot)

    @pl.when(jnp.logical_and(first, nx_ref[m] >= 0))
    def _():
        for k in range(n_w):
            w_copy(k, nx_ref[m], 1 - wslot).start()

    @pl.when(first)
    def _():
        for k in range(n_w):
            w_copy(k, e, wslot).wait()
            wb[k][...] = w32[k][wslot].astype(BF16)

    @pl.when(m < nu)
    def _():
        if gather:
            _gather_wait(h_hbm, gbuf, gsem, gslot)
            x = gbuf[gslot].astype(BF16)
        else:
            x = x_ref[...]
        o_ref[...] = compute(x, *wb).astype(o_ref.dtype)


def _expert_stage(src, ws, plan, layer, n_out, out_dtype, compute, name, slot_tok=None):
    be, wsl, nx, nu = plan
    nblk = be.shape[0]
    gather = slot_tok is not None
    anyspec = pl.BlockSpec(memory_space=pl.ANY)
    used = lambda m, nu_ref: jnp.minimum(m, nu_ref[0] - 1)
    n_w = len(ws)
    wshape = ws[0].shape[2:]
    scratch = [pltpu.VMEM((2,) + wshape, F32) for _ in ws] + [pltpu.VMEM(wshape, BF16) for _ in ws]
    scratch.append(pltpu.SemaphoreType.DMA((n_w, 2)))
    if gather:
        in_specs = [
            pl.BlockSpec((1, 1, MOE_TM), lambda m, be, ws, nx, nu: (used(m, nu), 0, 0), memory_space=pltpu.SMEM),
            pl.BlockSpec((1, 1, MOE_TM), lambda m, be, ws, nx, nu: (used(m + 1, nu), 0, 0),
                         memory_space=pltpu.SMEM),
            anyspec,
        ]
        args = [slot_tok, slot_tok, src]
        scratch += [pltpu.VMEM((2, MOE_TM, src.shape[1]), src.dtype), pltpu.SemaphoreType.DMA((2,))]
    else:
        in_specs = [pl.BlockSpec((MOE_TM, src.shape[1]), lambda m, be, ws, nx, nu: (used(m, nu), 0))]
        args = [src]
    return pl.pallas_call(
        functools.partial(_expert_block_kernel, n_w=n_w, layer=layer, gather=gather, compute=compute),
        grid_spec=pltpu.PrefetchScalarGridSpec(
            num_scalar_prefetch=4,
            grid=(nblk,),
            in_specs=in_specs + [anyspec] * n_w,
            out_specs=pl.BlockSpec((MOE_TM, n_out), lambda m, be, ws, nx, nu: (used(m, nu), 0)),
            scratch_shapes=scratch,
        ),
        out_shape=jax.ShapeDtypeStruct((nblk * MOE_TM, n_out), out_dtype),
        compiler_params=_params("arbitrary"),
        name=name,
    )(be, wsl, nx, nu, *args, *ws)


def _moe(h, eidx, wts, w_gate, w_up, w_down, layer):
    n = h.shape[0]
    i32 = jnp.int32
    flat_e = eidx.reshape(-1)
    experts = jnp.arange(N_EXP, dtype=i32)
    onehot = (flat_e[:, None] == experts[None, :]).astype(i32)
    csum = jnp.cumsum(onehot, axis=0)
    bcount = (csum[-1] + MOE_TM - 1) // MOE_TM
    bend = jnp.cumsum(bcount)
    bstart = bend - bcount
    dest = jnp.sum(onehot * (csum - 1 + (bstart * MOE_TM)[None, :]), axis=1)
    nblk = -(-2 * n // MOE_TM) + N_EXP
    n_used = bend[-1].astype(i32)
    blk = jnp.minimum(jnp.arange(nblk, dtype=i32), n_used - 1)
    block_exp = jnp.sum((blk[:, None] >= bend[None, :]).astype(i32), axis=1)
    owns = bcount > 0
    wslot = (jnp.cumsum(owns.astype(i32)) - 1) % 2
    later = jnp.logical_and(owns[None, :], experts[None, :] > experts[:, None])
    nxt = jnp.min(jnp.where(later, experts[None, :], N_EXP), axis=1)
    nxt = jnp.where(nxt == N_EXP, -1, nxt)
    of_blk = (block_exp[:, None] == experts[None, :]).astype(i32)
    plan = (block_exp.astype(i32), jnp.sum(of_blk * wslot[None, :], axis=1).astype(i32),
            jnp.sum(of_blk * nxt[None, :], axis=1).astype(i32), n_used.reshape(1))
    tok = jnp.tile(jnp.arange(n, dtype=i32), 2)
    pad_tok = jnp.arange(nblk * MOE_TM, dtype=i32) % n
    slot_tok = pad_tok.at[dest].set(tok).reshape(nblk, 1, MOE_TM)
    hb = _expert_stage(h, [w_gate, w_up], plan, layer, D_EXP, BF16,
                       lambda x, wg, wu: _silu(_dot(x, wg[...])) * _dot(x, wu[...]), "moe_up", slot_tok=slot_tok)
    yb = _expert_stage(hb, [w_down], plan, layer, D, F32, lambda x, wd: _dot(x, wd[...]), "moe_down")
    return yb, dest.reshape(2, n), wts.T


def _pos_tables():
    quarter = D // 4
    omega = 1.0 / (10000.0 ** (jnp.arange(quarter, dtype=F32) / quarter))
    ang_r = jnp.arange(T_LAT // GRID_W, dtype=F32)[:, None] * omega
    ang_c = jnp.arange(GRID_W, dtype=F32)[:, None] * omega
    emb_r = jnp.concatenate([jnp.sin(ang_r), jnp.cos(ang_r)], axis=-1)
    emb_c = jnp.concatenate([jnp.sin(ang_c), jnp.cos(ang_c)], axis=-1)
    return emb_r, emb_c


def _block_diag_gates(wa, wx, group):
    nb = wa.shape[1]
    per = group // wa.shape[2]
    eye = jnp.eye(per, dtype=F32)

    def bd(w):
        w = w.reshape(nb // per, per, w.shape[1], w.shape[2])
        return jnp.einsum("gnkj,nm->gnkmj", w, eye).reshape(nb // per, group, group)

    return jnp.concatenate([bd(wa[0]), bd(wa[1]), bd(wx[0]), bd(wx[1])], axis=-1).astype(BF16)


def kernel(x, c, ctx, c_ctx, mod_w, mod_b, norm1_g, norm2_g, ev_w_in, ev_conv_w, ev_conv_b, lru_wa, lru_ba, lru_wx, lru_bx, lru_lambda, gla_wg_up, gla_bg, gla_norm_g, ev_w_out, od_w_in, od_conv_w, od_conv_b, ssd_a_log, ssd_dt_bias, ssd_d, ssd_norm_g, od_w_out, router_w, router_b, exp_w_gate, exp_w_up, exp_w_down, final_norm_g):
    mods = _mod_vectors(c, c_ctx, mod_w, mod_b)
    emb_r, emb_c = _pos_tables()
    rwt = router_w.T
    rb = router_b.reshape(N_EXP, 1)
    tm = N_ALL // 8

    x0, h0 = _prep0(x[0], ctx[0], emb_r, emb_c, mods[0], norm1_g[0:1])
    w_in_t = jnp.swapaxes(ev_w_in[0], 0, 1)
    p0 = _mm([h0], w_in_t, ncols=EVEN_MAIN, tm=tm, tn=1024, w_t=True)
    pad = _mm([h0], w_in_t, ncols=LANES, tm=tm, tn=LANES, col_off=EVEN_MAIN // LANES, w_t=True)
    wbd = _block_diag_gates(lru_wa[0], lru_wx[0], 256)
    ya = _lru(p0, ev_conv_w[0], ev_conv_b[0:1], wbd, lru_ba[0], lru_bx[0], lru_lambda[0])
    of = _gla(p0, pad, gla_wg_up[0], gla_bg[0], None)
    yb = _gla(p0, pad, gla_wg_up[0], gla_bg[0], gla_norm_g[0:1], of=of)
    y0 = _mm([ya, yb], ev_w_out[0], ncols=D, tm=tm, tn=1024)
    nctx = T_CTX // ROWS
    x1, h1, e0, w0 = _token_stage(x0, y0, mods[0], mods[0], norm2_g[0:1], rwt, rb, gate_col=2, shift_col=3,
                                  scale_col=4, ctx_blocks=nctx, route=True)
    f0, dest0, wc0 = _moe(h1, e0, w0, exp_w_gate, exp_w_up, exp_w_down, 0)
    x2, h2 = _token_stage(x1, f0, mods[0], mods[1], norm1_g[1:2], rwt, rb, gate_col=5, shift_col=0,
                          scale_col=1, ctx_blocks=nctx, combine=(dest0, wc0))

    p1 = _mm([h2], od_w_in[0], ncols=ODD_MAIN, tm=tm, tn=1024)
    pdt = _mm([h2], od_w_in[0], ncols=LANES, tm=tm, tn=LANES, col_off=ODD_MAIN // LANES)
    xbc = _conv_silu(p1, od_conv_w[0], od_conv_b[0:1])
    yf = _ssd(xbc, pdt, ssd_dt_bias[0], ssd_a_log[0])
    d_rep = jnp.repeat(ssd_d[0], SSD_P).reshape(1, SSD_INNER)
    gy = _ssd(xbc, pdt, ssd_dt_bias[0], ssd_a_log[0], fin=(yf, p1, d_rep, ssd_norm_g[0:1]))
    y1 = _mm([gy], od_w_out[0], ncols=D, tm=1024, tn=512)
    x3, h3, e1, w1 = _token_stage(x2, y1, mods[1], mods[1], norm2_g[1:2], rwt, rb, gate_col=2, shift_col=3,
                                  scale_col=4, x_off=nctx, route=True)
    f1, dest1, wc1 = _moe(h3, e1, w1, exp_w_gate, exp_w_up, exp_w_down, 1)
    (out,) = _token_stage(x3, f1, mods[1], mods[1], final_norm_g.reshape(1, D), rwt, rb, gate_col=5,
                          combine=(dest1, wc1), final=True)
    return out[None]
```

```python
import functools
import math

import jax
import jax.numpy as jnp
from jax import lax
from jax.experimental import pallas as pl
from jax.experimental.pallas import tpu as pltpu

F32 = jnp.float32
BF16 = jnp.bfloat16
HIGHEST = lax.Precision.HIGHEST

D = 2048
T_LAT = 8192
T_CTX = 256
N_ALL = T_CTX + T_LAT
GRID_W = 64
EPS = 1e-6
CONV_W = 4
LRU_W = 1024
LRU_C = 8.0
GLA_H = 8
GLA_DK = 64
GLA_DV = 128
GLA_QK = GLA_H * GLA_DK
GLA_V = GLA_H * GLA_DV
GLA_RANK = 16
GLA_TAU = 16.0
CHUNK = 64
EVEN_MAIN = 2 * LRU_W + 2 * GLA_QK + 2 * GLA_V
SSD_INNER = 2 * D
SSD_P = 64
SSD_H = SSD_INNER // SSD_P
SSD_G = 8
SSD_Z = 128
SSD_XBC = SSD_INNER + 2 * SSD_G * SSD_Z
ODD_MAIN = SSD_INNER + SSD_XBC
N_EXP = 32
N_GRP = 4
GRP = N_EXP // N_GRP
D_EXP = 1024
MOE_TM = 256

ROWS = 256
LANES = 128
SUBLANES = 8
VMEM_LIMIT = 56 * 1024 * 1024


def _params(*sem):
    return pltpu.CompilerParams(dimension_semantics=sem, vmem_limit_bytes=VMEM_LIMIT)


def _silu(v):
    return v * jax.nn.sigmoid(v)


def _softplus(v):
    return jnp.maximum(v, 0.0) + jnp.log1p(jnp.exp(-jnp.abs(v)))


def _dot(a, b):
    return jnp.dot(a, b, preferred_element_type=F32)


def _dot_nt(a, b):
    return lax.dot_general(a, b, (((1,), (1,)), ((), ())), preferred_element_type=F32)


def _dot_tn(a, b):
    return lax.dot_general(a, b, (((0,), (0,)), ((), ())), preferred_element_type=F32)


def _mod_kernel(s_ref, w_ref, b_ref, o_ref):
    tn = w_ref.shape[1]
    nrep = tn // LANES

    def body(r, acc):
        a0, a1 = acc
        rows = pl.ds(pl.multiple_of(r * SUBLANES, SUBLANES), SUBLANES)
        w = w_ref[rows, :]
        s0 = _silu(s_ref[0, rows, :])
        s1 = _silu(s_ref[1, rows, :])
        a0 = a0 + w * jnp.concatenate([s0] * nrep, axis=1)
        a1 = a1 + w * jnp.concatenate([s1] * nrep, axis=1)
        return a0, a1

    zero = jnp.zeros((SUBLANES, tn), F32)
    a0, a1 = lax.fori_loop(0, D // SUBLANES, body, (zero, zero), unroll=4)
    o_ref[0:1, :] = jnp.sum(a0, axis=0, keepdims=True) + b_ref[...]
    o_ref[1:2, :] = jnp.sum(a1, axis=0, keepdims=True) + b_ref[...]


def _mod_vectors(c, c_ctx, mod_w, mod_b):
    depth = mod_w.shape[0]
    tn = 1024
    s = jnp.broadcast_to(jnp.stack([c[0], c_ctx])[:, :, None], (2, D, LANES))
    return pl.pallas_call(
        _mod_kernel,
        grid=(depth, 6 * D // tn),
        in_specs=[
            pl.BlockSpec((2, D, LANES), lambda l, j: (0, 0, 0)),
            pl.BlockSpec((None, D, tn), lambda l, j: (l, 0, j)),
            pl.BlockSpec((None, 1, tn), lambda l, j: (l, 0, j)),
        ],
        out_specs=pl.BlockSpec((None, 2, tn), lambda l, j: (l, 0, j)),
        out_shape=jax.ShapeDtypeStruct((depth, 2, 6 * D), F32),
        compiler_params=_params("arbitrary", "arbitrary"),
        name="mod_vectors",
    )(s, mod_w, mod_b.reshape(depth, 1, 6 * D))


def _mod_row(mod_ref, kind, col):
    return mod_ref[pl.ds(kind, 1), col * D:(col + 1) * D]


def _ada_norm(xv, g, shift, scale):
    ms = jnp.mean(xv * xv, axis=-1, keepdims=True)
    return (xv * lax.rsqrt(ms + EPS) * g) * (1.0 + scale) + shift


def _prep0_kernel(x_ref, ctx_ref, er_ref, ec_ref, mod_ref, g_ref, xo_ref, ho_ref):
    i = pl.program_id(0)

    @pl.when(i == 0)
    def _():
        xo_ref[...] = ctx_ref[...]

    @pl.when(i > 0)
    def _():
        r0 = (i - 1) * (ROWS // GRID_W)
        for j in range(ROWS // GRID_W):
            rs = slice(j * GRID_W, (j + 1) * GRID_W)
            xo_ref[rs, 0:D // 2] = x_ref[rs, 0:D // 2] + er_ref[pl.ds(r0 + j, 1), :]
            xo_ref[rs, D // 2:D] = x_ref[rs, D // 2:D] + ec_ref[...]

    kind = jnp.where(i == 0, 1, 0)
    h = _ada_norm(xo_ref[...], g_ref[...], _mod_row(mod_ref, kind, 0), _mod_row(mod_ref, kind, 1))
    ho_ref[...] = h.astype(BF16)


def _prep0(x, ctx, emb_r, emb_c, mod0, g):
    nblk = N_ALL // ROWS
    return pl.pallas_call(
        _prep0_kernel,
        grid=(nblk,),
        in_specs=[
            pl.BlockSpec((ROWS, D), lambda i: (jnp.maximum(i - 1, 0), 0)),
            pl.BlockSpec((ROWS, D), lambda i: (0, 0)),
            pl.BlockSpec(emb_r.shape, lambda i: (0, 0)),
            pl.BlockSpec(emb_c.shape, lambda i: (0, 0)),
            pl.BlockSpec((2, 6 * D), lambda i: (0, 0)),
            pl.BlockSpec((1, D), lambda i: (0, 0)),
        ],
        out_specs=[pl.BlockSpec((ROWS, D), lambda i: (i, 0)), pl.BlockSpec((ROWS, D), lambda i: (i, 0))],
        out_shape=[jax.ShapeDtypeStruct((N_ALL, D), F32), jax.ShapeDtypeStruct((N_ALL, D), BF16)],
        compiler_params=_params("arbitrary"),
        name="embed_norm",
    )(x, ctx, emb_r, emb_c, mod0, g)


def _route(hf32, rwt_ref, rb_ref, eo_ref, wo_ref):
    logits = lax.dot_general(rwt_ref[...], hf32, (((1,), (1,)), ((), ())),
                             precision=HIGHEST, preferred_element_type=F32)
    s = jax.nn.sigmoid(logits)
    sel = s + rb_ref[...]
    row = lax.broadcasted_iota(jnp.int32, (GRP, ROWS), 0)
    neg = jnp.float32(-jnp.inf)
    gs, i1s, i2s = [], [], []
    for g in range(N_GRP):
        blk = sel[g * GRP:(g + 1) * GRP, :]
        m1 = jnp.max(blk, axis=0, keepdims=True)
        i1 = jnp.min(jnp.where(blk == m1, row, GRP), axis=0, keepdims=True)
        blk2 = jnp.where(row == i1, neg, blk)
        m2 = jnp.max(blk2, axis=0, keepdims=True)
        i2 = jnp.min(jnp.where(blk2 == m2, row, GRP), axis=0, keepdims=True)
        gs.append(m1 + m2)
        i1s.append(i1)
        i2s.append(i2)
    best, gi, i1, i2 = gs[0], jnp.zeros((1, ROWS), jnp.int32), i1s[0], i2s[0]
    for g in range(1, N_GRP):
        upd = gs[g] > best
        best = jnp.where(upd, gs[g], best)
        gi = jnp.where(upd, g, gi)
        i1 = jnp.where(upd, i1s[g], i1)
        i2 = jnp.where(upd, i2s[g], i2)
    e1 = gi * GRP + i1
    e2 = gi * GRP + i2
    erow = lax.broadcasted_iota(jnp.int32, (N_EXP, ROWS), 0)
    s1 = jnp.sum(jnp.where(erow == e1, s, 0.0), axis=0, keepdims=True)
    s2 = jnp.sum(jnp.where(erow == e2, s, 0.0), axis=0, keepdims=True)
    tot = s1 + s2
    eo_ref[0:1, :] = e1
    eo_ref[1:2, :] = e2
    wo_ref[0:1, :] = s1 / tot
    wo_ref[1:2, :] = s2 / tot


def _combine_wait(yb_hbm, ybuf, sem, slot):
    pltpu.make_async_copy(yb_hbm.at[pl.ds(0, 2 * ROWS), :], ybuf.at[slot], sem.at[slot]).wait()


def _combine_issue(dest_ref, yb_hbm, ybuf, sem, slot):
    def body(r, carry):
        for k in range(2):
            row = dest_ref[0, k, r]
            pltpu.make_async_copy(yb_hbm.at[pl.ds(row, 1), :], ybuf.at[slot, pl.ds(k * ROWS + r, 1), :],
                                  sem.at[slot]).start()
        return carry

    lax.fori_loop(0, ROWS, body, 0, unroll=8)


def _token_kernel(*refs, gate_col, shift_col, scale_col, ctx_blocks, route, combine, final):
    refs = list(refs)
    if combine:
        dcur_ref, dnxt_ref = refs.pop(0), refs.pop(0)
    x_ref, y_ref = refs.pop(0), refs.pop(0)
    if combine:
        wc_ref = refs.pop(0)
    modg_ref, modn_ref, g_ref, rwt_ref, rb_ref = (refs.pop(0) for _ in range(5))
    i = pl.program_id(0)
    if combine:
        ybuf, sem = refs[-2], refs[-1]
        slot = lax.rem(i, 2)

        @pl.when(i == 0)
        def _():
            _combine_issue(dcur_ref, y_ref, ybuf, sem, 0)

        @pl.when(i + 1 < pl.num_programs(0))
        def _():
            _combine_issue(dnxt_ref, y_ref, ybuf, sem, 1 - slot)

        _combine_wait(y_ref, ybuf, sem, slot)
        y = wc_ref[:, 0:1] * ybuf[slot, 0:ROWS, :] + wc_ref[:, 1:2] * ybuf[slot, ROWS:2 * ROWS, :]
    else:
        y = y_ref[...]
    kind = jnp.where(i < ctx_blocks, 1, 0)
    xn = x_ref[...] + _mod_row(modg_ref, kind, gate_col) * y
    if final:
        ms = jnp.mean(xn * xn, axis=-1, keepdims=True)
        refs[0][...] = xn * lax.rsqrt(ms + EPS) * g_ref[...]
        return
    xo_ref, ho_ref = refs[0], refs[1]
    xo_ref[...] = xn
    h = _ada_norm(xn, g_ref[...], _mod_row(modn_ref, kind, shift_col), _mod_row(modn_ref, kind, scale_col))
    ho_ref[...] = h.astype(ho_ref.dtype)
    if route:
        _route(h, rwt_ref, rb_ref, refs[2], refs[3])


def _token_stage(x, y, modg, modn, g, rwt, rb, *, gate_col, shift_col=0, scale_col=0, x_off=0, ctx_blocks=0,
                 route=False, combine=None, final=False):
    n = x.shape[0] - x_off * ROWS
    nblk = n // ROWS
    row = lambda i: (i, 0)
    const2 = lambda i: (0, 0)
    in_specs, args, scratch = [], [], []
    if combine is not None:
        dest, wc = combine
        dest3 = dest.reshape(2, nblk, ROWS).transpose(1, 0, 2)
        in_specs += [pl.BlockSpec((1, 2, ROWS), lambda i: (i, 0, 0), memory_space=pltpu.SMEM),
                     pl.BlockSpec((1, 2, ROWS), lambda i: (jnp.minimum(i + 1, nblk - 1), 0, 0),
                                  memory_space=pltpu.SMEM)]
        args += [dest3, dest3]
    in_specs.append(pl.BlockSpec((ROWS, D), lambda i: (i + x_off, 0)))
    args.append(x)
    if combine is not None:
        in_specs += [pl.BlockSpec(memory_space=pl.ANY), pl.BlockSpec((ROWS, 2), row)]
        args += [y, wc]
        scratch = [pltpu.VMEM((2, 2 * ROWS, D), F32), pltpu.SemaphoreType.DMA((2,))]
    else:
        in_specs.append(pl.BlockSpec((ROWS, D), row))
        args.append(y)
    in_specs += [pl.BlockSpec((2, 6 * D), const2), pl.BlockSpec((2, 6 * D), const2), pl.BlockSpec((1, D), const2),
                 pl.BlockSpec((N_EXP, D), const2), pl.BlockSpec((N_EXP, 1), const2)]
    args += [modg, modn, g, rwt, rb]
    if final:
        out_specs = [pl.BlockSpec((ROWS, D), row)]
        out_shape = [jax.ShapeDtypeStruct((n, D), F32)]
    else:
        out_specs = [pl.BlockSpec((ROWS, D), row), pl.BlockSpec((ROWS, D), row)]
        out_shape = [jax.ShapeDtypeStruct((n, D), F32), jax.ShapeDtypeStruct((n, D), F32 if route else BF16)]
        if route:
            out_specs += [pl.BlockSpec((2, ROWS), lambda i: (0, i)), pl.BlockSpec((2, ROWS), lambda i: (0, i))]
            out_shape += [jax.ShapeDtypeStruct((2, n), jnp.int32), jax.ShapeDtypeStruct((2, n), F32)]
    kern = functools.partial(_token_kernel, gate_col=gate_col, shift_col=shift_col, scale_col=scale_col,
                             ctx_blocks=ctx_blocks, route=route, combine=combine is not None, final=final)
    return pl.pallas_call(
        kern,
        grid=(nblk,),
        in_specs=in_specs,
        out_specs=out_specs,
        out_shape=out_shape,
        scratch_shapes=scratch,
        compiler_params=_params("arbitrary"),
        name="token_stage",
    )(*args)


def _mm_kernel(*refs, nx, valid, w_t):
    x_refs, w_refs, o_ref, wb_refs = refs[:nx], refs[nx:2 * nx], refs[2 * nx], refs[2 * nx + 1:]
    out_axis = 0 if w_t else 1

    @pl.when(pl.program_id(1) == 0)
    def _():
        for w_ref, wb_ref in zip(w_refs, wb_refs):
            w = w_ref[...]
            if valid < w.shape[out_axis]:
                w = jnp.where(lax.broadcasted_iota(jnp.int32, w.shape, out_axis) < valid, w, 0.0)
            wb_ref[...] = w.astype(BF16)

    mul = _dot_nt if w_t else _dot
    acc = mul(x_refs[0][...], wb_refs[0][...])
    for x_ref, wb_ref in zip(x_refs[1:], wb_refs[1:]):
        acc = acc + mul(x_ref[...], wb_ref[...])
    o_ref[...] = acc


def _mm(xs, w, *, ncols, tm, tn, x_off=0, col_off=0, w_t=False):
    nx = len(xs)
    k = xs[0].shape[1]
    m = xs[0].shape[0] - x_off * tm
    n_total = w.shape[0] if w_t else w.shape[1]
    valid = min(tn, n_total - col_off * tn) if ncols == tn else tn
    in_specs = [pl.BlockSpec((tm, k), lambda j, i: (i + x_off, 0)) for _ in xs]
    if w_t:
        in_specs += [pl.BlockSpec((tn, k), functools.partial(lambda j, i, kk: (j + col_off, kk), kk=kk))
                     for kk in range(nx)]
    else:
        in_specs += [pl.BlockSpec((k, tn), functools.partial(lambda j, i, kk: (kk, j + col_off), kk=kk))
                     for kk in range(nx)]
    return pl.pallas_call(
        functools.partial(_mm_kernel, nx=nx, valid=valid, w_t=w_t),
        grid=(ncols // tn, m // tm),
        in_specs=in_specs,
        out_specs=pl.BlockSpec((tm, tn), lambda j, i: (i, j)),
        out_shape=jax.ShapeDtypeStruct((m, ncols), F32),
        scratch_shapes=[pltpu.VMEM((tn, k) if w_t else (k, tn), BF16) for _ in xs],
        compiler_params=_params("arbitrary", "arbitrary"),
        name="proj",
    )(*xs, *([w] * nx))


def _conv_block(cur, prev8, next8, w_ref, b_ref, i, nblk):
    ctx_edge = T_CTX // ROWS
    keep_prev = jnp.logical_and(i != 0, i != ctx_edge)
    keep_next = jnp.logical_and(i != nblk - 1, i != ctx_edge - 1)
    prev8 = jnp.where(keep_prev, prev8, 0.0)
    next8 = jnp.where(keep_next, next8, 0.0)
    ext = jnp.concatenate([prev8, cur, next8], axis=0)
    n_ext = ROWS + 2 * SUBLANES
    y = b_ref[...] + w_ref[CONV_W // 2:CONV_W // 2 + 1, :] * cur
    for j in range(CONV_W):
        off = j - CONV_W // 2
        if off != 0:
            shifted = pltpu.roll(ext, (-off) % n_ext, axis=0)[SUBLANES:SUBLANES + ROWS]
            y = y + w_ref[j:j + 1, :] * shifted
    return y


def _halo_specs(width, col_blk, nblk):
    per = ROWS // SUBLANES
    last8 = N_ALL // SUBLANES - 1
    return [
        pl.BlockSpec((ROWS, width), lambda i, *a: (i, col_blk(*a))),
        pl.BlockSpec((SUBLANES, width), lambda i, *a: (jnp.maximum(i * per - 1, 0), col_blk(*a))),
        pl.BlockSpec((SUBLANES, width), lambda i, *a: (jnp.minimum((i + 1) * per, last8), col_blk(*a))),
    ]


def _scan_tiles(a_ref, b_ref, h_ref, carry_ref, reverse):
    ntile = ROWS // SUBLANES
    row = lax.broadcasted_iota(jnp.int32, (SUBLANES, LRU_W), 0)

    def body(t, carry):
        tt = (ntile - 1 - t) if reverse else t
        rows = pl.ds(pl.multiple_of(tt * SUBLANES, SUBLANES), SUBLANES)
        a = a_ref[rows, :]
        b = b_ref[rows, :]
        for d in (1, 2, 4):
            shift = (SUBLANES - d) if reverse else d
            a_sh = pltpu.roll(a, shift, axis=0)
            b_sh = pltpu.roll(b, shift, axis=0)
            m = (row < SUBLANES - d) if reverse else (row >= d)
            b = jnp.where(m, a * b_sh, 0.0) + b
            a = jnp.where(m, a * a_sh, a)
        h = a * carry + b
        h_ref[rows, :] = h
        edge = h[0:1, :] if reverse else h[SUBLANES - 1:SUBLANES, :]
        return jnp.broadcast_to(edge, (SUBLANES, LRU_W))

    carry_ref[...] = lax.fori_loop(0, ntile, body, carry_ref[...])


def _lru_fwd_kernel(xa_ref, prev_ref, next_ref, cw_ref, cb_ref, wbd_ref, ba_ref, bx_ref, lam_ref,
                    hf_ref, a1_ref, b1_ref, carry_ref, a0_ref, b0_ref):
    i = pl.program_id(0)
    nblk = pl.num_programs(0)

    @pl.when(i == 0)
    def _():
        carry_ref[...] = jnp.zeros_like(carry_ref)

    xa = _conv_block(xa_ref[...], prev_ref[...], next_ref[...], cw_ref, cb_ref, i, nblk)
    xab = xa.astype(BF16)
    nsp = -LRU_C * _softplus(-lam_ref[...])
    gw = wbd_ref.shape[1]
    for g in range(LRU_W // gw):
        cs = slice(g * gw, (g + 1) * gw)
        z = _dot(xab[:, cs], wbd_ref[g])
        for d in range(2):
            r = jax.nn.sigmoid(z[:, d * gw:(d + 1) * gw] + ba_ref[d:d + 1, cs])
            ig = jax.nn.sigmoid(z[:, (2 + d) * gw:(3 + d) * gw] + bx_ref[d:d + 1, cs])
            log_a = r * nsp[d:d + 1, cs]
            a = jnp.exp(log_a)
            b = jnp.sqrt(-jnp.tanh(log_a) * (a * a + 1.0)) * ig * xa[:, cs]
            if d == 0:
                a0_ref[:, cs] = a
                b0_ref[:, cs] = b
            else:
                a1_ref[:, cs] = a
                b1_ref[:, cs] = b
    _scan_tiles(a0_ref, b0_ref, hf_ref, carry_ref, reverse=False)


def _gelu_tanh(v):
    return 0.5 * v * (1.0 + jnp.tanh(math.sqrt(2.0 / math.pi) * (v + 0.044715 * (v * v * v))))


def _lru_bwd_kernel(a1_ref, b1_ref, hf_ref, ga_ref, ya_ref, carry_ref, hb_ref):
    @pl.when(pl.program_id(0) == 0)
    def _():
        carry_ref[...] = jnp.zeros_like(carry_ref)

    _scan_tiles(a1_ref, b1_ref, hb_ref, carry_ref, reverse=True)
    ya_ref[...] = ((hf_ref[...] + hb_ref[...]) * _gelu_tanh(ga_ref[...])).astype(BF16)


def _rev_block(i, nblk):
    nctx = T_CTX // ROWS
    return jnp.where(i < nctx, nctx - 1 - i, nblk - 1 - (i - nctx))


def _lru(p0, conv_w, conv_b, wbd, ba, bx, lam):
    nblk = N_ALL // ROWS
    full = lambda shape: pl.BlockSpec(shape, lambda i: (0,) * len(shape))
    hf, a1, b1 = pl.pallas_call(
        _lru_fwd_kernel,
        grid=(nblk,),
        in_specs=_halo_specs(LRU_W, lambda: 0, nblk) + [
            full((CONV_W, LRU_W)), full((1, LRU_W)), full(wbd.shape),
            full((2, LRU_W)), full((2, LRU_W)), full((2, LRU_W)),
        ],
        out_specs=[pl.BlockSpec((ROWS, LRU_W), lambda i: (i, 0))] * 3,
        out_shape=[jax.ShapeDtypeStruct((N_ALL, LRU_W), F32)] * 3,
        scratch_shapes=[pltpu.VMEM((SUBLANES, LRU_W), F32), pltpu.VMEM((ROWS, LRU_W), F32),
                        pltpu.VMEM((ROWS, LRU_W), F32)],
        compiler_params=_params("arbitrary"),
        name="lru_fwd",
    )(p0, p0, p0, conv_w, conv_b, wbd, ba, bx, lam)
    rev = lambda i: (_rev_block(i, nblk), 0)
    ya = pl.pallas_call(
        _lru_bwd_kernel,
        grid=(nblk,),
        in_specs=[pl.BlockSpec((ROWS, LRU_W), rev)] * 3
        + [pl.BlockSpec((ROWS, LRU_W), lambda i: (_rev_block(i, nblk), 1))],
        out_specs=pl.BlockSpec((ROWS, LRU_W), rev),
        out_shape=jax.ShapeDtypeStruct((N_ALL, LRU_W), BF16),
        scratch_shapes=[pltpu.VMEM((SUBLANES, LRU_W), F32), pltpu.VMEM((ROWS, LRU_W), F32)],
        compiler_params=_params("arbitrary"),
        name="lru_bwd",
    )(a1, b1, hf, p0)
    return ya


def _tri(reverse):
    r = lax.broadcasted_iota(jnp.int32, (CHUNK, CHUNK), 0)
    c = lax.broadcasted_iota(jnp.int32, (CHUNK, CHUNK), 1)
    return (c >= r) if reverse else (c <= r)


def _cumsum_rows(tri_b, v):
    hi = v.astype(BF16)
    r1 = v - hi.astype(F32)
    mid = r1.astype(BF16)
    lo = (r1 - mid.astype(F32)).astype(BF16)
    return _dot(tri_b, hi) + _dot(tri_b, mid) + _dot(tri_b, lo)


def _gla_kernel(*refs, reverse):
    if reverse:
        (q_ref, k_ref, v_ref, ad_ref, wg_ref, bg_ref, of_ref, og_ref, gn_ref, o_ref,
         s_ref, qd_s, klo_s, khi_s, ke_s, dec_s, s2_s, upd_s) = refs
    else:
        (q_ref, k_ref, v_ref, ad_ref, wg_ref, bg_ref, o_ref,
         s_ref, qd_s, klo_s, khi_s, ke_s, dec_s, s2_s, upd_s) = refs
    d = 1 if reverse else 0

    @pl.when(pl.program_id(0) == 0)
    def _():
        s_ref[...] = jnp.zeros_like(s_ref)

    nch = ROWS // CHUNK
    pw = 2 * GLA_DK
    row2 = lax.broadcasted_iota(jnp.int32, (CHUNK, pw), 0)
    col2 = jnp.bitwise_and(lax.broadcasted_iota(jnp.int32, (CHUNK, pw), 1), GLA_DK - 1)
    causal2 = (col2 >= row2) if reverse else (col2 <= row2)
    eye = (lax.broadcasted_iota(jnp.int32, (pw, pw), 0) == lax.broadcasted_iota(jnp.int32, (pw, pw), 1)).astype(F32)
    zeros_v = jnp.zeros((CHUNK, GLA_DV), BF16)

    rr = lax.broadcasted_iota(jnp.int32, (ROWS, ROWS), 0)
    rc = lax.broadcasted_iota(jnp.int32, (ROWS, ROWS), 1)
    same = (rr // CHUNK) == (rc // CHUNK)
    tri_b = jnp.logical_and(same, (rc >= rr) if reverse else (rc <= rr)).astype(BF16)
    ad = ad_ref[:, d * GLA_RANK:(d + 1) * GLA_RANK]
    z = _dot(ad.astype(BF16), wg_ref[...].astype(BF16)) + bg_ref[...]
    lg = -_softplus(-z) * (1.0 / GLA_TAU)
    cum = _cumsum_rows(tri_b, lg)
    cum3 = cum.reshape(nch, CHUNK, GLA_QK)
    last3 = cum3[:, 0:1, :] if reverse else cum3[:, CHUNK - 1:CHUNK, :]
    to_end = jnp.exp(last3 - cum3).reshape(ROWS, GLA_QK)
    k = k_ref[...]
    k_inv = k * jnp.exp(-cum)
    low_all = jnp.bitwise_and(lax.broadcasted_iota(jnp.int32, (1, GLA_QK), 1), pw - 1) < GLA_DK
    qd_s[...] = (q_ref[...] * (GLA_DK ** -0.5) * jnp.exp(cum)).astype(BF16)
    klo_s[...] = jnp.where(low_all, k_inv, 0.0).astype(BF16)
    khi_s[...] = jnp.where(low_all, 0.0, k_inv).astype(BF16)
    ke_s[...] = (k * to_end).astype(BF16)
    dec_s[0:nch, :] = jnp.exp(last3.reshape(nch, GLA_QK))

    npair = GLA_H // 2
    for c in range(nch):
        rs = slice(c * CHUNK, (c + 1) * CHUNK)
        for p in range(npair):
            ls = slice(p * pw, (p + 1) * pw)
            kbd = jnp.concatenate([klo_s[rs, ls], khi_s[rs, ls]], axis=0)
            s2_s[c * npair + p] = jnp.where(causal2, _dot_nt(qd_s[rs, ls], kbd), 0.0).astype(BF16)
            vp = v_ref[rs, p * 2 * GLA_DV:(p + 1) * 2 * GLA_DV].astype(BF16)
            upd_s[c * npair + p] = _dot_tn(ke_s[rs, ls], vp)

    for cc in range(nch):
        c = (nch - 1 - cc) if reverse else cc
        rs = slice(c * CHUNK, (c + 1) * CHUNK)
        decay = dec_s[c:c + 1, :]
        for p in range(npair):
            ls = slice(p * pw, (p + 1) * pw)
            vs = slice(p * 2 * GLA_DV, (p + 1) * 2 * GLA_DV)
            qb = qd_s[rs, ls]
            s2 = s2_s[c * npair + p]
            upd = upd_s[c * npair + p]
            vp = v_ref[rs, vs].astype(BF16)
            vbd = jnp.concatenate([jnp.concatenate([vp[:, :GLA_DV], zeros_v], axis=1),
                                   jnp.concatenate([zeros_v, vp[:, GLA_DV:]], axis=1)], axis=0)
            sp = s_ref[p]
            o = _dot(jnp.concatenate([s2, qb], axis=1), jnp.concatenate([vbd, sp.astype(BF16)], axis=0))
            if reverse:
                for hh in range(2):
                    hs = slice((2 * p + hh) * GLA_DV, (2 * p + hh + 1) * GLA_DV)
                    tot = o[:, hh * GLA_DV:(hh + 1) * GLA_DV] + of_ref[rs, hs]
                    ms = jnp.mean(tot * tot, axis=-1, keepdims=True)
                    y = tot * lax.rsqrt(ms + EPS) * gn_ref[:, hs] * _silu(og_ref[rs, hs])
                    o_ref[rs, hs] = y.astype(BF16)
            else:
                o_ref[rs, vs] = o
            dcol =jnp.sum(eye * decay[:, ls], axis=1, keepdims=True)
            for hh in range(2):
                qr = slice(hh * GLA_DK, (hh + 1) * GLA_DK)
                qc = slice(hh * GLA_DV, (hh + 1) * GLA_DV)
                s_ref[p, qr, qc] = dcol[qr] * sp[qr, qc] + upd[qr, qc]


def _gla(p0, pad, wg, bg, gnorm, of=None):
    reverse = of is not None
    nblk = N_ALL // ROWS
    blk = (lambda i: _rev_block(i, nblk)) if reverse else (lambda i: i)
    d = 1 if reverse else 0
    qk_blk = 2 * LRU_W // GLA_QK
    in_specs = [
        pl.BlockSpec((ROWS, GLA_QK), lambda i: (blk(i), qk_blk)),
        pl.BlockSpec((ROWS, GLA_QK), lambda i: (blk(i), qk_blk + 1)),
        pl.BlockSpec((ROWS, GLA_V), lambda i: (blk(i), 3)),
        pl.BlockSpec((ROWS, LANES), lambda i: (blk(i), 0)),
        pl.BlockSpec((None, GLA_RANK, GLA_QK), lambda i: (d, 0, 0)),
        pl.BlockSpec((None, 1, GLA_QK), lambda i: (d, 0, 0)),
    ]
    args = [p0, p0, p0, pad, wg, bg.reshape(2, 1, GLA_QK)]
    if reverse:
        in_specs += [
            pl.BlockSpec((ROWS, GLA_V), lambda i: (blk(i), 0)),
            pl.BlockSpec((ROWS, GLA_V), lambda i: (blk(i), 4)),
            pl.BlockSpec((1, GLA_V), lambda i: (0, 0)),
        ]
        args += [of, p0, gnorm]
    return pl.pallas_call(
        functools.partial(_gla_kernel, reverse=reverse),
        grid=(nblk,),
        in_specs=in_specs,
        out_specs=pl.BlockSpec((ROWS, GLA_V), lambda i: (blk(i), 0)),
        out_shape=jax.ShapeDtypeStruct((N_ALL, GLA_V), BF16 if reverse else F32),
        scratch_shapes=[pltpu.VMEM((GLA_H // 2, 2 * GLA_DK, 2 * GLA_DV), F32)]
        + [pltpu.VMEM((ROWS, GLA_QK), BF16) for _ in range(4)] + [pltpu.VMEM((SUBLANES, GLA_QK), F32)]
        + [pltpu.VMEM((ROWS // CHUNK * GLA_H // 2, CHUNK, 2 * GLA_DK), BF16),
           pltpu.VMEM((ROWS // CHUNK * GLA_H // 2, 2 * GLA_DK, 2 * GLA_DV), F32)],
        compiler_params=_params("arbitrary"),
        name="gla_bwd" if reverse else "gla_fwd",
    )(*args)


def _conv_silu_kernel(x_ref, prev_ref, next_ref, w_ref, b_ref, o_ref):
    i = pl.program_id(0)
    y = _conv_block(x_ref[...], prev_ref[...], next_ref[...], w_ref, b_ref, i, pl.num_programs(0))
    o_ref[...] = _silu(y)


def _conv_silu(p1, conv_w, conv_b):
    nblk = N_ALL // ROWS
    tc = 2048
    off = SSD_INNER // tc
    return pl.pallas_call(
        _conv_silu_kernel,
        grid=(nblk, SSD_XBC // tc),
        in_specs=_halo_specs(tc, lambda j: j + off, nblk) + [
            pl.BlockSpec((CONV_W, tc), lambda i, j: (0, j)),
            pl.BlockSpec((1, tc), lambda i, j: (0, j)),
        ],
        out_specs=pl.BlockSpec((ROWS, tc), lambda i, j: (i, j)),
        out_shape=jax.ShapeDtypeStruct((N_ALL, SSD_XBC), F32),
        compiler_params=_params("arbitrary", "arbitrary"),
        name="ssd_conv",
    )(p1, p1, p1, conv_w, conv_b)


def _ssd_kernel(*refs, reverse):
    if reverse:
        (xs_ref, b_ref, c_ref, dt_ref, dtb_ref, alog_ref, acc_ref, z_ref, dsk_ref, gn_ref, y_ref,
         st_ref, cum_s, w2_s, ct2_s, dt2_s, ybuf) = refs
    else:
        xs_ref, b_ref, c_ref, dt_ref, dtb_ref, alog_ref, y_ref, st_ref, cum_s, w2_s, ct2_s, dt2_s = refs
    d = 1 if reverse else 0
    i = pl.program_id(0)

    @pl.when(i == 0)
    def _():
        st_ref[...] = jnp.zeros_like(st_ref)

    tri_b = _tri(reverse).astype(BF16)
    row2 = lax.broadcasted_iota(jnp.int32, (CHUNK, 2 * SSD_P), 0)
    col2 = jnp.bitwise_and(lax.broadcasted_iota(jnp.int32, (CHUNK, 2 * SSD_P), 1), SSD_P - 1)
    causal2 = (col2 >= row2) if reverse else (col2 <= row2)
    low = lax.broadcasted_iota(jnp.int32, (1, 2 * SSD_P), 1) < SSD_P
    a_neg = -jnp.exp(alog_ref[...])
    nch = ROWS // CHUNK
    hg = SSD_H // SSD_G
    pw = 2 * SSD_P

    low_c = lax.broadcasted_iota(jnp.int32, (CHUNK, 2 * SSD_P), 1) < SSD_P

    def pair(v, h0):
        if v.shape == (CHUNK, 2 * SSD_P):
            return jnp.take_along_axis(v, jnp.where(low_c, h0, h0 + 1), axis=1)
        return jnp.where(low, v[:, h0:h0 + 1], v[:, h0 + 1:h0 + 2])

    dtv = _softplus(dt_ref[:, d * SSD_H:(d + 1) * SSD_H] + dtb_ref[...])
    la = dtv * a_neg
    for c in range(nch):
        rs = slice(c * CHUNK, (c + 1) * CHUNK)
        cum = _cumsum_rows(tri_b, la[rs])
        last = cum[0:1, :] if reverse else cum[CHUNK - 1:CHUNK, :]
        w2 = dtv[rs] * jnp.exp(last - cum)
        cum_s[c] = jnp.concatenate([cum, cum], axis=1)
        w2_s[c] = jnp.concatenate([w2, w2], axis=1)
        cum_t = cum.T
        dt_t = dtv[rs].T
        ct2_s[c] = jnp.concatenate([cum_t, cum_t], axis=1)
        dt2_s[c] = jnp.concatenate([dt_t, dt_t], axis=1)

    def chunk(cc, carry):
        c = (nch - 1 - cc) if reverse else cc
        rs = pl.ds(pl.multiple_of(c * CHUNK, CHUNK), CHUNK)
        cum = cum_s[c]
        w2 = w2_s[c]
        ct2 = ct2_s[c]
        dt2 = dt2_s[c]
        dec = jnp.exp(cum[0:1, :] if reverse else cum[CHUNK - 1:CHUNK, :])
        cbs, yis = [], []
        for g in range(SSD_G):
            gs = slice(g * SSD_Z, (g + 1) * SSD_Z)
            cg = c_ref[rs, gs].astype(BF16)
            cbs.append(_dot_nt(cg, b_ref[rs, gs].astype(BF16)))
            yis.append(_dot(cg, st_ref[g].astype(BF16)))
        for g in range(SSD_G):
            gs = slice(g * SSD_Z, (g + 1) * SSD_Z)
            bgb = b_ref[rs, gs].astype(BF16)
            cb2 = jnp.concatenate([cbs[g], cbs[g]], axis=1)
            st = st_ref[g]
            y_inter = yis[g]
            x2s, decs = [], []
            for pr in range(hg // 2):
                h0 = g * hg + 2 * pr
                ps = slice(h0 * SSD_P, (h0 + 2) * SSD_P)
                xp = xs_ref[rs, ps]
                ccol = pair(cum, h0)
                rrow = jnp.where(low, ct2[h0:h0 + 1, :], ct2[h0 + 1:h0 + 2, :])
                drow = jnp.where(low, dt2[h0:h0 + 1, :], dt2[h0 + 1:h0 + 2, :])
                seg = jnp.exp(jnp.where(causal2, ccol - rrow, -jnp.inf))
                m = (cb2 * seg * drow).astype(BF16)
                xbd = jnp.concatenate([jnp.where(low, xp, 0.0), jnp.where(low, 0.0, xp)], axis=0).astype(BF16)
                y = _dot(m, xbd) + y_inter[:, pr * pw:(pr + 1) * pw] * jnp.exp(ccol)
                if reverse:
                    ybuf[:, ps] = y + acc_ref[rs, ps]
                else:
                    y_ref[rs, ps] = y
                x2s.append((xp * pair(w2, h0)).astype(BF16))
                decs.append(pair(dec, h0))
            upd = _dot_tn(bgb, jnp.concatenate(x2s, axis=1))
            st_ref[g] = st * jnp.concatenate(decs, axis=1) + upd
        if reverse:
            u = (ybuf[...] + dsk_ref[...] * xs_ref[rs, :]) * _silu(z_ref[rs, :])
            ms = jnp.mean(u * u, axis=-1, keepdims=True)
            y_ref[rs, :] = (u * lax.rsqrt(ms + EPS) * gn_ref[...]).astype(BF16)
        return carry

    lax.fori_loop(0, nch, chunk, 0)


def _ssd(xbc, pdt, dt_bias, a_log, fin=None):
    reverse = fin is not None
    nblk = N_ALL // ROWS
    nctx = T_CTX // ROWS
    blk = (lambda i: _rev_block(i, nblk)) if reverse else (lambda i: i)
    d = 1 if reverse else 0
    gz = SSD_G * SSD_Z
    full = lambda i: (blk(i), 0)
    in_specs = [
        pl.BlockSpec((ROWS, SSD_INNER), full),
        pl.BlockSpec((ROWS, gz), lambda i: (blk(i), SSD_INNER // gz)),
        pl.BlockSpec((ROWS, gz), lambda i: (blk(i), SSD_INNER // gz + 1)),
        pl.BlockSpec((ROWS, LANES), full),
        pl.BlockSpec((None, 1, SSD_H), lambda i: (d, 0, 0)),
        pl.BlockSpec((None, 1, SSD_H), lambda i: (d, 0, 0)),
    ]
    args = [xbc, xbc, xbc, pdt, dt_bias.reshape(2, 1, SSD_H), a_log.reshape(2, 1, SSD_H)]
    nch = ROWS // CHUNK
    scratch = [pltpu.VMEM((SSD_G, SSD_Z, SSD_INNER // SSD_G), F32),
               pltpu.VMEM((nch, CHUNK, 2 * SSD_H), F32), pltpu.VMEM((nch, CHUNK, 2 * SSD_H), F32),
               pltpu.VMEM((nch, SSD_H, 2 * CHUNK), F32), pltpu.VMEM((nch, SSD_H, 2 * CHUNK), F32)]
    if reverse:
        yf, p1, d_rep, norm_g = fin
        in_specs += [pl.BlockSpec((ROWS, SSD_INNER), full), pl.BlockSpec((ROWS, SSD_INNER), full),
                     pl.BlockSpec((1, SSD_INNER), lambda i: (0, 0)), pl.BlockSpec((1, SSD_INNER), lambda i: (0, 0))]
        args += [yf, p1, d_rep, norm_g]
        scratch.append(pltpu.VMEM((CHUNK, SSD_INNER), F32))
        out_spec = pl.BlockSpec((ROWS, SSD_INNER), lambda i: (blk(jnp.maximum(i, nctx)) - nctx, 0))
        out_shape = jax.ShapeDtypeStruct((T_LAT, SSD_INNER), BF16)
    else:
        out_spec = pl.BlockSpec((ROWS, SSD_INNER), full)
        out_shape = jax.ShapeDtypeStruct((N_ALL, SSD_INNER), F32)
    return pl.pallas_call(
        functools.partial(_ssd_kernel, reverse=reverse),
        grid=(nblk,),
        in_specs=in_specs,
        out_specs=out_spec,
        out_shape=out_shape,
        scratch_shapes=scratch,
        compiler_params=_params("arbitrary"),
        name="ssd_bwd" if reverse else "ssd_fwd",
    )(*args)


def _gather_wait(h_hbm, buf, sem, slot):
    pltpu.make_async_copy(h_hbm.at[pl.ds(0, MOE_TM), :], buf.at[slot], sem.at[slot]).wait()


def _gather_issue(idx_ref, h_hbm, buf, sem, slot):
    def body(r8, carry):
        base = pl.multiple_of(r8 * SUBLANES, SUBLANES)
        for u in range(SUBLANES):
            tok = idx_ref[0, 0, base + u]
            pltpu.make_async_copy(h_hbm.at[pl.ds(tok, 1), :], buf.at[slot, pl.ds(base + u, 1), :],
                                  sem.at[slot]).start()
        return carry

    lax.fori_loop(0, MOE_TM // SUBLANES, body, 0)


def _expert_block_kernel(be_ref, ws_ref, nx_ref, nu_ref, *refs, n_w, layer, gather, compute):
    if gather:
        icur_ref, inxt_ref, h_hbm = refs[:3]
        refs = refs[3:]
    else:
        x_ref = refs[0]
        refs = refs[1:]
    w_hbm, o_ref = refs[:n_w], refs[n_w]
    scr = refs[n_w + 1:]
    w32, wsem = scr[:n_w], scr[n_w]
    m = pl.program_id(0)
    nu = nu_ref[0]
    e = be_ref[m]
    wslot = ws_ref[m]
    first = jnp.logical_and(m < nu, jnp.logical_or(m == 0, e != be_ref[jnp.maximum(m - 1, 0)]))

    def w_copy(k, ee, slot):
        return pltpu.make_async_copy(w_hbm[k].at[layer, ee], w32[k].at[slot], wsem.at[k, slot])

    @pl.when(m == 0)
    def _():
        for k in range(n_w):
            w_copy(k, e, wslot).start()

    if gather:
        gbuf, gsem = scr[n_w + 1:]
        gslot = lax.rem(m, 2)

        @pl.when(m == 0)
        def _():
            _gather_issue(icur_ref, h_hbm, gbuf, gsem, 0)

        @pl.when(m + 1 < nu)
        def _():
            _gather_issue(inxt_ref, h_hbm, gbuf, gsem, 1 - gslot)

    @pl.when(jnp.logical_and(first, nx_ref[m] >= 0))
    def _():
        for k in range(n_w):
            w_copy(k, nx_ref[m], 1 - wslot).start()

    @pl.when(first)
    def _():
        for k in range(n_w):
            w_copy(k, e, wslot).wait()

    @pl.when(m < nu)
    def _():
        if gather:
            _gather_wait(h_hbm, gbuf, gsem, gslot)
            x = gbuf[gslot]
        else:
            x = x_ref[...].astype(F32)
        o_ref[...] = compute(x, *[w32[k][wslot] for k in range(n_w)]).astype(o_ref.dtype)


def _expert_stage(src, ws, plan, layer, n_out, out_dtype, compute, name, slot_tok=None):
    be, wsl, nx, nu = plan
    nblk = be.shape[0]
    gather = slot_tok is not None
    anyspec = pl.BlockSpec(memory_space=pl.ANY)
    used = lambda m, nu_ref: jnp.minimum(m, nu_ref[0] - 1)
    n_w = len(ws)
    wshape = ws[0].shape[2:]
    scratch = [pltpu.VMEM((2,) + wshape, F32) for _ in ws]
    scratch.append(pltpu.SemaphoreType.DMA((n_w, 2)))
    if gather:
        in_specs = [
            pl.BlockSpec((1, 1, MOE_TM), lambda m, be, ws, nx, nu: (used(m, nu), 0, 0), memory_space=pltpu.SMEM),
            pl.BlockSpec((1, 1, MOE_TM), lambda m, be, ws, nx, nu: (used(m + 1, nu), 0, 0),
                         memory_space=pltpu.SMEM),
            anyspec,
        ]
        args = [slot_tok, slot_tok, src]
        scratch += [pltpu.VMEM((2, MOE_TM, src.shape[1]), src.dtype), pltpu.SemaphoreType.DMA((2,))]
    else:
        in_specs = [pl.BlockSpec((MOE_TM, src.shape[1]), lambda m, be, ws, nx, nu: (used(m, nu), 0))]
        args = [src]
    return pl.pallas_call(
        functools.partial(_expert_block_kernel, n_w=n_w, layer=layer, gather=gather, compute=compute),
        grid_spec=pltpu.PrefetchScalarGridSpec(
            num_scalar_prefetch=4,
            grid=(nblk,),
            in_specs=in_specs + [anyspec] * n_w,
            out_specs=pl.BlockSpec((MOE_TM, n_out), lambda m, be, ws, nx, nu: (used(m, nu), 0)),
            scratch_shapes=scratch,
        ),
        out_shape=jax.ShapeDtypeStruct((nblk * MOE_TM, n_out), out_dtype),
        compiler_params=_params("arbitrary"),
        name=name,
    )(be, wsl, nx, nu, *args, *ws)


def _moe(h, eidx, wts, w_gate, w_up, w_down, layer):
    n = h.shape[0]
    i32 = jnp.int32
    flat_e = eidx.reshape(-1)
    experts = jnp.arange(N_EXP, dtype=i32)
    onehot = (flat_e[:, None] == experts[None, :]).astype(i32)
    csum = jnp.cumsum(onehot, axis=0)
    bcount = (csum[-1] + MOE_TM - 1) // MOE_TM
    bend = jnp.cumsum(bcount)
    bstart = bend - bcount
    dest = jnp.sum(onehot * (csum - 1 + (bstart * MOE_TM)[None, :]), axis=1)
    nblk = -(-2 * n // MOE_TM) + N_EXP
    n_used = bend[-1].astype(i32)
    blk = jnp.minimum(jnp.arange(nblk, dtype=i32), n_used - 1)
    block_exp = jnp.sum((blk[:, None] >= bend[None, :]).astype(i32), axis=1)
    owns = bcount > 0
    wslot = (jnp.cumsum(owns.astype(i32)) - 1) % 2
    later = jnp.logical_and(owns[None, :], experts[None, :] > experts[:, None])
    nxt = jnp.min(jnp.where(later, experts[None, :], N_EXP), axis=1)
    nxt = jnp.where(nxt == N_EXP, -1, nxt)
    of_blk = (block_exp[:, None] == experts[None, :]).astype(i32)
    plan = (block_exp.astype(i32), jnp.sum(of_blk * wslot[None, :], axis=1).astype(i32),
            jnp.sum(of_blk * nxt[None, :], axis=1).astype(i32), n_used.reshape(1))
    tok = jnp.tile(jnp.arange(n, dtype=i32), 2)
    pad_tok = jnp.arange(nblk * MOE_TM, dtype=i32) % n
    slot_tok = pad_tok.at[dest].set(tok).reshape(nblk, 1, MOE_TM)
    hb = _expert_stage(h, [w_gate, w_up], plan, layer, D_EXP, BF16,
                       lambda x, wg, wu: _silu(_dot(x, wg)) * _dot(x, wu), "moe_up", slot_tok=slot_tok)
    yb = _expert_stage(hb, [w_down], plan, layer, D, F32, lambda x, wd: _dot(x, wd), "moe_down")
    return yb, dest.reshape(2, n), wts.T


def _pos_tables():
    quarter = D // 4
    omega = 1.0 / (10000.0 ** (jnp.arange(quarter, dtype=F32) / quarter))
    ang_r = jnp.arange(T_LAT // GRID_W, dtype=F32)[:, None] * omega
    ang_c = jnp.arange(GRID_W, dtype=F32)[:, None] * omega
    emb_r = jnp.concatenate([jnp.sin(ang_r), jnp.cos(ang_r)], axis=-1)
    emb_c = jnp.concatenate([jnp.sin(ang_c), jnp.cos(ang_c)], axis=-1)
    return emb_r, emb_c


def _block_diag_gates(wa, wx, group):
    nb = wa.shape[1]
    per = group // wa.shape[2]
    eye = jnp.eye(per, dtype=F32)

    def bd(w):
        w = w.reshape(nb // per, per, w.shape[1], w.shape[2])
        return jnp.einsum("gnkj,nm->gnkmj", w, eye).reshape(nb // per, group, group)

    return jnp.concatenate([bd(wa[0]), bd(wa[1]), bd(wx[0]), bd(wx[1])], axis=-1).astype(BF16)


def kernel(x, c, ctx, c_ctx, mod_w, mod_b, norm1_g, norm2_g, ev_w_in, ev_conv_w, ev_conv_b, lru_wa, lru_ba, lru_wx, lru_bx, lru_lambda, gla_wg_up, gla_bg, gla_norm_g, ev_w_out, od_w_in, od_conv_w, od_conv_b, ssd_a_log, ssd_dt_bias, ssd_d, ssd_norm_g, od_w_out, router_w, router_b, exp_w_gate, exp_w_up, exp_w_down, final_norm_g):
    mods = _mod_vectors(c, c_ctx, mod_w, mod_b)
    emb_r, emb_c = _pos_tables()
    rwt = router_w.T
    rb = router_b.reshape(N_EXP, 1)
    tm = N_ALL // 8

    x0, h0 = _prep0(x[0], ctx[0], emb_r, emb_c, mods[0], norm1_g[0:1])
    w_in_t = jnp.swapaxes(ev_w_in[0], 0, 1)
    p0 = _mm([h0], w_in_t, ncols=EVEN_MAIN, tm=tm, tn=1024, w_t=True)
    pad = _mm([h0], w_in_t, ncols=LANES, tm=tm, tn=LANES, col_off=EVEN_MAIN // LANES, w_t=True)
    wbd = _block_diag_gates(lru_wa[0], lru_wx[0], 256)
    ya = _lru(p0, ev_conv_w[0], ev_conv_b[0:1], wbd, lru_ba[0], lru_bx[0], lru_lambda[0])
    of = _gla(p0, pad, gla_wg_up[0], gla_bg[0], None)
    yb = _gla(p0, pad, gla_wg_up[0], gla_bg[0], gla_norm_g[0:1], of=of)
    y0 = _mm([ya, yb], ev_w_out[0], ncols=D, tm=tm, tn=1024)
    nctx = T_CTX // ROWS
    x1, h1, e0, w0 = _token_stage(x0, y0, mods[0], mods[0], norm2_g[0:1], rwt, rb, gate_col=2, shift_col=3,
                                  scale_col=4, ctx_blocks=nctx, route=True)
    f0, dest0, wc0 = _moe(h1, e0, w0, exp_w_gate, exp_w_up, exp_w_down, 0)
    x2, h2 = _token_stage(x1, f0, mods[0], mods[1], norm1_g[1:2], rwt, rb, gate_col=5, shift_col=0,
                          scale_col=1, ctx_blocks=nctx, combine=(dest0, wc0))

    p1 = _mm([h2], od_w_in[0], ncols=ODD_MAIN, tm=tm, tn=1024)
    pdt = _mm([h2], od_w_in[0], ncols=LANES, tm=tm, tn=LANES, col_off=ODD_MAIN // LANES)
    xbc = _conv_silu(p1, od_conv_w[0], od_conv_b[0:1])
    yf = _ssd(xbc, pdt, ssd_dt_bias[0], ssd_a_log[0])
    d_rep = jnp.repeat(ssd_d[0], SSD_P).reshape(1, SSD_INNER)
    gy = _ssd(xbc, pdt, ssd_dt_bias[0], ssd_a_log[0], fin=(yf, p1, d_rep, ssd_norm_g[0:1]))
    y1 = _mm([gy], od_w_out[0], ncols=D, tm=1024, tn=512)
    x3, h3, e1, w1 = _token_stage(x2, y1, mods[1], mods[1], norm2_g[1:2], rwt, rb, gate_col=2, shift_col=3,
                                  scale_col=4, x_off=nctx, route=True)
    f1, dest1, wc1 = _moe(h3, e1, w1, exp_w_gate, exp_w_up, exp_w_down, 1)
    (out,) = _token_stage(x3, f1, mods[1], mods[1], final_norm_g.reshape(1, D), rwt, rb, gate_col=5,
                          combine=(dest1, wc1), final=True)
    return out[None]
```

```python
import functools
import math

import jax
import jax.numpy as jnp
from jax import lax
from jax.experimental import pallas as pl
from jax.experimental.pallas import tpu as pltpu

F32 = jnp.float32
BF16 = jnp.bfloat16
HIGHEST = lax.Precision.HIGHEST

D = 2048
T_LAT = 8192
T_CTX = 256
N_ALL = T_CTX + T_LAT
GRID_W = 64
EPS = 1e-6
CONV_W = 4
LRU_W = 1024
LRU_C = 8.0
GLA_H = 8
GLA_DK = 64
GLA_DV = 128
GLA_QK = GLA_H * GLA_DK
GLA_V = GLA_H * GLA_DV
GLA_RANK = 16
GLA_TAU = 16.0
CHUNK = 64
EVEN_MAIN = 2 * LRU_W + 2 * GLA_QK + 2 * GLA_V
SSD_INNER = 2 * D
SSD_P = 64
SSD_H = SSD_INNER // SSD_P
SSD_G = 8
SSD_Z = 128
SSD_XBC = SSD_INNER + 2 * SSD_G * SSD_Z
ODD_MAIN = SSD_INNER + SSD_XBC
N_EXP = 32
N_GRP = 4
GRP = N_EXP // N_GRP
D_EXP = 1024
MOE_TM = 256

ROWS = 256
LANES = 128
SUBLANES = 8
VMEM_LIMIT = 56 * 1024 * 1024
PROJ_TM = N_ALL // 8
PROJ_TN = 1024
OUT1_TM = 1024
OUT1_TN = 512
LRU_GATE_GROUP = 256


def _params(*sem):
    return pltpu.CompilerParams(dimension_semantics=sem, vmem_limit_bytes=VMEM_LIMIT)


def _silu(v):
    return v * jax.nn.sigmoid(v)


def _softplus(v):
    return jnp.maximum(v, 0.0) + jnp.log1p(jnp.exp(-jnp.abs(v)))


def _dot(a, b):
    return jnp.dot(a, b, preferred_element_type=F32)


def _dot_nt(a, b):
    return lax.dot_general(a, b, (((1,), (1,)), ((), ())), preferred_element_type=F32)


def _dot_tn(a, b):
    return lax.dot_general(a, b, (((0,), (0,)), ((), ())), preferred_element_type=F32)


def _mod_kernel(s_ref, w_ref, b_ref, o_ref):
    tn = w_ref.shape[1]
    nrep = tn // LANES

    def body(r, acc):
        a0, a1 = acc
        rows = pl.ds(pl.multiple_of(r * SUBLANES, SUBLANES), SUBLANES)
        w = w_ref[rows, :]
        s0 = _silu(s_ref[0, rows, :])
        s1 = _silu(s_ref[1, rows, :])
        a0 = a0 + w * jnp.concatenate([s0] * nrep, axis=1)
        a1 = a1 + w * jnp.concatenate([s1] * nrep, axis=1)
        return a0, a1

    zero = jnp.zeros((SUBLANES, tn), F32)
    a0, a1 = lax.fori_loop(0, D // SUBLANES, body, (zero, zero), unroll=4)
    o_ref[0:1, :] = jnp.sum(a0, axis=0, keepdims=True) + b_ref[...]
    o_ref[1:2, :] = jnp.sum(a1, axis=0, keepdims=True) + b_ref[...]


def _mod_vectors(c, c_ctx, mod_w, mod_b):
    depth = mod_w.shape[0]
    tn = 1024
    s = jnp.broadcast_to(jnp.stack([c[0], c_ctx])[:, :, None], (2, D, LANES))
    return pl.pallas_call(
        _mod_kernel,
        grid=(depth, 6 * D // tn),
        in_specs=[
            pl.BlockSpec((2, D, LANES), lambda l, j: (0, 0, 0)),
            pl.BlockSpec((None, D, tn), lambda l, j: (l, 0, j)),
            pl.BlockSpec((None, 1, tn), lambda l, j: (l, 0, j)),
        ],
        out_specs=pl.BlockSpec((None, 2, tn), lambda l, j: (l, 0, j)),
        out_shape=jax.ShapeDtypeStruct((depth, 2, 6 * D), F32),
        compiler_params=_params("arbitrary", "arbitrary"),
        name="mod_vectors",
    )(s, mod_w, mod_b.reshape(depth, 1, 6 * D))


def _mod_row(mod_ref, kind, col):
    return mod_ref[pl.ds(kind, 1), col * D:(col + 1) * D]


def _ada_norm(xv, g, shift, scale):
    ms = jnp.mean(xv * xv, axis=-1, keepdims=True)
    return (xv * lax.rsqrt(ms + EPS) * g) * (1.0 + scale) + shift


def _prep0_kernel(x_ref, ctx_ref, er_ref, ec_ref, mod_ref, g_ref, xo_ref, ho_ref):
    i = pl.program_id(0)

    @pl.when(i == 0)
    def _():
        xo_ref[...] = ctx_ref[...]

    @pl.when(i > 0)
    def _():
        r0 = (i - 1) * (ROWS // GRID_W)
        for j in range(ROWS // GRID_W):
            rs = slice(j * GRID_W, (j + 1) * GRID_W)
            xo_ref[rs, 0:D // 2] = x_ref[rs, 0:D // 2] + er_ref[pl.ds(r0 + j, 1), :]
            xo_ref[rs, D // 2:D] = x_ref[rs, D // 2:D] + ec_ref[...]

    kind = jnp.where(i == 0, 1, 0)
    h = _ada_norm(xo_ref[...], g_ref[...], _mod_row(mod_ref, kind, 0), _mod_row(mod_ref, kind, 1))
    ho_ref[...] = h.astype(BF16)


def _prep0(x, ctx, emb_r, emb_c, mod0, g):
    nblk = N_ALL // ROWS
    return pl.pallas_call(
        _prep0_kernel,
        grid=(nblk,),
        in_specs=[
            pl.BlockSpec((ROWS, D), lambda i: (jnp.maximum(i - 1, 0), 0)),
            pl.BlockSpec((ROWS, D), lambda i: (0, 0)),
            pl.BlockSpec(emb_r.shape, lambda i: (0, 0)),
            pl.BlockSpec(emb_c.shape, lambda i: (0, 0)),
            pl.BlockSpec((2, 6 * D), lambda i: (0, 0)),
            pl.BlockSpec((1, D), lambda i: (0, 0)),
        ],
        out_specs=[pl.BlockSpec((ROWS, D), lambda i: (i, 0)), pl.BlockSpec((ROWS, D), lambda i: (i, 0))],
        out_shape=[jax.ShapeDtypeStruct((N_ALL, D), F32), jax.ShapeDtypeStruct((N_ALL, D), BF16)],
        compiler_params=_params("arbitrary"),
        name="embed_norm",
    )(x, ctx, emb_r, emb_c, mod0, g)


def _route(hf32, rwt_ref, rb_ref, eo_ref, wo_ref):
    logits = lax.dot_general(rwt_ref[...], hf32, (((1,), (1,)), ((), ())),
                             precision=HIGHEST, preferred_element_type=F32)
    s = jax.nn.sigmoid(logits)
    sel = s + rb_ref[...]
    row = lax.broadcasted_iota(jnp.int32, (GRP, ROWS), 0)
    neg = jnp.float32(-jnp.inf)
    gs, i1s, i2s = [], [], []
    for g in range(N_GRP):
        blk = sel[g * GRP:(g + 1) * GRP, :]
        m1 = jnp.max(blk, axis=0, keepdims=True)
        i1 = jnp.min(jnp.where(blk == m1, row, GRP), axis=0, keepdims=True)
        blk2 = jnp.where(row == i1, neg, blk)
        m2 = jnp.max(blk2, axis=0, keepdims=True)
        i2 = jnp.min(jnp.where(blk2 == m2, row, GRP), axis=0, keepdims=True)
        gs.append(m1 + m2)
        i1s.append(i1)
        i2s.append(i2)
    best, gi, i1, i2 = gs[0], jnp.zeros((1, ROWS), jnp.int32), i1s[0], i2s[0]
    for g in range(1, N_GRP):
        upd = gs[g] > best
        best = jnp.where(upd, gs[g], best)
        gi = jnp.where(upd, g, gi)
        i1 = jnp.where(upd, i1s[g], i1)
        i2 = jnp.where(upd, i2s[g], i2)
    e1 = gi * GRP + i1
    e2 = gi * GRP + i2
    erow = lax.broadcasted_iota(jnp.int32, (N_EXP, ROWS), 0)
    s1 = jnp.sum(jnp.where(erow == e1, s, 0.0), axis=0, keepdims=True)
    s2 = jnp.sum(jnp.where(erow == e2, s, 0.0), axis=0, keepdims=True)
    tot = s1 + s2
    eo_ref[0:1, :] = e1
    eo_ref[1:2, :] = e2
    wo_ref[0:1, :] = s1 / tot
    wo_ref[1:2, :] = s2 / tot


def _combine_wait(yb_hbm, ybuf, sem, slot):
    pltpu.make_async_copy(yb_hbm.at[pl.ds(0, 2 * ROWS), :], ybuf.at[slot], sem.at[slot]).wait()


def _combine_issue(dest_ref, yb_hbm, ybuf, sem, slot):
    def body(r, carry):
        for k in range(2):
            row = dest_ref[0, k, r]
            pltpu.make_async_copy(yb_hbm.at[pl.ds(row, 1), :], ybuf.at[slot, pl.ds(k * ROWS + r, 1), :],
                                  sem.at[slot]).start()
        return carry

    lax.fori_loop(0, ROWS, body, 0, unroll=8)


def _token_kernel(*refs, gate_col, shift_col, scale_col, ctx_blocks, route, combine, final):
    refs = list(refs)
    if combine:
        dcur_ref, dnxt_ref = refs.pop(0), refs.pop(0)
    x_ref, y_ref = refs.pop(0), refs.pop(0)
    if combine:
        wc_ref = refs.pop(0)
    modg_ref, modn_ref, g_ref, rwt_ref, rb_ref = (refs.pop(0) for _ in range(5))
    i = pl.program_id(0)
    if combine:
        ybuf, sem = refs[-2], refs[-1]
        slot = lax.rem(i, 2)

        @pl.when(i == 0)
        def _():
            _combine_issue(dcur_ref, y_ref, ybuf, sem, 0)

        @pl.when(i + 1 < pl.num_programs(0))
        def _():
            _combine_issue(dnxt_ref, y_ref, ybuf, sem, 1 - slot)

        _combine_wait(y_ref, ybuf, sem, slot)
        y = wc_ref[:, 0:1] * ybuf[slot, 0:ROWS, :] + wc_ref[:, 1:2] * ybuf[slot, ROWS:2 * ROWS, :]
    else:
        y = y_ref[...]
    kind = jnp.where(i < ctx_blocks, 1, 0)
    xn = x_ref[...] + _mod_row(modg_ref, kind, gate_col) * y
    if final:
        ms = jnp.mean(xn * xn, axis=-1, keepdims=True)
        refs[0][...] = xn * lax.rsqrt(ms + EPS) * g_ref[...]
        return
    xo_ref, ho_ref = refs[0], refs[1]
    xo_ref[...] = xn
    h = _ada_norm(xn, g_ref[...], _mod_row(modn_ref, kind, shift_col), _mod_row(modn_ref, kind, scale_col))
    ho_ref[...] = h.astype(ho_ref.dtype)
    if route:
        _route(h, rwt_ref, rb_ref, refs[2], refs[3])


def _token_stage(x, y, modg, modn, g, rwt, rb, *, gate_col, shift_col=0, scale_col=0, x_off=0, ctx_blocks=0,
                 route=False, combine=None, final=False):
    n = x.shape[0] - x_off * ROWS
    nblk = n // ROWS
    row = lambda i: (i, 0)
    const2 = lambda i: (0, 0)
    in_specs, args, scratch = [], [], []
    if combine is not None:
        dest, wc = combine
        dest3 = dest.reshape(2, nblk, ROWS).transpose(1, 0, 2)
        in_specs += [pl.BlockSpec((1, 2, ROWS), lambda i: (i, 0, 0), memory_space=pltpu.SMEM),
                     pl.BlockSpec((1, 2, ROWS), lambda i: (jnp.minimum(i + 1, nblk - 1), 0, 0),
                                  memory_space=pltpu.SMEM)]
        args += [dest3, dest3]
    in_specs.append(pl.BlockSpec((ROWS, D), lambda i: (i + x_off, 0)))
    args.append(x)
    if combine is not None:
        in_specs += [pl.BlockSpec(memory_space=pl.ANY), pl.BlockSpec((ROWS, 2), row)]
        args += [y, wc]
        scratch = [pltpu.VMEM((2, 2 * ROWS, D), F32), pltpu.SemaphoreType.DMA((2,))]
    else:
        in_specs.append(pl.BlockSpec((ROWS, D), row))
        args.append(y)
    in_specs += [pl.BlockSpec((2, 6 * D), const2), pl.BlockSpec((2, 6 * D), const2), pl.BlockSpec((1, D), const2),
                 pl.BlockSpec((N_EXP, D), const2), pl.BlockSpec((N_EXP, 1), const2)]
    args += [modg, modn, g, rwt, rb]
    if final:
        out_specs = [pl.BlockSpec((ROWS, D), row)]
        out_shape = [jax.ShapeDtypeStruct((n, D), F32)]
    else:
        out_specs = [pl.BlockSpec((ROWS, D), row), pl.BlockSpec((ROWS, D), row)]
        out_shape = [jax.ShapeDtypeStruct((n, D), F32), jax.ShapeDtypeStruct((n, D), F32 if route else BF16)]
        if route:
            out_specs += [pl.BlockSpec((2, ROWS), lambda i: (0, i)), pl.BlockSpec((2, ROWS), lambda i: (0, i))]
            out_shape += [jax.ShapeDtypeStruct((2, n), jnp.int32), jax.ShapeDtypeStruct((2, n), F32)]
    kern = functools.partial(_token_kernel, gate_col=gate_col, shift_col=shift_col, scale_col=scale_col,
                             ctx_blocks=ctx_blocks, route=route, combine=combine is not None, final=final)
    return pl.pallas_call(
        kern,
        grid=(nblk,),
        in_specs=in_specs,
        out_specs=out_specs,
        out_shape=out_shape,
        scratch_shapes=scratch,
        compiler_params=_params("arbitrary"),
        name="token_stage",
    )(*args)


def _mm_kernel(*refs, nx, valid, w_t):
    x_refs, w_refs, o_ref, wb_refs = refs[:nx], refs[nx:2 * nx], refs[2 * nx], refs[2 * nx + 1:]
    out_axis = 0 if w_t else 1

    @pl.when(pl.program_id(1) == 0)
    def _():
        for w_ref, wb_ref in zip(w_refs, wb_refs):
            w = w_ref[...]
            if valid < w.shape[out_axis]:
                w = jnp.where(lax.broadcasted_iota(jnp.int32, w.shape, out_axis) < valid, w, 0.0)
            wb_ref[...] = w.astype(BF16)

    mul = _dot_nt if w_t else _dot
    acc = mul(x_refs[0][...], wb_refs[0][...])
    for x_ref, wb_ref in zip(x_refs[1:], wb_refs[1:]):
        acc = acc + mul(x_ref[...], wb_ref[...])
    o_ref[...] = acc


def _mm(xs, w, *, ncols, tm, tn, col_off=0, w_t=False):
    nx = len(xs)
    k = xs[0].shape[1]
    m = xs[0].shape[0]
    n_total = w.shape[0] if w_t else w.shape[1]
    valid = min(tn, n_total - col_off * tn) if ncols == tn else tn
    in_specs = [pl.BlockSpec((tm, k), lambda j, i: (i, 0)) for _ in xs]
    if w_t:
        in_specs += [pl.BlockSpec((tn, k), functools.partial(lambda j, i, kk: (j + col_off, kk), kk=kk))
                     for kk in range(nx)]
    else:
        in_specs += [pl.BlockSpec((k, tn), functools.partial(lambda j, i, kk: (kk, j + col_off), kk=kk))
                     for kk in range(nx)]
    return pl.pallas_call(
        functools.partial(_mm_kernel, nx=nx, valid=valid, w_t=w_t),
        grid=(ncols // tn, m // tm),
        in_specs=in_specs,
        out_specs=pl.BlockSpec((tm, tn), lambda j, i: (i, j)),
        out_shape=jax.ShapeDtypeStruct((m, ncols), F32),
        scratch_shapes=[pltpu.VMEM((tn, k) if w_t else (k, tn), BF16) for _ in xs],
        compiler_params=_params("arbitrary", "arbitrary"),
        name="proj",
    )(*xs, *([w] * nx))


def _conv_block(cur, prev8, next8, w_ref, b_ref, i, nblk):
    ctx_edge = T_CTX // ROWS
    keep_prev = jnp.logical_and(i != 0, i != ctx_edge)
    keep_next = jnp.logical_and(i != nblk - 1, i != ctx_edge - 1)
    prev8 = jnp.where(keep_prev, prev8, 0.0)
    next8 = jnp.where(keep_next, next8, 0.0)
    ext = jnp.concatenate([prev8, cur, next8], axis=0)
    n_ext = ROWS + 2 * SUBLANES
    y = b_ref[...] + w_ref[CONV_W // 2:CONV_W // 2 + 1, :] * cur
    for j in range(CONV_W):
        off = j - CONV_W // 2
        if off != 0:
            shifted = pltpu.roll(ext, (-off) % n_ext, axis=0)[SUBLANES:SUBLANES + ROWS]
            y = y + w_ref[j:j + 1, :] * shifted
    return y


def _halo_specs(width, col_blk, nblk):
    per = ROWS // SUBLANES
    last8 = N_ALL // SUBLANES - 1
    return [
        pl.BlockSpec((ROWS, width), lambda i, *a: (i, col_blk(*a))),
        pl.BlockSpec((SUBLANES, width), lambda i, *a: (jnp.maximum(i * per - 1, 0), col_blk(*a))),
        pl.BlockSpec((SUBLANES, width), lambda i, *a: (jnp.minimum((i + 1) * per, last8), col_blk(*a))),
    ]


def _scan_tiles(a_ref, b_ref, h_ref, carry_ref, reverse):
    ntile = ROWS // SUBLANES
    row = lax.broadcasted_iota(jnp.int32, (SUBLANES, LRU_W), 0)

    def body(t, carry):
        tt = (ntile - 1 - t) if reverse else t
        rows = pl.ds(pl.multiple_of(tt * SUBLANES, SUBLANES), SUBLANES)
        a = a_ref[rows, :]
        b = b_ref[rows, :]
        for d in (1, 2, 4):
            shift = (SUBLANES - d) if reverse else d
            a_sh = pltpu.roll(a, shift, axis=0)
            b_sh = pltpu.roll(b, shift, axis=0)
            m = (row < SUBLANES - d) if reverse else (row >= d)
            b = jnp.where(m, a * b_sh, 0.0) + b
            a = jnp.where(m, a * a_sh, a)
        h = a * carry + b
        h_ref[rows, :] = h
        edge = h[0:1, :] if reverse else h[SUBLANES - 1:SUBLANES, :]
        return jnp.broadcast_to(edge, (SUBLANES, LRU_W))

    carry_ref[...] = lax.fori_loop(0, ntile, body, carry_ref[...])


def _lru_fwd_kernel(xa_ref, prev_ref, next_ref, cw_ref, cb_ref, wbd_ref, ba_ref, bx_ref, lam_ref,
                    hf_ref, a1_ref, b1_ref, carry_ref, a0_ref, b0_ref):
    i = pl.program_id(0)
    nblk = pl.num_programs(0)

    @pl.when(i == 0)
    def _():
        carry_ref[...] = jnp.zeros_like(carry_ref)

    xa = _conv_block(xa_ref[...], prev_ref[...], next_ref[...], cw_ref, cb_ref, i, nblk)
    xab = xa.astype(BF16)
    nsp = -LRU_C * _softplus(-lam_ref[...])
    gw = wbd_ref.shape[1]
    for g in range(LRU_W // gw):
        cs = slice(g * gw, (g + 1) * gw)
        z = _dot(xab[:, cs], wbd_ref[g])
        for d in range(2):
            r = jax.nn.sigmoid(z[:, d * gw:(d + 1) * gw] + ba_ref[d:d + 1, cs])
            ig = jax.nn.sigmoid(z[:, (2 + d) * gw:(3 + d) * gw] + bx_ref[d:d + 1, cs])
            log_a = r * nsp[d:d + 1, cs]
            a = jnp.exp(log_a)
            b = jnp.sqrt(-jnp.tanh(log_a) * (a * a + 1.0)) * ig * xa[:, cs]
            if d == 0:
                a0_ref[:, cs] = a
                b0_ref[:, cs] = b
            else:
                a1_ref[:, cs] = a
                b1_ref[:, cs] = b
    _scan_tiles(a0_ref, b0_ref, hf_ref, carry_ref, reverse=False)


def _gelu_tanh(v):
    return 0.5 * v * (1.0 + jnp.tanh(math.sqrt(2.0 / math.pi) * (v + 0.044715 * (v * v * v))))


def _lru_bwd_kernel(a1_ref, b1_ref, hf_ref, ga_ref, ya_ref, carry_ref, hb_ref):
    @pl.when(pl.program_id(0) == 0)
    def _():
        carry_ref[...] = jnp.zeros_like(carry_ref)

    _scan_tiles(a1_ref, b1_ref, hb_ref, carry_ref, reverse=True)
    ya_ref[...] = ((hf_ref[...] + hb_ref[...]) * _gelu_tanh(ga_ref[...])).astype(BF16)


def _rev_block(i, nblk):
    nctx = T_CTX // ROWS
    return jnp.where(i < nctx, nctx - 1 - i, nblk - 1 - (i - nctx))


def _lru(p0, conv_w, conv_b, wbd, ba, bx, lam):
    nblk = N_ALL // ROWS
    full = lambda shape: pl.BlockSpec(shape, lambda i: (0,) * len(shape))
    hf, a1, b1 = pl.pallas_call(
        _lru_fwd_kernel,
        grid=(nblk,),
        in_specs=_halo_specs(LRU_W, lambda: 0, nblk) + [
            full((CONV_W, LRU_W)), full((1, LRU_W)), full(wbd.shape),
            full((2, LRU_W)), full((2, LRU_W)), full((2, LRU_W)),
        ],
        out_specs=[pl.BlockSpec((ROWS, LRU_W), lambda i: (i, 0))] * 3,
        out_shape=[jax.ShapeDtypeStruct((N_ALL, LRU_W), F32)] * 3,
        scratch_shapes=[pltpu.VMEM((SUBLANES, LRU_W), F32), pltpu.VMEM((ROWS, LRU_W), F32),
                        pltpu.VMEM((ROWS, LRU_W), F32)],
        compiler_params=_params("arbitrary"),
        name="lru_fwd",
    )(p0, p0, p0, conv_w, conv_b, wbd, ba, bx, lam)
    rev = lambda i: (_rev_block(i, nblk), 0)
    ya = pl.pallas_call(
        _lru_bwd_kernel,
        grid=(nblk,),
        in_specs=[pl.BlockSpec((ROWS, LRU_W), rev)] * 3
        + [pl.BlockSpec((ROWS, LRU_W), lambda i: (_rev_block(i, nblk), 1))],
        out_specs=pl.BlockSpec((ROWS, LRU_W), rev),
        out_shape=jax.ShapeDtypeStruct((N_ALL, LRU_W), BF16),
        scratch_shapes=[pltpu.VMEM((SUBLANES, LRU_W), F32), pltpu.VMEM((ROWS, LRU_W), F32)],
        compiler_params=_params("arbitrary"),
        name="lru_bwd",
    )(a1, b1, hf, p0)
    return ya


def _tri(reverse):
    r = lax.broadcasted_iota(jnp.int32, (CHUNK, CHUNK), 0)
    c = lax.broadcasted_iota(jnp.int32, (CHUNK, CHUNK), 1)
    return (c >= r) if reverse else (c <= r)


def _cumsum_rows(tri_b, v):
    hi = v.astype(BF16)
    r1 = v - hi.astype(F32)
    mid = r1.astype(BF16)
    lo = (r1 - mid.astype(F32)).astype(BF16)
    return _dot(tri_b, hi) + _dot(tri_b, mid) + _dot(tri_b, lo)


def _gla_kernel(*refs, reverse):
    if reverse:
        (q_ref, k_ref, v_ref, ad_ref, wg_ref, bg_ref, of_ref, og_ref, gn_ref, o_ref,
         s_ref, qd_s, klo_s, khi_s, ke_s, dec_s, s2_s, upd_s) = refs
    else:
        (q_ref, k_ref, v_ref, ad_ref, wg_ref, bg_ref, o_ref,
         s_ref, qd_s, klo_s, khi_s, ke_s, dec_s, s2_s, upd_s) = refs
    d = 1 if reverse else 0

    @pl.when(pl.program_id(0) == 0)
    def _():
        s_ref[...] = jnp.zeros_like(s_ref)

    nch = ROWS // CHUNK
    pw = 2 * GLA_DK
    row2 = lax.broadcasted_iota(jnp.int32, (CHUNK, pw), 0)
    col2 = jnp.bitwise_and(lax.broadcasted_iota(jnp.int32, (CHUNK, pw), 1), GLA_DK - 1)
    causal2 = (col2 >= row2) if reverse else (col2 <= row2)
    eye = (lax.broadcasted_iota(jnp.int32, (pw, pw), 0) == lax.broadcasted_iota(jnp.int32, (pw, pw), 1)).astype(F32)
    zeros_v = jnp.zeros((CHUNK, GLA_DV), BF16)

    rr = lax.broadcasted_iota(jnp.int32, (ROWS, ROWS), 0)
    rc = lax.broadcasted_iota(jnp.int32, (ROWS, ROWS), 1)
    same = (rr // CHUNK) == (rc // CHUNK)
    tri_b = jnp.logical_and(same, (rc >= rr) if reverse else (rc <= rr)).astype(BF16)
    ad = ad_ref[:, d * GLA_RANK:(d + 1) * GLA_RANK]
    z = _dot(ad.astype(BF16), wg_ref[...].astype(BF16)) + bg_ref[...]
    lg = -_softplus(-z) * (1.0 / GLA_TAU)
    cum = _cumsum_rows(tri_b, lg)
    cum3 = cum.reshape(nch, CHUNK, GLA_QK)
    last3 = cum3[:, 0:1, :] if reverse else cum3[:, CHUNK - 1:CHUNK, :]
    to_end = jnp.exp(last3 - cum3).reshape(ROWS, GLA_QK)
    k = k_ref[...]
    k_inv = k * jnp.exp(-cum)
    low_all = jnp.bitwise_and(lax.broadcasted_iota(jnp.int32, (1, GLA_QK), 1), pw - 1) < GLA_DK
    qd_s[...] = (q_ref[...] * (GLA_DK ** -0.5) * jnp.exp(cum)).astype(BF16)
    klo_s[...] = jnp.where(low_all, k_inv, 0.0).astype(BF16)
    khi_s[...] = jnp.where(low_all, 0.0, k_inv).astype(BF16)
    ke_s[...] = (k * to_end).astype(BF16)
    dec_s[0:nch, :] = jnp.exp(last3.reshape(nch, GLA_QK))

    npair = GLA_H // 2
    for c in range(nch):
        rs = slice(c * CHUNK, (c + 1) * CHUNK)
        for p in range(npair):
            ls = slice(p * pw, (p + 1) * pw)
            kbd = jnp.concatenate([klo_s[rs, ls], khi_s[rs, ls]], axis=0)
            s2_s[c * npair + p] = jnp.where(causal2, _dot_nt(qd_s[rs, ls], kbd), 0.0).astype(BF16)
            vp = v_ref[rs, p * 2 * GLA_DV:(p + 1) * 2 * GLA_DV].astype(BF16)
            upd_s[c * npair + p] = _dot_tn(ke_s[rs, ls], vp)

    for cc in range(nch):
        c = (nch - 1 - cc) if reverse else cc
        rs = slice(c * CHUNK, (c + 1) * CHUNK)
        decay = dec_s[c:c + 1, :]
        for p in range(npair):
            ls = slice(p * pw, (p + 1) * pw)
            vs = slice(p * 2 * GLA_DV, (p + 1) * 2 * GLA_DV)
            qb = qd_s[rs, ls]
            s2 = s2_s[c * npair + p]
            upd = upd_s[c * npair + p]
            vp = v_ref[rs, vs].astype(BF16)
            vbd = jnp.concatenate([jnp.concatenate([vp[:, :GLA_DV], zeros_v], axis=1),
                                   jnp.concatenate([zeros_v, vp[:, GLA_DV:]], axis=1)], axis=0)
            sp = s_ref[p]
            o = _dot(jnp.concatenate([s2, qb], axis=1), jnp.concatenate([vbd, sp.astype(BF16)], axis=0))
            if reverse:
                for hh in range(2):
                    hs = slice((2 * p + hh) * GLA_DV, (2 * p + hh + 1) * GLA_DV)
                    tot = o[:, hh * GLA_DV:(hh + 1) * GLA_DV] + of_ref[rs, hs]
                    ms = jnp.mean(tot * tot, axis=-1, keepdims=True)
                    y = tot * lax.rsqrt(ms + EPS) * gn_ref[:, hs] * _silu(og_ref[rs, hs])
                    o_ref[rs, hs] = y.astype(BF16)
            else:
                o_ref[rs, vs] = o
            dcol = jnp.sum(eye * decay[:, ls], axis=1, keepdims=True)
            for hh in range(2):
                qr = slice(hh * GLA_DK, (hh + 1) * GLA_DK)
                qc = slice(hh * GLA_DV, (hh + 1) * GLA_DV)
                s_ref[p, qr, qc] = dcol[qr] * sp[qr, qc] + upd[qr, qc]


def _gla(p0, pad, wg, bg, gnorm, of=None):
    reverse = of is not None
    nblk = N_ALL // ROWS
    blk = (lambda i: _rev_block(i, nblk)) if reverse else (lambda i: i)
    d = 1 if reverse else 0
    qk_blk = 2 * LRU_W // GLA_QK
    in_specs = [
        pl.BlockSpec((ROWS, GLA_QK), lambda i: (blk(i), qk_blk)),
        pl.BlockSpec((ROWS, GLA_QK), lambda i: (blk(i), qk_blk + 1)),
        pl.BlockSpec((ROWS, GLA_V), lambda i: (blk(i), 3)),
        pl.BlockSpec((ROWS, LANES), lambda i: (blk(i), 0)),
        pl.BlockSpec((None, GLA_RANK, GLA_QK), lambda i: (d, 0, 0)),
        pl.BlockSpec((None, 1, GLA_QK), lambda i: (d, 0, 0)),
    ]
    args = [p0, p0, p0, pad, wg, bg.reshape(2, 1, GLA_QK)]
    if reverse:
        in_specs += [
            pl.BlockSpec((ROWS, GLA_V), lambda i: (blk(i), 0)),
            pl.BlockSpec((ROWS, GLA_V), lambda i: (blk(i), 4)),
            pl.BlockSpec((1, GLA_V), lambda i: (0, 0)),
        ]
        args += [of, p0, gnorm]
    return pl.pallas_call(
        functools.partial(_gla_kernel, reverse=reverse),
        grid=(nblk,),
        in_specs=in_specs,
        out_specs=pl.BlockSpec((ROWS, GLA_V), lambda i: (blk(i), 0)),
        out_shape=jax.ShapeDtypeStruct((N_ALL, GLA_V), BF16 if reverse else F32),
        scratch_shapes=[pltpu.VMEM((GLA_H // 2, 2 * GLA_DK, 2 * GLA_DV), F32)]
        + [pltpu.VMEM((ROWS, GLA_QK), BF16) for _ in range(4)] + [pltpu.VMEM((SUBLANES, GLA_QK), F32)]
        + [pltpu.VMEM((ROWS // CHUNK * GLA_H // 2, CHUNK, 2 * GLA_DK), BF16),
           pltpu.VMEM((ROWS // CHUNK * GLA_H // 2, 2 * GLA_DK, 2 * GLA_DV), F32)],
        compiler_params=_params("arbitrary"),
        name="gla_bwd" if reverse else "gla_fwd",
    )(*args)


def _conv_silu_kernel(x_ref, prev_ref, next_ref, w_ref, b_ref, o_ref):
    i = pl.program_id(0)
    y = _conv_block(x_ref[...], prev_ref[...], next_ref[...], w_ref, b_ref, i, pl.num_programs(0))
    o_ref[...] = _silu(y)


def _conv_silu(p1, conv_w, conv_b):
    nblk = N_ALL // ROWS
    tc = 2048
    off = SSD_INNER // tc
    return pl.pallas_call(
        _conv_silu_kernel,
        grid=(nblk, SSD_XBC // tc),
        in_specs=_halo_specs(tc, lambda j: j + off, nblk) + [
            pl.BlockSpec((CONV_W, tc), lambda i, j: (0, j)),
            pl.BlockSpec((1, tc), lambda i, j: (0, j)),
        ],
        out_specs=pl.BlockSpec((ROWS, tc), lambda i, j: (i, j)),
        out_shape=jax.ShapeDtypeStruct((N_ALL, SSD_XBC), F32),
        compiler_params=_params("arbitrary", "arbitrary"),
        name="ssd_conv",
    )(p1, p1, p1, conv_w, conv_b)


def _ssd_kernel(*refs, reverse):
    if reverse:
        (xs_ref, b_ref, c_ref, dt_ref, dtb_ref, alog_ref, acc_ref, z_ref, dsk_ref, gn_ref, y_ref,
         st_ref, cum_s, w2_s, ct2_s, dt2_s, ybuf) = refs
    else:
        xs_ref, b_ref, c_ref, dt_ref, dtb_ref, alog_ref, y_ref, st_ref, cum_s, w2_s, ct2_s, dt2_s = refs
    d = 1 if reverse else 0
    i = pl.program_id(0)

    @pl.when(i == 0)
    def _():
        st_ref[...] = jnp.zeros_like(st_ref)

    tri_b = _tri(reverse).astype(BF16)
    row2 = lax.broadcasted_iota(jnp.int32, (CHUNK, 2 * SSD_P), 0)
    col2 = jnp.bitwise_and(lax.broadcasted_iota(jnp.int32, (CHUNK, 2 * SSD_P), 1), SSD_P - 1)
    causal2 = (col2 >= row2) if reverse else (col2 <= row2)
    low = lax.broadcasted_iota(jnp.int32, (1, 2 * SSD_P), 1) < SSD_P
    a_neg = -jnp.exp(alog_ref[...])
    nch = ROWS // CHUNK
    hg = SSD_H // SSD_G
    pw = 2 * SSD_P

    low_c = lax.broadcasted_iota(jnp.int32, (CHUNK, 2 * SSD_P), 1) < SSD_P

    def pair(v, h0):
        if v.shape == (CHUNK, 2 * SSD_P):
            return jnp.take_along_axis(v, jnp.where(low_c, h0, h0 + 1), axis=1)
        return jnp.where(low, v[:, h0:h0 + 1], v[:, h0 + 1:h0 + 2])

    dtv = _softplus(dt_ref[:, d * SSD_H:(d + 1) * SSD_H] + dtb_ref[...])
    la = dtv * a_neg
    for c in range(nch):
        rs = slice(c * CHUNK, (c + 1) * CHUNK)
        cum = _cumsum_rows(tri_b, la[rs])
        last = cum[0:1, :] if reverse else cum[CHUNK - 1:CHUNK, :]
        w2 = dtv[rs] * jnp.exp(last - cum)
        cum_s[c] = jnp.concatenate([cum, cum], axis=1)
        w2_s[c] = jnp.concatenate([w2, w2], axis=1)
        cum_t = cum.T
        dt_t = dtv[rs].T
        ct2_s[c] = jnp.concatenate([cum_t, cum_t], axis=1)
        dt2_s[c] = jnp.concatenate([dt_t, dt_t], axis=1)

    def chunk(cc, carry):
        c = (nch - 1 - cc) if reverse else cc
        rs = pl.ds(pl.multiple_of(c * CHUNK, CHUNK), CHUNK)
        cum = cum_s[c]
        w2 = w2_s[c]
        ct2 = ct2_s[c]
        dt2 = dt2_s[c]
        dec = jnp.exp(cum[0:1, :] if reverse else cum[CHUNK - 1:CHUNK, :])
        cbs, yis = [], []
        for g in range(SSD_G):
            gs = slice(g * SSD_Z, (g + 1) * SSD_Z)
            cg = c_ref[rs, gs].astype(BF16)
            cbs.append(_dot_nt(cg, b_ref[rs, gs].astype(BF16)))
            yis.append(_dot(cg, st_ref[g].astype(BF16)))
        for g in range(SSD_G):
            gs = slice(g * SSD_Z, (g + 1) * SSD_Z)
            bgb = b_ref[rs, gs].astype(BF16)
            cb2 = jnp.concatenate([cbs[g], cbs[g]], axis=1)
            st = st_ref[g]
            y_inter = yis[g]
            x2s, decs = [], []
            for pr in range(hg // 2):
                h0 = g * hg + 2 * pr
                ps = slice(h0 * SSD_P, (h0 + 2) * SSD_P)
                xp = xs_ref[rs, ps]
                ccol = pair(cum, h0)
                rrow = jnp.where(low, ct2[h0:h0 + 1, :], ct2[h0 + 1:h0 + 2, :])
                drow = jnp.where(low, dt2[h0:h0 + 1, :], dt2[h0 + 1:h0 + 2, :])
                seg = jnp.exp(jnp.where(causal2, ccol - rrow, -jnp.inf))
                m = (cb2 * seg * drow).astype(BF16)
                xbd = jnp.concatenate([jnp.where(low, xp, 0.0), jnp.where(low, 0.0, xp)], axis=0).astype(BF16)
                y = _dot(m, xbd) + y_inter[:, pr * pw:(pr + 1) * pw] * jnp.exp(ccol)
                if reverse:
                    ybuf[:, ps] = y + acc_ref[rs, ps]
                else:
                    y_ref[rs, ps] = y
                x2s.append((xp * pair(w2, h0)).astype(BF16))
                decs.append(pair(dec, h0))
            upd = _dot_tn(bgb, jnp.concatenate(x2s, axis=1))
            st_ref[g] = st * jnp.concatenate(decs, axis=1) + upd
        if reverse:
            u = (ybuf[...] + dsk_ref[...] * xs_ref[rs, :]) * _silu(z_ref[rs, :])
            ms = jnp.mean(u * u, axis=-1, keepdims=True)
            y_ref[rs, :] = (u * lax.rsqrt(ms + EPS) * gn_ref[...]).astype(BF16)
        return carry

    lax.fori_loop(0, nch, chunk, 0)


def _ssd(xbc, pdt, dt_bias, a_log, fin=None):
    reverse = fin is not None
    nblk = N_ALL // ROWS
    nctx = T_CTX // ROWS
    blk = (lambda i: _rev_block(i, nblk)) if reverse else (lambda i: i)
    d = 1 if reverse else 0
    gz = SSD_G * SSD_Z
    full = lambda i: (blk(i), 0)
    in_specs = [
        pl.BlockSpec((ROWS, SSD_INNER), full),
        pl.BlockSpec((ROWS, gz), lambda i: (blk(i), SSD_INNER // gz)),
        pl.BlockSpec((ROWS, gz), lambda i: (blk(i), SSD_INNER // gz + 1)),
        pl.BlockSpec((ROWS, LANES), full),
        pl.BlockSpec((None, 1, SSD_H), lambda i: (d, 0, 0)),
        pl.BlockSpec((None, 1, SSD_H), lambda i: (d, 0, 0)),
    ]
    args = [xbc, xbc, xbc, pdt, dt_bias.reshape(2, 1, SSD_H), a_log.reshape(2, 1, SSD_H)]
    nch = ROWS // CHUNK
    scratch = [pltpu.VMEM((SSD_G, SSD_Z, SSD_INNER // SSD_G), F32),
               pltpu.VMEM((nch, CHUNK, 2 * SSD_H), F32), pltpu.VMEM((nch, CHUNK, 2 * SSD_H), F32),
               pltpu.VMEM((nch, SSD_H, 2 * CHUNK), F32), pltpu.VMEM((nch, SSD_H, 2 * CHUNK), F32)]
    if reverse:
        yf, p1, d_rep, norm_g = fin
        in_specs += [pl.BlockSpec((ROWS, SSD_INNER), full), pl.BlockSpec((ROWS, SSD_INNER), full),
                     pl.BlockSpec((1, SSD_INNER), lambda i: (0, 0)), pl.BlockSpec((1, SSD_INNER), lambda i: (0, 0))]
        args += [yf, p1, d_rep, norm_g]
        scratch.append(pltpu.VMEM((CHUNK, SSD_INNER), F32))
        out_spec = pl.BlockSpec((ROWS, SSD_INNER), lambda i: (blk(jnp.maximum(i, nctx)) - nctx, 0))
        out_shape = jax.ShapeDtypeStruct((T_LAT, SSD_INNER), BF16)
    else:
        out_spec = pl.BlockSpec((ROWS, SSD_INNER), full)
        out_shape = jax.ShapeDtypeStruct((N_ALL, SSD_INNER), F32)
    return pl.pallas_call(
        functools.partial(_ssd_kernel, reverse=reverse),
        grid=(nblk,),
        in_specs=in_specs,
        out_specs=out_spec,
        out_shape=out_shape,
        scratch_shapes=scratch,
        compiler_params=_params("arbitrary"),
        name="ssd_bwd" if reverse else "ssd_fwd",
    )(*args)


def _gather_wait(h_hbm, buf, sem, slot):
    pltpu.make_async_copy(h_hbm.at[pl.ds(0, MOE_TM), :], buf.at[slot], sem.at[slot]).wait()


def _gather_issue(idx_ref, h_hbm, buf, sem, slot):
    def body(r8, carry):
        base = pl.multiple_of(r8 * SUBLANES, SUBLANES)
        for u in range(SUBLANES):
            tok = idx_ref[0, 0, base + u]
            pltpu.make_async_copy(h_hbm.at[pl.ds(tok, 1), :], buf.at[slot, pl.ds(base + u, 1), :],
                                  sem.at[slot]).start()
        return carry

    lax.fori_loop(0, MOE_TM // SUBLANES, body, 0)


def _expert_block_kernel(be_ref, ws_ref, nx_ref, nu_ref, *refs, n_w, layer, gather, compute):
    if gather:
        icur_ref, inxt_ref, h_hbm = refs[:3]
        refs = refs[3:]
    else:
        x_ref = refs[0]
        refs = refs[1:]
    w_hbm, o_ref = refs[:n_w], refs[n_w]
    scr = refs[n_w + 1:]
    w32, wsem = scr[:n_w], scr[n_w]
    m = pl.program_id(0)
    nu = nu_ref[0]
    e = be_ref[m]
    wslot = ws_ref[m]
    first = jnp.logical_and(m < nu, jnp.logical_or(m == 0, e != be_ref[jnp.maximum(m - 1, 0)]))

    def w_copy(k, ee, slot):
        return pltpu.make_async_copy(w_hbm[k].at[layer, ee], w32[k].at[slot], wsem.at[k, slot])

    @pl.when(m == 0)
    def _():
        for k in range(n_w):
            w_copy(k, e, wslot).start()

    if gather:
        gbuf, gsem = scr[n_w + 1:]
        gslot = lax.rem(m, 2)

        @pl.when(m == 0)
        def _():
            _gather_issue(icur_ref, h_hbm, gbuf, gsem, 0)

        @pl.when(m + 1 < nu)
        def _():
            _gather_issue(inxt_ref, h_hbm, gbuf, gsem, 1 - gslot)

    @pl.when(jnp.logical_and(first, nx_ref[m] >= 0))
    def _():
        for k in range(n_w):
            w_copy(k, nx_ref[m], 1 - wslot).start()

    @pl.when(first)
    def _():
        for k in range(n_w):
            w_copy(k, e, wslot).wait()

    @pl.when(m < nu)
    def _():
        if gather:
            _gather_wait(h_hbm, gbuf, gsem, gslot)
            x = gbuf[gslot]
        else:
            x = x_ref[...].astype(F32)
        o_ref[...] = compute(x, *[w32[k][wslot] for k in range(n_w)]).astype(o_ref.dtype)


def _expert_stage(src, ws, plan, layer, n_out, out_dtype, compute, name, slot_tok=None):
    be, wsl, nx, nu = plan
    nblk = be.shape[0]
    gather = slot_tok is not None
    anyspec = pl.BlockSpec(memory_space=pl.ANY)
    used = lambda m, nu_ref: jnp.minimum(m, nu_ref[0] - 1)
    n_w = len(ws)
    wshape = ws[0].shape[2:]
    scratch = [pltpu.VMEM((2,) + wshape, F32) for _ in ws]
    scratch.append(pltpu.SemaphoreType.DMA((n_w, 2)))
    if gather:
        in_specs = [
            pl.BlockSpec((1, 1, MOE_TM), lambda m, be, ws, nx, nu: (used(m, nu), 0, 0), memory_space=pltpu.SMEM),
            pl.BlockSpec((1, 1, MOE_TM), lambda m, be, ws, nx, nu: (used(m + 1, nu), 0, 0),
                         memory_space=pltpu.SMEM),
            anyspec,
        ]
        args = [slot_tok, slot_tok, src]
        scratch += [pltpu.VMEM((2, MOE_TM, src.shape[1]), src.dtype), pltpu.SemaphoreType.DMA((2,))]
    else:
        in_specs = [pl.BlockSpec((MOE_TM, src.shape[1]), lambda m, be, ws, nx, nu: (used(m, nu), 0))]
        args = [src]
    return pl.pallas_call(
        functools.partial(_expert_block_kernel, n_w=n_w, layer=layer, gather=gather, compute=compute),
        grid_spec=pltpu.PrefetchScalarGridSpec(
            num_scalar_prefetch=4,
            grid=(nblk,),
            in_specs=in_specs + [anyspec] * n_w,
            out_specs=pl.BlockSpec((MOE_TM, n_out), lambda m, be, ws, nx, nu: (used(m, nu), 0)),
            scratch_shapes=scratch,
        ),
        out_shape=jax.ShapeDtypeStruct((nblk * MOE_TM, n_out), out_dtype),
        compiler_params=_params("arbitrary"),
        name=name,
    )(be, wsl, nx, nu, *args, *ws)


def _moe(h, eidx, wts, w_gate, w_up, w_down, layer):
    n = h.shape[0]
    i32 = jnp.int32
    flat_e = eidx.reshape(-1)
    experts = jnp.arange(N_EXP, dtype=i32)
    onehot = (flat_e[:, None] == experts[None, :]).astype(i32)
    csum = jnp.cumsum(onehot, axis=0)
    bcount = (csum[-1] + MOE_TM - 1) // MOE_TM
    bend = jnp.cumsum(bcount)
    bstart = bend - bcount
    dest = jnp.sum(onehot * (csum - 1 + (bstart * MOE_TM)[None, :]), axis=1)
    nblk = -(-2 * n // MOE_TM) + N_EXP
    n_used = bend[-1].astype(i32)
    blk = jnp.minimum(jnp.arange(nblk, dtype=i32), n_used - 1)
    block_exp = jnp.sum((blk[:, None] >= bend[None, :]).astype(i32), axis=1)
    owns = bcount > 0
    wslot = (jnp.cumsum(owns.astype(i32)) - 1) % 2
    later = jnp.logical_and(owns[None, :], experts[None, :] > experts[:, None])
    nxt = jnp.min(jnp.where(later, experts[None, :], N_EXP), axis=1)
    nxt = jnp.where(nxt == N_EXP, -1, nxt)
    of_blk = (block_exp[:, None] == experts[None, :]).astype(i32)
    plan = (block_exp.astype(i32), jnp.sum(of_blk * wslot[None, :], axis=1).astype(i32),
            jnp.sum(of_blk * nxt[None, :], axis=1).astype(i32), n_used.reshape(1))
    tok = jnp.tile(jnp.arange(n, dtype=i32), 2)
    pad_tok = jnp.arange(nblk * MOE_TM, dtype=i32) % n
    slot_tok = pad_tok.at[dest].set(tok).reshape(nblk, 1, MOE_TM)
    hb = _expert_stage(h, [w_gate, w_up], plan, layer, D_EXP, BF16,
                       lambda x, wg, wu: _silu(_dot(x, wg)) * _dot(x, wu), "moe_up", slot_tok=slot_tok)
    yb = _expert_stage(hb, [w_down], plan, layer, D, F32, lambda x, wd: _dot(x, wd), "moe_down")
    return yb, dest.reshape(2, n), wts.T


def _pos_tables():
    quarter = D // 4
    omega = 1.0 / (10000.0 ** (jnp.arange(quarter, dtype=F32) / quarter))
    ang_r = jnp.arange(T_LAT // GRID_W, dtype=F32)[:, None] * omega
    ang_c = jnp.arange(GRID_W, dtype=F32)[:, None] * omega
    emb_r = jnp.concatenate([jnp.sin(ang_r), jnp.cos(ang_r)], axis=-1)
    emb_c = jnp.concatenate([jnp.sin(ang_c), jnp.cos(ang_c)], axis=-1)
    return emb_r, emb_c


def _block_diag_gates(wa, wx, group):
    nb = wa.shape[1]
    per = group // wa.shape[2]
    eye = jnp.eye(per, dtype=F32)

    def bd(w):
        w = w.reshape(nb // per, per, w.shape[1], w.shape[2])
        return jnp.einsum("gnkj,nm->gnkmj", w, eye).reshape(nb // per, group, group)

    return jnp.concatenate([bd(wa[0]), bd(wa[1]), bd(wx[0]), bd(wx[1])], axis=-1).astype(BF16)


def kernel(x, c, ctx, c_ctx, mod_w, mod_b, norm1_g, norm2_g, ev_w_in, ev_conv_w, ev_conv_b, lru_wa, lru_ba, lru_wx, lru_bx, lru_lambda, gla_wg_up, gla_bg, gla_norm_g, ev_w_out, od_w_in, od_conv_w, od_conv_b, ssd_a_log, ssd_dt_bias, ssd_d, ssd_norm_g, od_w_out, router_w, router_b, exp_w_gate, exp_w_up, exp_w_down, final_norm_g):
    mods = _mod_vectors(c, c_ctx, mod_w, mod_b)
    emb_r, emb_c = _pos_tables()
    rwt = router_w.T
    rb = router_b.reshape(N_EXP, 1)
    tm, tn = PROJ_TM, PROJ_TN

    x0, h0 = _prep0(x[0], ctx[0], emb_r, emb_c, mods[0], norm1_g[0:1])
    w_in_t = jnp.swapaxes(ev_w_in[0], 0, 1)
    p0 = _mm([h0], w_in_t, ncols=EVEN_MAIN, tm=tm, tn=tn, w_t=True)
    pad = _mm([h0], w_in_t, ncols=LANES, tm=tm, tn=LANES, col_off=EVEN_MAIN // LANES, w_t=True)
    wbd = _block_diag_gates(lru_wa[0], lru_wx[0], LRU_GATE_GROUP)
    ya = _lru(p0, ev_conv_w[0], ev_conv_b[0:1], wbd, lru_ba[0], lru_bx[0], lru_lambda[0])
    of = _gla(p0, pad, gla_wg_up[0], gla_bg[0], None)
    yb = _gla(p0, pad, gla_wg_up[0], gla_bg[0], gla_norm_g[0:1], of=of)
    y0 = _mm([ya, yb], ev_w_out[0], ncols=D, tm=tm, tn=tn)
    nctx = T_CTX // ROWS
    x1, h1, e0, w0 = _token_stage(x0, y0, mods[0], mods[0], norm2_g[0:1], rwt, rb, gate_col=2, shift_col=3,
                                  scale_col=4, ctx_blocks=nctx, route=True)
    f0, dest0, wc0 = _moe(h1, e0, w0, exp_w_gate, exp_w_up, exp_w_down, 0)
    x2, h2 = _token_stage(x1, f0, mods[0], mods[1], norm1_g[1:2], rwt, rb, gate_col=5, shift_col=0,
                          scale_col=1, ctx_blocks=nctx, combine=(dest0, wc0))

    p1 = _mm([h2], od_w_in[0], ncols=ODD_MAIN, tm=tm, tn=tn)
    pdt = _mm([h2], od_w_in[0], ncols=LANES, tm=tm, tn=LANES, col_off=ODD_MAIN // LANES)
    xbc = _conv_silu(p1, od_conv_w[0], od_conv_b[0:1])
    yf = _ssd(xbc, pdt, ssd_dt_bias[0], ssd_a_log[0])
    d_rep = jnp.repeat(ssd_d[0], SSD_P).reshape(1, SSD_INNER)
    gy = _ssd(xbc, pdt, ssd_dt_bias[0], ssd_a_log[0], fin=(yf, p1, d_rep, ssd_norm_g[0:1]))
    y1 = _mm([gy], od_w_out[0], ncols=D, tm=OUT1_TM, tn=OUT1_TN)
    x3, h3, e1, w1 = _token_stage(x2, y1, mods[1], mods[1], norm2_g[1:2], rwt, rb, gate_col=2, shift_col=3,
                                  scale_col=4, x_off=nctx, route=True)
    f1, dest1, wc1 = _moe(h3, e1, w1, exp_w_gate, exp_w_up, exp_w_down, 1)
    (out,) = _token_stage(x3, f1, mods[1], mods[1], final_norm_g.reshape(1, D), rwt, rb, gate_col=5,
                          combine=(dest1, wc1), final=True)
    return out[None]
```

```python
import functools
import math

import jax
import jax.numpy as jnp
from jax import lax
from jax.experimental import pallas as pl
from jax.experimental.pallas import tpu as pltpu

F32 = jnp.float32
BF16 = jnp.bfloat16
HIGHEST = lax.Precision.HIGHEST

D = 2048
T_LAT = 8192
T_CTX = 256
N_ALL = T_CTX + T_LAT
GRID_W = 64
EPS = 1e-6
CONV_W = 4
LRU_W = 1024
LRU_C = 8.0
GLA_H = 8
GLA_DK = 64
GLA_DV = 128
GLA_QK = GLA_H * GLA_DK
GLA_V = GLA_H * GLA_DV
GLA_RANK = 16
GLA_TAU = 16.0
CHUNK = 64
EVEN_MAIN = 2 * LRU_W + 2 * GLA_QK + 2 * GLA_V
SSD_INNER = 2 * D
SSD_P = 64
SSD_H = SSD_INNER // SSD_P
SSD_G = 8
SSD_Z = 128
SSD_XBC = SSD_INNER + 2 * SSD_G * SSD_Z
ODD_MAIN = SSD_INNER + SSD_XBC
N_EXP = 32
N_GRP = 4
GRP = N_EXP // N_GRP
D_EXP = 1024
MOE_TM = 256

ROWS = 256
LANES = 128
SUBLANES = 8
VMEM_LIMIT = 56 * 1024 * 1024
PROJ_TM = N_ALL // 8
PROJ_TN = 1024
OUT1_TM = 1024
OUT1_TN = 512
LRU_GATE_GROUP = 256


def _params(*sem):
    return pltpu.CompilerParams(dimension_semantics=sem, vmem_limit_bytes=VMEM_LIMIT)


def _silu(v):
    return v * jax.nn.sigmoid(v)


def _softplus(v):
    return jnp.maximum(v, 0.0) + jnp.log1p(jnp.exp(-jnp.abs(v)))


def _dot(a, b):
    return jnp.dot(a, b, preferred_element_type=F32)


def _dot_nt(a, b):
    return lax.dot_general(a, b, (((1,), (1,)), ((), ())), preferred_element_type=F32)


def _dot_tn(a, b):
    return lax.dot_general(a, b, (((0,), (0,)), ((), ())), preferred_element_type=F32)


def _mod_kernel(s_ref, w_ref, b_ref, o_ref):
    tn = w_ref.shape[1]
    nrep = tn // LANES

    def body(r, acc):
        a0, a1 = acc
        rows = pl.ds(pl.multiple_of(r * SUBLANES, SUBLANES), SUBLANES)
        w = w_ref[rows, :]
        s0 = _silu(s_ref[0, rows, :])
        s1 = _silu(s_ref[1, rows, :])
        a0 = a0 + w * jnp.concatenate([s0] * nrep, axis=1)
        a1 = a1 + w * jnp.concatenate([s1] * nrep, axis=1)
        return a0, a1

    zero = jnp.zeros((SUBLANES, tn), F32)
    a0, a1 = lax.fori_loop(0, D // SUBLANES, body, (zero, zero), unroll=4)
    o_ref[0:1, :] = jnp.sum(a0, axis=0, keepdims=True) + b_ref[...]
    o_ref[1:2, :] = jnp.sum(a1, axis=0, keepdims=True) + b_ref[...]


def _mod_vectors(c, c_ctx, mod_w, mod_b):
    depth = mod_w.shape[0]
    tn = 1024
    s = jnp.broadcast_to(jnp.stack([c[0], c_ctx])[:, :, None], (2, D, LANES))
    return pl.pallas_call(
        _mod_kernel,
        grid=(depth, 6 * D // tn),
        in_specs=[
            pl.BlockSpec((2, D, LANES), lambda l, j: (0, 0, 0)),
            pl.BlockSpec((None, D, tn), lambda l, j: (l, 0, j)),
            pl.BlockSpec((None, 1, tn), lambda l, j: (l, 0, j)),
        ],
        out_specs=pl.BlockSpec((None, 2, tn), lambda l, j: (l, 0, j)),
        out_shape=jax.ShapeDtypeStruct((depth, 2, 6 * D), F32),
        compiler_params=_params("arbitrary", "arbitrary"),
        name="mod_vectors",
    )(s, mod_w, mod_b.reshape(depth, 1, 6 * D))


def _mod_row(mod_ref, kind, col):
    return mod_ref[pl.ds(kind, 1), col * D:(col + 1) * D]


def _ada_norm(xv, g, shift, scale):
    ms = jnp.mean(xv * xv, axis=-1, keepdims=True)
    return (xv * lax.rsqrt(ms + EPS) * g) * (1.0 + scale) + shift


def _prep0_kernel(x_ref, ctx_ref, er_ref, ec_ref, mod_ref, g_ref, xo_ref, ho_ref):
    i = pl.program_id(0)

    @pl.when(i == 0)
    def _():
        xo_ref[...] = ctx_ref[...]

    @pl.when(i > 0)
    def _():
        r0 = (i - 1) * (ROWS // GRID_W)
        for j in range(ROWS // GRID_W):
            rs = slice(j * GRID_W, (j + 1) * GRID_W)
            xo_ref[rs, 0:D // 2] = x_ref[rs, 0:D // 2] + er_ref[pl.ds(r0 + j, 1), :]
            xo_ref[rs, D // 2:D] = x_ref[rs, D // 2:D] + ec_ref[...]

    kind = jnp.where(i == 0, 1, 0)
    h = _ada_norm(xo_ref[...], g_ref[...], _mod_row(mod_ref, kind, 0), _mod_row(mod_ref, kind, 1))
    ho_ref[...] = h.astype(BF16)


def _prep0(x, ctx, emb_r, emb_c, mod0, g):
    nblk = N_ALL // ROWS
    return pl.pallas_call(
        _prep0_kernel,
        grid=(nblk,),
        in_specs=[
            pl.BlockSpec((ROWS, D), lambda i: (jnp.maximum(i - 1, 0), 0)),
            pl.BlockSpec((ROWS, D), lambda i: (0, 0)),
            pl.BlockSpec(emb_r.shape, lambda i: (0, 0)),
            pl.BlockSpec(emb_c.shape, lambda i: (0, 0)),
            pl.BlockSpec((2, 6 * D), lambda i: (0, 0)),
            pl.BlockSpec((1, D), lambda i: (0, 0)),
        ],
        out_specs=[pl.BlockSpec((ROWS, D), lambda i: (i, 0)), pl.BlockSpec((ROWS, D), lambda i: (i, 0))],
        out_shape=[jax.ShapeDtypeStruct((N_ALL, D), F32), jax.ShapeDtypeStruct((N_ALL, D), BF16)],
        compiler_params=_params("arbitrary"),
        name="embed_norm",
    )(x, ctx, emb_r, emb_c, mod0, g)


def _route(hf32, rwt_ref, rb_ref, eo_ref, wo_ref):
    logits = lax.dot_general(rwt_ref[...], hf32, (((1,), (1,)), ((), ())),
                             precision=HIGHEST, preferred_element_type=F32)
    s = jax.nn.sigmoid(logits)
    sel = s + rb_ref[...]
    row = lax.broadcasted_iota(jnp.int32, (GRP, ROWS), 0)
    neg = jnp.float32(-jnp.inf)
    gs, i1s, i2s = [], [], []
    for g in range(N_GRP):
        blk = sel[g * GRP:(g + 1) * GRP, :]
        m1 = jnp.max(blk, axis=0, keepdims=True)
        i1 = jnp.min(jnp.where(blk == m1, row, GRP), axis=0, keepdims=True)
        blk2 = jnp.where(row == i1, neg, blk)
        m2 = jnp.max(blk2, axis=0, keepdims=True)
        i2 = jnp.min(jnp.where(blk2 == m2, row, GRP), axis=0, keepdims=True)
        gs.append(m1 + m2)
        i1s.append(i1)
        i2s.append(i2)
    best, gi, i1, i2 = gs[0], jnp.zeros((1, ROWS), jnp.int32), i1s[0], i2s[0]
    for g in range(1, N_GRP):
        upd = gs[g] > best
        best = jnp.where(upd, gs[g], best)
        gi = jnp.where(upd, g, gi)
        i1 = jnp.where(upd, i1s[g], i1)
        i2 = jnp.where(upd, i2s[g], i2)
    e1 = gi * GRP + i1
    e2 = gi * GRP + i2
    erow = lax.broadcasted_iota(jnp.int32, (N_EXP, ROWS), 0)
    s1 = jnp.sum(jnp.where(erow == e1, s, 0.0), axis=0, keepdims=True)
    s2 = jnp.sum(jnp.where(erow == e2, s, 0.0), axis=0, keepdims=True)
    tot = s1 + s2
    eo_ref[0:1, :] = e1
    eo_ref[1:2, :] = e2
    wo_ref[0:1, :] = s1 / tot
    wo_ref[1:2, :] = s2 / tot


def _combine_wait(yb_hbm, ybuf, sem, slot):
    pltpu.make_async_copy(yb_hbm.at[pl.ds(0, 2 * ROWS), :], ybuf.at[slot], sem.at[slot]).wait()


def _combine_issue(dest_ref, yb_hbm, ybuf, sem, slot):
    def body(r, carry):
        for k in range(2):
            row = dest_ref[0, k, r]
            pltpu.make_async_copy(yb_hbm.at[pl.ds(row, 1), :], ybuf.at[slot, pl.ds(k * ROWS + r, 1), :],
                                  sem.at[slot]).start()
        return carry

    lax.fori_loop(0, ROWS, body, 0, unroll=8)


def _token_kernel(*refs, gate_col, shift_col, scale_col, ctx_blocks, route, combine, final):
    refs = list(refs)
    if combine:
        dcur_ref, dnxt_ref = refs.pop(0), refs.pop(0)
    x_ref, y_ref = refs.pop(0), refs.pop(0)
    if combine:
        wc_ref = refs.pop(0)
    modg_ref, modn_ref, g_ref, rwt_ref, rb_ref = (refs.pop(0) for _ in range(5))
    i = pl.program_id(0)
    if combine:
        ybuf, sem = refs[-2], refs[-1]
        slot = lax.rem(i, 2)

        @pl.when(i == 0)
        def _():
            _combine_issue(dcur_ref, y_ref, ybuf, sem, 0)

        @pl.when(i + 1 < pl.num_programs(0))
        def _():
            _combine_issue(dnxt_ref, y_ref, ybuf, sem, 1 - slot)

        _combine_wait(y_ref, ybuf, sem, slot)
        y = wc_ref[:, 0:1] * ybuf[slot, 0:ROWS, :] + wc_ref[:, 1:2] * ybuf[slot, ROWS:2 * ROWS, :]
    else:
        y = y_ref[...]
    kind = jnp.where(i < ctx_blocks, 1, 0)
    xn = x_ref[...] + _mod_row(modg_ref, kind, gate_col) * y
    if final:
        ms = jnp.mean(xn * xn, axis=-1, keepdims=True)
        refs[0][...] = xn * lax.rsqrt(ms + EPS) * g_ref[...]
        return
    xo_ref, ho_ref = refs[0], refs[1]
    xo_ref[...] = xn
    h = _ada_norm(xn, g_ref[...], _mod_row(modn_ref, kind, shift_col), _mod_row(modn_ref, kind, scale_col))
    ho_ref[...] = h.astype(ho_ref.dtype)
    if route:
        _route(h, rwt_ref, rb_ref, refs[2], refs[3])


def _token_stage(x, y, modg, modn, g, rwt, rb, *, gate_col, shift_col=0, scale_col=0, x_off=0, ctx_blocks=0,
                 route=False, combine=None, final=False):
    n = x.shape[0] - x_off * ROWS
    nblk = n // ROWS
    row = lambda i: (i, 0)
    const2 = lambda i: (0, 0)
    in_specs, args, scratch = [], [], []
    if combine is not None:
        dest, wc = combine
        dest3 = dest.reshape(2, nblk, ROWS).transpose(1, 0, 2)
        in_specs += [pl.BlockSpec((1, 2, ROWS), lambda i: (i, 0, 0), memory_space=pltpu.SMEM),
                     pl.BlockSpec((1, 2, ROWS), lambda i: (jnp.minimum(i + 1, nblk - 1), 0, 0),
                                  memory_space=pltpu.SMEM)]
        args += [dest3, dest3]
    in_specs.append(pl.BlockSpec((ROWS, D), lambda i: (i + x_off, 0)))
    args.append(x)
    if combine is not None:
        in_specs += [pl.BlockSpec(memory_space=pl.ANY), pl.BlockSpec((ROWS, 2), row)]
        args += [y, wc]
        scratch = [pltpu.VMEM((2, 2 * ROWS, D), F32), pltpu.SemaphoreType.DMA((2,))]
    else:
        in_specs.append(pl.BlockSpec((ROWS, D), row))
        args.append(y)
    in_specs += [pl.BlockSpec((2, 6 * D), const2), pl.BlockSpec((2, 6 * D), const2), pl.BlockSpec((1, D), const2),
                 pl.BlockSpec((N_EXP, D), const2), pl.BlockSpec((N_EXP, 1), const2)]
    args += [modg, modn, g, rwt, rb]
    if final:
        out_specs = [pl.BlockSpec((ROWS, D), row)]
        out_shape = [jax.ShapeDtypeStruct((n, D), F32)]
    else:
        out_specs = [pl.BlockSpec((ROWS, D), row), pl.BlockSpec((ROWS, D), row)]
        out_shape = [jax.ShapeDtypeStruct((n, D), F32), jax.ShapeDtypeStruct((n, D), F32 if route else BF16)]
        if route:
            out_specs += [pl.BlockSpec((2, ROWS), lambda i: (0, i)), pl.BlockSpec((2, ROWS), lambda i: (0, i))]
            out_shape += [jax.ShapeDtypeStruct((2, n), jnp.int32), jax.ShapeDtypeStruct((2, n), F32)]
    kern = functools.partial(_token_kernel, gate_col=gate_col, shift_col=shift_col, scale_col=scale_col,
                             ctx_blocks=ctx_blocks, route=route, combine=combine is not None, final=final)
    return pl.pallas_call(
        kern,
        grid=(nblk,),
        in_specs=in_specs,
        out_specs=out_specs,
        out_shape=out_shape,
        scratch_shapes=scratch,
        compiler_params=_params("arbitrary"),
        name="token_stage",
    )(*args)


def _mm_kernel(*refs, nx, valid, w_t):
    x_refs, w_refs, o_ref, wb_refs = refs[:nx], refs[nx:2 * nx], refs[2 * nx], refs[2 * nx + 1:]
    out_axis = 0 if w_t else 1

    @pl.when(pl.program_id(1) == 0)
    def _():
        for w_ref, wb_ref in zip(w_refs, wb_refs):
            w = w_ref[...]
            if valid < w.shape[out_axis]:
                w = jnp.where(lax.broadcasted_iota(jnp.int32, w.shape, out_axis) < valid, w, 0.0)
            wb_ref[...] = w.astype(BF16)

    mul = _dot_nt if w_t else _dot
    acc = mul(x_refs[0][...], wb_refs[0][...])
    for x_ref, wb_ref in zip(x_refs[1:], wb_refs[1:]):
        acc = acc + mul(x_ref[...], wb_ref[...])
    o_ref[...] = acc


def _mm(xs, w, *, ncols, tm, tn, col_off=0, w_t=False):
    nx = len(xs)
    k = xs[0].shape[1]
    m = xs[0].shape[0]
    n_total = w.shape[0] if w_t else w.shape[1]
    valid = min(tn, n_total - col_off * tn) if ncols == tn else tn
    in_specs = [pl.BlockSpec((tm, k), lambda j, i: (i, 0)) for _ in xs]
    if w_t:
        in_specs += [pl.BlockSpec((tn, k), functools.partial(lambda j, i, kk: (j + col_off, kk), kk=kk))
                     for kk in range(nx)]
    else:
        in_specs += [pl.BlockSpec((k, tn), functools.partial(lambda j, i, kk: (kk, j + col_off), kk=kk))
                     for kk in range(nx)]
    return pl.pallas_call(
        functools.partial(_mm_kernel, nx=nx, valid=valid, w_t=w_t),
        grid=(ncols // tn, m // tm),
        in_specs=in_specs,
        out_specs=pl.BlockSpec((tm, tn), lambda j, i: (i, j)),
        out_shape=jax.ShapeDtypeStruct((m, ncols), F32),
        scratch_shapes=[pltpu.VMEM((tn, k) if w_t else (k, tn), BF16) for _ in xs],
        compiler_params=_params("arbitrary", "arbitrary"),
        name="proj",
    )(*xs, *([w] * nx))


def _conv_block(cur, prev8, next8, w_ref, b_ref, i, nblk):
    ctx_edge = T_CTX // ROWS
    keep_prev = jnp.logical_and(i != 0, i != ctx_edge)
    keep_next = jnp.logical_and(i != nblk - 1, i != ctx_edge - 1)
    prev8 = jnp.where(keep_prev, prev8, 0.0)
    next8 = jnp.where(keep_next, next8, 0.0)
    ext = jnp.concatenate([prev8, cur, next8], axis=0)
    n_ext = ROWS + 2 * SUBLANES
    y = b_ref[...] + w_ref[CONV_W // 2:CONV_W // 2 + 1, :] * cur
    for j in range(CONV_W):
        off = j - CONV_W // 2
        if off != 0:
            shifted = pltpu.roll(ext, (-off) % n_ext, axis=0)[SUBLANES:SUBLANES + ROWS]
            y = y + w_ref[j:j + 1, :] * shifted
    return y


def _halo_specs(width, col_blk, nblk):
    per = ROWS // SUBLANES
    last8 = N_ALL // SUBLANES - 1
    return [
        pl.BlockSpec((ROWS, width), lambda i, *a: (i, col_blk(*a))),
        pl.BlockSpec((SUBLANES, width), lambda i, *a: (jnp.maximum(i * per - 1, 0), col_blk(*a))),
        pl.BlockSpec((SUBLANES, width), lambda i, *a: (jnp.minimum((i + 1) * per, last8), col_blk(*a))),
    ]


def _scan_tiles(a_ref, b_ref, h_ref, carry_ref, reverse):
    ntile = ROWS // SUBLANES
    row = lax.broadcasted_iota(jnp.int32, (SUBLANES, LRU_W), 0)

    def body(t, carry):
        tt = (ntile - 1 - t) if reverse else t
        rows = pl.ds(pl.multiple_of(tt * SUBLANES, SUBLANES), SUBLANES)
        a = a_ref[rows, :]
        b = b_ref[rows, :]
        for d in (1, 2, 4):
            shift = (SUBLANES - d) if reverse else d
            a_sh = pltpu.roll(a, shift, axis=0)
            b_sh = pltpu.roll(b, shift, axis=0)
            m = (row < SUBLANES - d) if reverse else (row >= d)
            b = jnp.where(m, a * b_sh, 0.0) + b
            a = jnp.where(m, a * a_sh, a)
        h = a * carry + b
        h_ref[rows, :] = h
        edge = h[0:1, :] if reverse else h[SUBLANES - 1:SUBLANES, :]
        return jnp.broadcast_to(edge, (SUBLANES, LRU_W))

    carry_ref[...] = lax.fori_loop(0, ntile, body, carry_ref[...])


def _lru_fwd_kernel(xa_ref, prev_ref, next_ref, cw_ref, cb_ref, wbd_ref, ba_ref, bx_ref, lam_ref,
                    hf_ref, a1_ref, b1_ref, carry_ref, a0_ref, b0_ref):
    i = pl.program_id(0)
    nblk = pl.num_programs(0)

    @pl.when(i == 0)
    def _():
        carry_ref[...] = jnp.zeros_like(carry_ref)

    xa = _conv_block(xa_ref[...], prev_ref[...], next_ref[...], cw_ref, cb_ref, i, nblk)
    xab = xa.astype(BF16)
    nsp = -LRU_C * _softplus(-lam_ref[...])
    gw = wbd_ref.shape[1]
    for g in range(LRU_W // gw):
        cs = slice(g * gw, (g + 1) * gw)
        z = _dot(xab[:, cs], wbd_ref[g])
        for d in range(2):
            r = jax.nn.sigmoid(z[:, d * gw:(d + 1) * gw] + ba_ref[d:d + 1, cs])
            ig = jax.nn.sigmoid(z[:, (2 + d) * gw:(3 + d) * gw] + bx_ref[d:d + 1, cs])
            log_a = r * nsp[d:d + 1, cs]
            a = jnp.exp(log_a)
            b = jnp.sqrt(-jnp.tanh(log_a) * (a * a + 1.0)) * ig * xa[:, cs]
            if d == 0:
                a0_ref[:, cs] = a
                b0_ref[:, cs] = b
            else:
                a1_ref[:, cs] = a
                b1_ref[:, cs] = b
    _scan_tiles(a0_ref, b0_ref, hf_ref, carry_ref, reverse=False)


def _gelu_tanh(v):
    return 0.5 * v * (1.0 + jnp.tanh(math.sqrt(2.0 / math.pi) * (v + 0.044715 * (v * v * v))))


def _lru_bwd_kernel(a1_ref, b1_ref, hf_ref, ga_ref, ya_ref, carry_ref, hb_ref):
    @pl.when(pl.program_id(0) == 0)
    def _():
        carry_ref[...] = jnp.zeros_like(carry_ref)

    _scan_tiles(a1_ref, b1_ref, hb_ref, carry_ref, reverse=True)
    ya_ref[...] = ((hf_ref[...] + hb_ref[...]) * _gelu_tanh(ga_ref[...])).astype(BF16)


def _rev_block(i, nblk):
    nctx = T_CTX // ROWS
    return jnp.where(i < nctx, nctx - 1 - i, nblk - 1 - (i - nctx))


def _lru(p0, conv_w, conv_b, wbd, ba, bx, lam):
    nblk = N_ALL // ROWS
    full = lambda shape: pl.BlockSpec(shape, lambda i: (0,) * len(shape))
    hf, a1, b1 = pl.pallas_call(
        _lru_fwd_kernel,
        grid=(nblk,),
        in_specs=_halo_specs(LRU_W, lambda: 0, nblk) + [
            full((CONV_W, LRU_W)), full((1, LRU_W)), full(wbd.shape),
            full((2, LRU_W)), full((2, LRU_W)), full((2, LRU_W)),
        ],
        out_specs=[pl.BlockSpec((ROWS, LRU_W), lambda i: (i, 0))] * 3,
        out_shape=[jax.ShapeDtypeStruct((N_ALL, LRU_W), F32)] * 3,
        scratch_shapes=[pltpu.VMEM((SUBLANES, LRU_W), F32), pltpu.VMEM((ROWS, LRU_W), F32),
                        pltpu.VMEM((ROWS, LRU_W), F32)],
        compiler_params=_params("arbitrary"),
        name="lru_fwd",
    )(p0, p0, p0, conv_w, conv_b, wbd, ba, bx, lam)
    rev = lambda i: (_rev_block(i, nblk), 0)
    ya = pl.pallas_call(
        _lru_bwd_kernel,
        grid=(nblk,),
        in_specs=[pl.BlockSpec((ROWS, LRU_W), rev)] * 3
        + [pl.BlockSpec((ROWS, LRU_W), lambda i: (_rev_block(i, nblk), 1))],
        out_specs=pl.BlockSpec((ROWS, LRU_W), rev),
        out_shape=jax.ShapeDtypeStruct((N_ALL, LRU_W), BF16),
        scratch_shapes=[pltpu.VMEM((SUBLANES, LRU_W), F32), pltpu.VMEM((ROWS, LRU_W), F32)],
        compiler_params=_params("arbitrary"),
        name="lru_bwd",
    )(a1, b1, hf, p0)
    return ya


def _tri(reverse):
    r = lax.broadcasted_iota(jnp.int32, (CHUNK, CHUNK), 0)
    c = lax.broadcasted_iota(jnp.int32, (CHUNK, CHUNK), 1)
    return (c >= r) if reverse else (c <= r)


def _cumsum_rows(tri_b, v):
    hi = v.astype(BF16)
    r1 = v - hi.astype(F32)
    mid = r1.astype(BF16)
    lo = (r1 - mid.astype(F32)).astype(BF16)
    return _dot(tri_b, hi) + _dot(tri_b, mid) + _dot(tri_b, lo)


def _gla_kernel(*refs, reverse):
    if reverse:
        (q_ref, k_ref, v_ref, ad_ref, wg_ref, bg_ref, of_ref, og_ref, gn_ref, o_ref,
         s_ref, qd_s, klo_s, khi_s, ke_s, dec_s, s2_s, upd_s) = refs
    else:
        (q_ref, k_ref, v_ref, ad_ref, wg_ref, bg_ref, o_ref,
         s_ref, qd_s, klo_s, khi_s, ke_s, dec_s, s2_s, upd_s) = refs
    d = 1 if reverse else 0

    @pl.when(pl.program_id(0) == 0)
    def _():
        s_ref[...] = jnp.zeros_like(s_ref)

    nch = ROWS // CHUNK
    pw = 2 * GLA_DK
    row2 = lax.broadcasted_iota(jnp.int32, (CHUNK, pw), 0)
    col2 = jnp.bitwise_and(lax.broadcasted_iota(jnp.int32, (CHUNK, pw), 1), GLA_DK - 1)
    causal2 = (col2 >= row2) if reverse else (col2 <= row2)
    eye = (lax.broadcasted_iota(jnp.int32, (pw, pw), 0) == lax.broadcasted_iota(jnp.int32, (pw, pw), 1)).astype(F32)
    zeros_v = jnp.zeros((CHUNK, GLA_DV), BF16)

    rr = lax.broadcasted_iota(jnp.int32, (ROWS, ROWS), 0)
    rc = lax.broadcasted_iota(jnp.int32, (ROWS, ROWS), 1)
    same = (rr // CHUNK) == (rc // CHUNK)
    tri_b = jnp.logical_and(same, (rc >= rr) if reverse else (rc <= rr)).astype(BF16)
    ad = ad_ref[:, d * GLA_RANK:(d + 1) * GLA_RANK]
    z = _dot(ad.astype(BF16), wg_ref[...].astype(BF16)) + bg_ref[...]
    lg = -_softplus(-z) * (1.0 / GLA_TAU)
    cum = _cumsum_rows(tri_b, lg)
    cum3 = cum.reshape(nch, CHUNK, GLA_QK)
    last3 = cum3[:, 0:1, :] if reverse else cum3[:, CHUNK - 1:CHUNK, :]
    to_end = jnp.exp(last3 - cum3).reshape(ROWS, GLA_QK)
    k = k_ref[...]
    k_inv = k * jnp.exp(-cum)
    low_all = jnp.bitwise_and(lax.broadcasted_iota(jnp.int32, (1, GLA_QK), 1), pw - 1) < GLA_DK
    qd_s[...] = (q_ref[...] * (GLA_DK ** -0.5) * jnp.exp(cum)).astype(BF16)
    klo_s[...] = jnp.where(low_all, k_inv, 0.0).astype(BF16)
    khi_s[...] = jnp.where(low_all, 0.0, k_inv).astype(BF16)
    ke_s[...] = (k * to_end).astype(BF16)
    dec_s[0:nch, :] = jnp.exp(last3.reshape(nch, GLA_QK))

    npair = GLA_H // 2
    for c in range(nch):
        rs = slice(c * CHUNK, (c + 1) * CHUNK)
        for p in range(npair):
            ls = slice(p * pw, (p + 1) * pw)
            kbd = jnp.concatenate([klo_s[rs, ls], khi_s[rs, ls]], axis=0)
            s2_s[c * npair + p] = jnp.where(causal2, _dot_nt(qd_s[rs, ls], kbd), 0.0).astype(BF16)
            vp = v_ref[rs, p * 2 * GLA_DV:(p + 1) * 2 * GLA_DV].astype(BF16)
            upd_s[c * npair + p] = _dot_tn(ke_s[rs, ls], vp)

    for cc in range(nch):
        c = (nch - 1 - cc) if reverse else cc
        rs = slice(c * CHUNK, (c + 1) * CHUNK)
        decay = dec_s[c:c + 1, :]
        for p in range(npair):
            ls = slice(p * pw, (p + 1) * pw)
            vs = slice(p * 2 * GLA_DV, (p + 1) * 2 * GLA_DV)
            qb = qd_s[rs, ls]
            s2 = s2_s[c * npair + p]
            upd = upd_s[c * npair + p]
            vp = v_ref[rs, vs].astype(BF16)
            vbd = jnp.concatenate([jnp.concatenate([vp[:, :GLA_DV], zeros_v], axis=1),
                                   jnp.concatenate([zeros_v, vp[:, GLA_DV:]], axis=1)], axis=0)
            sp = s_ref[p]
            o = _dot(jnp.concatenate([s2, qb], axis=1), jnp.concatenate([vbd, sp.astype(BF16)], axis=0))
            if reverse:
                for hh in range(2):
                    hs = slice((2 * p + hh) * GLA_DV, (2 * p + hh + 1) * GLA_DV)
                    tot = o[:, hh * GLA_DV:(hh + 1) * GLA_DV] + of_ref[rs, hs]
                    ms = jnp.mean(tot * tot, axis=-1, keepdims=True)
                    y = tot * lax.rsqrt(ms + EPS) * gn_ref[:, hs] * _silu(og_ref[rs, hs])
                    o_ref[rs, hs] = y.astype(BF16)
            else:
                o_ref[rs, vs] = o
            dcol = jnp.sum(eye * decay[:, ls], axis=1, keepdims=True)
            for hh in range(2):
                qr = slice(hh * GLA_DK, (hh + 1) * GLA_DK)
                qc = slice(hh * GLA_DV, (hh + 1) * GLA_DV)
                s_ref[p, qr, qc] = dcol[qr] * sp[qr, qc] + upd[qr, qc]


def _gla(p0, pad, wg, bg, gnorm, of=None):
    reverse = of is not None
    nblk = N_ALL // ROWS
    blk = (lambda i: _rev_block(i, nblk)) if reverse else (lambda i: i)
    d = 1 if reverse else 0
    qk_blk = 2 * LRU_W // GLA_QK
    in_specs = [
        pl.BlockSpec((ROWS, GLA_QK), lambda i: (blk(i), qk_blk)),
        pl.BlockSpec((ROWS, GLA_QK), lambda i: (blk(i), qk_blk + 1)),
        pl.BlockSpec((ROWS, GLA_V), lambda i: (blk(i), 3)),
        pl.BlockSpec((ROWS, LANES), lambda i: (blk(i), 0)),
        pl.BlockSpec((None, GLA_RANK, GLA_QK), lambda i: (d, 0, 0)),
        pl.BlockSpec((None, 1, GLA_QK), lambda i: (d, 0, 0)),
    ]
    args = [p0, p0, p0, pad, wg, bg.reshape(2, 1, GLA_QK)]
    if reverse:
        in_specs += [
            pl.BlockSpec((ROWS, GLA_V), lambda i: (blk(i), 0)),
            pl.BlockSpec((ROWS, GLA_V), lambda i: (blk(i), 4)),
            pl.BlockSpec((1, GLA_V), lambda i: (0, 0)),
        ]
        args += [of, p0, gnorm]
    return pl.pallas_call(
        functools.partial(_gla_kernel, reverse=reverse),
        grid=(nblk,),
        in_specs=in_specs,
        out_specs=pl.BlockSpec((ROWS, GLA_V), lambda i: (blk(i), 0)),
        out_shape=jax.ShapeDtypeStruct((N_ALL, GLA_V), BF16 if reverse else F32),
        scratch_shapes=[pltpu.VMEM((GLA_H // 2, 2 * GLA_DK, 2 * GLA_DV), F32)]
        + [pltpu.VMEM((ROWS, GLA_QK), BF16) for _ in range(4)] + [pltpu.VMEM((SUBLANES, GLA_QK), F32)]
        + [pltpu.VMEM((ROWS // CHUNK * GLA_H // 2, CHUNK, 2 * GLA_DK), BF16),
           pltpu.VMEM((ROWS // CHUNK * GLA_H // 2, 2 * GLA_DK, 2 * GLA_DV), F32)],
        compiler_params=_params("arbitrary"),
        name="gla_bwd" if reverse else "gla_fwd",
    )(*args)


def _conv_silu_kernel(x_ref, prev_ref, next_ref, w_ref, b_ref, o_ref):
    i = pl.program_id(0)
    y = _conv_block(x_ref[...], prev_ref[...], next_ref[...], w_ref, b_ref, i, pl.num_programs(0))
    o_ref[...] = _silu(y)


def _conv_silu(p1, conv_w, conv_b):
    nblk = N_ALL // ROWS
    tc = 2048
    off = SSD_INNER // tc
    return pl.pallas_call(
        _conv_silu_kernel,
        grid=(nblk, SSD_XBC // tc),
        in_specs=_halo_specs(tc, lambda j: j + off, nblk) + [
            pl.BlockSpec((CONV_W, tc), lambda i, j: (0, j)),
            pl.BlockSpec((1, tc), lambda i, j: (0, j)),
        ],
        out_specs=pl.BlockSpec((ROWS, tc), lambda i, j: (i, j)),
        out_shape=jax.ShapeDtypeStruct((N_ALL, SSD_XBC), F32),
        compiler_params=_params("arbitrary", "arbitrary"),
        name="ssd_conv",
    )(p1, p1, p1, conv_w, conv_b)


def _ssd_kernel(*refs, reverse):
    if reverse:
        (xs_ref, b_ref, c_ref, dt_ref, dtb_ref, alog_ref, acc_ref, z_ref, dsk_ref, gn_ref, y_ref,
         st_ref, cum_s, w2_s, ct2_s, dt2_s, ybuf) = refs
    else:
        xs_ref, b_ref, c_ref, dt_ref, dtb_ref, alog_ref, y_ref, st_ref, cum_s, w2_s, ct2_s, dt2_s = refs
    d = 1 if reverse else 0
    i = pl.program_id(0)

    @pl.when(i == 0)
    def _():
        st_ref[...] = jnp.zeros_like(st_ref)

    tri_b = _tri(reverse).astype(BF16)
    row2 = lax.broadcasted_iota(jnp.int32, (CHUNK, 2 * SSD_P), 0)
    col2 = jnp.bitwise_and(lax.broadcasted_iota(jnp.int32, (CHUNK, 2 * SSD_P), 1), SSD_P - 1)
    causal2 = (col2 >= row2) if reverse else (col2 <= row2)
    low = lax.broadcasted_iota(jnp.int32, (1, 2 * SSD_P), 1) < SSD_P
    a_neg = -jnp.exp(alog_ref[...])
    nch = ROWS // CHUNK
    hg = SSD_H // SSD_G
    pw = 2 * SSD_P

    low_c = lax.broadcasted_iota(jnp.int32, (CHUNK, 2 * SSD_P), 1) < SSD_P

    def pair(v, h0):
        if v.shape == (CHUNK, 2 * SSD_P):
            return jnp.take_along_axis(v, jnp.where(low_c, h0, h0 + 1), axis=1)
        return jnp.where(low, v[:, h0:h0 + 1], v[:, h0 + 1:h0 + 2])

    dtv = _softplus(dt_ref[:, d * SSD_H:(d + 1) * SSD_H] + dtb_ref[...])
    la = dtv * a_neg
    for c in range(nch):
        rs = slice(c * CHUNK, (c + 1) * CHUNK)
        cum = _cumsum_rows(tri_b, la[rs])
        last = cum[0:1, :] if reverse else cum[CHUNK - 1:CHUNK, :]
        w2 = dtv[rs] * jnp.exp(last - cum)
        cum_s[c] = jnp.concatenate([cum, cum], axis=1)
        w2_s[c] = jnp.concatenate([w2, w2], axis=1)
        cum_t = cum.T
        dt_t = dtv[rs].T
        ct2_s[c] = jnp.concatenate([cum_t, cum_t], axis=1)
        dt2_s[c] = jnp.concatenate([dt_t, dt_t], axis=1)

    def chunk(cc, carry):
        c = (nch - 1 - cc) if reverse else cc
        rs = pl.ds(pl.multiple_of(c * CHUNK, CHUNK), CHUNK)
        cum = cum_s[c]
        w2 = w2_s[c]
        ct2 = ct2_s[c]
        dt2 = dt2_s[c]
        dec = jnp.exp(cum[0:1, :] if reverse else cum[CHUNK - 1:CHUNK, :])
        cbs, yis = [], []
        for g in range(SSD_G):
            gs = slice(g * SSD_Z, (g + 1) * SSD_Z)
            cg = c_ref[rs, gs].astype(BF16)
            cbs.append(_dot_nt(cg, b_ref[rs, gs].astype(BF16)))
            yis.append(_dot(cg, st_ref[g].astype(BF16)))
        for g in range(SSD_G):
            gs = slice(g * SSD_Z, (g + 1) * SSD_Z)
            bgb = b_ref[rs, gs].astype(BF16)
            cb2 = jnp.concatenate([cbs[g], cbs[g]], axis=1)
            st = st_ref[g]
            y_inter = yis[g]
            x2s, decs = [], []
            for pr in range(hg // 2):
                h0 = g * hg + 2 * pr
                ps = slice(h0 * SSD_P, (h0 + 2) * SSD_P)
                xp = xs_ref[rs, ps]
                ccol = pair(cum, h0)
                rrow = jnp.where(low, ct2[h0:h0 + 1, :], ct2[h0 + 1:h0 + 2, :])
                drow = jnp.where(low, dt2[h0:h0 + 1, :], dt2[h0 + 1:h0 + 2, :])
                seg = jnp.exp(jnp.where(causal2, ccol - rrow, -jnp.inf))
                m = (cb2 * seg * drow).astype(BF16)
                xbd = jnp.concatenate([jnp.where(low, xp, 0.0), jnp.where(low, 0.0, xp)], axis=0).astype(BF16)
                y = _dot(m, xbd) + y_inter[:, pr * pw:(pr + 1) * pw] * jnp.exp(ccol)
                if reverse:
                    ybuf[:, ps] = y + acc_ref[rs, ps]
                else:
                    y_ref[rs, ps] = y
                x2s.append((xp * pair(w2, h0)).astype(BF16))
                decs.append(pair(dec, h0))
            upd = _dot_tn(bgb, jnp.concatenate(x2s, axis=1))
            st_ref[g] = st * jnp.concatenate(decs, axis=1) + upd
        if reverse:
            u = (ybuf[...] + dsk_ref[...] * xs_ref[rs, :]) * _silu(z_ref[rs, :])
            ms = jnp.mean(u * u, axis=-1, keepdims=True)
            y_ref[rs, :] = (u * lax.rsqrt(ms + EPS) * gn_ref[...]).astype(BF16)
        return carry

    lax.fori_loop(0, nch, chunk, 0)


def _ssd(xbc, pdt, dt_bias, a_log, fin=None):
    reverse = fin is not None
    nblk = N_ALL // ROWS
    nctx = T_CTX // ROWS
    blk = (lambda i: _rev_block(i, nblk)) if reverse else (lambda i: i)
    d = 1 if reverse else 0
    gz = SSD_G * SSD_Z
    full = lambda i: (blk(i), 0)
    in_specs = [
        pl.BlockSpec((ROWS, SSD_INNER), full),
        pl.BlockSpec((ROWS, gz), lambda i: (blk(i), SSD_INNER // gz)),
        pl.BlockSpec((ROWS, gz), lambda i: (blk(i), SSD_INNER // gz + 1)),
        pl.BlockSpec((ROWS, LANES), full),
        pl.BlockSpec((None, 1, SSD_H), lambda i: (d, 0, 0)),
        pl.BlockSpec((None, 1, SSD_H), lambda i: (d, 0, 0)),
    ]
    args = [xbc, xbc, xbc, pdt, dt_bias.reshape(2, 1, SSD_H), a_log.reshape(2, 1, SSD_H)]
    nch = ROWS // CHUNK
    scratch = [pltpu.VMEM((SSD_G, SSD_Z, SSD_INNER // SSD_G), F32),
               pltpu.VMEM((nch, CHUNK, 2 * SSD_H), F32), pltpu.VMEM((nch, CHUNK, 2 * SSD_H), F32),
               pltpu.VMEM((nch, SSD_H, 2 * CHUNK), F32), pltpu.VMEM((nch, SSD_H, 2 * CHUNK), F32)]
    if reverse:
        yf, p1, d_rep, norm_g = fin
        in_specs += [pl.BlockSpec((ROWS, SSD_INNER), full), pl.BlockSpec((ROWS, SSD_INNER), full),
                     pl.BlockSpec((1, SSD_INNER), lambda i: (0, 0)), pl.BlockSpec((1, SSD_INNER), lambda i: (0, 0))]
        args += [yf, p1, d_rep, norm_g]
        scratch.append(pltpu.VMEM((CHUNK, SSD_INNER), F32))
        out_spec = pl.BlockSpec((ROWS, SSD_INNER), lambda i: (blk(jnp.maximum(i, nctx)) - nctx, 0))
        out_shape = jax.ShapeDtypeStruct((T_LAT, SSD_INNER), BF16)
    else:
        out_spec = pl.BlockSpec((ROWS, SSD_INNER), full)
        out_shape = jax.ShapeDtypeStruct((N_ALL, SSD_INNER), F32)
    return pl.pallas_call(
        functools.partial(_ssd_kernel, reverse=reverse),
        grid=(nblk,),
        in_specs=in_specs,
        out_specs=out_spec,
        out_shape=out_shape,
        scratch_shapes=scratch,
        compiler_params=_params("arbitrary"),
        name="ssd_bwd" if reverse else "ssd_fwd",
    )(*args)


def _gather_wait(h_hbm, buf, sem, slot):
    pltpu.make_async_copy(h_hbm.at[pl.ds(0, MOE_TM), :], buf.at[slot], sem.at[slot]).wait()


GATHER_SEG = 32


def _gather_issue(idx_ref, n_valid, h_hbm, buf, sem, slot):
    def body(r8, carry):
        base = pl.multiple_of(r8 * SUBLANES, SUBLANES)
        for u in range(SUBLANES):
            tok = idx_ref[0, 0, base + u]
            pltpu.make_async_copy(h_hbm.at[pl.ds(tok, 1), :], buf.at[slot, pl.ds(base + u, 1), :],
                                  sem.at[slot]).start()
        return carry

    per = GATHER_SEG // SUBLANES
    for s in range(MOE_TM // GATHER_SEG):
        @pl.when(s * GATHER_SEG < n_valid)
        def _():
            lax.fori_loop(s * per, (s + 1) * per, body, 0)

        @pl.when(s * GATHER_SEG >= n_valid)
        def _():
            pltpu.make_async_copy(h_hbm.at[pl.ds(0, GATHER_SEG), :],
                                  buf.at[slot, pl.ds(s * GATHER_SEG, GATHER_SEG), :], sem.at[slot]).start()


def _expert_block_kernel(be_ref, ws_ref, nx_ref, nv_ref, nu_ref, *refs, n_w, layer, gather, compute):
    if gather:
        icur_ref, inxt_ref, h_hbm = refs[:3]
        refs = refs[3:]
    else:
        x_ref = refs[0]
        refs = refs[1:]
    w_hbm, o_ref = refs[:n_w], refs[n_w]
    scr = refs[n_w + 1:]
    w32, wsem = scr[:n_w], scr[n_w]
    m = pl.program_id(0)
    nu = nu_ref[0]
    e = be_ref[m]
    wslot = ws_ref[m]
    first = jnp.logical_and(m < nu, jnp.logical_or(m == 0, e != be_ref[jnp.maximum(m - 1, 0)]))

    def w_copy(k, ee, slot):
        return pltpu.make_async_copy(w_hbm[k].at[layer, ee], w32[k].at[slot], wsem.at[k, slot])

    @pl.when(m == 0)
    def _():
        for k in range(n_w):
            w_copy(k, e, wslot).start()

    if gather:
        gbuf, gsem = scr[n_w + 1:]
        gslot = lax.rem(m, 2)

        @pl.when(m == 0)
        def _():
            _gather_issue(icur_ref, nv_ref[0], h_hbm, gbuf, gsem, 0)

        @pl.when(m + 1 < nu)
        def _():
            _gather_issue(inxt_ref, nv_ref[m + 1], h_hbm, gbuf, gsem, 1 - gslot)

    @pl.when(jnp.logical_and(first, nx_ref[m] >= 0))
    def _():
        for k in range(n_w):
            w_copy(k, nx_ref[m], 1 - wslot).start()

    @pl.when(first)
    def _():
        for k in range(n_w):
            w_copy(k, e, wslot).wait()

    @pl.when(m < nu)
    def _():
        if gather:
            _gather_wait(h_hbm, gbuf, gsem, gslot)
            x = gbuf[gslot]
        else:
            x = x_ref[...].astype(F32)
        o_ref[...] = compute(x, *[w32[k][wslot] for k in range(n_w)]).astype(o_ref.dtype)


def _expert_stage(src, ws, plan, layer, n_out, out_dtype, compute, name, slot_tok=None):
    nblk = plan[0].shape[0]
    gather = slot_tok is not None
    anyspec = pl.BlockSpec(memory_space=pl.ANY)
    used = lambda m, p: jnp.minimum(m, p[-1][0] - 1)
    n_w = len(ws)
    wshape = ws[0].shape[2:]
    scratch = [pltpu.VMEM((2,) + wshape, F32) for _ in ws]
    scratch.append(pltpu.SemaphoreType.DMA((n_w, 2)))
    if gather:
        in_specs = [
            pl.BlockSpec((1, 1, MOE_TM), lambda m, *p: (used(m, p), 0, 0), memory_space=pltpu.SMEM),
            pl.BlockSpec((1, 1, MOE_TM), lambda m, *p: (used(m + 1, p), 0, 0), memory_space=pltpu.SMEM),
            anyspec,
        ]
        args = [slot_tok, slot_tok, src]
        scratch += [pltpu.VMEM((2, MOE_TM, src.shape[1]), src.dtype), pltpu.SemaphoreType.DMA((2,))]
    else:
        in_specs = [pl.BlockSpec((MOE_TM, src.shape[1]), lambda m, *p: (used(m, p), 0))]
        args = [src]
    return pl.pallas_call(
        functools.partial(_expert_block_kernel, n_w=n_w, layer=layer, gather=gather, compute=compute),
        grid_spec=pltpu.PrefetchScalarGridSpec(
            num_scalar_prefetch=len(plan),
            grid=(nblk,),
            in_specs=in_specs + [anyspec] * n_w,
            out_specs=pl.BlockSpec((MOE_TM, n_out), lambda m, *p: (used(m, p), 0)),
            scratch_shapes=scratch,
        ),
        out_shape=jax.ShapeDtypeStruct((nblk * MOE_TM, n_out), out_dtype),
        compiler_params=_params("arbitrary"),
        name=name,
    )(*plan, *args, *ws)


def _moe(h, eidx, wts, w_gate, w_up, w_down, layer):
    n = h.shape[0]
    i32 = jnp.int32
    flat_e = eidx.reshape(-1)
    experts = jnp.arange(N_EXP, dtype=i32)
    onehot = (flat_e[:, None] == experts[None, :]).astype(i32)
    csum = jnp.cumsum(onehot, axis=0)
    bcount = (csum[-1] + MOE_TM - 1) // MOE_TM
    bend = jnp.cumsum(bcount)
    bstart = bend - bcount
    dest = jnp.sum(onehot * (csum - 1 + (bstart * MOE_TM)[None, :]), axis=1)
    nblk = -(-2 * n // MOE_TM) + N_EXP
    n_used = bend[-1].astype(i32)
    blk = jnp.minimum(jnp.arange(nblk, dtype=i32), n_used - 1)
    block_exp = jnp.sum((blk[:, None] >= bend[None, :]).astype(i32), axis=1)
    owns = bcount > 0
    wslot = (jnp.cumsum(owns.astype(i32)) - 1) % 2
    later = jnp.logical_and(owns[None, :], experts[None, :] > experts[:, None])
    nxt = jnp.min(jnp.where(later, experts[None, :], N_EXP), axis=1)
    nxt = jnp.where(nxt == N_EXP, -1, nxt)
    of_blk = (block_exp[:, None] == experts[None, :]).astype(i32)
    rows_left = jnp.sum(of_blk * (csum[-1] + bstart * MOE_TM)[None, :], axis=1) - blk * MOE_TM
    n_valid = jnp.clip(rows_left, 0, MOE_TM)
    plan = (block_exp.astype(i32), jnp.sum(of_blk * wslot[None, :], axis=1).astype(i32),
            jnp.sum(of_blk * nxt[None, :], axis=1).astype(i32), n_valid.astype(i32), n_used.reshape(1))
    tok = jnp.tile(jnp.arange(n, dtype=i32), 2)
    pad_tok = jnp.arange(nblk * MOE_TM, dtype=i32) % n
    slot_tok = pad_tok.at[dest].set(tok).reshape(nblk, 1, MOE_TM)
    hb = _expert_stage(h, [w_gate, w_up], plan, layer, D_EXP, BF16,
                       lambda x, wg, wu: _silu(_dot(x, wg)) * _dot(x, wu), "moe_up", slot_tok=slot_tok)
    yb = _expert_stage(hb, [w_down], plan, layer, D, F32, lambda x, wd: _dot(x, wd), "moe_down")
    return yb, dest.reshape(2, n), wts.T


def _pos_tables():
    quarter = D // 4
    omega = 1.0 / (10000.0 ** (jnp.arange(quarter, dtype=F32) / quarter))
    ang_r = jnp.arange(T_LAT // GRID_W, dtype=F32)[:, None] * omega
    ang_c = jnp.arange(GRID_W, dtype=F32)[:, None] * omega
    emb_r = jnp.concatenate([jnp.sin(ang_r), jnp.cos(ang_r)], axis=-1)
    emb_c = jnp.concatenate([jnp.sin(ang_c), jnp.cos(ang_c)], axis=-1)
    return emb_r, emb_c


def _block_diag_gates(wa, wx, group):
    nb = wa.shape[1]
    per = group // wa.shape[2]
    eye = jnp.eye(per, dtype=F32)

    def bd(w):
        w = w.reshape(nb // per, per, w.shape[1], w.shape[2])
        return jnp.einsum("gnkj,nm->gnkmj", w, eye).reshape(nb // per, group, group)

    return jnp.concatenate([bd(wa[0]), bd(wa[1]), bd(wx[0]), bd(wx[1])], axis=-1).astype(BF16)


def kernel(x, c, ctx, c_ctx, mod_w, mod_b, norm1_g, norm2_g, ev_w_in, ev_conv_w, ev_conv_b, lru_wa, lru_ba, lru_wx, lru_bx, lru_lambda, gla_wg_up, gla_bg, gla_norm_g, ev_w_out, od_w_in, od_conv_w, od_conv_b, ssd_a_log, ssd_dt_bias, ssd_d, ssd_norm_g, od_w_out, router_w, router_b, exp_w_gate, exp_w_up, exp_w_down, final_norm_g):
    mods = _mod_vectors(c, c_ctx, mod_w, mod_b)
    emb_r, emb_c = _pos_tables()
    rwt = router_w.T
    rb = router_b.reshape(N_EXP, 1)
    tm, tn = PROJ_TM, PROJ_TN

    x0, h0 = _prep0(x[0], ctx[0], emb_r, emb_c, mods[0], norm1_g[0:1])
    w_in_t = jnp.swapaxes(ev_w_in[0], 0, 1)
    p0 = _mm([h0], w_in_t, ncols=EVEN_MAIN, tm=tm, tn=tn, w_t=True)
    pad = _mm([h0], w_in_t, ncols=LANES, tm=tm, tn=LANES, col_off=EVEN_MAIN // LANES, w_t=True)
    wbd = _block_diag_gates(lru_wa[0], lru_wx[0], LRU_GATE_GROUP)
    ya = _lru(p0, ev_conv_w[0], ev_conv_b[0:1], wbd, lru_ba[0], lru_bx[0], lru_lambda[0])
    of = _gla(p0, pad, gla_wg_up[0], gla_bg[0], None)
    yb = _gla(p0, pad, gla_wg_up[0], gla_bg[0], gla_norm_g[0:1], of=of)
    y0 = _mm([ya, yb], ev_w_out[0], ncols=D, tm=tm, tn=tn)
    nctx = T_CTX // ROWS
    x1, h1, e0, w0 = _token_stage(x0, y0, mods[0], mods[0], norm2_g[0:1], rwt, rb, gate_col=2, shift_col=3,
                                  scale_col=4, ctx_blocks=nctx, route=True)
    f0, dest0, wc0 = _moe(h1, e0, w0, exp_w_gate, exp_w_up, exp_w_down, 0)
    x2, h2 = _token_stage(x1, f0, mods[0], mods[1], norm1_g[1:2], rwt, rb, gate_col=5, shift_col=0,
                          scale_col=1, ctx_blocks=nctx, combine=(dest0, wc0))

    p1 = _mm([h2], od_w_in[0], ncols=ODD_MAIN, tm=tm, tn=tn)
    pdt = _mm([h2], od_w_in[0], ncols=LANES, tm=tm, tn=LANES, col_off=ODD_MAIN // LANES)
    xbc = _conv_silu(p1, od_conv_w[0], od_conv_b[0:1])
    yf = _ssd(xbc, pdt, ssd_dt_bias[0], ssd_a_log[0])
    d_rep = jnp.repeat(ssd_d[0], SSD_P).reshape(1, SSD_INNER)
    gy = _ssd(xbc, pdt, ssd_dt_bias[0], ssd_a_log[0], fin=(yf, p1, d_rep, ssd_norm_g[0:1]))
    y1 = _mm([gy], od_w_out[0], ncols=D, tm=OUT1_TM, tn=OUT1_TN)
    x3, h3, e1, w1 = _token_stage(x2, y1, mods[1], mods[1], norm2_g[1:2], rwt, rb, gate_col=2, shift_col=3,
                                  scale_col=4, x_off=nctx, route=True)
    f1, dest1, wc1 = _moe(h3, e1, w1, exp_w_gate, exp_w_up, exp_w_down, 1)
    (out,) = _token_stage(x3, f1, mods[1], mods[1], final_norm_g.reshape(1, D), rwt, rb, gate_col=5,
                          combine=(dest1, wc1), final=True)
    return out[None]
```

```python
import functools
import math

import jax
import jax.numpy as jnp
from jax import lax
from jax.experimental import pallas as pl
from jax.experimental.pallas import tpu as pltpu

F32 = jnp.float32
BF16 = jnp.bfloat16
HIGHEST = lax.Precision.HIGHEST

D = 2048
T_LAT = 8192
T_CTX = 256
N_ALL = T_CTX + T_LAT
GRID_W = 64
EPS = 1e-6
CONV_W = 4
LRU_W = 1024
LRU_C = 8.0
GLA_H = 8
GLA_DK = 64
GLA_DV = 128
GLA_QK = GLA_H * GLA_DK
GLA_V = GLA_H * GLA_DV
GLA_RANK = 16
GLA_TAU = 16.0
CHUNK = 64
EVEN_MAIN = 2 * LRU_W + 2 * GLA_QK + 2 * GLA_V
SSD_INNER = 2 * D
SSD_P = 64
SSD_H = SSD_INNER // SSD_P
SSD_G = 8
SSD_Z = 128
SSD_XBC = SSD_INNER + 2 * SSD_G * SSD_Z
ODD_MAIN = SSD_INNER + SSD_XBC
N_EXP = 32
N_GRP = 4
GRP = N_EXP // N_GRP
D_EXP = 1024
MOE_TM = 256

ROWS = 256
LANES = 128
SUBLANES = 8
VMEM_LIMIT = 56 * 1024 * 1024
PROJ_TM = N_ALL // 6
PROJ_TN = 1024
OUT1_TM = 1024
OUT1_TN = 512
LRU_GATE_GROUP = 256


def _params(*sem):
    return pltpu.CompilerParams(dimension_semantics=sem, vmem_limit_bytes=VMEM_LIMIT)


def _silu(v):
    return v * jax.nn.sigmoid(v)


def _softplus(v):
    return jnp.maximum(v, 0.0) + jnp.log1p(jnp.exp(-jnp.abs(v)))


def _dot(a, b):
    return jnp.dot(a, b, preferred_element_type=F32)


def _dot_nt(a, b):
    return lax.dot_general(a, b, (((1,), (1,)), ((), ())), preferred_element_type=F32)


def _dot_tn(a, b):
    return lax.dot_general(a, b, (((0,), (0,)), ((), ())), preferred_element_type=F32)


def _mod_kernel(s_ref, w_ref, b_ref, o_ref):
    tn = w_ref.shape[1]
    nrep = tn // LANES

    def body(r, acc):
        a0, a1 = acc
        rows = pl.ds(pl.multiple_of(r * SUBLANES, SUBLANES), SUBLANES)
        w = w_ref[rows, :]
        s0 = _silu(s_ref[0, rows, :])
        s1 = _silu(s_ref[1, rows, :])
        a0 = a0 + w * jnp.concatenate([s0] * nrep, axis=1)
        a1 = a1 + w * jnp.concatenate([s1] * nrep, axis=1)
        return a0, a1

    zero = jnp.zeros((SUBLANES, tn), F32)
    a0, a1 = lax.fori_loop(0, D // SUBLANES, body, (zero, zero), unroll=4)
    o_ref[0:1, :] = jnp.sum(a0, axis=0, keepdims=True) + b_ref[...]
    o_ref[1:2, :] = jnp.sum(a1, axis=0, keepdims=True) + b_ref[...]


def _mod_vectors(c, c_ctx, mod_w, mod_b):
    depth = mod_w.shape[0]
    tn = 2048
    s = jnp.broadcast_to(jnp.stack([c[0], c_ctx])[:, :, None], (2, D, LANES))
    return pl.pallas_call(
        _mod_kernel,
        grid=(depth, 6 * D // tn),
        in_specs=[
            pl.BlockSpec((2, D, LANES), lambda l, j: (0, 0, 0)),
            pl.BlockSpec((None, D, tn), lambda l, j: (l, 0, j)),
            pl.BlockSpec((None, 1, tn), lambda l, j: (l, 0, j)),
        ],
        out_specs=pl.BlockSpec((None, 2, tn), lambda l, j: (l, 0, j)),
        out_shape=jax.ShapeDtypeStruct((depth, 2, 6 * D), F32),
        compiler_params=_params("arbitrary", "arbitrary"),
        name="mod_vectors",
    )(s, mod_w, mod_b.reshape(depth, 1, 6 * D))


def _mod_row(mod_ref, kind, col):
    return mod_ref[pl.ds(kind, 1), col * D:(col + 1) * D]


def _ada_norm(xv, g, shift, scale):
    ms = jnp.mean(xv * xv, axis=-1, keepdims=True)
    return (xv * lax.rsqrt(ms + EPS) * g) * (1.0 + scale) + shift


def _prep0_kernel(x_ref, ctx_ref, er_ref, ec_ref, mod_ref, g_ref, xo_ref, ho_ref):
    i = pl.program_id(0)

    @pl.when(i == 0)
    def _():
        xo_ref[...] = ctx_ref[...]

    @pl.when(i > 0)
    def _():
        r0 = (i - 1) * (ROWS // GRID_W)
        for j in range(ROWS // GRID_W):
            rs = slice(j * GRID_W, (j + 1) * GRID_W)
            xo_ref[rs, 0:D // 2] = x_ref[rs, 0:D // 2] + er_ref[pl.ds(r0 + j, 1), :]
            xo_ref[rs, D // 2:D] = x_ref[rs, D // 2:D] + ec_ref[...]

    kind = jnp.where(i == 0, 1, 0)
    h = _ada_norm(xo_ref[...], g_ref[...], _mod_row(mod_ref, kind, 0), _mod_row(mod_ref, kind, 1))
    ho_ref[...] = h.astype(BF16)


def _prep0(x, ctx, emb_r, emb_c, mod0, g):
    nblk = N_ALL // ROWS
    return pl.pallas_call(
        _prep0_kernel,
        grid=(nblk,),
        in_specs=[
            pl.BlockSpec((ROWS, D), lambda i: (jnp.maximum(i - 1, 0), 0)),
            pl.BlockSpec((ROWS, D), lambda i: (0, 0)),
            pl.BlockSpec(emb_r.shape, lambda i: (0, 0)),
            pl.BlockSpec(emb_c.shape, lambda i: (0, 0)),
            pl.BlockSpec((2, 6 * D), lambda i: (0, 0)),
            pl.BlockSpec((1, D), lambda i: (0, 0)),
        ],
        out_specs=[pl.BlockSpec((ROWS, D), lambda i: (i, 0)), pl.BlockSpec((ROWS, D), lambda i: (i, 0))],
        out_shape=[jax.ShapeDtypeStruct((N_ALL, D), F32), jax.ShapeDtypeStruct((N_ALL, D), BF16)],
        compiler_params=_params("arbitrary"),
        name="embed_norm",
    )(x, ctx, emb_r, emb_c, mod0, g)


def _route(hf32, rwt_ref, rb_ref, eo_ref, wo_ref):
    logits = lax.dot_general(rwt_ref[...], hf32, (((1,), (1,)), ((), ())),
                             precision=HIGHEST, preferred_element_type=F32)
    s = jax.nn.sigmoid(logits)
    sel = s + rb_ref[...]
    row = lax.broadcasted_iota(jnp.int32, (GRP, ROWS), 0)
    neg = jnp.float32(-jnp.inf)
    gs, i1s, i2s = [], [], []
    for g in range(N_GRP):
        blk = sel[g * GRP:(g + 1) * GRP, :]
        m1 = jnp.max(blk, axis=0, keepdims=True)
        i1 = jnp.min(jnp.where(blk == m1, row, GRP), axis=0, keepdims=True)
        blk2 = jnp.where(row == i1, neg, blk)
        m2 = jnp.max(blk2, axis=0, keepdims=True)
        i2 = jnp.min(jnp.where(blk2 == m2, row, GRP), axis=0, keepdims=True)
        gs.append(m1 + m2)
        i1s.append(i1)
        i2s.append(i2)
    best, gi, i1, i2 = gs[0], jnp.zeros((1, ROWS), jnp.int32), i1s[0], i2s[0]
    for g in range(1, N_GRP):
        upd = gs[g] > best
        best = jnp.where(upd, gs[g], best)
        gi = jnp.where(upd, g, gi)
        i1 = jnp.where(upd, i1s[g], i1)
        i2 = jnp.where(upd, i2s[g], i2)
    e1 = gi * GRP + i1
    e2 = gi * GRP + i2
    erow = lax.broadcasted_iota(jnp.int32, (N_EXP, ROWS), 0)
    s1 = jnp.sum(jnp.where(erow == e1, s, 0.0), axis=0, keepdims=True)
    s2 = jnp.sum(jnp.where(erow == e2, s, 0.0), axis=0, keepdims=True)
    tot = s1 + s2
    eo_ref[0:1, :] = e1
    eo_ref[1:2, :] = e2
    wo_ref[0:1, :] = s1 / tot
    wo_ref[1:2, :] = s2 / tot


def _combine_wait(yb_hbm, ybuf, sem, slot):
    pltpu.make_async_copy(yb_hbm.at[pl.ds(0, 2 * ROWS), :], ybuf.at[slot], sem.at[slot]).wait()


def _combine_issue(dest_ref, yb_hbm, ybuf, sem, slot):
    def body(r, carry):
        for k in range(2):
            row = dest_ref[0, k, r]
            pltpu.make_async_copy(yb_hbm.at[pl.ds(row, 1), :], ybuf.at[slot, pl.ds(k * ROWS + r, 1), :],
                                  sem.at[slot]).start()
        return carry

    lax.fori_loop(0, ROWS, body, 0, unroll=8)


def _token_kernel(*refs, gate_col, shift_col, scale_col, ctx_blocks, route, combine, final):
    refs = list(refs)
    if combine:
        dcur_ref, dnxt_ref = refs.pop(0), refs.pop(0)
    x_ref, y_ref = refs.pop(0), refs.pop(0)
    if combine:
        wc_ref = refs.pop(0)
    modg_ref, modn_ref, g_ref, rwt_ref, rb_ref = (refs.pop(0) for _ in range(5))
    i = pl.program_id(0)
    if combine:
        ybuf, sem = refs[-2], refs[-1]
        slot = lax.rem(i, 2)

        @pl.when(i == 0)
        def _():
            _combine_issue(dcur_ref, y_ref, ybuf, sem, 0)

        @pl.when(i + 1 < pl.num_programs(0))
        def _():
            _combine_issue(dnxt_ref, y_ref, ybuf, sem, 1 - slot)

        _combine_wait(y_ref, ybuf, sem, slot)
        y = wc_ref[:, 0:1] * ybuf[slot, 0:ROWS, :] + wc_ref[:, 1:2] * ybuf[slot, ROWS:2 * ROWS, :]
    else:
        y = y_ref[...]
    kind = jnp.where(i < ctx_blocks, 1, 0)
    xn = x_ref[...] + _mod_row(modg_ref, kind, gate_col) * y
    if final:
        ms = jnp.mean(xn * xn, axis=-1, keepdims=True)
        refs[0][...] = xn * lax.rsqrt(ms + EPS) * g_ref[...]
        return
    xo_ref, ho_ref = refs[0], refs[1]
    xo_ref[...] = xn
    h = _ada_norm(xn, g_ref[...], _mod_row(modn_ref, kind, shift_col), _mod_row(modn_ref, kind, scale_col))
    ho_ref[...] = h.astype(ho_ref.dtype)
    if route:
        _route(h, rwt_ref, rb_ref, refs[2], refs[3])


def _token_stage(x, y, modg, modn, g, rwt, rb, *, gate_col, shift_col=0, scale_col=0, x_off=0, ctx_blocks=0,
                 route=False, combine=None, final=False):
    n = x.shape[0] - x_off * ROWS
    nblk = n // ROWS
    row = lambda i: (i, 0)
    const2 = lambda i: (0, 0)
    in_specs, args, scratch = [], [], []
    if combine is not None:
        dest, wc = combine
        dest3 = dest.reshape(2, nblk, ROWS).transpose(1, 0, 2)
        in_specs += [pl.BlockSpec((1, 2, ROWS), lambda i: (i, 0, 0), memory_space=pltpu.SMEM),
                     pl.BlockSpec((1, 2, ROWS), lambda i: (jnp.minimum(i + 1, nblk - 1), 0, 0),
                                  memory_space=pltpu.SMEM)]
        args += [dest3, dest3]
    in_specs.append(pl.BlockSpec((ROWS, D), lambda i: (i + x_off, 0)))
    args.append(x)
    if combine is not None:
        in_specs += [pl.BlockSpec(memory_space=pl.ANY), pl.BlockSpec((ROWS, 2), row)]
        args += [y, wc]
        scratch = [pltpu.VMEM((2, 2 * ROWS, D), F32), pltpu.SemaphoreType.DMA((2,))]
    else:
        in_specs.append(pl.BlockSpec((ROWS, D), row))
        args.append(y)
    in_specs += [pl.BlockSpec((2, 6 * D), const2), pl.BlockSpec((2, 6 * D), const2), pl.BlockSpec((1, D), const2),
                 pl.BlockSpec((N_EXP, D), const2), pl.BlockSpec((N_EXP, 1), const2)]
    args += [modg, modn, g, rwt, rb]
    if final:
        out_specs = [pl.BlockSpec((ROWS, D), row)]
        out_shape = [jax.ShapeDtypeStruct((n, D), F32)]
    else:
        out_specs = [pl.BlockSpec((ROWS, D), row), pl.BlockSpec((ROWS, D), row)]
        out_shape = [jax.ShapeDtypeStruct((n, D), F32), jax.ShapeDtypeStruct((n, D), F32 if route else BF16)]
        if route:
            out_specs += [pl.BlockSpec((2, ROWS), lambda i: (0, i)), pl.BlockSpec((2, ROWS), lambda i: (0, i))]
            out_shape += [jax.ShapeDtypeStruct((2, n), jnp.int32), jax.ShapeDtypeStruct((2, n), F32)]
    kern = functools.partial(_token_kernel, gate_col=gate_col, shift_col=shift_col, scale_col=scale_col,
                             ctx_blocks=ctx_blocks, route=route, combine=combine is not None, final=final)
    return pl.pallas_call(
        kern,
        grid=(nblk,),
        in_specs=in_specs,
        out_specs=out_specs,
        out_shape=out_shape,
        scratch_shapes=scratch,
        compiler_params=_params("arbitrary"),
        name="token_stage",
    )(*args)


def _mm_kernel(*refs, nx, valid, w_t):
    x_refs, w_refs, o_ref, wb_refs = refs[:nx], refs[nx:2 * nx], refs[2 * nx], refs[2 * nx + 1:]
    out_axis = 0 if w_t else 1

    @pl.when(pl.program_id(1) == 0)
    def _():
        for w_ref, wb_ref in zip(w_refs, wb_refs):
            w = w_ref[...]
            if valid < w.shape[out_axis]:
                w = jnp.where(lax.broadcasted_iota(jnp.int32, w.shape, out_axis) < valid, w, 0.0)
            wb_ref[...] = w.astype(BF16)

    mul = _dot_nt if w_t else _dot
    acc = mul(x_refs[0][...], wb_refs[0][...])
    for x_ref, wb_ref in zip(x_refs[1:], wb_refs[1:]):
        acc = acc + mul(x_ref[...], wb_ref[...])
    o_ref[...] = acc


def _mm(xs, w, *, ncols, tm, tn, col_off=0, w_t=False):
    nx = len(xs)
    k = xs[0].shape[1]
    m = xs[0].shape[0]
    n_total = w.shape[0] if w_t else w.shape[1]
    valid = min(tn, n_total - col_off * tn) if ncols == tn else tn
    in_specs = [pl.BlockSpec((tm, k), lambda j, i: (i, 0)) for _ in xs]
    if w_t:
        in_specs += [pl.BlockSpec((tn, k), functools.partial(lambda j, i, kk: (j + col_off, kk), kk=kk))
                     for kk in range(nx)]
    else:
        in_specs += [pl.BlockSpec((k, tn), functools.partial(lambda j, i, kk: (kk, j + col_off), kk=kk))
                     for kk in range(nx)]
    return pl.pallas_call(
        functools.partial(_mm_kernel, nx=nx, valid=valid, w_t=w_t),
        grid=(ncols // tn, m // tm),
        in_specs=in_specs,
        out_specs=pl.BlockSpec((tm, tn), lambda j, i: (i, j)),
        out_shape=jax.ShapeDtypeStruct((m, ncols), F32),
        scratch_shapes=[pltpu.VMEM((tn, k) if w_t else (k, tn), BF16) for _ in xs],
        compiler_params=_params("arbitrary", "arbitrary"),
        name="proj",
    )(*xs, *([w] * nx))


def _conv_block(cur, prev8, next8, w_ref, b_ref, i, nblk):
    ctx_edge = T_CTX // ROWS
    keep_prev = jnp.logical_and(i != 0, i != ctx_edge)
    keep_next = jnp.logical_and(i != nblk - 1, i != ctx_edge - 1)
    prev8 = jnp.where(keep_prev, prev8, 0.0)
    next8 = jnp.where(keep_next, next8, 0.0)
    ext = jnp.concatenate([prev8, cur, next8], axis=0)
    n_ext = ROWS + 2 * SUBLANES
    y = b_ref[...] + w_ref[CONV_W // 2:CONV_W // 2 + 1, :] * cur
    for j in range(CONV_W):
        off = j - CONV_W // 2
        if off != 0:
            shifted = pltpu.roll(ext, (-off) % n_ext, axis=0)[SUBLANES:SUBLANES + ROWS]
            y = y + w_ref[j:j + 1, :] * shifted
    return y


def _halo_specs(width, col_blk, nblk):
    per = ROWS // SUBLANES
    last8 = N_ALL // SUBLANES - 1
    return [
        pl.BlockSpec((ROWS, width), lambda i, *a: (i, col_blk(*a))),
        pl.BlockSpec((SUBLANES, width), lambda i, *a: (jnp.maximum(i * per - 1, 0), col_blk(*a))),
        pl.BlockSpec((SUBLANES, width), lambda i, *a: (jnp.minimum((i + 1) * per, last8), col_blk(*a))),
    ]


def _scan_tiles(a_ref, b_ref, h_ref, carry_ref, reverse):
    ntile = ROWS // SUBLANES
    row = lax.broadcasted_iota(jnp.int32, (SUBLANES, LRU_W), 0)

    def body(t, carry):
        tt = (ntile - 1 - t) if reverse else t
        rows = pl.ds(pl.multiple_of(tt * SUBLANES, SUBLANES), SUBLANES)
        a = a_ref[rows, :]
        b = b_ref[rows, :]
        for d in (1, 2, 4):
            shift = (SUBLANES - d) if reverse else d
            a_sh = pltpu.roll(a, shift, axis=0)
            b_sh = pltpu.roll(b, shift, axis=0)
            m = (row < SUBLANES - d) if reverse else (row >= d)
            b = jnp.where(m, a * b_sh, 0.0) + b
            a = jnp.where(m, a * a_sh, a)
        h = a * carry + b
        h_ref[rows, :] = h
        edge = h[0:1, :] if reverse else h[SUBLANES - 1:SUBLANES, :]
        return jnp.broadcast_to(edge, (SUBLANES, LRU_W))

    carry_ref[...] = lax.fori_loop(0, ntile, body, carry_ref[...])


def _lru_fwd_kernel(xa_ref, prev_ref, next_ref, cw_ref, cb_ref, wbd_ref, ba_ref, bx_ref, lam_ref,
                    hf_ref, a1_ref, b1_ref, carry_ref, a0_ref, b0_ref):
    i = pl.program_id(0)
    nblk = pl.num_programs(0)

    @pl.when(i == 0)
    def _():
        carry_ref[...] = jnp.zeros_like(carry_ref)

    xa = _conv_block(xa_ref[...], prev_ref[...], next_ref[...], cw_ref, cb_ref, i, nblk)
    xab = xa.astype(BF16)
    nsp = -LRU_C * _softplus(-lam_ref[...])
    gw = wbd_ref.shape[1]
    for g in range(LRU_W // gw):
        cs = slice(g * gw, (g + 1) * gw)
        z = _dot(xab[:, cs], wbd_ref[g])
        for d in range(2):
            r = jax.nn.sigmoid(z[:, d * gw:(d + 1) * gw] + ba_ref[d:d + 1, cs])
            ig = jax.nn.sigmoid(z[:, (2 + d) * gw:(3 + d) * gw] + bx_ref[d:d + 1, cs])
            log_a = r * nsp[d:d + 1, cs]
            a = jnp.exp(log_a)
            b = jnp.sqrt(-jnp.tanh(log_a) * (a * a + 1.0)) * ig * xa[:, cs]
            if d == 0:
                a0_ref[:, cs] = a
                b0_ref[:, cs] = b
            else:
                a1_ref[:, cs] = a
                b1_ref[:, cs] = b
    _scan_tiles(a0_ref, b0_ref, hf_ref, carry_ref, reverse=False)


def _gelu_tanh(v):
    return 0.5 * v * (1.0 + jnp.tanh(math.sqrt(2.0 / math.pi) * (v + 0.044715 * (v * v * v))))


def _lru_bwd_kernel(a1_ref, b1_ref, hf_ref, ga_ref, ya_ref, carry_ref, hb_ref):
    @pl.when(pl.program_id(0) == 0)
    def _():
        carry_ref[...] = jnp.zeros_like(carry_ref)

    _scan_tiles(a1_ref, b1_ref, hb_ref, carry_ref, reverse=True)
    ya_ref[...] = ((hf_ref[...] + hb_ref[...]) * _gelu_tanh(ga_ref[...])).astype(BF16)


def _rev_block(i, nblk):
    nctx = T_CTX // ROWS
    return jnp.where(i < nctx, nctx - 1 - i, nblk - 1 - (i - nctx))


def _lru(p0, conv_w, conv_b, wbd, ba, bx, lam):
    nblk = N_ALL // ROWS
    full = lambda shape: pl.BlockSpec(shape, lambda i: (0,) * len(shape))
    hf, a1, b1 = pl.pallas_call(
        _lru_fwd_kernel,
        grid=(nblk,),
        in_specs=_halo_specs(LRU_W, lambda: 0, nblk) + [
            full((CONV_W, LRU_W)), full((1, LRU_W)), full(wbd.shape),
            full((2, LRU_W)), full((2, LRU_W)), full((2, LRU_W)),
        ],
        out_specs=[pl.BlockSpec((ROWS, LRU_W), lambda i: (i, 0))] * 3,
        out_shape=[jax.ShapeDtypeStruct((N_ALL, LRU_W), F32)] * 3,
        scratch_shapes=[pltpu.VMEM((SUBLANES, LRU_W), F32), pltpu.VMEM((ROWS, LRU_W), F32),
                        pltpu.VMEM((ROWS, LRU_W), F32)],
        compiler_params=_params("arbitrary"),
        name="lru_fwd",
    )(p0, p0, p0, conv_w, conv_b, wbd, ba, bx, lam)
    rev = lambda i: (_rev_block(i, nblk), 0)
    ya = pl.pallas_call(
        _lru_bwd_kernel,
        grid=(nblk,),
        in_specs=[pl.BlockSpec((ROWS, LRU_W), rev)] * 3
        + [pl.BlockSpec((ROWS, LRU_W), lambda i: (_rev_block(i, nblk), 1))],
        out_specs=pl.BlockSpec((ROWS, LRU_W), rev),
        out_shape=jax.ShapeDtypeStruct((N_ALL, LRU_W), BF16),
        scratch_shapes=[pltpu.VMEM((SUBLANES, LRU_W), F32), pltpu.VMEM((ROWS, LRU_W), F32)],
        compiler_params=_params("arbitrary"),
        name="lru_bwd",
    )(a1, b1, hf, p0)
    return ya


def _tri(reverse):
    r = lax.broadcasted_iota(jnp.int32, (CHUNK, CHUNK), 0)
    c = lax.broadcasted_iota(jnp.int32, (CHUNK, CHUNK), 1)
    return (c >= r) if reverse else (c <= r)


def _cumsum_rows(tri_b, v):
    hi = v.astype(BF16)
    r1 = v - hi.astype(F32)
    mid = r1.astype(BF16)
    lo = (r1 - mid.astype(F32)).astype(BF16)
    return _dot(tri_b, hi) + _dot(tri_b, mid) + _dot(tri_b, lo)


def _gla_kernel(*refs, reverse):
    if reverse:
        (q_ref, k_ref, v_ref, ad_ref, wg_ref, bg_ref, of_ref, og_ref, gn_ref, o_ref,
         s_ref, qd_s, klo_s, khi_s, ke_s, dec_s, s2_s, upd_s) = refs
    else:
        (q_ref, k_ref, v_ref, ad_ref, wg_ref, bg_ref, o_ref,
         s_ref, qd_s, klo_s, khi_s, ke_s, dec_s, s2_s, upd_s) = refs
    d = 1 if reverse else 0

    @pl.when(pl.program_id(0) == 0)
    def _():
        s_ref[...] = jnp.zeros_like(s_ref)

    nch = ROWS // CHUNK
    pw = 2 * GLA_DK
    row2 = lax.broadcasted_iota(jnp.int32, (CHUNK, pw), 0)
    col2 = jnp.bitwise_and(lax.broadcasted_iota(jnp.int32, (CHUNK, pw), 1), GLA_DK - 1)
    causal2 = (col2 >= row2) if reverse else (col2 <= row2)
    eye = (lax.broadcasted_iota(jnp.int32, (pw, pw), 0) == lax.broadcasted_iota(jnp.int32, (pw, pw), 1)).astype(F32)
    zeros_v = jnp.zeros((CHUNK, GLA_DV), BF16)

    rr = lax.broadcasted_iota(jnp.int32, (ROWS, ROWS), 0)
    rc = lax.broadcasted_iota(jnp.int32, (ROWS, ROWS), 1)
    same = (rr // CHUNK) == (rc // CHUNK)
    tri_b = jnp.logical_and(same, (rc >= rr) if reverse else (rc <= rr)).astype(BF16)
    ad = ad_ref[:, d * GLA_RANK:(d + 1) * GLA_RANK]
    z = _dot(ad.astype(BF16), wg_ref[...].astype(BF16)) + bg_ref[...]
    lg = -_softplus(-z) * (1.0 / GLA_TAU)
    cum = _cumsum_rows(tri_b, lg)
    cum3 = cum.reshape(nch, CHUNK, GLA_QK)
    last3 = cum3[:, 0:1, :] if reverse else cum3[:, CHUNK - 1:CHUNK, :]
    to_end = jnp.exp(last3 - cum3).reshape(ROWS, GLA_QK)
    k = k_ref[...]
    k_inv = k * jnp.exp(-cum)
    low_all = jnp.bitwise_and(lax.broadcasted_iota(jnp.int32, (1, GLA_QK), 1), pw - 1) < GLA_DK
    qd_s[...] = (q_ref[...] * (GLA_DK ** -0.5) * jnp.exp(cum)).astype(BF16)
    klo_s[...] = jnp.where(low_all, k_inv, 0.0).astype(BF16)
    khi_s[...] = jnp.where(low_all, 0.0, k_inv).astype(BF16)
    ke_s[...] = (k * to_end).astype(BF16)
    dec_s[0:nch, :] = jnp.exp(last3.reshape(nch, GLA_QK))

    npair = GLA_H // 2
    for c in range(nch):
        rs = slice(c * CHUNK, (c + 1) * CHUNK)
        for p in range(npair):
            ls = slice(p * pw, (p + 1) * pw)
            kbd = jnp.concatenate([klo_s[rs, ls], khi_s[rs, ls]], axis=0)
            s2_s[c * npair + p] = jnp.where(causal2, _dot_nt(qd_s[rs, ls], kbd), 0.0).astype(BF16)
            vp = v_ref[rs, p * 2 * GLA_DV:(p + 1) * 2 * GLA_DV].astype(BF16)
            upd_s[c * npair + p] = _dot_tn(ke_s[rs, ls], vp)

    for cc in range(nch):
        c = (nch - 1 - cc) if reverse else cc
        rs = slice(c * CHUNK, (c + 1) * CHUNK)
        decay = dec_s[c:c + 1, :]
        for p in range(npair):
            ls = slice(p * pw, (p + 1) * pw)
            vs = slice(p * 2 * GLA_DV, (p + 1) * 2 * GLA_DV)
            qb = qd_s[rs, ls]
            s2 = s2_s[c * npair + p]
            upd = upd_s[c * npair + p]
            vp = v_ref[rs, vs].astype(BF16)
            vbd = jnp.concatenate([jnp.concatenate([vp[:, :GLA_DV], zeros_v], axis=1),
                                   jnp.concatenate([zeros_v, vp[:, GLA_DV:]], axis=1)], axis=0)
            sp = s_ref[p]
            o = _dot(jnp.concatenate([s2, qb], axis=1), jnp.concatenate([vbd, sp.astype(BF16)], axis=0))
            if reverse:
                for hh in range(2):
                    hs = slice((2 * p + hh) * GLA_DV, (2 * p + hh + 1) * GLA_DV)
                    tot = o[:, hh * GLA_DV:(hh + 1) * GLA_DV] + of_ref[rs, hs]
                    ms = jnp.mean(tot * tot, axis=-1, keepdims=True)
                    y = tot * lax.rsqrt(ms + EPS) * gn_ref[:, hs] * _silu(og_ref[rs, hs])
                    o_ref[rs, hs] = y.astype(BF16)
            else:
                o_ref[rs, vs] = o
            dcol = jnp.sum(eye * decay[:, ls], axis=1, keepdims=True)
            for hh in range(2):
                qr = slice(hh * GLA_DK, (hh + 1) * GLA_DK)
                qc = slice(hh * GLA_DV, (hh + 1) * GLA_DV)
                s_ref[p, qr, qc] = dcol[qr] * sp[qr, qc] + upd[qr, qc]


def _gla(p0, pad, wg, bg, gnorm, of=None):
    reverse = of is not None
    nblk = N_ALL // ROWS
    blk = (lambda i: _rev_block(i, nblk)) if reverse else (lambda i: i)
    d = 1 if reverse else 0
    qk_blk = 2 * LRU_W // GLA_QK
    in_specs = [
        pl.BlockSpec((ROWS, GLA_QK), lambda i: (blk(i), qk_blk)),
        pl.BlockSpec((ROWS, GLA_QK), lambda i: (blk(i), qk_blk + 1)),
        pl.BlockSpec((ROWS, GLA_V), lambda i: (blk(i), 3)),
        pl.BlockSpec((ROWS, LANES), lambda i: (blk(i), 0)),
        pl.BlockSpec((None, GLA_RANK, GLA_QK), lambda i: (d, 0, 0)),
        pl.BlockSpec((None, 1, GLA_QK), lambda i: (d, 0, 0)),
    ]
    args = [p0, p0, p0, pad, wg, bg.reshape(2, 1, GLA_QK)]
    if reverse:
        in_specs += [
            pl.BlockSpec((ROWS, GLA_V), lambda i: (blk(i), 0)),
            pl.BlockSpec((ROWS, GLA_V), lambda i: (blk(i), 4)),
            pl.BlockSpec((1, GLA_V), lambda i: (0, 0)),
        ]
        args += [of, p0, gnorm]
    return pl.pallas_call(
        functools.partial(_gla_kernel, reverse=reverse),
        grid=(nblk,),
        in_specs=in_specs,
        out_specs=pl.BlockSpec((ROWS, GLA_V), lambda i: (blk(i), 0)),
        out_shape=jax.ShapeDtypeStruct((N_ALL, GLA_V), BF16 if reverse else F32),
        scratch_shapes=[pltpu.VMEM((GLA_H // 2, 2 * GLA_DK, 2 * GLA_DV), F32)]
        + [pltpu.VMEM((ROWS, GLA_QK), BF16) for _ in range(4)] + [pltpu.VMEM((SUBLANES, GLA_QK), F32)]
        + [pltpu.VMEM((ROWS // CHUNK * GLA_H // 2, CHUNK, 2 * GLA_DK), BF16),
           pltpu.VMEM((ROWS // CHUNK * GLA_H // 2, 2 * GLA_DK, 2 * GLA_DV), F32)],
        compiler_params=_params("arbitrary"),
        name="gla_bwd" if reverse else "gla_fwd",
    )(*args)


def _conv_silu_kernel(x_ref, prev_ref, next_ref, w_ref, b_ref, o_ref):
    i = pl.program_id(0)
    y = _conv_block(x_ref[...], prev_ref[...], next_ref[...], w_ref, b_ref, i, pl.num_programs(0))
    o_ref[...] = _silu(y)


def _conv_silu(p1, conv_w, conv_b):
    nblk = N_ALL // ROWS
    tc = 2048
    off = SSD_INNER // tc
    return pl.pallas_call(
        _conv_silu_kernel,
        grid=(nblk, SSD_XBC // tc),
        in_specs=_halo_specs(tc, lambda j: j + off, nblk) + [
            pl.BlockSpec((CONV_W, tc), lambda i, j: (0, j)),
            pl.BlockSpec((1, tc), lambda i, j: (0, j)),
        ],
        out_specs=pl.BlockSpec((ROWS, tc), lambda i, j: (i, j)),
        out_shape=jax.ShapeDtypeStruct((N_ALL, SSD_XBC), F32),
        compiler_params=_params("arbitrary", "arbitrary"),
        name="ssd_conv",
    )(p1, p1, p1, conv_w, conv_b)


def _ssd_kernel(*refs, reverse):
    if reverse:
        (xs_ref, b_ref, c_ref, dt_ref, dtb_ref, alog_ref, acc_ref, z_ref, dsk_ref, gn_ref, y_ref,
         st_ref, cum_s, w2_s, ct2_s, dt2_s, ybuf) = refs
    else:
        xs_ref, b_ref, c_ref, dt_ref, dtb_ref, alog_ref, y_ref, st_ref, cum_s, w2_s, ct2_s, dt2_s = refs
    d = 1 if reverse else 0
    i = pl.program_id(0)

    @pl.when(i == 0)
    def _():
        st_ref[...] = jnp.zeros_like(st_ref)

    tri_b = _tri(reverse).astype(BF16)
    row2 = lax.broadcasted_iota(jnp.int32, (CHUNK, 2 * SSD_P), 0)
    col2 = jnp.bitwise_and(lax.broadcasted_iota(jnp.int32, (CHUNK, 2 * SSD_P), 1), SSD_P - 1)
    causal2 = (col2 >= row2) if reverse else (col2 <= row2)
    low = lax.broadcasted_iota(jnp.int32, (1, 2 * SSD_P), 1) < SSD_P
    a_neg = -jnp.exp(alog_ref[...])
    nch = ROWS // CHUNK
    hg = SSD_H // SSD_G
    pw = 2 * SSD_P

    low_c = lax.broadcasted_iota(jnp.int32, (CHUNK, 2 * SSD_P), 1) < SSD_P

    def pair(v, h0):
        if v.shape == (CHUNK, 2 * SSD_P):
            return jnp.take_along_axis(v, jnp.where(low_c, h0, h0 + 1), axis=1)
        return jnp.where(low, v[:, h0:h0 + 1], v[:, h0 + 1:h0 + 2])

    dtv = _softplus(dt_ref[:, d * SSD_H:(d + 1) * SSD_H] + dtb_ref[...])
    la = dtv * a_neg
    for c in range(nch):
        rs = slice(c * CHUNK, (c + 1) * CHUNK)
        cum = _cumsum_rows(tri_b, la[rs])
        last = cum[0:1, :] if reverse else cum[CHUNK - 1:CHUNK, :]
        w2 = dtv[rs] * jnp.exp(last - cum)
        cum_s[c] = jnp.concatenate([cum, cum], axis=1)
        w2_s[c] = jnp.concatenate([w2, w2], axis=1)
        cum_t = cum.T
        dt_t = dtv[rs].T
        ct2_s[c] = jnp.concatenate([cum_t, cum_t], axis=1)
        dt2_s[c] = jnp.concatenate([dt_t, dt_t], axis=1)

    def chunk(cc, carry):
        c = (nch - 1 - cc) if reverse else cc
        rs = pl.ds(pl.multiple_of(c * CHUNK, CHUNK), CHUNK)
        cum = cum_s[c]
        w2 = w2_s[c]
        ct2 = ct2_s[c]
        dt2 = dt2_s[c]
        dec = jnp.exp(cum[0:1, :] if reverse else cum[CHUNK - 1:CHUNK, :])
        cbs, yis = [], []
        for g in range(SSD_G):
            gs = slice(g * SSD_Z, (g + 1) * SSD_Z)
            cg = c_ref[rs, gs].astype(BF16)
            cbs.append(_dot_nt(cg, b_ref[rs, gs].astype(BF16)))
            yis.append(_dot(cg, st_ref[g].astype(BF16)))
        for g in range(SSD_G):
            gs = slice(g * SSD_Z, (g + 1) * SSD_Z)
            bgb = b_ref[rs, gs].astype(BF16)
            cb2 = jnp.concatenate([cbs[g], cbs[g]], axis=1)
            st = st_ref[g]
            y_inter = yis[g]
            x2s, decs = [], []
            for pr in range(hg // 2):
                h0 = g * hg + 2 * pr
                ps = slice(h0 * SSD_P, (h0 + 2) * SSD_P)
                xp = xs_ref[rs, ps]
                ccol = pair(cum, h0)
                rrow = jnp.where(low, ct2[h0:h0 + 1, :], ct2[h0 + 1:h0 + 2, :])
                drow = jnp.where(low, dt2[h0:h0 + 1, :], dt2[h0 + 1:h0 + 2, :])
                seg = jnp.exp(jnp.where(causal2, ccol - rrow, -jnp.inf))
                m = (cb2 * seg * drow).astype(BF16)
                xbd = jnp.concatenate([jnp.where(low, xp, 0.0), jnp.where(low, 0.0, xp)], axis=0).astype(BF16)
                y = _dot(m, xbd) + y_inter[:, pr * pw:(pr + 1) * pw] * jnp.exp(ccol)
                if reverse:
                    ybuf[:, ps] = y + acc_ref[rs, ps]
                else:
                    y_ref[rs, ps] = y
                x2s.append((xp * pair(w2, h0)).astype(BF16))
                decs.append(pair(dec, h0))
            upd = _dot_tn(bgb, jnp.concatenate(x2s, axis=1))
            st_ref[g] = st * jnp.concatenate(decs, axis=1) + upd
        if reverse:
            u = (ybuf[...] + dsk_ref[...] * xs_ref[rs, :]) * _silu(z_ref[rs, :])
            ms = jnp.mean(u * u, axis=-1, keepdims=True)
            y_ref[rs, :] = (u * lax.rsqrt(ms + EPS) * gn_ref[...]).astype(BF16)
        return carry

    lax.fori_loop(0, nch, chunk, 0)


def _ssd(xbc, pdt, dt_bias, a_log, fin=None):
    reverse = fin is not None
    nblk = N_ALL // ROWS
    nctx = T_CTX // ROWS
    blk = (lambda i: _rev_block(i, nblk)) if reverse else (lambda i: i)
    d = 1 if reverse else 0
    gz = SSD_G * SSD_Z
    full = lambda i: (blk(i), 0)
    in_specs = [
        pl.BlockSpec((ROWS, SSD_INNER), full),
        pl.BlockSpec((ROWS, gz), lambda i: (blk(i), SSD_INNER // gz)),
        pl.BlockSpec((ROWS, gz), lambda i: (blk(i), SSD_INNER // gz + 1)),
        pl.BlockSpec((ROWS, LANES), full),
        pl.BlockSpec((None, 1, SSD_H), lambda i: (d, 0, 0)),
        pl.BlockSpec((None, 1, SSD_H), lambda i: (d, 0, 0)),
    ]
    args = [xbc, xbc, xbc, pdt, dt_bias.reshape(2, 1, SSD_H), a_log.reshape(2, 1, SSD_H)]
    nch = ROWS // CHUNK
    scratch = [pltpu.VMEM((SSD_G, SSD_Z, SSD_INNER // SSD_G), F32),
               pltpu.VMEM((nch, CHUNK, 2 * SSD_H), F32), pltpu.VMEM((nch, CHUNK, 2 * SSD_H), F32),
               pltpu.VMEM((nch, SSD_H, 2 * CHUNK), F32), pltpu.VMEM((nch, SSD_H, 2 * CHUNK), F32)]
    if reverse:
        yf, p1, d_rep, norm_g = fin
        in_specs += [pl.BlockSpec((ROWS, SSD_INNER), full), pl.BlockSpec((ROWS, SSD_INNER), full),
                     pl.BlockSpec((1, SSD_INNER), lambda i: (0, 0)), pl.BlockSpec((1, SSD_INNER), lambda i: (0, 0))]
        args += [yf, p1, d_rep, norm_g]
        scratch.append(pltpu.VMEM((CHUNK, SSD_INNER), F32))
        out_spec = pl.BlockSpec((ROWS, SSD_INNER), lambda i: (blk(jnp.maximum(i, nctx)) - nctx, 0))
        out_shape = jax.ShapeDtypeStruct((T_LAT, SSD_INNER), BF16)
    else:
        out_spec = pl.BlockSpec((ROWS, SSD_INNER), full)
        out_shape = jax.ShapeDtypeStruct((N_ALL, SSD_INNER), F32)
    return pl.pallas_call(
        functools.partial(_ssd_kernel, reverse=reverse),
        grid=(nblk,),
        in_specs=in_specs,
        out_specs=out_spec,
        out_shape=out_shape,
        scratch_shapes=scratch,
        compiler_params=_params("arbitrary"),
        name="ssd_bwd" if reverse else "ssd_fwd",
    )(*args)


def _gather_wait(h_hbm, buf, sem, slot):
    pltpu.make_async_copy(h_hbm.at[pl.ds(0, MOE_TM), :], buf.at[slot], sem.at[slot]).wait()


GATHER_SEG = 16


def _gather_issue(idx_ref, n_valid, h_hbm, buf, sem, slot):
    def body(r8, carry):
        base = pl.multiple_of(r8 * SUBLANES, SUBLANES)
        for u in range(SUBLANES):
            tok = idx_ref[0, 0, base + u]
            pltpu.make_async_copy(h_hbm.at[pl.ds(tok, 1), :], buf.at[slot, pl.ds(base + u, 1), :],
                                  sem.at[slot]).start()
        return carry

    per = GATHER_SEG // SUBLANES
    for s in range(MOE_TM // GATHER_SEG):
        @pl.when(s * GATHER_SEG < n_valid)
        def _():
            lax.fori_loop(s * per, (s + 1) * per, body, 0)

        @pl.when(s * GATHER_SEG >= n_valid)
        def _():
            pltpu.make_async_copy(h_hbm.at[pl.ds(0, GATHER_SEG), :],
                                  buf.at[slot, pl.ds(s * GATHER_SEG, GATHER_SEG), :], sem.at[slot]).start()


def _expert_block_kernel(be_ref, ws_ref, nx_ref, nv_ref, nu_ref, *refs, n_w, layer, gather, compute):
    if gather:
        icur_ref, inxt_ref, h_hbm = refs[:3]
        refs = refs[3:]
    else:
        x_ref = refs[0]
        refs = refs[1:]
    w_hbm, o_ref = refs[:n_w], refs[n_w]
    scr = refs[n_w + 1:]
    w32, wsem = scr[:n_w], scr[n_w]
    m = pl.program_id(0)
    nu = nu_ref[0]
    e = be_ref[m]
    wslot = ws_ref[m]
    first = jnp.logical_and(m < nu, jnp.logical_or(m == 0, e != be_ref[jnp.maximum(m - 1, 0)]))

    def w_copy(k, ee, slot):
        return pltpu.make_async_copy(w_hbm[k].at[layer, ee], w32[k].at[slot], wsem.at[k, slot])

    @pl.when(m == 0)
    def _():
        for k in range(n_w):
            w_copy(k, e, wslot).start()

    if gather:
        gbuf, gsem = scr[n_w + 1:]
        gslot = lax.rem(m, 2)

        @pl.when(m == 0)
        def _():
            _gather_issue(icur_ref, nv_ref[0], h_hbm, gbuf, gsem, 0)

        @pl.when(m + 1 < nu)
        def _():
            _gather_issue(inxt_ref, nv_ref[m + 1], h_hbm, gbuf, gsem, 1 - gslot)

    @pl.when(jnp.logical_and(first, nx_ref[m] >= 0))
    def _():
        for k in range(n_w):
            w_copy(k, nx_ref[m], 1 - wslot).start()

    @pl.when(first)
    def _():
        for k in range(n_w):
            w_copy(k, e, wslot).wait()

    @pl.when(m < nu)
    def _():
        if gather:
            _gather_wait(h_hbm, gbuf, gsem, gslot)
            x = gbuf[gslot]
        else:
            x = x_ref[...].astype(F32)
        o_ref[...] = compute(x, *[w32[k][wslot] for k in range(n_w)]).astype(o_ref.dtype)


def _expert_stage(src, ws, plan, layer, n_out, out_dtype, compute, name, slot_tok=None):
    nblk = plan[0].shape[0]
    gather = slot_tok is not None
    anyspec = pl.BlockSpec(memory_space=pl.ANY)
    used = lambda m, p: jnp.minimum(m, p[-1][0] - 1)
    n_w = len(ws)
    wshape = ws[0].shape[2:]
    scratch = [pltpu.VMEM((2,) + wshape, F32) for _ in ws]
    scratch.append(pltpu.SemaphoreType.DMA((n_w, 2)))
    if gather:
        in_specs = [
            pl.BlockSpec((1, 1, MOE_TM), lambda m, *p: (used(m, p), 0, 0), memory_space=pltpu.SMEM),
            pl.BlockSpec((1, 1, MOE_TM), lambda m, *p: (used(m + 1, p), 0, 0), memory_space=pltpu.SMEM),
            anyspec,
        ]
        args = [slot_tok, slot_tok, src]
        scratch += [pltpu.VMEM((2, MOE_TM, src.shape[1]), src.dtype), pltpu.SemaphoreType.DMA((2,))]
    else:
        in_specs = [pl.BlockSpec((MOE_TM, src.shape[1]), lambda m, *p: (used(m, p), 0))]
        args = [src]
    return pl.pallas_call(
        functools.partial(_expert_block_kernel, n_w=n_w, layer=layer, gather=gather, compute=compute),
        grid_spec=pltpu.PrefetchScalarGridSpec(
            num_scalar_prefetch=len(plan),
            grid=(nblk,),
            in_specs=in_specs + [anyspec] * n_w,
            out_specs=pl.BlockSpec((MOE_TM, n_out), lambda m, *p: (used(m, p), 0)),
            scratch_shapes=scratch,
        ),
        out_shape=jax.ShapeDtypeStruct((nblk * MOE_TM, n_out), out_dtype),
        compiler_params=_params("arbitrary"),
        name=name,
    )(*plan, *args, *ws)


def _moe(h, eidx, wts, w_gate, w_up, w_down, layer):
    n = h.shape[0]
    i32 = jnp.int32
    flat_e = eidx.reshape(-1)
    experts = jnp.arange(N_EXP, dtype=i32)
    onehot = (flat_e[:, None] == experts[None, :]).astype(i32)
    csum = jnp.cumsum(onehot, axis=0)
    bcount = (csum[-1] + MOE_TM - 1) // MOE_TM
    bend = jnp.cumsum(bcount)
    bstart = bend - bcount
    dest = jnp.sum(onehot * (csum - 1 + (bstart * MOE_TM)[None, :]), axis=1)
    nblk = -(-2 * n // MOE_TM) + N_EXP
    n_used = bend[-1].astype(i32)
    blk = jnp.minimum(jnp.arange(nblk, dtype=i32), n_used - 1)
    block_exp = jnp.sum((blk[:, None] >= bend[None, :]).astype(i32), axis=1)
    owns = bcount > 0
    wslot = (jnp.cumsum(owns.astype(i32)) - 1) % 2
    later = jnp.logical_and(owns[None, :], experts[None, :] > experts[:, None])
    nxt = jnp.min(jnp.where(later, experts[None, :], N_EXP), axis=1)
    nxt = jnp.where(nxt == N_EXP, -1, nxt)
    of_blk = (block_exp[:, None] == experts[None, :]).astype(i32)
    rows_left = jnp.sum(of_blk * (csum[-1] + bstart * MOE_TM)[None, :], axis=1) - blk * MOE_TM
    n_valid = jnp.clip(rows_left, 0, MOE_TM)
    plan = (block_exp.astype(i32), jnp.sum(of_blk * wslot[None, :], axis=1).astype(i32),
            jnp.sum(of_blk * nxt[None, :], axis=1).astype(i32), n_valid.astype(i32), n_used.reshape(1))
    tok = jnp.tile(jnp.arange(n, dtype=i32), 2)
    pad_tok = jnp.arange(nblk * MOE_TM, dtype=i32) % n
    slot_tok = pad_tok.at[dest].set(tok).reshape(nblk, 1, MOE_TM)
    hb = _expert_stage(h, [w_gate, w_up], plan, layer, D_EXP, BF16,
                       lambda x, wg, wu: _silu(_dot(x, wg)) * _dot(x, wu), "moe_up", slot_tok=slot_tok)
    yb = _expert_stage(hb, [w_down], plan, layer, D, F32, lambda x, wd: _dot(x, wd), "moe_down")
    return yb, dest.reshape(2, n), wts.T


def _pos_tables():
    quarter = D // 4
    omega = 1.0 / (10000.0 ** (jnp.arange(quarter, dtype=F32) / quarter))
    ang_r = jnp.arange(T_LAT // GRID_W, dtype=F32)[:, None] * omega
    ang_c = jnp.arange(GRID_W, dtype=F32)[:, None] * omega
    emb_r = jnp.concatenate([jnp.sin(ang_r), jnp.cos(ang_r)], axis=-1)
    emb_c = jnp.concatenate([jnp.sin(ang_c), jnp.cos(ang_c)], axis=-1)
    return emb_r, emb_c


def _block_diag_gates(wa, wx, group):
    nb = wa.shape[1]
    per = group // wa.shape[2]
    eye = jnp.eye(per, dtype=F32)

    def bd(w):
        w = w.reshape(nb // per, per, w.shape[1], w.shape[2])
        return jnp.einsum("gnkj,nm->gnkmj", w, eye).reshape(nb // per, group, group)

    return jnp.concatenate([bd(wa[0]), bd(wa[1]), bd(wx[0]), bd(wx[1])], axis=-1).astype(BF16)


def kernel(x, c, ctx, c_ctx, mod_w, mod_b, norm1_g, norm2_g, ev_w_in, ev_conv_w, ev_conv_b, lru_wa, lru_ba, lru_wx, lru_bx, lru_lambda, gla_wg_up, gla_bg, gla_norm_g, ev_w_out, od_w_in, od_conv_w, od_conv_b, ssd_a_log, ssd_dt_bias, ssd_d, ssd_norm_g, od_w_out, router_w, router_b, exp_w_gate, exp_w_up, exp_w_down, final_norm_g):
    mods = _mod_vectors(c, c_ctx, mod_w, mod_b)
    emb_r, emb_c = _pos_tables()
    rwt = router_w.T
    rb = router_b.reshape(N_EXP, 1)
    tm, tn = PROJ_TM, PROJ_TN

    x0, h0 = _prep0(x[0], ctx[0], emb_r, emb_c, mods[0], norm1_g[0:1])
    w_in_t = jnp.swapaxes(ev_w_in[0], 0, 1)
    p0 = _mm([h0], w_in_t, ncols=EVEN_MAIN, tm=tm, tn=tn, w_t=True)
    pad = _mm([h0], w_in_t, ncols=LANES, tm=tm, tn=LANES, col_off=EVEN_MAIN // LANES, w_t=True)
    wbd = _block_diag_gates(lru_wa[0], lru_wx[0], LRU_GATE_GROUP)
    ya = _lru(p0, ev_conv_w[0], ev_conv_b[0:1], wbd, lru_ba[0], lru_bx[0], lru_lambda[0])
    of = _gla(p0, pad, gla_wg_up[0], gla_bg[0], None)
    yb = _gla(p0, pad, gla_wg_up[0], gla_bg[0], gla_norm_g[0:1], of=of)
    y0 = _mm([ya, yb], ev_w_out[0], ncols=D, tm=tm, tn=tn)
    nctx = T_CTX // ROWS
    x1, h1, e0, w0 = _token_stage(x0, y0, mods[0], mods[0], norm2_g[0:1], rwt, rb, gate_col=2, shift_col=3,
                                  scale_col=4, ctx_blocks=nctx, route=True)
    f0, dest0, wc0 = _moe(h1, e0, w0, exp_w_gate, exp_w_up, exp_w_down, 0)
    x2, h2 = _token_stage(x1, f0, mods[0], mods[1], norm1_g[1:2], rwt, rb, gate_col=5, shift_col=0,
                          scale_col=1, ctx_blocks=nctx, combine=(dest0, wc0))

    p1 = _mm([h2], od_w_in[0], ncols=ODD_MAIN, tm=tm, tn=tn)
    pdt = _mm([h2], od_w_in[0], ncols=LANES, tm=tm, tn=LANES, col_off=ODD_MAIN // LANES)
    xbc = _conv_silu(p1, od_conv_w[0], od_conv_b[0:1])
    yf = _ssd(xbc, pdt, ssd_dt_bias[0], ssd_a_log[0])
    d_rep = jnp.repeat(ssd_d[0], SSD_P).reshape(1, SSD_INNER)
    gy = _ssd(xbc, pdt, ssd_dt_bias[0], ssd_a_log[0], fin=(yf, p1, d_rep, ssd_norm_g[0:1]))
    y1 = _mm([gy], od_w_out[0], ncols=D, tm=OUT1_TM, tn=OUT1_TN)
    x3, h3, e1, w1 = _token_stage(x2, y1, mods[1], mods[1], norm2_g[1:2], rwt, rb, gate_col=2, shift_col=3,
                                  scale_col=4, x_off=nctx, route=True)
    f1, dest1, wc1 = _moe(h3, e1, w1, exp_w_gate, exp_w_up, exp_w_down, 1)
    (out,) = _token_stage(x3, f1, mods[1], mods[1], final_norm_g.reshape(1, D), rwt, rb, gate_col=5,
                          combine=(dest1, wc1), final=True)
    return out[None]
```

```python
import functools
import math

import jax
import jax.numpy as jnp
from jax import lax
from jax.experimental import pallas as pl
from jax.experimental.pallas import tpu as pltpu

F32 = jnp.float32
BF16 = jnp.bfloat16
HIGHEST = lax.Precision.HIGHEST

D = 2048
T_LAT = 8192
T_CTX = 256
N_ALL = T_CTX + T_LAT
GRID_W = 64
EPS = 1e-6
CONV_W = 4
LRU_W = 1024
LRU_C = 8.0
GLA_H = 8
GLA_DK = 64
GLA_DV = 128
GLA_QK = GLA_H * GLA_DK
GLA_V = GLA_H * GLA_DV
GLA_RANK = 16
GLA_TAU = 16.0
CHUNK = 64
EVEN_MAIN = 2 * LRU_W + 2 * GLA_QK + 2 * GLA_V
SSD_INNER = 2 * D
SSD_P = 64
SSD_H = SSD_INNER // SSD_P
SSD_G = 8
SSD_Z = 128
SSD_XBC = SSD_INNER + 2 * SSD_G * SSD_Z
ODD_MAIN = SSD_INNER + SSD_XBC
N_EXP = 32
N_GRP = 4
GRP = N_EXP // N_GRP
D_EXP = 1024
MOE_TM = 256

ROWS = 256
LANES = 128
SUBLANES = 8
VMEM_LIMIT = 56 * 1024 * 1024
PROJ_TM = N_ALL // 6
PROJ_TN = 1024
OUT1_TM = 1024
OUT1_TN = 512
LRU_GATE_GROUP = 256


def _params(*sem):
    return pltpu.CompilerParams(dimension_semantics=sem, vmem_limit_bytes=VMEM_LIMIT)


def _silu(v):
    return v * jax.nn.sigmoid(v)


def _softplus(v):
    return jnp.maximum(v, 0.0) + jnp.log1p(jnp.exp(-jnp.abs(v)))


def _dot(a, b):
    return jnp.dot(a, b, preferred_element_type=F32)


def _dot_nt(a, b):
    return lax.dot_general(a, b, (((1,), (1,)), ((), ())), preferred_element_type=F32)


def _dot_tn(a, b):
    return lax.dot_general(a, b, (((0,), (0,)), ((), ())), preferred_element_type=F32)


def _mod_kernel(s_ref, w_ref, b_ref, o_ref):
    tn = w_ref.shape[1]
    nrep = tn // LANES

    def body(r, acc):
        a0, a1 = acc
        rows = pl.ds(pl.multiple_of(r * SUBLANES, SUBLANES), SUBLANES)
        w = w_ref[rows, :]
        s0 = _silu(s_ref[0, rows, :])
        s1 = _silu(s_ref[1, rows, :])
        a0 = a0 + w * jnp.concatenate([s0] * nrep, axis=1)
        a1 = a1 + w * jnp.concatenate([s1] * nrep, axis=1)
        return a0, a1

    zero = jnp.zeros((SUBLANES, tn), F32)
    a0, a1 = lax.fori_loop(0, D // SUBLANES, body, (zero, zero), unroll=4)
    o_ref[0:1, :] = jnp.sum(a0, axis=0, keepdims=True) + b_ref[...]
    o_ref[1:2, :] = jnp.sum(a1, axis=0, keepdims=True) + b_ref[...]


def _mod_vectors(c, c_ctx, mod_w, mod_b):
    depth = mod_w.shape[0]
    tn = 2048
    s = jnp.broadcast_to(jnp.stack([c[0], c_ctx])[:, :, None], (2, D, LANES))
    return pl.pallas_call(
        _mod_kernel,
        grid=(depth, 6 * D // tn),
        in_specs=[
            pl.BlockSpec((2, D, LANES), lambda l, j: (0, 0, 0)),
            pl.BlockSpec((None, D, tn), lambda l, j: (l, 0, j)),
            pl.BlockSpec((None, 1, tn), lambda l, j: (l, 0, j)),
        ],
        out_specs=pl.BlockSpec((None, 2, tn), lambda l, j: (l, 0, j)),
        out_shape=jax.ShapeDtypeStruct((depth, 2, 6 * D), F32),
        compiler_params=_params("arbitrary", "arbitrary"),
        name="mod_vectors",
    )(s, mod_w, mod_b.reshape(depth, 1, 6 * D))


def _mod_row(mod_ref, kind, col):
    return mod_ref[pl.ds(kind, 1), col * D:(col + 1) * D]


def _ada_norm(xv, g, shift, scale):
    ms = jnp.mean(xv * xv, axis=-1, keepdims=True)
    return (xv * lax.rsqrt(ms + EPS) * g) * (1.0 + scale) + shift


def _prep0_kernel(x_ref, ctx_ref, er_ref, ec_ref, mod_ref, g_ref, xo_ref, ho_ref):
    i = pl.program_id(0)

    @pl.when(i == 0)
    def _():
        xo_ref[...] = ctx_ref[...]

    @pl.when(i > 0)
    def _():
        r0 = (i - 1) * (ROWS // GRID_W)
        for j in range(ROWS // GRID_W):
            rs = slice(j * GRID_W, (j + 1) * GRID_W)
            xo_ref[rs, 0:D // 2] = x_ref[rs, 0:D // 2] + er_ref[pl.ds(r0 + j, 1), :]
            xo_ref[rs, D // 2:D] = x_ref[rs, D // 2:D] + ec_ref[...]

    kind = jnp.where(i == 0, 1, 0)
    h = _ada_norm(xo_ref[...], g_ref[...], _mod_row(mod_ref, kind, 0), _mod_row(mod_ref, kind, 1))
    ho_ref[...] = h.astype(BF16)


def _prep0(x, ctx, emb_r, emb_c, mod0, g):
    nblk = N_ALL // ROWS
    return pl.pallas_call(
        _prep0_kernel,
        grid=(nblk,),
        in_specs=[
            pl.BlockSpec((ROWS, D), lambda i: (jnp.maximum(i - 1, 0), 0)),
            pl.BlockSpec((ROWS, D), lambda i: (0, 0)),
            pl.BlockSpec(emb_r.shape, lambda i: (0, 0)),
            pl.BlockSpec(emb_c.shape, lambda i: (0, 0)),
            pl.BlockSpec((2, 6 * D), lambda i: (0, 0)),
            pl.BlockSpec((1, D), lambda i: (0, 0)),
        ],
        out_specs=[pl.BlockSpec((ROWS, D), lambda i: (i, 0)), pl.BlockSpec((ROWS, D), lambda i: (i, 0))],
        out_shape=[jax.ShapeDtypeStruct((N_ALL, D), F32), jax.ShapeDtypeStruct((N_ALL, D), BF16)],
        compiler_params=_params("arbitrary"),
        name="embed_norm",
    )(x, ctx, emb_r, emb_c, mod0, g)


def _route(hf32, rwt_ref, rb_ref, eo_ref, wo_ref):
    logits = lax.dot_general(rwt_ref[...], hf32, (((1,), (1,)), ((), ())),
                             precision=HIGHEST, preferred_element_type=F32)
    s = jax.nn.sigmoid(logits)
    sel = s + rb_ref[...]
    row = lax.broadcasted_iota(jnp.int32, (GRP, ROWS), 0)
    neg = jnp.float32(-jnp.inf)
    gs, i1s, i2s = [], [], []
    for g in range(N_GRP):
        blk = sel[g * GRP:(g + 1) * GRP, :]
        m1 = jnp.max(blk, axis=0, keepdims=True)
        i1 = jnp.min(jnp.where(blk == m1, row, GRP), axis=0, keepdims=True)
        blk2 = jnp.where(row == i1, neg, blk)
        m2 = jnp.max(blk2, axis=0, keepdims=True)
        i2 = jnp.min(jnp.where(blk2 == m2, row, GRP), axis=0, keepdims=True)
        gs.append(m1 + m2)
        i1s.append(i1)
        i2s.append(i2)
    best, gi, i1, i2 = gs[0], jnp.zeros((1, ROWS), jnp.int32), i1s[0], i2s[0]
    for g in range(1, N_GRP):
        upd = gs[g] > best
        best = jnp.where(upd, gs[g], best)
        gi = jnp.where(upd, g, gi)
        i1 = jnp.where(upd, i1s[g], i1)
        i2 = jnp.where(upd, i2s[g], i2)
    e1 = gi * GRP + i1
    e2 = gi * GRP + i2
    erow = lax.broadcasted_iota(jnp.int32, (N_EXP, ROWS), 0)
    s1 = jnp.sum(jnp.where(erow == e1, s, 0.0), axis=0, keepdims=True)
    s2 = jnp.sum(jnp.where(erow == e2, s, 0.0), axis=0, keepdims=True)
    tot = s1 + s2
    eo_ref[0:1, :] = e1
    eo_ref[1:2, :] = e2
    wo_ref[0:1, :] = s1 / tot
    wo_ref[1:2, :] = s2 / tot


def _combine_wait(yb_hbm, ybuf, sem, slot):
    pltpu.make_async_copy(yb_hbm.at[pl.ds(0, 2 * ROWS), :], ybuf.at[slot], sem.at[slot]).wait()


def _combine_issue(dest_ref, yb_hbm, ybuf, sem, slot):
    def body(r, carry):
        for k in range(2):
            row = dest_ref[0, k, r]
            pltpu.make_async_copy(yb_hbm.at[pl.ds(row, 1), :], ybuf.at[slot, pl.ds(k * ROWS + r, 1), :],
                                  sem.at[slot]).start()
        return carry

    lax.fori_loop(0, ROWS, body, 0, unroll=8)


def _token_kernel(*refs, gate_col, shift_col, scale_col, ctx_blocks, route, combine, final):
    refs = list(refs)
    if combine:
        dcur_ref, dnxt_ref = refs.pop(0), refs.pop(0)
    x_ref, y_ref = refs.pop(0), refs.pop(0)
    if combine:
        wc_ref = refs.pop(0)
    modg_ref, modn_ref, g_ref, rwt_ref, rb_ref = (refs.pop(0) for _ in range(5))
    i = pl.program_id(0)
    if combine:
        ybuf, sem = refs[-2], refs[-1]
        slot = lax.rem(i, 2)

        @pl.when(i == 0)
        def _():
            _combine_issue(dcur_ref, y_ref, ybuf, sem, 0)

        @pl.when(i + 1 < pl.num_programs(0))
        def _():
            _combine_issue(dnxt_ref, y_ref, ybuf, sem, 1 - slot)

        _combine_wait(y_ref, ybuf, sem, slot)
        y = wc_ref[:, 0:1] * ybuf[slot, 0:ROWS, :] + wc_ref[:, 1:2] * ybuf[slot, ROWS:2 * ROWS, :]
    else:
        y = y_ref[...]
    kind = jnp.where(i < ctx_blocks, 1, 0)
    xn = x_ref[...] + _mod_row(modg_ref, kind, gate_col) * y
    if final:
        ms = jnp.mean(xn * xn, axis=-1, keepdims=True)
        refs[0][...] = xn * lax.rsqrt(ms + EPS) * g_ref[...]
        return
    xo_ref, ho_ref = refs[0], refs[1]
    xo_ref[...] = xn
    h = _ada_norm(xn, g_ref[...], _mod_row(modn_ref, kind, shift_col), _mod_row(modn_ref, kind, scale_col))
    ho_ref[...] = h.astype(ho_ref.dtype)
    if route:
        _route(h, rwt_ref, rb_ref, refs[2], refs[3])


def _token_stage(x, y, modg, modn, g, rwt, rb, *, gate_col, shift_col=0, scale_col=0, x_off=0, ctx_blocks=0,
                 route=False, combine=None, final=False):
    n = x.shape[0] - x_off * ROWS
    nblk = n // ROWS
    row = lambda i: (i, 0)
    const2 = lambda i: (0, 0)
    in_specs, args, scratch = [], [], []
    if combine is not None:
        dest, wc = combine
        dest3 = dest.reshape(2, nblk, ROWS).transpose(1, 0, 2)
        in_specs += [pl.BlockSpec((1, 2, ROWS), lambda i: (i, 0, 0), memory_space=pltpu.SMEM),
                     pl.BlockSpec((1, 2, ROWS), lambda i: (jnp.minimum(i + 1, nblk - 1), 0, 0),
                                  memory_space=pltpu.SMEM)]
        args += [dest3, dest3]
    in_specs.append(pl.BlockSpec((ROWS, D), lambda i: (i + x_off, 0)))
    args.append(x)
    if combine is not None:
        in_specs += [pl.BlockSpec(memory_space=pl.ANY), pl.BlockSpec((ROWS, 2), row)]
        args += [y, wc]
        scratch = [pltpu.VMEM((2, 2 * ROWS, D), F32), pltpu.SemaphoreType.DMA((2,))]
    else:
        in_specs.append(pl.BlockSpec((ROWS, D), row))
        args.append(y)
    in_specs += [pl.BlockSpec((2, 6 * D), const2), pl.BlockSpec((2, 6 * D), const2), pl.BlockSpec((1, D), const2),
                 pl.BlockSpec((N_EXP, D), const2), pl.BlockSpec((N_EXP, 1), const2)]
    args += [modg, modn, g, rwt, rb]
    if final:
        out_specs = [pl.BlockSpec((ROWS, D), row)]
        out_shape = [jax.ShapeDtypeStruct((n, D), F32)]
    else:
        out_specs = [pl.BlockSpec((ROWS, D), row), pl.BlockSpec((ROWS, D), row)]
        out_shape = [jax.ShapeDtypeStruct((n, D), F32), jax.ShapeDtypeStruct((n, D), F32 if route else BF16)]
        if route:
            out_specs += [pl.BlockSpec((2, ROWS), lambda i: (0, i)), pl.BlockSpec((2, ROWS), lambda i: (0, i))]
            out_shape += [jax.ShapeDtypeStruct((2, n), jnp.int32), jax.ShapeDtypeStruct((2, n), F32)]
    kern = functools.partial(_token_kernel, gate_col=gate_col, shift_col=shift_col, scale_col=scale_col,
                             ctx_blocks=ctx_blocks, route=route, combine=combine is not None, final=final)
    return pl.pallas_call(
        kern,
        grid=(nblk,),
        in_specs=in_specs,
        out_specs=out_specs,
        out_shape=out_shape,
        scratch_shapes=scratch,
        compiler_params=_params("arbitrary"),
        name="token_stage",
    )(*args)


def _mm_kernel(*refs, nx, valid, w_t):
    x_refs, w_refs, o_ref, wb_refs = refs[:nx], refs[nx:2 * nx], refs[2 * nx], refs[2 * nx + 1:]
    out_axis = 0 if w_t else 1

    @pl.when(pl.program_id(1) == 0)
    def _():
        for w_ref, wb_ref in zip(w_refs, wb_refs):
            w = w_ref[...]
            if valid < w.shape[out_axis]:
                w = jnp.where(lax.broadcasted_iota(jnp.int32, w.shape, out_axis) < valid, w, 0.0)
            wb_ref[...] = w.astype(BF16)

    mul = _dot_nt if w_t else _dot
    acc = mul(x_refs[0][...], wb_refs[0][...])
    for x_ref, wb_ref in zip(x_refs[1:], wb_refs[1:]):
        acc = acc + mul(x_ref[...], wb_ref[...])
    o_ref[...] = acc


def _mm(xs, w, *, ncols, tm, tn, col_off=0, w_t=False):
    nx = len(xs)
    k = xs[0].shape[1]
    m = xs[0].shape[0]
    n_total = w.shape[0] if w_t else w.shape[1]
    valid = min(tn, n_total - col_off * tn) if ncols == tn else tn
    in_specs = [pl.BlockSpec((tm, k), lambda j, i: (i, 0)) for _ in xs]
    if w_t:
        in_specs += [pl.BlockSpec((tn, k), functools.partial(lambda j, i, kk: (j + col_off, kk), kk=kk))
                     for kk in range(nx)]
    else:
        in_specs += [pl.BlockSpec((k, tn), functools.partial(lambda j, i, kk: (kk, j + col_off), kk=kk))
                     for kk in range(nx)]
    return pl.pallas_call(
        functools.partial(_mm_kernel, nx=nx, valid=valid, w_t=w_t),
        grid=(ncols // tn, m // tm),
        in_specs=in_specs,
        out_specs=pl.BlockSpec((tm, tn), lambda j, i: (i, j)),
        out_shape=jax.ShapeDtypeStruct((m, ncols), F32),
        scratch_shapes=[pltpu.VMEM((tn, k) if w_t else (k, tn), BF16) for _ in xs],
        compiler_params=_params("arbitrary", "arbitrary"),
        name="proj",
    )(*xs, *([w] * nx))


def _conv_block(cur, prev8, next8, w_ref, b_ref, i, nblk):
    ctx_edge = T_CTX // ROWS
    keep_prev = jnp.logical_and(i != 0, i != ctx_edge)
    keep_next = jnp.logical_and(i != nblk - 1, i != ctx_edge - 1)
    prev8 = jnp.where(keep_prev, prev8, 0.0)
    next8 = jnp.where(keep_next, next8, 0.0)
    ext = jnp.concatenate([prev8, cur, next8], axis=0)
    n_ext = ROWS + 2 * SUBLANES
    y = b_ref[...] + w_ref[CONV_W // 2:CONV_W // 2 + 1, :] * cur
    for j in range(CONV_W):
        off = j - CONV_W // 2
        if off != 0:
            shifted = pltpu.roll(ext, (-off) % n_ext, axis=0)[SUBLANES:SUBLANES + ROWS]
            y = y + w_ref[j:j + 1, :] * shifted
    return y


def _halo_specs(width, col_blk, nblk):
    per = ROWS // SUBLANES
    last8 = N_ALL // SUBLANES - 1
    return [
        pl.BlockSpec((ROWS, width), lambda i, *a: (i, col_blk(*a))),
        pl.BlockSpec((SUBLANES, width), lambda i, *a: (jnp.maximum(i * per - 1, 0), col_blk(*a))),
        pl.BlockSpec((SUBLANES, width), lambda i, *a: (jnp.minimum((i + 1) * per, last8), col_blk(*a))),
    ]


def _scan_tiles(a_ref, b_ref, h_ref, carry_ref, reverse):
    ntile = ROWS // SUBLANES
    row = lax.broadcasted_iota(jnp.int32, (SUBLANES, LRU_W), 0)

    def body(t, carry):
        tt = (ntile - 1 - t) if reverse else t
        rows = pl.ds(pl.multiple_of(tt * SUBLANES, SUBLANES), SUBLANES)
        a = a_ref[rows, :]
        b = b_ref[rows, :]
        for d in (1, 2, 4):
            shift = (SUBLANES - d) if reverse else d
            a_sh = pltpu.roll(a, shift, axis=0)
            b_sh = pltpu.roll(b, shift, axis=0)
            m = (row < SUBLANES - d) if reverse else (row >= d)
            b = jnp.where(m, a * b_sh, 0.0) + b
            a = jnp.where(m, a * a_sh, a)
        h = a * carry + b
        h_ref[rows, :] = h
        edge = h[0:1, :] if reverse else h[SUBLANES - 1:SUBLANES, :]
        return jnp.broadcast_to(edge, (SUBLANES, LRU_W))

    carry_ref[...] = lax.fori_loop(0, ntile, body, carry_ref[...])


def _lru_fwd_kernel(xa_ref, prev_ref, next_ref, cw_ref, cb_ref, wbd_ref, ba_ref, bx_ref, lam_ref,
                    hf_ref, a1_ref, b1_ref, carry_ref, a0_ref, b0_ref):
    i = pl.program_id(0)
    nblk = pl.num_programs(0)

    @pl.when(i == 0)
    def _():
        carry_ref[...] = jnp.zeros_like(carry_ref)

    xa = _conv_block(xa_ref[...], prev_ref[...], next_ref[...], cw_ref, cb_ref, i, nblk)
    xab = xa.astype(BF16)
    nsp = -LRU_C * _softplus(-lam_ref[...])
    gw = wbd_ref.shape[1]
    for g in range(LRU_W // gw):
        cs = slice(g * gw, (g + 1) * gw)
        z = _dot(xab[:, cs], wbd_ref[g])
        for d in range(2):
            r = jax.nn.sigmoid(z[:, d * gw:(d + 1) * gw] + ba_ref[d:d + 1, cs])
            ig = jax.nn.sigmoid(z[:, (2 + d) * gw:(3 + d) * gw] + bx_ref[d:d + 1, cs])
            log_a = r * nsp[d:d + 1, cs]
            a = jnp.exp(log_a)
            b = jnp.sqrt(-jnp.tanh(log_a) * (a * a + 1.0)) * ig * xa[:, cs]
            if d == 0:
                a0_ref[:, cs] = a
                b0_ref[:, cs] = b
            else:
                a1_ref[:, cs] = a
                b1_ref[:, cs] = b
    _scan_tiles(a0_ref, b0_ref, hf_ref, carry_ref, reverse=False)


def _gelu_tanh(v):
    return 0.5 * v * (1.0 + jnp.tanh(math.sqrt(2.0 / math.pi) * (v + 0.044715 * (v * v * v))))


def _lru_bwd_kernel(a1_ref, b1_ref, hf_ref, ga_ref, ya_ref, carry_ref, hb_ref):
    @pl.when(pl.program_id(0) == 0)
    def _():
        carry_ref[...] = jnp.zeros_like(carry_ref)

    _scan_tiles(a1_ref, b1_ref, hb_ref, carry_ref, reverse=True)
    ya_ref[...] = ((hf_ref[...] + hb_ref[...]) * _gelu_tanh(ga_ref[...])).astype(BF16)


def _rev_block(i, nblk):
    nctx = T_CTX // ROWS
    return jnp.where(i < nctx, nctx - 1 - i, nblk - 1 - (i - nctx))


def _lru(p0, conv_w, conv_b, wbd, ba, bx, lam):
    nblk = N_ALL // ROWS
    full = lambda shape: pl.BlockSpec(shape, lambda i: (0,) * len(shape))
    hf, a1, b1 = pl.pallas_call(
        _lru_fwd_kernel,
        grid=(nblk,),
        in_specs=_halo_specs(LRU_W, lambda: 0, nblk) + [
            full((CONV_W, LRU_W)), full((1, LRU_W)), full(wbd.shape),
            full((2, LRU_W)), full((2, LRU_W)), full((2, LRU_W)),
        ],
        out_specs=[pl.BlockSpec((ROWS, LRU_W), lambda i: (i, 0))] * 3,
        out_shape=[jax.ShapeDtypeStruct((N_ALL, LRU_W), F32)] * 3,
        scratch_shapes=[pltpu.VMEM((SUBLANES, LRU_W), F32), pltpu.VMEM((ROWS, LRU_W), F32),
                        pltpu.VMEM((ROWS, LRU_W), F32)],
        compiler_params=_params("arbitrary"),
        name="lru_fwd",
    )(p0, p0, p0, conv_w, conv_b, wbd, ba, bx, lam)
    rev = lambda i: (_rev_block(i, nblk), 0)
    ya = pl.pallas_call(
        _lru_bwd_kernel,
        grid=(nblk,),
        in_specs=[pl.BlockSpec((ROWS, LRU_W), rev)] * 3
        + [pl.BlockSpec((ROWS, LRU_W), lambda i: (_rev_block(i, nblk), 1))],
        out_specs=pl.BlockSpec((ROWS, LRU_W), rev),
        out_shape=jax.ShapeDtypeStruct((N_ALL, LRU_W), BF16),
        scratch_shapes=[pltpu.VMEM((SUBLANES, LRU_W), F32), pltpu.VMEM((ROWS, LRU_W), F32)],
        compiler_params=_params("arbitrary"),
        name="lru_bwd",
    )(a1, b1, hf, p0)
    return ya


def _tri(reverse):
    r = lax.broadcasted_iota(jnp.int32, (CHUNK, CHUNK), 0)
    c = lax.broadcasted_iota(jnp.int32, (CHUNK, CHUNK), 1)
    return (c >= r) if reverse else (c <= r)


def _cumsum_rows(tri_b, v):
    hi = v.astype(BF16)
    r1 = v - hi.astype(F32)
    mid = r1.astype(BF16)
    lo = (r1 - mid.astype(F32)).astype(BF16)
    return _dot(tri_b, hi) + _dot(tri_b, mid) + _dot(tri_b, lo)


def _gla_kernel(*refs, reverse):
    if reverse:
        (q_ref, k_ref, v_ref, ad_ref, wg_ref, bg_ref, of_ref, og_ref, gn_ref, o_ref,
         s_ref, qd_s, klo_s, khi_s, ke_s, dec_s, s2_s, upd_s) = refs
    else:
        (q_ref, k_ref, v_ref, ad_ref, wg_ref, bg_ref, o_ref,
         s_ref, qd_s, klo_s, khi_s, ke_s, dec_s, s2_s, upd_s) = refs
    d = 1 if reverse else 0

    @pl.when(pl.program_id(0) == 0)
    def _():
        s_ref[...] = jnp.zeros_like(s_ref)

    nch = ROWS // CHUNK
    pw = 2 * GLA_DK
    row2 = lax.broadcasted_iota(jnp.int32, (CHUNK, pw), 0)
    col2 = jnp.bitwise_and(lax.broadcasted_iota(jnp.int32, (CHUNK, pw), 1), GLA_DK - 1)
    causal2 = (col2 >= row2) if reverse else (col2 <= row2)
    eye = (lax.broadcasted_iota(jnp.int32, (pw, pw), 0) == lax.broadcasted_iota(jnp.int32, (pw, pw), 1)).astype(F32)
    zeros_v = jnp.zeros((CHUNK, GLA_DV), BF16)

    rr = lax.broadcasted_iota(jnp.int32, (ROWS, ROWS), 0)
    rc = lax.broadcasted_iota(jnp.int32, (ROWS, ROWS), 1)
    same = (rr // CHUNK) == (rc // CHUNK)
    tri_b = jnp.logical_and(same, (rc >= rr) if reverse else (rc <= rr)).astype(BF16)
    ad = ad_ref[:, d * GLA_RANK:(d + 1) * GLA_RANK]
    z = _dot(ad.astype(BF16), wg_ref[...].astype(BF16)) + bg_ref[...]
    lg = -_softplus(-z) * (1.0 / GLA_TAU)
    cum = _cumsum_rows(tri_b, lg)
    cum3 = cum.reshape(nch, CHUNK, GLA_QK)
    last3 = cum3[:, 0:1, :] if reverse else cum3[:, CHUNK - 1:CHUNK, :]
    to_end = jnp.exp(last3 - cum3).reshape(ROWS, GLA_QK)
    k = k_ref[...]
    k_inv = k * jnp.exp(-cum)
    low_all = jnp.bitwise_and(lax.broadcasted_iota(jnp.int32, (1, GLA_QK), 1), pw - 1) < GLA_DK
    qd_s[...] = (q_ref[...] * (GLA_DK ** -0.5) * jnp.exp(cum)).astype(BF16)
    klo_s[...] = jnp.where(low_all, k_inv, 0.0).astype(BF16)
    khi_s[...] = jnp.where(low_all, 0.0, k_inv).astype(BF16)
    ke_s[...] = (k * to_end).astype(BF16)
    dec_s[0:nch, :] = jnp.exp(last3.reshape(nch, GLA_QK))

    npair = GLA_H // 2
    for c in range(nch):
        rs = slice(c * CHUNK, (c + 1) * CHUNK)
        for p in range(npair):
            ls = slice(p * pw, (p + 1) * pw)
            kbd = jnp.concatenate([klo_s[rs, ls], khi_s[rs, ls]], axis=0)
            s2_s[c * npair + p] = jnp.where(causal2, _dot_nt(qd_s[rs, ls], kbd), 0.0).astype(BF16)
            vp = v_ref[rs, p * 2 * GLA_DV:(p + 1) * 2 * GLA_DV].astype(BF16)
            upd_s[c * npair + p] = _dot_tn(ke_s[rs, ls], vp)

    for cc in range(nch):
        c = (nch - 1 - cc) if reverse else cc
        rs = slice(c * CHUNK, (c + 1) * CHUNK)
        decay = dec_s[c:c + 1, :]
        for p in range(npair):
            ls = slice(p * pw, (p + 1) * pw)
            vs = slice(p * 2 * GLA_DV, (p + 1) * 2 * GLA_DV)
            qb = qd_s[rs, ls]
            s2 = s2_s[c * npair + p]
            upd = upd_s[c * npair + p]
            vp = v_ref[rs, vs].astype(BF16)
            vbd = jnp.concatenate([jnp.concatenate([vp[:, :GLA_DV], zeros_v], axis=1),
                                   jnp.concatenate([zeros_v, vp[:, GLA_DV:]], axis=1)], axis=0)
            sp = s_ref[p]
            o = _dot(jnp.concatenate([s2, qb], axis=1), jnp.concatenate([vbd, sp.astype(BF16)], axis=0))
            if reverse:
                for hh in range(2):
                    hs = slice((2 * p + hh) * GLA_DV, (2 * p + hh + 1) * GLA_DV)
                    tot = o[:, hh * GLA_DV:(hh + 1) * GLA_DV] + of_ref[rs, hs]
                    ms = jnp.mean(tot * tot, axis=-1, keepdims=True)
                    y = tot * lax.rsqrt(ms + EPS) * gn_ref[:, hs] * _silu(og_ref[rs, hs])
                    o_ref[rs, hs] = y.astype(BF16)
            else:
                o_ref[rs, vs] = o
            dcol = jnp.sum(eye * decay[:, ls], axis=1, keepdims=True)
            for hh in range(2):
                qr = slice(hh * GLA_DK, (hh + 1) * GLA_DK)
                qc = slice(hh * GLA_DV, (hh + 1) * GLA_DV)
                s_ref[p, qr, qc] = dcol[qr] * sp[qr, qc] + upd[qr, qc]


def _gla(p0, pad, wg, bg, gnorm, of=None):
    reverse = of is not None
    nblk = N_ALL // ROWS
    blk = (lambda i: _rev_block(i, nblk)) if reverse else (lambda i: i)
    d = 1 if reverse else 0
    qk_blk = 2 * LRU_W // GLA_QK
    in_specs = [
        pl.BlockSpec((ROWS, GLA_QK), lambda i: (blk(i), qk_blk)),
        pl.BlockSpec((ROWS, GLA_QK), lambda i: (blk(i), qk_blk + 1)),
        pl.BlockSpec((ROWS, GLA_V), lambda i: (blk(i), 3)),
        pl.BlockSpec((ROWS, LANES), lambda i: (blk(i), 0)),
        pl.BlockSpec((None, GLA_RANK, GLA_QK), lambda i: (d, 0, 0)),
        pl.BlockSpec((None, 1, GLA_QK), lambda i: (d, 0, 0)),
    ]
    args = [p0, p0, p0, pad, wg, bg.reshape(2, 1, GLA_QK)]
    if reverse:
        in_specs += [
            pl.BlockSpec((ROWS, GLA_V), lambda i: (blk(i), 0)),
            pl.BlockSpec((ROWS, GLA_V), lambda i: (blk(i), 4)),
            pl.BlockSpec((1, GLA_V), lambda i: (0, 0)),
        ]
        args += [of, p0, gnorm]
    return pl.pallas_call(
        functools.partial(_gla_kernel, reverse=reverse),
        grid=(nblk,),
        in_specs=in_specs,
        out_specs=pl.BlockSpec((ROWS, GLA_V), lambda i: (blk(i), 0)),
        out_shape=jax.ShapeDtypeStruct((N_ALL, GLA_V), BF16 if reverse else F32),
        scratch_shapes=[pltpu.VMEM((GLA_H // 2, 2 * GLA_DK, 2 * GLA_DV), F32)]
        + [pltpu.VMEM((ROWS, GLA_QK), BF16) for _ in range(4)] + [pltpu.VMEM((SUBLANES, GLA_QK), F32)]
        + [pltpu.VMEM((ROWS // CHUNK * GLA_H // 2, CHUNK, 2 * GLA_DK), BF16),
           pltpu.VMEM((ROWS // CHUNK * GLA_H // 2, 2 * GLA_DK, 2 * GLA_DV), F32)],
        compiler_params=_params("arbitrary"),
        name="gla_bwd" if reverse else "gla_fwd",
    )(*args)


def _conv_silu_kernel(x_ref, prev_ref, next_ref, w_ref, b_ref, o_ref):
    i = pl.program_id(0)
    y = _conv_block(x_ref[...], prev_ref[...], next_ref[...], w_ref, b_ref, i, pl.num_programs(0))
    o_ref[...] = _silu(y)


def _conv_silu(p1, conv_w, conv_b):
    nblk = N_ALL // ROWS
    tc = 2048
    off = SSD_INNER // tc
    return pl.pallas_call(
        _conv_silu_kernel,
        grid=(nblk, SSD_XBC // tc),
        in_specs=_halo_specs(tc, lambda j: j + off, nblk) + [
            pl.BlockSpec((CONV_W, tc), lambda i, j: (0, j)),
            pl.BlockSpec((1, tc), lambda i, j: (0, j)),
        ],
        out_specs=pl.BlockSpec((ROWS, tc), lambda i, j: (i, j)),
        out_shape=jax.ShapeDtypeStruct((N_ALL, SSD_XBC), F32),
        compiler_params=_params("arbitrary", "arbitrary"),
        name="ssd_conv",
    )(p1, p1, p1, conv_w, conv_b)


def _ssd_kernel(*refs, reverse):
    if reverse:
        (xs_ref, b_ref, c_ref, dt_ref, dtb_ref, alog_ref, acc_ref, z_ref, dsk_ref, gn_ref, y_ref,
         st_ref, cum_s, w2_s, ct2_s, dt2_s, ybuf) = refs
    else:
        xs_ref, b_ref, c_ref, dt_ref, dtb_ref, alog_ref, y_ref, st_ref, cum_s, w2_s, ct2_s, dt2_s = refs
    d = 1 if reverse else 0
    i = pl.program_id(0)

    @pl.when(i == 0)
    def _():
        st_ref[...] = jnp.zeros_like(st_ref)

    tri_b = _tri(reverse).astype(BF16)
    row2 = lax.broadcasted_iota(jnp.int32, (CHUNK, 2 * SSD_P), 0)
    col2 = jnp.bitwise_and(lax.broadcasted_iota(jnp.int32, (CHUNK, 2 * SSD_P), 1), SSD_P - 1)
    causal2 = (col2 >= row2) if reverse else (col2 <= row2)
    low = lax.broadcasted_iota(jnp.int32, (1, 2 * SSD_P), 1) < SSD_P
    a_neg = -jnp.exp(alog_ref[...])
    nch = ROWS // CHUNK
    hg = SSD_H // SSD_G
    pw = 2 * SSD_P

    low_c = lax.broadcasted_iota(jnp.int32, (CHUNK, 2 * SSD_P), 1) < SSD_P

    def pair(v, h0):
        if v.shape == (CHUNK, 2 * SSD_P):
            return jnp.take_along_axis(v, jnp.where(low_c, h0, h0 + 1), axis=1)
        return jnp.where(low, v[:, h0:h0 + 1], v[:, h0 + 1:h0 + 2])

    dtv = _softplus(dt_ref[:, d * SSD_H:(d + 1) * SSD_H] + dtb_ref[...])
    la = dtv * a_neg
    for c in range(nch):
        rs = slice(c * CHUNK, (c + 1) * CHUNK)
        cum = _cumsum_rows(tri_b, la[rs])
        last = cum[0:1, :] if reverse else cum[CHUNK - 1:CHUNK, :]
        w2 = dtv[rs] * jnp.exp(last - cum)
        cum_s[c] = jnp.concatenate([cum, cum], axis=1)
        w2_s[c] = jnp.concatenate([w2, w2], axis=1)
        cum_t = cum.T
        dt_t = dtv[rs].T
        ct2_s[c] = jnp.concatenate([cum_t, cum_t], axis=1)
        dt2_s[c] = jnp.concatenate([dt_t, dt_t], axis=1)

    def chunk(cc, carry):
        c = (nch - 1 - cc) if reverse else cc
        rs = pl.ds(pl.multiple_of(c * CHUNK, CHUNK), CHUNK)
        cum = cum_s[c]
        w2 = w2_s[c]
        ct2 = ct2_s[c]
        dt2 = dt2_s[c]
        dec = jnp.exp(cum[0:1, :] if reverse else cum[CHUNK - 1:CHUNK, :])
        cbs, yis = [], []
        for g in range(SSD_G):
            gs = slice(g * SSD_Z, (g + 1) * SSD_Z)
            cg = c_ref[rs, gs].astype(BF16)
            cbs.append(_dot_nt(cg, b_ref[rs, gs].astype(BF16)))
            yis.append(_dot(cg, st_ref[g].astype(BF16)))
        for g in range(SSD_G):
            gs = slice(g * SSD_Z, (g + 1) * SSD_Z)
            bgb = b_ref[rs, gs].astype(BF16)
            cb2 = jnp.concatenate([cbs[g], cbs[g]], axis=1)
            st = st_ref[g]
            y_inter = yis[g]
            x2s, decs = [], []
            for pr in range(hg // 2):
                h0 = g * hg + 2 * pr
                ps = slice(h0 * SSD_P, (h0 + 2) * SSD_P)
                xp = xs_ref[rs, ps]
                ccol = pair(cum, h0)
                rrow = jnp.where(low, ct2[h0:h0 + 1, :], ct2[h0 + 1:h0 + 2, :])
                drow = jnp.where(low, dt2[h0:h0 + 1, :], dt2[h0 + 1:h0 + 2, :])
                seg = jnp.exp(jnp.where(causal2, ccol - rrow, -jnp.inf))
                m = (cb2 * seg * drow).astype(BF16)
                xbd = jnp.concatenate([jnp.where(low, xp, 0.0), jnp.where(low, 0.0, xp)], axis=0).astype(BF16)
                y = _dot(m, xbd) + y_inter[:, pr * pw:(pr + 1) * pw] * jnp.exp(ccol)
                if reverse:
                    ybuf[:, ps] = y + acc_ref[rs, ps]
                else:
                    y_ref[rs, ps] = y
                x2s.append((xp * pair(w2, h0)).astype(BF16))
                decs.append(pair(dec, h0))
            upd = _dot_tn(bgb, jnp.concatenate(x2s, axis=1))
            st_ref[g] = st * jnp.concatenate(decs, axis=1) + upd
        if reverse:
            u = (ybuf[...] + dsk_ref[...] * xs_ref[rs, :]) * _silu(z_ref[rs, :])
            ms = jnp.mean(u * u, axis=-1, keepdims=True)
            y_ref[rs, :] = (u * lax.rsqrt(ms + EPS) * gn_ref[...]).astype(BF16)
        return carry

    lax.fori_loop(0, nch, chunk, 0)


def _ssd(xbc, pdt, dt_bias, a_log, fin=None):
    reverse = fin is not None
    nblk = N_ALL // ROWS
    nctx = T_CTX // ROWS
    blk = (lambda i: _rev_block(i, nblk)) if reverse else (lambda i: i)
    d = 1 if reverse else 0
    gz = SSD_G * SSD_Z
    full = lambda i: (blk(i), 0)
    in_specs = [
        pl.BlockSpec((ROWS, SSD_INNER), full),
        pl.BlockSpec((ROWS, gz), lambda i: (blk(i), SSD_INNER // gz)),
        pl.BlockSpec((ROWS, gz), lambda i: (blk(i), SSD_INNER // gz + 1)),
        pl.BlockSpec((ROWS, LANES), full),
        pl.BlockSpec((None, 1, SSD_H), lambda i: (d, 0, 0)),
        pl.BlockSpec((None, 1, SSD_H), lambda i: (d, 0, 0)),
    ]
    args = [xbc, xbc, xbc, pdt, dt_bias.reshape(2, 1, SSD_H), a_log.reshape(2, 1, SSD_H)]
    nch = ROWS // CHUNK
    scratch = [pltpu.VMEM((SSD_G, SSD_Z, SSD_INNER // SSD_G), F32),
               pltpu.VMEM((nch, CHUNK, 2 * SSD_H), F32), pltpu.VMEM((nch, CHUNK, 2 * SSD_H), F32),
               pltpu.VMEM((nch, SSD_H, 2 * CHUNK), F32), pltpu.VMEM((nch, SSD_H, 2 * CHUNK), F32)]
    if reverse:
        yf, p1, d_rep, norm_g = fin
        in_specs += [pl.BlockSpec((ROWS, SSD_INNER), full), pl.BlockSpec((ROWS, SSD_INNER), full),
                     pl.BlockSpec((1, SSD_INNER), lambda i: (0, 0)), pl.BlockSpec((1, SSD_INNER), lambda i: (0, 0))]
        args += [yf, p1, d_rep, norm_g]
        scratch.append(pltpu.VMEM((CHUNK, SSD_INNER), F32))
        out_spec = pl.BlockSpec((ROWS, SSD_INNER), lambda i: (blk(jnp.maximum(i, nctx)) - nctx, 0))
        out_shape = jax.ShapeDtypeStruct((T_LAT, SSD_INNER), BF16)
    else:
        out_spec = pl.BlockSpec((ROWS, SSD_INNER), full)
        out_shape = jax.ShapeDtypeStruct((N_ALL, SSD_INNER), F32)
    return pl.pallas_call(
        functools.partial(_ssd_kernel, reverse=reverse),
        grid=(nblk,),
        in_specs=in_specs,
        out_specs=out_spec,
        out_shape=out_shape,
        scratch_shapes=scratch,
        compiler_params=_params("arbitrary"),
        name="ssd_bwd" if reverse else "ssd_fwd",
    )(*args)


def _gather_wait(h_hbm, buf, sem, slot):
    pltpu.make_async_copy(h_hbm.at[pl.ds(0, MOE_TM), :], buf.at[slot], sem.at[slot]).wait()


GATHER_SEG = 32


def _gather_issue(idx_ref, n_valid, h_hbm, buf, sem, slot):
    def body(r8, carry):
        base = pl.multiple_of(r8 * SUBLANES, SUBLANES)
        for u in range(SUBLANES):
            tok = idx_ref[0, 0, base + u]
            pltpu.make_async_copy(h_hbm.at[pl.ds(tok, 1), :], buf.at[slot, pl.ds(base + u, 1), :],
                                  sem.at[slot]).start()
        return carry

    per = GATHER_SEG // SUBLANES
    for s in range(MOE_TM // GATHER_SEG):
        @pl.when(s * GATHER_SEG < n_valid)
        def _():
            lax.fori_loop(s * per, (s + 1) * per, body, 0)

        @pl.when(s * GATHER_SEG >= n_valid)
        def _():
            pltpu.make_async_copy(h_hbm.at[pl.ds(0, GATHER_SEG), :],
                                  buf.at[slot, pl.ds(s * GATHER_SEG, GATHER_SEG), :], sem.at[slot]).start()


def _expert_block_kernel(be_ref, ws_ref, nx_ref, nv_ref, nu_ref, *refs, n_w, layer, gather, compute):
    if gather:
        icur_ref, inxt_ref, h_hbm = refs[:3]
        refs = refs[3:]
    else:
        x_ref = refs[0]
        refs = refs[1:]
    w_hbm, o_ref = refs[:n_w], refs[n_w]
    scr = refs[n_w + 1:]
    w32, wsem = scr[:n_w], scr[n_w]
    m = pl.program_id(0)
    nu = nu_ref[0]
    e = be_ref[m]
    wslot = ws_ref[m]
    first = jnp.logical_and(m < nu, jnp.logical_or(m == 0, e != be_ref[jnp.maximum(m - 1, 0)]))

    def w_copy(k, ee, slot):
        return pltpu.make_async_copy(w_hbm[k].at[layer, ee], w32[k].at[slot], wsem.at[k, slot])

    @pl.when(m == 0)
    def _():
        for k in range(n_w):
            w_copy(k, e, wslot).start()

    if gather:
        gbuf, gsem = scr[n_w + 1:]
        gslot = lax.rem(m, 2)

        @pl.when(m == 0)
        def _():
            _gather_issue(icur_ref, nv_ref[0], h_hbm, gbuf, gsem, 0)

        @pl.when(m + 1 < nu)
        def _():
            _gather_issue(inxt_ref, nv_ref[m + 1], h_hbm, gbuf, gsem, 1 - gslot)

    @pl.when(jnp.logical_and(first, nx_ref[m] >= 0))
    def _():
        for k in range(n_w):
            w_copy(k, nx_ref[m], 1 - wslot).start()

    @pl.when(first)
    def _():
        for k in range(n_w):
            w_copy(k, e, wslot).wait()

    @pl.when(m < nu)
    def _():
        if gather:
            _gather_wait(h_hbm, gbuf, gsem, gslot)
            x = gbuf[gslot]
        else:
            x = x_ref[...].astype(F32)
        o_ref[...] = compute(x, *[w32[k][wslot] for k in range(n_w)]).astype(o_ref.dtype)


def _expert_stage(src, ws, plan, layer, n_out, out_dtype, compute, name, slot_tok=None):
    nblk = plan[0].shape[0]
    gather = slot_tok is not None
    anyspec = pl.BlockSpec(memory_space=pl.ANY)
    used = lambda m, p: jnp.minimum(m, p[-1][0] - 1)
    n_w = len(ws)
    wshape = ws[0].shape[2:]
    scratch = [pltpu.VMEM((2,) + wshape, F32) for _ in ws]
    scratch.append(pltpu.SemaphoreType.DMA((n_w, 2)))
    if gather:
        in_specs = [
            pl.BlockSpec((1, 1, MOE_TM), lambda m, *p: (used(m, p), 0, 0), memory_space=pltpu.SMEM),
            pl.BlockSpec((1, 1, MOE_TM), lambda m, *p: (used(m + 1, p), 0, 0), memory_space=pltpu.SMEM),
            anyspec,
        ]
        args = [slot_tok, slot_tok, src]
        scratch += [pltpu.VMEM((2, MOE_TM, src.shape[1]), src.dtype), pltpu.SemaphoreType.DMA((2,))]
    else:
        in_specs = [pl.BlockSpec((MOE_TM, src.shape[1]), lambda m, *p: (used(m, p), 0))]
        args = [src]
    return pl.pallas_call(
        functools.partial(_expert_block_kernel, n_w=n_w, layer=layer, gather=gather, compute=compute),
        grid_spec=pltpu.PrefetchScalarGridSpec(
            num_scalar_prefetch=len(plan),
            grid=(nblk,),
            in_specs=in_specs + [anyspec] * n_w,
            out_specs=pl.BlockSpec((MOE_TM, n_out), lambda m, *p: (used(m, p), 0)),
            scratch_shapes=scratch,
        ),
        out_shape=jax.ShapeDtypeStruct((nblk * MOE_TM, n_out), out_dtype),
        compiler_params=_params("arbitrary"),
        name=name,
    )(*plan, *args, *ws)


def _moe(h, eidx, wts, w_gate, w_up, w_down, layer):
    n = h.shape[0]
    i32 = jnp.int32
    flat_e = eidx.reshape(-1)
    experts = jnp.arange(N_EXP, dtype=i32)
    onehot = (flat_e[:, None] == experts[None, :]).astype(i32)
    csum = jnp.cumsum(onehot, axis=0)
    bcount = (csum[-1] + MOE_TM - 1) // MOE_TM
    bend = jnp.cumsum(bcount)
    bstart = bend - bcount
    dest = jnp.sum(onehot * (csum - 1 + (bstart * MOE_TM)[None, :]), axis=1)
    nblk = -(-2 * n // MOE_TM) + N_EXP
    n_used = bend[-1].astype(i32)
    blk = jnp.minimum(jnp.arange(nblk, dtype=i32), n_used - 1)
    block_exp = jnp.sum((blk[:, None] >= bend[None, :]).astype(i32), axis=1)
    owns = bcount > 0
    wslot = (jnp.cumsum(owns.astype(i32)) - 1) % 2
    later = jnp.logical_and(owns[None, :], experts[None, :] > experts[:, None])
    nxt = jnp.min(jnp.where(later, experts[None, :], N_EXP), axis=1)
    nxt = jnp.where(nxt == N_EXP, -1, nxt)
    of_blk = (block_exp[:, None] == experts[None, :]).astype(i32)
    rows_left = jnp.sum(of_blk * (csum[-1] + bstart * MOE_TM)[None, :], axis=1) - blk * MOE_TM
    n_valid = jnp.clip(rows_left, 0, MOE_TM)
    plan = (block_exp.astype(i32), jnp.sum(of_blk * wslot[None, :], axis=1).astype(i32),
            jnp.sum(of_blk * nxt[None, :], axis=1).astype(i32), n_valid.astype(i32), n_used.reshape(1))
    tok = jnp.tile(jnp.arange(n, dtype=i32), 2)
    pad_tok = jnp.arange(nblk * MOE_TM, dtype=i32) % n
    slot_tok = pad_tok.at[dest].set(tok).reshape(nblk, 1, MOE_TM)
    hb = _expert_stage(h, [w_gate, w_up], plan, layer, D_EXP, BF16,
                       lambda x, wg, wu: _silu(_dot(x, wg)) * _dot(x, wu), "moe_up", slot_tok=slot_tok)
    yb = _expert_stage(hb, [w_down], plan, layer, D, F32, lambda x, wd: _dot(x, wd), "moe_down")
    return yb, dest.reshape(2, n), wts.T


def _pos_tables():
    quarter = D // 4
    omega = 1.0 / (10000.0 ** (jnp.arange(quarter, dtype=F32) / quarter))
    ang_r = jnp.arange(T_LAT // GRID_W, dtype=F32)[:, None] * omega
    ang_c = jnp.arange(GRID_W, dtype=F32)[:, None] * omega
    emb_r = jnp.concatenate([jnp.sin(ang_r), jnp.cos(ang_r)], axis=-1)
    emb_c = jnp.concatenate([jnp.sin(ang_c), jnp.cos(ang_c)], axis=-1)
    return emb_r, emb_c


def _block_diag_gates(wa, wx, group):
    nb = wa.shape[1]
    per = group // wa.shape[2]
    eye = jnp.eye(per, dtype=F32)

    def bd(w):
        w = w.reshape(nb // per, per, w.shape[1], w.shape[2])
        return jnp.einsum("gnkj,nm->gnkmj", w, eye).reshape(nb // per, group, group)

    return jnp.concatenate([bd(wa[0]), bd(wa[1]), bd(wx[0]), bd(wx[1])], axis=-1).astype(BF16)


def kernel(x, c, ctx, c_ctx, mod_w, mod_b, norm1_g, norm2_g, ev_w_in, ev_conv_w, ev_conv_b, lru_wa, lru_ba, lru_wx, lru_bx, lru_lambda, gla_wg_up, gla_bg, gla_norm_g, ev_w_out, od_w_in, od_conv_w, od_conv_b, ssd_a_log, ssd_dt_bias, ssd_d, ssd_norm_g, od_w_out, router_w, router_b, exp_w_gate, exp_w_up, exp_w_down, final_norm_g):
    mods = _mod_vectors(c, c_ctx, mod_w, mod_b)
    emb_r, emb_c = _pos_tables()
    rwt = router_w.T
    rb = router_b.reshape(N_EXP, 1)
    tm, tn = PROJ_TM, PROJ_TN

    x0, h0 = _prep0(x[0], ctx[0], emb_r, emb_c, mods[0], norm1_g[0:1])
    w_in_t = jnp.swapaxes(ev_w_in[0], 0, 1)
    p0 = _mm([h0], w_in_t, ncols=EVEN_MAIN, tm=tm, tn=tn, w_t=True)
    pad = _mm([h0], w_in_t, ncols=LANES, tm=tm, tn=LANES, col_off=EVEN_MAIN // LANES, w_t=True)
    wbd = _block_diag_gates(lru_wa[0], lru_wx[0], LRU_GATE_GROUP)
    ya = _lru(p0, ev_conv_w[0], ev_conv_b[0:1], wbd, lru_ba[0], lru_bx[0], lru_lambda[0])
    of = _gla(p0, pad, gla_wg_up[0], gla_bg[0], None)
    yb = _gla(p0, pad, gla_wg_up[0], gla_bg[0], gla_norm_g[0:1], of=of)
    y0 = _mm([ya, yb], ev_w_out[0], ncols=D, tm=tm, tn=tn)
    nctx = T_CTX // ROWS
    x1, h1, e0, w0 = _token_stage(x0, y0, mods[0], mods[0], norm2_g[0:1], rwt, rb, gate_col=2, shift_col=3,
                                  scale_col=4, ctx_blocks=nctx, route=True)
    f0, dest0, wc0 = _moe(h1, e0, w0, exp_w_gate, exp_w_up, exp_w_down, 0)
    x2, h2 = _token_stage(x1, f0, mods[0], mods[1], norm1_g[1:2], rwt, rb, gate_col=5, shift_col=0,
                          scale_col=1, ctx_blocks=nctx, combine=(dest0, wc0))

    p1 = _mm([h2], od_w_in[0], ncols=ODD_MAIN, tm=tm, tn=tn)
    pdt = _mm([h2], od_w_in[0], ncols=LANES, tm=tm, tn=LANES, col_off=ODD_MAIN // LANES)
    xbc = _conv_silu(p1, od_conv_w[0], od_conv_b[0:1])
    yf = _ssd(xbc, pdt, ssd_dt_bias[0], ssd_a_log[0])
    d_rep = jnp.repeat(ssd_d[0], SSD_P).reshape(1, SSD_INNER)
    gy = _ssd(xbc, pdt, ssd_dt_bias[0], ssd_a_log[0], fin=(yf, p1, d_rep, ssd_norm_g[0:1]))
    y1 = _mm([gy], od_w_out[0], ncols=D, tm=OUT1_TM, tn=OUT1_TN)
    x3, h3, e1, w1 = _token_stage(x2, y1, mods[1], mods[1], norm2_g[1:2], rwt, rb, gate_col=2, shift_col=3,
                                  scale_col=4, x_off=nctx, route=True)
    f1, dest1, wc1 = _moe(h3, e1, w1, exp_w_gate, exp_w_up, exp_w_down, 1)
    (out,) = _token_stage(x3, f1, mods[1], mods[1], final_norm_g.reshape(1, D), rwt, rb, gate_col=5,
                          combine=(dest1, wc1), final=True)
    return out[None]
```
